```python
import math
import jax, jax.numpy as jnp
from jax import lax
import numpy as np

D_MODEL = 2048
BATCH = 4
SEQ = 2048
DEPTH = 2
DEC_BATCH = 8
DEC_SEQ = 4
PAST_LEN = 16384
PAGE_SIZE = 128

N_MIXERS = 2
N_NSA_LAYERS = (DEPTH + 1) // 2
N_SB_LAYERS = DEPTH // 2
HEAD_DIM = 128
N_HEADS = D_MODEL // HEAD_DIM
NSA_KV_HEADS = 4
NSA_GROUP = N_HEADS // NSA_KV_HEADS
CMP_BLOCK = 32
CMP_STRIDE = 16
SEL_BLOCK = 64
SEL_TOPK = 16
WINDOW = 512
N_BRANCH = 3
NSA_Q_BLOCK = 16
NSA_IN = N_HEADS * HEAD_DIM + 6 * NSA_KV_HEADS * HEAD_DIM + N_BRANCH * N_HEADS
FORCE_BONUS = 1000.0
SB_Q_BLOCK = 128
SB_IN = 3 * N_HEADS * HEAD_DIM
REL_BUCKETS = 32
REL_MAX_DIST = 128
D_FF = -(-8 * D_MODEL // (3 * 256)) * 256
RMS_EPS = 1e-6
NEG_INF = -1e30

kernel_name = 'nsa_stickbreak_hybrid_step'


def rmsnorm(x, w):
    xf = x.astype(jnp.float32)
    y = xf * lax.rsqrt(jnp.mean(xf * xf, axis=-1, keepdims=True) + RMS_EPS)
    return (y * w.astype(jnp.float32)).astype(x.dtype)


def swiglu(x, w1, w3, w2):
    return (jax.nn.silu(x @ w1) * (x @ w3)) @ w2


def masked_softmax(s, mask):
    p = jax.nn.softmax(jnp.where(mask, s, NEG_INF), axis=-1)
    return jnp.where(mask, p, 0.0)


def t5_bucket(dist):
    n = jnp.maximum(dist, 0)
    max_exact = REL_BUCKETS // 2
    nf = jnp.maximum(n, 1).astype(jnp.float32)
    large = max_exact + (jnp.log(nf / max_exact) / math.log(REL_MAX_DIST / max_exact)
                         * (REL_BUCKETS - max_exact)).astype(jnp.int32)
    large = jnp.minimum(large, REL_BUCKETS - 1)
    return jnp.where(n < max_exact, n, large)


def bias_shared(table, dist):
    t, n = dist.shape
    b = table[t5_bucket(dist)].reshape(t, n, NSA_KV_HEADS, NSA_GROUP)
    return b.transpose(0, 2, 3, 1).astype(jnp.float32)


def bias_gathered(table, dist):
    tbl = table.reshape(REL_BUCKETS, NSA_KV_HEADS, NSA_GROUP)
    hi = jnp.arange(NSA_KV_HEADS)[None, None, :, None]
    b = tbl[t5_bucket(dist), hi]
    return jnp.moveaxis(b, -1, 3).astype(jnp.float32)


def nsa_project(xn, w_in):
    b, t = xn.shape[:2]
    hd = N_HEADS * HEAD_DIM
    kvd = NSA_KV_HEADS * HEAD_DIM
    y = xn @ w_in
    q = y[..., :hd].reshape(b, t, NSA_KV_HEADS, NSA_GROUP, HEAD_DIM)
    kv = y[..., hd:hd + 6 * kvd].reshape(b, t, 6, NSA_KV_HEADS, HEAD_DIM)
    g = jax.nn.sigmoid(y[..., hd + 6 * kvd:].astype(jnp.float32)).astype(xn.dtype)
    g = g.reshape(b, t, NSA_KV_HEADS, NSA_GROUP, N_BRANCH)
    return q, kv, g


def compress(k, pe, w1, w2):
    b, L = k.shape[:2]
    ratio = CMP_BLOCK // CMP_STRIDE
    nch = L // CMP_STRIDE
    nb = nch - ratio + 1
    ch = k[:, :nch * CMP_STRIDE].reshape(b, nch, CMP_STRIDE, NSA_KV_HEADS, HEAD_DIM)
    blk = jnp.concatenate([ch[:, j:j + nb] for j in range(ratio)], axis=2)
    blk = blk + pe[:, None, :]
    flat = blk.transpose(0, 1, 3, 2, 4).reshape(b, nb, NSA_KV_HEADS, CMP_BLOCK * HEAD_DIM)
    return jax.nn.silu(flat @ w1) @ w2


def nsa_prepare(kv4, pe_k, pe_v, k1, k2, v1, v2):
    b, L = kv4.shape[:2]
    kcb = compress(kv4[:, :, 0], pe_k, k1, k2)
    vcb = compress(kv4[:, :, 1], pe_v, v1, v2)
    nb = kcb.shape[1]
    cmp_start = jnp.arange(nb) * CMP_STRIDE
    cmp_end = cmp_start + CMP_BLOCK - 1
    nblk = -(-L // SEL_BLOCK)
    pad = nblk * SEL_BLOCK - L
    sel = jnp.pad(kv4[:, :, 2:4], ((0, 0), (0, pad), (0, 0), (0, 0), (0, 0)))
    sel = sel.reshape(b, nblk, SEL_BLOCK, 2, NSA_KV_HEADS, HEAD_DIM).transpose(3, 0, 4, 1, 2, 5)
    ks_b, vs_b = sel[0], sel[1]
    jst = jnp.arange(nblk) * SEL_BLOCK
    overlap = ((cmp_start[:, None] < jst[None, :] + SEL_BLOCK)
               & (cmp_start[:, None] + CMP_BLOCK > jst[None, :])).astype(jnp.float32)
    return kcb, vcb, cmp_end, ks_b, vs_b, overlap


def nsa_chunk(q, pos_q, g, kcb, vcb, cmp_end, ks_b, vs_b, overlap, kw, vw, pos_w, rel_bias):
    b, t = q.shape[:2]
    scale = HEAD_DIM ** -0.5
    f32 = jnp.float32
    dist_c = pos_q[:, None] - cmp_end[None, :]
    s_c = jnp.einsum('btkgd,bnkd->btkgn', q, kcb, preferred_element_type=f32) * scale
    s_c = s_c + bias_shared(rel_bias, dist_c)[None]
    p_c = masked_softmax(s_c, (dist_c >= 0)[None, :, None, None, :])
    o_c = jnp.einsum('btkgn,bnkd->btkgd', p_c.astype(vcb.dtype), vcb)
    nblk = ks_b.shape[2]
    n_top = min(SEL_TOPK, nblk)
    blk = jnp.arange(nblk)[None, :]
    cur = (pos_q // SEL_BLOCK)[:, None]
    eligible = blk * SEL_BLOCK <= pos_q[:, None]
    forced = (blk == 0) | (blk == cur) | (blk == cur - 1)
    imp = jnp.einsum('btkgn,nj->btkj', p_c, overlap)
    score = jnp.where(eligible[None, :, None, :],
                      imp + jnp.where(forced, FORCE_BONUS, 0.0)[None, :, None, :], -jnp.inf)
    top_v, top_i = lax.top_k(score, n_top)
    bi = jnp.arange(b)[:, None, None, None]
    hi = jnp.arange(NSA_KV_HEADS)[None, None, :, None]
    n_sel = n_top * SEL_BLOCK
    ks_g = ks_b[bi, hi, top_i].reshape(b, t, NSA_KV_HEADS, n_sel, HEAD_DIM)
    vs_g = vs_b[bi, hi, top_i].reshape(b, t, NSA_KV_HEADS, n_sel, HEAD_DIM)
    pos_s = top_i[..., None] * SEL_BLOCK + jnp.arange(SEL_BLOCK)
    mask_s = jnp.isfinite(top_v)[..., None] & (pos_s <= pos_q[None, :, None, None, None])
    pos_s = pos_s.reshape(b, t, NSA_KV_HEADS, n_sel)
    mask_s = mask_s.reshape(b, t, NSA_KV_HEADS, n_sel)
    s_s = jnp.einsum('btkgd,btksd->btkgs', q, ks_g, preferred_element_type=f32) * scale
    s_s = s_s + bias_gathered(rel_bias, pos_q[None, :, None, None] - pos_s)
    p_s = masked_softmax(s_s, mask_s[:, :, :, None, :])
    o_s = jnp.einsum('btkgs,btksd->btkgd', p_s.astype(vs_g.dtype), vs_g)
    dist_w = pos_q[:, None] - pos_w[None, :]
    mask_w = (dist_w >= 0) & (dist_w < WINDOW) & (pos_w >= 0)[None, :]
    s_w = jnp.einsum('btkgd,bskd->btkgs', q, kw, preferred_element_type=f32) * scale
    s_w = s_w + bias_shared(rel_bias, dist_w)[None]
    p_w = masked_softmax(s_w, mask_w[None, :, None, None, :])
    o_w = jnp.einsum('btkgs,bskd->btkgd', p_w.astype(vw.dtype), vw)
    return g[..., 0:1] * o_c + g[..., 1:2] * o_s + g[..., 2:3] * o_w


def nsa_prompt(xn, rel_bias, w_in, w_out, pe_k, pe_v, k1, k2, v1, v2):
    b, t = xn.shape[:2]
    q, kv, g = nsa_project(xn, w_in)
    kcb, vcb, cmp_end, ks_b, vs_b, overlap = nsa_prepare(kv[:, :, :4], pe_k, pe_v, k1, k2, v1, v2)
    kw_pad = jnp.pad(kv[:, :, 4], ((0, 0), (WINDOW, 0), (0, 0), (0, 0)))
    vw_pad = jnp.pad(kv[:, :, 5], ((0, 0), (WINDOW, 0), (0, 0), (0, 0)))
    span = WINDOW + NSA_Q_BLOCK

    def step(ci):
        start = ci * NSA_Q_BLOCK
        qc = lax.dynamic_slice_in_dim(q, start, NSA_Q_BLOCK, axis=1)
        gc = lax.dynamic_slice_in_dim(g, start, NSA_Q_BLOCK, axis=1)
        kwc = lax.dynamic_slice_in_dim(kw_pad, start, span, axis=1)
        vwc = lax.dynamic_slice_in_dim(vw_pad, start, span, axis=1)
        pos_c = start + jnp.arange(NSA_Q_BLOCK)
        pos_w = start - WINDOW + jnp.arange(span)
        return nsa_chunk(qc, pos_c, gc, kcb, vcb, cmp_end, ks_b, vs_b, overlap, kwc, vwc, pos_w, rel_bias)

    o = lax.map(step, jnp.arange(t // NSA_Q_BLOCK))
    o = o.transpose(1, 0, 2, 3, 4, 5).reshape(b, t, N_HEADS * HEAD_DIM)
    win_keep = min(WINDOW, t)
    return o @ w_out, kv[:, :, :4], kv[:, t - win_keep:, 4:6]


def nsa_sample(xn, cache_nsa, win_state, page_table, li, rel_bias, w_in, w_out, pe_k, pe_v, k1, k2, v1, v2):
    b, t = xn.shape[:2]
    past_len = page_table.shape[1] * PAGE_SIZE
    q, kv, g = nsa_project(xn, w_in)
    past = cache_nsa[li, page_table].reshape(b, past_len, 4, NSA_KV_HEADS, HEAD_DIM)
    kv4 = jnp.concatenate([past, kv[:, :, :4]], axis=1)
    kcb, vcb, cmp_end, ks_b, vs_b, overlap = nsa_prepare(kv4, pe_k, pe_v, k1, k2, v1, v2)
    wb = win_state.shape[1]
    win_all = jnp.concatenate([win_state, kv[:, :, 4:6]], axis=1)
    pos_w = past_len - wb + jnp.arange(wb + t)
    pos_q = past_len + jnp.arange(t)
    o = nsa_chunk(q, pos_q, g, kcb, vcb, cmp_end, ks_b, vs_b, overlap,
                  win_all[:, :, 0], win_all[:, :, 1], pos_w, rel_bias)
    o = o.reshape(b, t, N_HEADS * HEAD_DIM)
    return o @ w_out, kv[:, :, :4], win_all[:, t:]


def sb_project(xn, w_in):
    b, t = xn.shape[:2]
    y = (xn @ w_in).reshape(b, t, 3, N_HEADS, HEAD_DIM)
    return y[:, :, 0], y[:, :, 1:3]


def sb_chunk(q, pos_q, k, v, pos_k):
    z = jnp.einsum('bthd,bshd->bhts', q, k, preferred_element_type=jnp.float32) * (HEAD_DIM ** -0.5)
    mask = (pos_k[None, :] < pos_q[:, None])[None, None]
    log_beta = jax.nn.log_sigmoid(z)
    log_rest = jnp.where(mask, jax.nn.log_sigmoid(-z), 0.0)
    later = lax.cumsum(log_rest, axis=3, reverse=True) - log_rest
    a = jnp.where(mask, jnp.exp(log_beta + later), 0.0)
    return jnp.einsum('bhts,bshd->bthd', a.astype(v.dtype), v)


def sb_prompt(xn, w_in, w_out):
    b, t = xn.shape[:2]
    q, kv = sb_project(xn, w_in)
    k, v = kv[:, :, 0], kv[:, :, 1]
    pos_k = jnp.arange(t)

    def step(ci):
        start = ci * SB_Q_BLOCK
        qc = lax.dynamic_slice_in_dim(q, start, SB_Q_BLOCK, axis=1)
        return sb_chunk(qc, start + jnp.arange(SB_Q_BLOCK), k, v, pos_k)

    o = lax.map(step, jnp.arange(t // SB_Q_BLOCK))
    o = o.transpose(1, 0, 2, 3, 4).reshape(b, t, N_HEADS * HEAD_DIM)
    return o @ w_out, kv


def sb_sample(xn, cache_sb, page_table, li, w_in, w_out):
    b, t = xn.shape[:2]
    past_len = page_table.shape[1] * PAGE_SIZE
    q, kv = sb_project(xn, w_in)
    past = cache_sb[li, page_table].reshape(b, past_len, 2, N_HEADS, HEAD_DIM)
    kv_all = jnp.concatenate([past, kv], axis=1)
    pos_k = jnp.arange(past_len + t)
    pos_q = past_len + jnp.arange(t)
    o = sb_chunk(q, pos_q, kv_all[:, :, 0], kv_all[:, :, 1], pos_k)
    return o.reshape(b, t, N_HEADS * HEAD_DIM) @ w_out, kv


def setup_inputs(seed: int = 0) -> dict:
    key = jax.random.key(seed)
    ks = jax.random.split(key, 24)
    f32 = jnp.float32
    n_pages = PAST_LEN // PAGE_SIZE
    n_used = DEC_BATCH * n_pages
    n_phys = n_used + max(1, n_used // 4)
    win_buf = min(WINDOW, PAST_LEN)
    hd = N_HEADS * HEAD_DIM

    def nrm(k, shape, scale):
        return jax.random.normal(k, shape, f32) * scale

    return {
        'x_prompt': nrm(ks[0], (BATCH, SEQ, D_MODEL), 1.0),
        'x_sample': nrm(ks[1], (DEC_BATCH, DEC_SEQ, D_MODEL), 1.0),
        'cache_nsa': nrm(ks[2], (N_NSA_LAYERS, n_phys, PAGE_SIZE, 4, NSA_KV_HEADS, HEAD_DIM), 1.0),
        'state_nsa_win': nrm(ks[3], (N_NSA_LAYERS, DEC_BATCH, win_buf, 2, NSA_KV_HEADS, HEAD_DIM), 1.0),
        'cache_sb': nrm(ks[4], (N_SB_LAYERS, n_phys, PAGE_SIZE, 2, N_HEADS, HEAD_DIM), 1.0),
        'page_table': jax.random.permutation(ks[5], n_phys)[:n_used].reshape(DEC_BATCH, n_pages).astype(jnp.int32),
        'rel_bias': nrm(ks[6], (REL_BUCKETS, N_HEADS), 0.5),
        'norm_w': 1.0 + nrm(ks[7], (DEPTH, 2, D_MODEL), 0.01),
        'final_norm_w': 1.0 + nrm(ks[8], (D_MODEL,), 0.01),
        'nsa_w_in': nrm(ks[9], (N_NSA_LAYERS, D_MODEL, NSA_IN), D_MODEL ** -0.5),
        'nsa_w_out': nrm(ks[10], (N_NSA_LAYERS, hd, D_MODEL), hd ** -0.5),
        'nsa_pe_k': nrm(ks[11], (N_NSA_LAYERS, CMP_BLOCK, HEAD_DIM), 0.1),
        'nsa_pe_v': nrm(ks[12], (N_NSA_LAYERS, CMP_BLOCK, HEAD_DIM), 0.1),
        'nsa_phi_k1': nrm(ks[13], (N_NSA_LAYERS, CMP_BLOCK * HEAD_DIM, HEAD_DIM), (CMP_BLOCK * HEAD_DIM) ** -0.5),
        'nsa_phi_k2': nrm(ks[14], (N_NSA_LAYERS, HEAD_DIM, HEAD_DIM), HEAD_DIM ** -0.5),
        'nsa_phi_v1': nrm(ks[15], (N_NSA_LAYERS, CMP_BLOCK * HEAD_DIM, HEAD_DIM), (CMP_BLOCK * HEAD_DIM) ** -0.5),
        'nsa_phi_v2': nrm(ks[16], (N_NSA_LAYERS, HEAD_DIM, HEAD_DIM), HEAD_DIM ** -0.5),
        'sb_w_in': nrm(ks[17], (N_SB_LAYERS, D_MODEL, SB_IN), D_MODEL ** -0.5),
        'sb_w_out': nrm(ks[18], (N_SB_LAYERS, hd, D_MODEL), hd ** -0.5),
        'ffn_w1': nrm(ks[19], (DEPTH, D_MODEL, D_FF), D_MODEL ** -0.5),
        'ffn_w3': nrm(ks[20], (DEPTH, D_MODEL, D_FF), D_MODEL ** -0.5),
        'ffn_w2': nrm(ks[21], (DEPTH, D_FF, D_MODEL), D_FF ** -0.5),
    }


def reference(x_prompt, x_sample, cache_nsa, state_nsa_win, cache_sb, page_table, rel_bias,
              norm_w, final_norm_w, nsa_w_in, nsa_w_out, nsa_pe_k, nsa_pe_v, nsa_phi_k1,
              nsa_phi_k2, nsa_phi_v1, nsa_phi_v2, sb_w_in, sb_w_out, ffn_w1, ffn_w3, ffn_w2):
    hp, hs = x_prompt, x_sample
    nsa_kv_p, nsa_kv_s, nsa_win_p, nsa_win_s, sb_kv_p, sb_kv_s = [], [], [], [], [], []
    for i in range(DEPTH):
        li = i // N_MIXERS
        xp = rmsnorm(hp, norm_w[i, 0])
        xs = rmsnorm(hs, norm_w[i, 0])
        if i % N_MIXERS == 0:
            w = (rel_bias, nsa_w_in[li], nsa_w_out[li], nsa_pe_k[li], nsa_pe_v[li],
                 nsa_phi_k1[li], nsa_phi_k2[li], nsa_phi_v1[li], nsa_phi_v2[li])
            mp, kvp, wp = nsa_prompt(xp, *w)
            ms, kvs, ws = nsa_sample(xs, cache_nsa, state_nsa_win[li], page_table, li, *w)
            nsa_kv_p.append(kvp)
            nsa_kv_s.append(kvs)
            nsa_win_p.append(wp)
            nsa_win_s.append(ws)
        else:
            mp, kvp = sb_prompt(xp, sb_w_in[li], sb_w_out[li])
            ms, kvs = sb_sample(xs, cache_sb, page_table, li, sb_w_in[li], sb_w_out[li])
            sb_kv_p.append(kvp)
            sb_kv_s.append(kvs)
        hp = hp + mp
        hs = hs + ms
        hp = hp + swiglu(rmsnorm(hp, norm_w[i, 1]), ffn_w1[i], ffn_w3[i], ffn_w2[i])
        hs = hs + swiglu(rmsnorm(hs, norm_w[i, 1]), ffn_w1[i], ffn_w3[i], ffn_w2[i])
    y_prompt = rmsnorm(hp, final_norm_w)
    y_sample = rmsnorm(hs, final_norm_w)
    nsa_kv_prompt = jnp.stack(nsa_kv_p)
    nsa_kv_sample = jnp.stack(nsa_kv_s)
    nsa_win_prompt = jnp.stack(nsa_win_p)
    nsa_win_sample = jnp.stack(nsa_win_s)
    sb_kv_prompt = jnp.stack(sb_kv_p)
    sb_kv_sample = jnp.stack(sb_kv_s)
    return (y_prompt, y_sample, nsa_kv_prompt, nsa_kv_sample, nsa_win_prompt, nsa_win_sample, sb_kv_prompt, sb_kv_sample)
```

```python
import functools
import math

import numpy as np
import jax
import jax.numpy as jnp
from jax import lax
from jax.experimental import pallas as pl
from jax.experimental.pallas import tpu as pltpu

F32 = jnp.float32
BF16 = jnp.bfloat16
I32 = jnp.int32

HEAD_DIM = 128
N_HEADS = 16
NSA_KV_HEADS = 4
NSA_GROUP = N_HEADS // NSA_KV_HEADS
CMP_BLOCK = 32
CMP_STRIDE = 16
SEL_BLOCK = 64
SEL_TOPK = 16
WINDOW = 512
FORCE_BONUS = 1000.0
REL_BUCKETS = 32
REL_MAX_DIST = 128
RMS_EPS = 1e-6
NEG_INF = -1e30
SCALE = HEAD_DIM ** -0.5
PAGE_SIZE = 128

LANES = 128
SUBLANES = 8
TQ = 128
T8 = SUBLANES
VMEM_LIMIT = 56 * 1024 * 1024
KV = NSA_KV_HEADS
G = NSA_GROUP


def _t5_thresholds():
    d = np.arange(0, 4 * REL_MAX_DIST)
    max_exact = REL_BUCKETS // 2
    nf = np.maximum(d, 1).astype(np.float32)
    large = max_exact + (np.log(nf / np.float32(max_exact)) / np.float32(math.log(REL_MAX_DIST / max_exact))
                         * np.float32(REL_BUCKETS - max_exact)).astype(np.int32)
    b = np.where(d < max_exact, d, np.minimum(large, REL_BUCKETS - 1))
    return [int(np.argmax(b >= k)) for k in range(REL_BUCKETS)]


T5_THR = _t5_thresholds()
FAR_DIST = T5_THR[-1]


def _dot(a, b):
    return jnp.dot(a, b, preferred_element_type=F32)


def _dot_nt(a, b):
    return lax.dot_general(a, b, (((1,), (1,)), ((), ())), preferred_element_type=F32)


def _params(*sem):
    return pltpu.CompilerParams(dimension_semantics=sem, vmem_limit_bytes=VMEM_LIMIT)


def _rms(x, w):
    return x * lax.rsqrt(jnp.mean(x * x, axis=-1, keepdims=True) + RMS_EPS) * w


def _masked_softmax(s, mask):
    sm = jnp.where(mask, s, NEG_INF)
    m = jnp.max(sm, axis=1, keepdims=True)
    e = jnp.where(mask, jnp.exp(sm - m), 0.0)
    l = jnp.sum(e, axis=1, keepdims=True)
    return e * jnp.where(l > 0.0, 1.0 / l, 0.0)


def _safe_inv(l):
    return jnp.where(l > 0.0, 1.0 / l, 0.0)


def _rmsnorm_body(x_ref, w_ref, o_ref):
    o_ref[...] = _rms(x_ref[...], w_ref[...]).astype(o_ref.dtype)


def rmsnorm_rows(x, w, out_dtype):
    m, d = x.shape
    tm = min(m, 512)
    return pl.pallas_call(
        _rmsnorm_body, grid=(m // tm,),
        in_specs=[pl.BlockSpec((tm, d), lambda i: (i, 0)), pl.BlockSpec((1, d), lambda i: (0, 0))],
        out_specs=pl.BlockSpec((tm, d), lambda i: (i, 0)),
        out_shape=jax.ShapeDtypeStruct((m, d), out_dtype),
        compiler_params=_params("parallel"), name="rmsnorm")(x, w.reshape(1, d))


def _matmul_body(a_ref, w_ref, o_ref, *, sigmoid):
    y = _dot(a_ref[...], w_ref[...])
    if sigmoid:
        y = jax.nn.sigmoid(y)
    o_ref[...] = y.astype(o_ref.dtype)


def matmul(a, w, out_dtype, sigmoid=False):
    m, k = a.shape
    n = w.shape[1]
    tm = min(m, 512)
    tn = min(n, 2048)
    return pl.pallas_call(
        functools.partial(_matmul_body, sigmoid=sigmoid), grid=(m // tm, n // tn),
        in_specs=[pl.BlockSpec((tm, k), lambda i, j: (i, 0)), pl.BlockSpec((k, tn), lambda i, j: (0, j))],
        out_specs=pl.BlockSpec((tm, tn), lambda i, j: (i, j)),
        out_shape=jax.ShapeDtypeStruct((m, n), out_dtype),
        compiler_params=_params("parallel", "arbitrary"), name="matmul")(a, w)


def _mm_res_norm_body(a_ref, w_ref, h_ref, nw_ref, hout_ref, xn_ref):
    h = h_ref[...] + _dot(a_ref[...], w_ref[...])
    hout_ref[...] = h
    xn_ref[...] = _rms(h, nw_ref[...]).astype(xn_ref.dtype)


def matmul_res_norm(a, w, h, nw):
    m, k = a.shape
    d = w.shape[1]
    tm = min(m, 512)
    return pl.pallas_call(
        _mm_res_norm_body, grid=(m // tm,),
        in_specs=[pl.BlockSpec((tm, k), lambda i: (i, 0)), pl.BlockSpec((k, d), lambda i: (0, 0)),
                  pl.BlockSpec((tm, d), lambda i: (i, 0)), pl.BlockSpec((1, d), lambda i: (0, 0))],
        out_specs=[pl.BlockSpec((tm, d), lambda i: (i, 0)), pl.BlockSpec((tm, d), lambda i: (i, 0))],
        out_shape=[jax.ShapeDtypeStruct((m, d), F32), jax.ShapeDtypeStruct((m, d), BF16)],
        compiler_params=_params("parallel"), name="matmul_res_norm")(a, w, h, nw.reshape(1, d))


def _ffn_body(xn_ref, w1_ref, w3_ref, w2_ref, h_ref, nw_ref, hout_ref, yn_ref, acc_ref):
    f = pl.program_id(1)

    @pl.when(f == 0)
    def _():
        acc_ref[...] = jnp.zeros_like(acc_ref)

    x = xn_ref[...]
    g = _dot(x, w1_ref[...])
    u = _dot(x, w3_ref[...])
    mid = (g * jax.nn.sigmoid(g) * u).astype(BF16)
    acc_ref[...] += _dot(mid, w2_ref[...])

    @pl.when(f == pl.num_programs(1) - 1)
    def _():
        h = h_ref[...] + acc_ref[...]
        hout_ref[...] = h
        yn_ref[...] = _rms(h, nw_ref[...]).astype(yn_ref.dtype)


def ffn_res_norm(xn, w1, w3, w2, h, nw, norm_dtype):
    m, d = xn.shape
    dff = w1.shape[1]
    tm = min(m, 512)
    tf = 512
    return pl.pallas_call(
        _ffn_body, grid=(m // tm, dff // tf),
        in_specs=[pl.BlockSpec((tm, d), lambda i, f: (i, 0)),
                  pl.BlockSpec((d, tf), lambda i, f: (0, f)), pl.BlockSpec((d, tf), lambda i, f: (0, f)),
                  pl.BlockSpec((tf, d), lambda i, f: (f, 0)),
                  pl.BlockSpec((tm, d), lambda i, f: (i, 0)), pl.BlockSpec((1, d), lambda i, f: (0, 0))],
        out_specs=[pl.BlockSpec((tm, d), lambda i, f: (i, 0)), pl.BlockSpec((tm, d), lambda i, f: (i, 0))],
        out_shape=[jax.ShapeDtypeStruct((m, d), F32), jax.ShapeDtypeStruct((m, d), norm_dtype)],
        scratch_shapes=[pltpu.VMEM((tm, d), F32)],
        compiler_params=_params("parallel", "arbitrary"), name="ffn")(xn, w1, w3, w2, h, nw.reshape(1, d))


def _bias_body(tab_ref, d_ref, o_ref):
    h = pl.program_id(0)
    d = d_ref[...]
    acc = jnp.full(d.shape, tab_ref[0, h], F32)
    for b in range(1, REL_BUCKETS):
        acc = jnp.where(d >= T5_THR[b], tab_ref[b, h], acc)
    o_ref[...] = acc


def bias_from_dist(rel_bias, dist):
    r = dist.shape[0]
    rt = LANES
    return pl.pallas_call(
        _bias_body, grid=(N_HEADS, r // rt),
        in_specs=[pl.BlockSpec(memory_space=pltpu.SMEM), pl.BlockSpec((rt, LANES), lambda h, i: (i, 0))],
        out_specs=pl.BlockSpec((None, rt, LANES), lambda h, i: (h, i, 0)),
        out_shape=jax.ShapeDtypeStruct((N_HEADS, r, LANES), F32),
        compiler_params=_params("parallel", "parallel"), name="rel_bias")(rel_bias, dist)


CMP_PAGES = 16


def _cmp1_body(pt_ref, *refs):
    pages = refs[:CMP_PAGES]
    perm = refs[CMP_PAGES][...]
    w_refs = refs[CMP_PAGES + 1:CMP_PAGES + 3]
    o_ref = refs[CMP_PAGES + 3]
    cpp = PAGE_SIZE // CMP_STRIDE
    for kind in range(2):
        w = w_refs[kind][...]
        for kv in range(KV):
            c0 = (kind * KV + kv) * HEAD_DIM
            xs = [_dot(perm, pg[:, c0:c0 + HEAD_DIM].astype(BF16)) for pg in pages]
            pieces = [jnp.concatenate([x[r * cpp:(r + 1) * cpp] for x in xs], axis=0).astype(BF16)
                      for r in range(CMP_STRIDE)]
            o_ref[kind * KV + kv] = _dot(jnp.concatenate(pieces, axis=1), w)


def compress_stage1(rows3, page_table, w1k, w1v):
    nb_seq, n_pages = page_table.shape
    steps = n_pages // CMP_PAGES
    cpp = PAGE_SIZE // CMP_STRIDE
    width = 2 * KV * HEAD_DIM

    def page_spec(i):
        return pl.BlockSpec((None, PAGE_SIZE, width), lambda b, s, pt: (pt[b, s * CMP_PAGES + i], 0, 0))

    wspec = pl.BlockSpec((CMP_STRIDE * HEAD_DIM, 2 * HEAD_DIM), lambda b, s, pt: (0, 0))
    src = np.arange(PAGE_SIZE)
    perm = np.zeros((PAGE_SIZE, PAGE_SIZE), np.float32)
    perm[(src % CMP_STRIDE) * cpp + src // CMP_STRIDE, src] = 1.0
    perm = jnp.asarray(perm, BF16)
    grid_spec = pltpu.PrefetchScalarGridSpec(
        num_scalar_prefetch=1, grid=(nb_seq, steps),
        in_specs=[page_spec(i) for i in range(CMP_PAGES)]
        + [pl.BlockSpec((PAGE_SIZE, PAGE_SIZE), lambda b, s, pt: (0, 0)), wspec, wspec],
        out_specs=pl.BlockSpec((None, 2 * KV, CMP_PAGES * cpp, 2 * HEAD_DIM), lambda b, s, pt: (b, 0, s, 0)))
    return pl.pallas_call(
        _cmp1_body, grid_spec=grid_spec,
        out_shape=jax.ShapeDtypeStruct((nb_seq, 2 * KV, n_pages * cpp, 2 * HEAD_DIM), F32),
        compiler_params=_params("parallel", "arbitrary"), name="compress1")(
            page_table, *([rows3] * CMP_PAGES), perm, w1k, w1v)


def _cmp2_body(ab_ref, pe_ref, w1_ref, w2_ref, o_ref):
    nch = ab_ref.shape[0]
    ab = ab_ref[...]
    c = _dot(pe_ref[...].astype(BF16), w1_ref[...].astype(BF16))
    pre = ab[:, :HEAD_DIM] + pltpu.roll(ab[:, HEAD_DIM:], nch - 1, 0) + c[0:1]
    y = _dot((pre * jax.nn.sigmoid(pre)).astype(BF16), w2_ref[...].astype(BF16))
    rowi = lax.broadcasted_iota(I32, y.shape, 0)
    o_ref[...] = jnp.where(rowi < nch - 1, y, 0.0)


def compress_stage2(ab, pe, w1, w2):
    nb_seq, _, nch, _ = ab.shape
    return pl.pallas_call(
        _cmp2_body, grid=(nb_seq, 2 * KV),
        in_specs=[pl.BlockSpec((None, None, nch, 2 * HEAD_DIM), lambda b, j: (b, j, 0, 0)),
                  pl.BlockSpec((None, SUBLANES, CMP_BLOCK * HEAD_DIM), lambda b, j: (j // KV, 0, 0)),
                  pl.BlockSpec((None, CMP_BLOCK * HEAD_DIM, HEAD_DIM), lambda b, j: (j // KV, 0, 0)),
                  pl.BlockSpec((None, HEAD_DIM, HEAD_DIM), lambda b, j: (j // KV, 0, 0))],
        out_specs=pl.BlockSpec((None, None, nch, HEAD_DIM), lambda b, j: (b, j, 0, 0)),
        out_shape=jax.ShapeDtypeStruct((nb_seq, 2 * KV, nch, HEAD_DIM), F32),
        compiler_params=_params("parallel", "parallel"), name="compress2")(ab, pe, w1, w2)


def compress(rows3, page_table, cw):
    ab = compress_stage1(rows3, page_table, cw["w1k"], cw["w1v"])
    return compress_stage2(ab, cw["pe"], cw["w1"], cw["w2"])


def _flash_update(s, mask, v, m_ref, l_ref, acc_ref, rows):
    m_prev = m_ref[rows, :]
    m_new = jnp.maximum(m_prev, jnp.max(jnp.where(mask, s, NEG_INF), axis=1, keepdims=True))
    alpha = jnp.exp(m_prev - m_new)
    p = jnp.where(mask, jnp.exp(s - m_new), 0.0)
    l_ref[rows, :] = alpha * l_ref[rows, :] + jnp.sum(p, axis=1, keepdims=True)
    acc_ref[rows, :] = alpha * acc_ref[rows, :] + _dot(p.astype(BF16), v)
    m_ref[rows, :] = m_new


def _flash_init(m_ref, l_ref, acc_ref):
    m_ref[...] = jnp.full(m_ref.shape, NEG_INF, F32)
    l_ref[...] = jnp.zeros_like(l_ref)
    acc_ref[...] = jnp.zeros_like(acc_ref)


def _rank_select(score, eligible, blk, n_blocks, n_top, row_of):
    def body(j, cnt):
        rowj = row_of(j)
        beats = (rowj > score) | ((rowj == score) & (blk > j))
        return cnt + beats.astype(I32)
    cnt = lax.fori_loop(0, n_blocks, body, jnp.zeros(score.shape, I32))
    return jnp.where((cnt < n_top) & eligible, 1.0, 0.0)


def _nsa_prompt_body(q_ref, ks_ref, vs_ref, kw_ref, vw_ref, kc_ref, vc_ref, g_ref, bc_ref, bt_ref, ovl_ref, exp_ref,
                     o_ref, m_s, l_s, a_s, m_w, l_w, a_w, mk_ref, sc_ref, *, n_cmp, n_blk):
    qi = pl.program_id(2)
    rows = G * TQ
    q = jnp.concatenate([q_ref[:, g * HEAD_DIM:(g + 1) * HEAD_DIM] for g in range(G)], axis=0)
    row5 = lax.broadcasted_iota(I32, (rows, LANES), 0)
    lane5 = lax.broadcasted_iota(I32, (rows, LANES), 1)
    tpos5 = qi * TQ + (row5 & (TQ - 1))

    s = _dot_nt(q, kc_ref[...].astype(BF16)) * SCALE
    s = s + jnp.concatenate([bc_ref[g] for g in range(G)], axis=0)
    mask_c = (tpos5 >= lane5 * CMP_STRIDE + (CMP_BLOCK - 1)) & (lane5 < n_cmp)
    p_c = _masked_softmax(s, mask_c)
    pb = p_c.astype(BF16)
    o_c = _dot(pb, vc_ref[...].astype(BF16))

    imp = _dot_nt(ovl_ref[...], pb[0:TQ])
    for g in range(1, G):
        imp = imp + _dot_nt(ovl_ref[...], pb[g * TQ:(g + 1) * TQ])
    blk = lax.broadcasted_iota(I32, (LANES, TQ), 0)
    tq_l = qi * TQ + lax.broadcasted_iota(I32, (LANES, TQ), 1)
    cur = lax.shift_right_arithmetic(tq_l, int(math.log2(SEL_BLOCK)))
    eligible = blk * SEL_BLOCK <= tq_l
    forced = (blk == 0) | (blk == cur) | (blk == cur - 1)
    score = jnp.where(eligible, imp + jnp.where(forced, FORCE_BONUS, 0.0), -jnp.inf)
    sc_ref[...] = score
    sel_t = _rank_select(score, eligible, blk, n_blk, min(SEL_TOPK, n_blk), lambda j: sc_ref[pl.ds(j, 1), :])
    mk_ref[...] = _dot(sel_t.T.astype(BF16), exp_ref[...])

    def bias_tile(kt):
        off = pl.multiple_of(jnp.minimum(qi - kt, 2) * TQ, TQ)
        return jnp.concatenate([bt_ref[g, pl.ds(off, TQ), :] for g in range(G)], axis=0)

    _flash_init(m_s, l_s, a_s)
    _flash_init(m_w, l_w, a_w)
    all_rows = pl.ds(0, rows)

    def sel_step(kt, carry):
        off = pl.multiple_of(kt * TQ, TQ)
        k = ks_ref[pl.ds(off, TQ), :].astype(BF16)
        v = vs_ref[pl.ds(off, TQ), :].astype(BF16)
        s = _dot_nt(q, k) * SCALE + bias_tile(kt)
        mk = mk_ref[:, pl.ds(off, TQ)]
        mask = (jnp.concatenate([mk] * G, axis=0) > 0.5) & (off + lane5 <= tpos5)
        _flash_update(s, mask, v, m_s, l_s, a_s, all_rows)
        return carry

    lax.fori_loop(0, qi + 1, sel_step, 0)

    def win_step(kt, carry):
        off = pl.multiple_of(kt * TQ, TQ)
        k = kw_ref[pl.ds(off, TQ), :].astype(BF16)
        v = vw_ref[pl.ds(off, TQ), :].astype(BF16)
        s = _dot_nt(q, k) * SCALE + bias_tile(kt)
        dist = tpos5 - (off + lane5)
        mask = (dist >= 0) & (dist < WINDOW)
        _flash_update(s, mask, v, m_w, l_w, a_w, all_rows)
        return carry

    lax.fori_loop(jnp.maximum(qi - WINDOW // TQ, 0), qi + 1, win_step, 0)

    o_s = a_s[...] * _safe_inv(l_s[...])
    o_w = a_w[...] * _safe_inv(l_w[...])
    gt = g_ref[...]
    for g in range(G):
        r = slice(g * TQ, (g + 1) * TQ)
        out = (gt[:, 3 * g:3 * g + 1] * o_c[r] + gt[:, 3 * g + 1:3 * g + 2] * o_s[r]
               + gt[:, 3 * g + 2:3 * g + 3] * o_w[r])
        o_ref[:, g * HEAD_DIM:(g + 1) * HEAD_DIM] = out.astype(o_ref.dtype)


def nsa_prompt_attention(q, kv4, kwin, gates, kcb, bias_all, b, t):
    nq = t // TQ
    n_chunks = t // CMP_STRIDE
    n_cmp = n_chunks - CMP_BLOCK // CMP_STRIDE + 1
    n_blk = -(-t // SEL_BLOCK)
    assert n_chunks == LANES and n_blk <= LANES and t % TQ == 0 and TQ >= FAR_DIST
    cmp_start = np.arange(LANES) * CMP_STRIDE
    jst = np.arange(LANES) * SEL_BLOCK
    overlap = ((cmp_start[:, None] < jst[None, :] + SEL_BLOCK) & (cmp_start[:, None] + CMP_BLOCK > jst[None, :])
               & (np.arange(LANES)[:, None] < n_cmp) & (np.arange(LANES)[None, :] < n_blk))
    ovl_t = jnp.asarray(overlap.T, BF16)
    expand = jnp.asarray(np.arange(LANES)[:, None] == (np.arange(t)[None, :] // SEL_BLOCK), BF16)
    kvw = KV * HEAD_DIM
    colblk = lambda base: (lambda bb, kv, qi: (bb, base + kv))
    rowblk = lambda bb, kv, qi: (bb * nq + qi, kv)
    toe = 3 * TQ // LANES
    in_specs = [
        pl.BlockSpec((TQ, G * HEAD_DIM), rowblk),
        pl.BlockSpec((t, HEAD_DIM), colblk(2 * KV)), pl.BlockSpec((t, HEAD_DIM), colblk(3 * KV)),
        pl.BlockSpec((t, HEAD_DIM), colblk(0)), pl.BlockSpec((t, HEAD_DIM), colblk(KV)),
        pl.BlockSpec((None, None, n_chunks, HEAD_DIM), lambda bb, kv, qi: (bb, kv, 0, 0)),
        pl.BlockSpec((None, None, n_chunks, HEAD_DIM), lambda bb, kv, qi: (bb, KV + kv, 0, 0)),
        pl.BlockSpec((TQ, LANES), rowblk),
        pl.BlockSpec((G, TQ, LANES), lambda bb, kv, qi: (kv, toe + qi, 0)),
        pl.BlockSpec((G, 3 * TQ, LANES), lambda bb, kv, qi: (kv, 0, 0)),
        pl.BlockSpec((LANES, LANES), lambda bb, kv, qi: (0, 0)),
        pl.BlockSpec((LANES, t), lambda bb, kv, qi: (0, 0)),
    ]
    rows = G * TQ
    scratch = [pltpu.VMEM((rows, LANES), F32) for _ in range(6)] + [pltpu.VMEM((TQ, t), F32),
                                                                   pltpu.VMEM((LANES, TQ), F32)]
    del kvw
    return pl.pallas_call(
        functools.partial(_nsa_prompt_body, n_cmp=n_cmp, n_blk=n_blk), grid=(b, KV, nq),
        in_specs=in_specs, out_specs=pl.BlockSpec((TQ, G * HEAD_DIM), rowblk),
        out_shape=jax.ShapeDtypeStruct((b * t, N_HEADS * HEAD_DIM), BF16),
        scratch_shapes=scratch,
        compiler_params=_params("parallel", "parallel", "arbitrary"), name="nsa_prompt_attn")(
            q, kv4, kv4, kwin, kwin, kcb, kcb, gates, bias_all, bias_all, ovl_t, expand)


def _stick_tile(z, mask, u, carry, along_lanes=True):
    sp = jnp.log1p(jnp.exp(-jnp.abs(z)))
    log_beta = jnp.minimum(z, 0.0) - sp
    log_rest = -jnp.maximum(z, 0.0) - sp
    if mask is not None:
        log_rest = jnp.where(mask, log_rest, 0.0)
    hi = log_rest.astype(BF16)
    r1 = log_rest - hi.astype(F32)
    mid = r1.astype(BF16)
    lo = (r1 - mid.astype(F32)).astype(BF16)
    n = z.shape[0]
    l3 = _dot(jnp.concatenate([hi, mid, lo], axis=0), u)
    later = l3[0:n] + l3[n:2 * n] + l3[2 * n:3 * n] + carry
    a = jnp.exp(log_beta + later)
    if mask is not None:
        a = jnp.where(mask, a, 0.0)
    return a, jnp.sum(log_rest, axis=1, keepdims=True)


def _sb_prompt_body(q_ref, k_ref, v_ref, u_ref, o_ref, carry_ref, acc_ref):
    qi = pl.program_id(2)
    q = q_ref[...]
    row = lax.broadcasted_iota(I32, (TQ, TQ), 0)
    lane = lax.broadcasted_iota(I32, (TQ, TQ), 1)
    carry_ref[...] = jnp.zeros_like(carry_ref)
    acc_ref[...] = jnp.zeros_like(acc_ref)

    def step(i, c):
        kt = qi - i
        off = pl.multiple_of(kt * TQ, TQ)
        k = k_ref[pl.ds(off, TQ), :].astype(BF16)
        v = v_ref[pl.ds(off, TQ), :].astype(BF16)
        z = _dot_nt(q, k) * SCALE
        mask = (off + lane) < (qi * TQ + row)
        a, tot = _stick_tile(z, mask, u_ref[...], carry_ref[...])
        acc_ref[...] += _dot(a.astype(BF16), v)
        carry_ref[...] += tot
        return c

    lax.fori_loop(0, qi + 1, step, 0)
    o_ref[...] = acc_ref[...].astype(o_ref.dtype)


def _suffix_matrix():
    j = np.arange(TQ)
    return jnp.asarray(j[:, None] > j[None, :], BF16)


def sb_prompt_attention(q, kvp, b, t):
    nq = t // TQ
    rowblk = lambda bb, h, qi: (bb * nq + qi, h)
    return pl.pallas_call(
        _sb_prompt_body, grid=(b, N_HEADS, nq),
        in_specs=[pl.BlockSpec((TQ, HEAD_DIM), rowblk),
                  pl.BlockSpec((t, HEAD_DIM), lambda bb, h, qi: (bb, h)),
                  pl.BlockSpec((t, HEAD_DIM), lambda bb, h, qi: (bb, N_HEADS + h)),
                  pl.BlockSpec((TQ, TQ), lambda bb, h, qi: (0, 0))],
        out_specs=pl.BlockSpec((TQ, HEAD_DIM), rowblk),
        out_shape=jax.ShapeDtypeStruct((b * t, N_HEADS * HEAD_DIM), BF16),
        scratch_shapes=[pltpu.VMEM((TQ, TQ), F32), pltpu.VMEM((TQ, HEAD_DIM), F32)],
        compiler_params=_params("parallel", "parallel", "arbitrary"), name="sb_prompt_attn")(
            q, kvp, kvp, _suffix_matrix())


def _nsa_sample_select_body(q_ref, kcb_ref, bc_ref, mc_ref, ovl_ref, fadd_ref, elig_ref, oc_ref, sel_ref, sc_ref,
                            *, n_blk):
    nbp = kcb_ref.shape[1]
    grp = G * T8
    ps = []
    for kv in range(KV):
        qk = jnp.concatenate([q_ref[:, (kv * G + g) * HEAD_DIM:(kv * G + g + 1) * HEAD_DIM] for g in range(G)],
                             axis=0)
        s = _dot_nt(qk, kcb_ref[kv].astype(BF16)) * SCALE + bc_ref[kv * grp:(kv + 1) * grp, :]
        p = _masked_softmax(s, mc_ref[...] > 0.5)
        oc_ref[kv * grp:(kv + 1) * grp, :] = _dot(p.astype(BF16), kcb_ref[KV + kv].astype(BF16))
        ps.append(p)
    pad = jnp.zeros((LANES - KV * T8, nbp), F32)
    imp = None
    for g in range(G):
        xg = jnp.concatenate([ps[kv][g * T8:(g + 1) * T8] for kv in range(KV)] + [pad], axis=0)
        part = _dot_nt(ovl_ref[...], xg.astype(BF16))
        imp = part if imp is None else imp + part
    eligible = elig_ref[...] > 0.5
    score = jnp.where(eligible, imp + fadd_ref[...], -jnp.inf)
    sc_ref[...] = score
    blk = lax.broadcasted_iota(I32, score.shape, 0)
    sel_t = _rank_select(score, eligible, blk, n_blk, min(SEL_TOPK, n_blk), lambda j: sc_ref[pl.ds(j, 1), :])
    sel_ref[...] = sel_t.T[0:KV * T8, :]


def _nsa_sample_attn_body(pt_ref, q_ref, pg_ref, sel_ref, new_ref, win_ref, wnew_ref, g_ref, oc_ref,
                          bl_ref, bf_ref, bn_ref, bw_ref, o_ref, m_s, l_s, a_s, *, n_pages, t_new, past):
    p = pl.program_id(1)
    grp = G * T8
    kvw = KV * HEAD_DIM
    wb = win_ref.shape[0]
    is_last = p == n_pages - 1

    @pl.when(p == 0)
    def _():
        _flash_init(m_s, l_s, a_s)

    def q_of(kv):
        return jnp.concatenate([q_ref[:, (kv * G + g) * HEAD_DIM:(kv * G + g + 1) * HEAD_DIM] for g in range(G)],
                               axis=0)

    def pad_keys(x):
        return jnp.concatenate([x, jnp.zeros((TQ - x.shape[0], x.shape[1]), x.dtype)], axis=0).astype(BF16)

    nblkp = sel_ref.shape[1]
    jb = lax.broadcasted_iota(I32, (nblkp, PAGE_SIZE), 0)
    sl = lax.broadcasted_iota(I32, (nblkp, PAGE_SIZE), 1)
    per_page = PAGE_SIZE // SEL_BLOCK
    expand = jnp.where(jb == p * per_page + lax.shift_right_arithmetic(sl, int(math.log2(SEL_BLOCK))), 1.0, 0.0)
    mk = _dot(sel_ref[...].astype(BF16), expand.astype(BF16))

    for kv in range(KV):
        rows = pl.ds(kv * grp, grp)
        k = pg_ref[:, kv * HEAD_DIM:(kv + 1) * HEAD_DIM].astype(BF16)
        v = pg_ref[:, kvw + kv * HEAD_DIM:kvw + (kv + 1) * HEAD_DIM].astype(BF16)
        bias = jnp.where(is_last, bl_ref[rows, :], bf_ref[rows, :])
        s = _dot_nt(q_of(kv), k) * SCALE + bias
        mask = jnp.concatenate([mk[kv * T8:(kv + 1) * T8]] * G, axis=0) > 0.5
        _flash_update(s, mask, v, m_s, l_s, a_s, rows)

    @pl.when(is_last)
    def _():
        lane = lax.broadcasted_iota(I32, (grp, LANES), 1)
        t_row = lax.broadcasted_iota(I32, (grp, LANES), 0) & (T8 - 1)
        lane_w = lax.broadcasted_iota(I32, (grp, wb), 1)
        t_row_w = lax.broadcasted_iota(I32, (grp, wb), 0) & (T8 - 1)
        new_blk = past // SEL_BLOCK
        gt = g_ref[...]
        for kv in range(KV):
            rows = pl.ds(kv * grp, grp)
            qk = q_of(kv)
            kn = pad_keys(new_ref[:, 2 * kvw + kv * HEAD_DIM:2 * kvw + (kv + 1) * HEAD_DIM])
            vn = pad_keys(new_ref[:, 3 * kvw + kv * HEAD_DIM:3 * kvw + (kv + 1) * HEAD_DIM])
            s = _dot_nt(qk, kn) * SCALE + bn_ref[rows, :]
            picked = jnp.concatenate([sel_ref[kv * T8:(kv + 1) * T8, new_blk:new_blk + 1]] * G, axis=0) > 0.5
            mask = (lane <= t_row) & (lane < t_new) & picked
            _flash_update(s, mask, vn, m_s, l_s, a_s, rows)
            o_s = a_s[rows, :] * _safe_inv(l_s[rows, :])
            kw = win_ref[:, kv * HEAD_DIM:(kv + 1) * HEAD_DIM].astype(BF16)
            vw = win_ref[:, kvw + kv * HEAD_DIM:kvw + (kv + 1) * HEAD_DIM].astype(BF16)
            kwn = pad_keys(wnew_ref[:, kv * HEAD_DIM:(kv + 1) * HEAD_DIM])
            vwn = pad_keys(wnew_ref[:, kvw + kv * HEAD_DIM:kvw + (kv + 1) * HEAD_DIM])
            s1 = _dot_nt(qk, kw) * SCALE + bw_ref[rows, 0:wb]
            s2 = _dot_nt(qk, kwn) * SCALE + bw_ref[rows, wb:wb + LANES]
            d1 = wb + t_row_w - lane_w
            mask1 = (d1 >= 0) & (d1 < WINDOW)
            d2 = t_row - lane
            mask2 = (d2 >= 0) & (d2 < WINDOW) & (lane < t_new)
            m = jnp.maximum(jnp.max(jnp.where(mask1, s1, NEG_INF), axis=1, keepdims=True),
                            jnp.max(jnp.where(mask2, s2, NEG_INF), axis=1, keepdims=True))
            e1 = jnp.where(mask1, jnp.exp(s1 - m), 0.0)
            e2 = jnp.where(mask2, jnp.exp(s2 - m), 0.0)
            l = jnp.sum(e1, axis=1, keepdims=True) + jnp.sum(e2, axis=1, keepdims=True)
            o_w = (_dot(e1.astype(BF16), vw) + _dot(e2.astype(BF16), vwn)) * _safe_inv(l)
            o_c = oc_ref[rows, :]
            for g in range(G):
                r = slice(g * T8, (g + 1) * T8)
                c = kv * LANES + 3 * g
                out = gt[:, c:c + 1] * o_c[r] + gt[:, c + 1:c + 2] * o_s[r] + gt[:, c + 2:c + 3] * o_w[r]
                o_ref[:, (kv * G + g) * HEAD_DIM:(kv * G + g + 1) * HEAD_DIM] = out.astype(o_ref.dtype)


def nsa_sample_attention(q8, cache3, page_table, kcb, new8, win3, wnew8, gates8, bias_s, past, t_new):
    bd, n_pages = page_table.shape
    n_chunks = kcb.shape[2]
    n_cmp = n_chunks - CMP_BLOCK // CMP_STRIDE + 1
    total = past + t_new
    n_blk = -(-total // SEL_BLOCK)
    nblkp = -(-n_blk // LANES) * LANES
    wb = win3.shape[1]
    assert past % PAGE_SIZE == 0 and PAGE_SIZE % SEL_BLOCK == 0 and t_new <= T8 and PAGE_SIZE > FAR_DIST
    assert (total // CMP_STRIDE) == n_chunks and wb % LANES == 0

    pos_q = past + np.arange(T8)
    cmp_start = np.arange(n_chunks) * CMP_STRIDE
    cmp_end = cmp_start + CMP_BLOCK - 1
    mask_c = ((pos_q[:, None] >= cmp_end[None, :]) & (np.arange(n_chunks)[None, :] < n_cmp))
    mask_c = jnp.asarray(np.tile(mask_c, (G, 1)), F32)
    jst = np.arange(nblkp) * SEL_BLOCK
    overlap = ((cmp_start[:, None] < jst[None, :] + SEL_BLOCK) & (cmp_start[:, None] + CMP_BLOCK > jst[None, :])
               & (np.arange(n_chunks)[:, None] < n_cmp) & (np.arange(nblkp)[None, :] < n_blk))
    ovl_t = jnp.asarray(overlap.T, BF16)
    blk = np.arange(nblkp)[:, None]
    pq = np.tile(pos_q, KV)[None, :]
    tvalid = np.tile(np.arange(T8) < t_new, KV)[None, :]
    cur = pq // SEL_BLOCK
    elig = (blk * SEL_BLOCK <= pq) & (blk < n_blk) & tvalid
    forced = (blk == 0) | (blk == cur) | (blk == cur - 1)
    pad_l = LANES - KV * T8
    elig = jnp.asarray(np.pad(elig, ((0, 0), (0, pad_l))), F32)
    fadd = jnp.asarray(np.pad(np.where(forced, FORCE_BONUS, 0.0), ((0, 0), (0, pad_l))), F32)

    grp_all = KV * G * T8
    o_c, sel = pl.pallas_call(
        functools.partial(_nsa_sample_select_body, n_blk=n_blk), grid=(bd,),
        in_specs=[pl.BlockSpec((None, T8, N_HEADS * HEAD_DIM), lambda b: (b, 0, 0)),
                  pl.BlockSpec((None, 2 * KV, n_chunks, HEAD_DIM), lambda b: (b, 0, 0, 0)),
                  pl.BlockSpec((grp_all, n_chunks), lambda b: (0, 0)),
                  pl.BlockSpec((G * T8, n_chunks), lambda b: (0, 0)),
                  pl.BlockSpec((nblkp, n_chunks), lambda b: (0, 0)),
                  pl.BlockSpec((nblkp, LANES), lambda b: (0, 0)),
                  pl.BlockSpec((nblkp, LANES), lambda b: (0, 0))],
        out_specs=[pl.BlockSpec((None, grp_all, HEAD_DIM), lambda b: (b, 0, 0)),
                   pl.BlockSpec((None, KV * T8, nblkp), lambda b: (b, 0, 0))],
        out_shape=[jax.ShapeDtypeStruct((bd, grp_all, HEAD_DIM), F32),
                   jax.ShapeDtypeStruct((bd, KV * T8, nblkp), F32)],
        scratch_shapes=[pltpu.VMEM((nblkp, LANES), F32)],
        compiler_params=_params("parallel"), name="nsa_sample_select")(
            q8, kcb, bias_s["cmp"], mask_c, ovl_t, fadd, elig)

    kvw = KV * HEAD_DIM
    const2 = lambda b, p, pt: (0, 0)
    per_b = lambda b, p, pt: (b, 0, 0)
    grid_spec = pltpu.PrefetchScalarGridSpec(
        num_scalar_prefetch=1, grid=(bd, n_pages),
        in_specs=[pl.BlockSpec((None, T8, N_HEADS * HEAD_DIM), per_b),
                  pl.BlockSpec((None, PAGE_SIZE, 2 * kvw), lambda b, p, pt: (pt[b, p], 0, 1)),
                  pl.BlockSpec((None, KV * T8, nblkp), per_b),
                  pl.BlockSpec((None, T8, 4 * kvw), per_b),
                  pl.BlockSpec((None, wb, 2 * kvw), per_b),
                  pl.BlockSpec((None, T8, 2 * kvw), per_b),
                  pl.BlockSpec((None, T8, KV * LANES), per_b),
                  pl.BlockSpec((None, grp_all, HEAD_DIM), per_b),
                  pl.BlockSpec((grp_all, LANES), const2), pl.BlockSpec((grp_all, LANES), const2),
                  pl.BlockSpec((grp_all, LANES), const2), pl.BlockSpec((grp_all, wb + LANES), const2)],
        out_specs=pl.BlockSpec((None, T8, N_HEADS * HEAD_DIM), per_b),
        scratch_shapes=[pltpu.VMEM((grp_all, LANES), F32) for _ in range(3)])
    return pl.pallas_call(
        functools.partial(_nsa_sample_attn_body, n_pages=n_pages, t_new=t_new, past=past), grid_spec=grid_spec,
        out_shape=jax.ShapeDtypeStruct((bd, T8, N_HEADS * HEAD_DIM), BF16),
        compiler_params=_params("parallel", "arbitrary"), name="nsa_sample_attn")(
            page_table, q8, cache3, sel, new8, win3, wnew8, gates8, o_c,
            bias_s["last"], bias_s["far"], bias_s["new"], bias_s["win"])


def _sb_sample_body(pt_ref, qbd_ref, pg_ref, new_ref, u_ref, o_ref, carry_ref, acc_ref, *, n_pages, t_new):
    p = pl.program_id(1)
    hw = N_HEADS * HEAD_DIM
    qbd = qbd_ref[...]

    def accumulate(a, v_of):
        for h in range(N_HEADS):
            r = slice(h * T8, (h + 1) * T8)
            acc_ref[r, :] += _dot(a[r].astype(BF16), v_of(h))

    @pl.when(p == 0)
    def _():
        carry_ref[...] = jnp.zeros_like(carry_ref)
        acc_ref[...] = jnp.zeros_like(acc_ref)
        pad = jnp.zeros((TQ - T8, 2 * hw), F32)
        kvn = jnp.concatenate([new_ref[...], pad], axis=0)
        z = _dot_nt(qbd, kvn[:, :hw].astype(BF16)) * SCALE
        lane = lax.broadcasted_iota(I32, (TQ, TQ), 1)
        t_row = lax.broadcasted_iota(I32, (TQ, TQ), 0) & (T8 - 1)
        mask = (lane < t_row) & (lane < t_new)
        a, tot = _stick_tile(z, mask, u_ref[...], carry_ref[...])
        accumulate(a, lambda h: kvn[:, hw + h * HEAD_DIM:hw + (h + 1) * HEAD_DIM].astype(BF16))
        carry_ref[...] += tot

    z = _dot_nt(qbd, pg_ref[:, :hw].astype(BF16)) * SCALE
    a, tot = _stick_tile(z, None, u_ref[...], carry_ref[...])
    accumulate(a, lambda h: pg_ref[:, hw + h * HEAD_DIM:hw + (h + 1) * HEAD_DIM].astype(BF16))
    carry_ref[...] += tot

    @pl.when(p == n_pages - 1)
    def _():
        o_ref[...] = acc_ref[...].astype(o_ref.dtype)


def sb_sample_attention(qbd, cache3, page_table, new8, t_new):
    bd, n_pages = page_table.shape
    hw = N_HEADS * HEAD_DIM
    per_b = lambda b, p, pt: (b, 0, 0)
    grid_spec = pltpu.PrefetchScalarGridSpec(
        num_scalar_prefetch=1, grid=(bd, n_pages),
        in_specs=[pl.BlockSpec((None, N_HEADS * T8, hw), per_b),
                  pl.BlockSpec((None, PAGE_SIZE, 2 * hw), lambda b, p, pt: (pt[b, n_pages - 1 - p], 0, 0)),
                  pl.BlockSpec((None, T8, 2 * hw), per_b),
                  pl.BlockSpec((TQ, TQ), lambda b, p, pt: (0, 0))],
        out_specs=pl.BlockSpec((None, N_HEADS * T8, HEAD_DIM), per_b),
        scratch_shapes=[pltpu.VMEM((N_HEADS * T8, LANES), F32), pltpu.VMEM((N_HEADS * T8, HEAD_DIM), F32)])
    return pl.pallas_call(
        functools.partial(_sb_sample_body, n_pages=n_pages, t_new=t_new), grid_spec=grid_spec,
        out_shape=jax.ShapeDtypeStruct((bd, N_HEADS * T8, HEAD_DIM), BF16),
        compiler_params=_params("parallel", "arbitrary"), name="sb_sample_attn")(
            page_table, qbd, cache3, new8, _suffix_matrix())


def _pad_tokens(x, bd, t_new):
    x = x.reshape(bd, t_new, x.shape[-1])
    return jnp.pad(x, ((0, 0), (0, T8 - t_new), (0, 0)))


def _bias_distances(t, past, n_chunks_s, wb):
    r = np.arange(TQ)[:, None]
    c = np.arange(LANES)[None, :]
    toeplitz = np.concatenate([r - c + off for off in (0, TQ, 2 * TQ)], axis=0)
    cmp_p = np.arange(t)[:, None] - (c * CMP_STRIDE + CMP_BLOCK - 1)
    pos_q = past + np.arange(T8)[:, None]
    cmp_s = (pos_q - (np.arange(n_chunks_s)[None, :] * CMP_STRIDE + CMP_BLOCK - 1)).reshape(-1, LANES)
    last = pos_q - (past - PAGE_SIZE + c)
    far = np.full((T8, LANES), 4 * REL_MAX_DIST)
    new = np.arange(T8)[:, None] - c
    win = (wb + np.arange(T8)[:, None] - np.arange(wb + LANES)[None, :]).reshape(-1, LANES)
    parts = [toeplitz, cmp_p, cmp_s, last, far, new, win]
    rows = sum(p.shape[0] for p in parts)
    pad = -rows % LANES
    parts.append(np.zeros((pad, LANES), np.int64))
    offs = np.cumsum([0] + [p.shape[0] for p in parts])
    return jnp.asarray(np.concatenate(parts, axis=0), I32), offs


def _sample_bias(bias_all, offs, n_chunks_s, wb):
    def rows_kgt(x, width):
        return x.reshape(N_HEADS * T8, width)
    seg = lambda i: bias_all[:, offs[i]:offs[i + 1]]
    return {"cmp": rows_kgt(seg(2), n_chunks_s), "last": rows_kgt(seg(3), LANES), "far": rows_kgt(seg(4), LANES),
            "new": rows_kgt(seg(5), LANES), "win": rows_kgt(seg(6), wb + LANES)}


def kernel(x_prompt, x_sample, cache_nsa, state_nsa_win, cache_sb, page_table, rel_bias, norm_w, final_norm_w,
           nsa_w_in, nsa_w_out, nsa_pe_k, nsa_pe_v, nsa_phi_k1, nsa_phi_k2, nsa_phi_v1, nsa_phi_v2,
           sb_w_in, sb_w_out, ffn_w1, ffn_w3, ffn_w2):
    b, t, d = x_prompt.shape
    bd, t_new, _ = x_sample.shape
    n_pages = page_table.shape[1]
    past = n_pages * PAGE_SIZE
    wb = state_nsa_win.shape[2]
    hd = N_HEADS * HEAD_DIM
    kvw = KV * HEAD_DIM
    n_chunks_s = (past + t_new) // CMP_STRIDE
    half = CMP_STRIDE * HEAD_DIM

    dist, offs = _bias_distances(t, past, n_chunks_s, wb)
    bias_all = bias_from_dist(rel_bias, dist)
    bias_s = _sample_bias(bias_all, offs, n_chunks_s, wb)

    w_in = nsa_w_in[0]
    wq = w_in[:, :hd].astype(BF16)
    wkv4 = w_in[:, hd:hd + 4 * kvw].astype(BF16)
    wwin = w_in[:, hd + 4 * kvw:hd + 6 * kvw].astype(BF16)
    wg = w_in[:, hd + 6 * kvw:].reshape(d, KV, G * 3)
    wg = jnp.pad(wg, ((0, 0), (0, 0), (0, LANES - G * 3))).reshape(d, KV * LANES).astype(BF16)
    w_out_nsa = nsa_w_out[0].astype(BF16)
    cw = {
        "w1k": jnp.concatenate([nsa_phi_k1[0, :half], nsa_phi_k1[0, half:]], axis=1).astype(BF16),
        "w1v": jnp.concatenate([nsa_phi_v1[0, :half], nsa_phi_v1[0, half:]], axis=1).astype(BF16),
        "pe": jnp.broadcast_to(jnp.stack([nsa_pe_k[0].reshape(1, -1), nsa_pe_v[0].reshape(1, -1)]),
                               (2, SUBLANES, CMP_BLOCK * HEAD_DIM)),
        "w1": jnp.stack([nsa_phi_k1[0], nsa_phi_v1[0]]),
        "w2": jnp.stack([nsa_phi_k2[0], nsa_phi_v2[0]]),
    }
    wsb = sb_w_in[0]
    wsq = wsb[:, :hd].astype(BF16)
    wskv = wsb[:, hd:].astype(BF16)
    w_out_sb = sb_w_out[0].astype(BF16)
    w1 = ffn_w1.astype(BF16)
    w3 = ffn_w3.astype(BF16)
    w2 = ffn_w2.astype(BF16)

    hp = x_prompt.reshape(b * t, d)
    hs = x_sample.reshape(bd * t_new, d)

    xp = rmsnorm_rows(hp, norm_w[0, 0], BF16)
    xs = rmsnorm_rows(hs, norm_w[0, 0], BF16)
    q_p, q_s = matmul(xp, wq, BF16), matmul(xs, wq, BF16)
    kv4_p, kv4_s = matmul(xp, wkv4, F32), matmul(xs, wkv4, F32)
    win_p, win_s = matmul(xp, wwin, F32), matmul(xs, wwin, F32)
    g_p, g_s = matmul(xp, wg, F32, sigmoid=True), matmul(xs, wg, F32, sigmoid=True)

    pt_prompt = jnp.arange(b * (t // PAGE_SIZE), dtype=I32).reshape(b, t // PAGE_SIZE)
    kcb_p = compress(kv4_p.reshape(b * t // PAGE_SIZE, PAGE_SIZE, 4 * kvw), pt_prompt, cw)
    attn_p = nsa_prompt_attention(q_p, kv4_p, win_p, g_p, kcb_p, bias_all, b, t)

    cache3 = cache_nsa[0].reshape(cache_nsa.shape[1], PAGE_SIZE, 4 * kvw)
    kcb_s = compress(cache3, page_table, cw)
    attn_s = nsa_sample_attention(
        _pad_tokens(q_s, bd, t_new), cache3, page_table, kcb_s, _pad_tokens(kv4_s, bd, t_new),
        state_nsa_win[0].reshape(bd, wb, 2 * kvw), _pad_tokens(win_s, bd, t_new), _pad_tokens(g_s, bd, t_new),
        bias_s, past, t_new)
    attn_s = attn_s[:, :t_new].reshape(bd * t_new, hd)

    hp, xp = matmul_res_norm(attn_p, w_out_nsa, hp, norm_w[0, 1])
    hs, xs = matmul_res_norm(attn_s, w_out_nsa, hs, norm_w[0, 1])
    hp, xp = ffn_res_norm(xp, w1[0], w3[0], w2[0], hp, norm_w[1, 0], BF16)
    hs, xs = ffn_res_norm(xs, w1[0], w3[0], w2[0], hs, norm_w[1, 0], BF16)

    sq_p, sq_s = matmul(xp, wsq, BF16), matmul(xs, wsq, BF16)
    skv_p, skv_s = matmul(xp, wskv, F32), matmul(xs, wskv, F32)
    sattn_p = sb_prompt_attention(sq_p, skv_p, b, t)

    q4 = _pad_tokens(sq_s, bd, t_new).reshape(bd, T8, N_HEADS, HEAD_DIM).transpose(0, 2, 1, 3)
    eye = jnp.eye(N_HEADS, dtype=BF16)
    qbd = (q4[:, :, :, None, :] * eye[None, :, None, :, None]).reshape(bd, N_HEADS * T8, hd)
    csb3 = cache_sb[0].reshape(cache_sb.shape[1], PAGE_SIZE, 2 * hd)
    sattn_s = sb_sample_attention(qbd, csb3, page_table, _pad_tokens(skv_s, bd, t_new), t_new)
    sattn_s = sattn_s.reshape(bd, N_HEADS, T8, HEAD_DIM)[:, :, :t_new].transpose(0, 2, 1, 3).reshape(bd * t_new, hd)

    hp, xp = matmul_res_norm(sattn_p, w_out_sb, hp, norm_w[1, 1])
    hs, xs = matmul_res_norm(sattn_s, w_out_sb, hs, norm_w[1, 1])
    _, y_p = ffn_res_norm(xp, w1[1], w3[1], w2[1], hp, final_norm_w, F32)
    _, y_s = ffn_res_norm(xs, w1[1], w3[1], w2[1], hs, final_norm_w, F32)

    win_keep = min(WINDOW, t)
    nsa_win_prompt = win_p.reshape(b, t, 2, KV, HEAD_DIM)[:, t - win_keep:]
    win_all = jnp.concatenate([state_nsa_win[0], win_s.reshape(bd, t_new, 2, KV, HEAD_DIM)], axis=1)
    return (y_p.reshape(b, t, d), y_s.reshape(bd, t_new, d),
            kv4_p.reshape(1, b, t, 4, KV, HEAD_DIM), kv4_s.reshape(1, bd, t_new, 4, KV, HEAD_DIM),
            nsa_win_prompt[None], win_all[:, t_new:][None],
            skv_p.reshape(1, b, t, 2, N_HEADS, HEAD_DIM), skv_s.reshape(1, bd, t_new, 2, N_HEADS, HEAD_DIM))
```

```python
import functools
import math

import numpy as np
import jax
import jax.numpy as jnp
from jax import lax
from jax.experimental import pallas as pl
from jax.experimental.pallas import tpu as pltpu

F32 = jnp.float32
BF16 = jnp.bfloat16
I32 = jnp.int32

HEAD_DIM = 128
N_HEADS = 16
NSA_KV_HEADS = 4
NSA_GROUP = N_HEADS // NSA_KV_HEADS
CMP_BLOCK = 32
CMP_STRIDE = 16
SEL_BLOCK = 64
SEL_TOPK = 16
WINDOW = 512
FORCE_BONUS = 1000.0
REL_BUCKETS = 32
REL_MAX_DIST = 128
RMS_EPS = 1e-6
NEG_INF = -1e30
SCALE = HEAD_DIM ** -0.5
PAGE_SIZE = 128

LANES = 128
SUBLANES = 8
TQ = 128
T8 = SUBLANES
VMEM_LIMIT = 56 * 1024 * 1024
KV = NSA_KV_HEADS
G = NSA_GROUP


def _t5_thresholds():
    d = np.arange(0, 4 * REL_MAX_DIST)
    max_exact = REL_BUCKETS // 2
    nf = np.maximum(d, 1).astype(np.float32)
    large = max_exact + (np.log(nf / np.float32(max_exact)) / np.float32(math.log(REL_MAX_DIST / max_exact))
                         * np.float32(REL_BUCKETS - max_exact)).astype(np.int32)
    b = np.where(d < max_exact, d, np.minimum(large, REL_BUCKETS - 1))
    return [int(np.argmax(b >= k)) for k in range(REL_BUCKETS)]


T5_THR = _t5_thresholds()
FAR_DIST = T5_THR[-1]


def _dot(a, b):
    return jnp.dot(a, b, preferred_element_type=F32)


def _dot_nt(a, b):
    return lax.dot_general(a, b, (((1,), (1,)), ((), ())), preferred_element_type=F32)


def _params(*sem):
    return pltpu.CompilerParams(dimension_semantics=sem, vmem_limit_bytes=VMEM_LIMIT)


def _rms(x, w):
    return x * lax.rsqrt(jnp.mean(x * x, axis=-1, keepdims=True) + RMS_EPS) * w


def _masked_softmax(s, mask):
    sm = jnp.where(mask, s, NEG_INF)
    m = jnp.max(sm, axis=1, keepdims=True)
    e = jnp.where(mask, jnp.exp(sm - m), 0.0)
    l = jnp.sum(e, axis=1, keepdims=True)
    return e * jnp.where(l > 0.0, 1.0 / l, 0.0)


def _safe_inv(l):
    return jnp.where(l > 0.0, 1.0 / l, 0.0)


def _rmsnorm_body(x_ref, w_ref, o_ref):
    o_ref[...] = _rms(x_ref[...], w_ref[...]).astype(o_ref.dtype)


def rmsnorm_rows(x, w, out_dtype):
    m, d = x.shape
    tm = min(m, 512)
    return pl.pallas_call(
        _rmsnorm_body, grid=(m // tm,),
        in_specs=[pl.BlockSpec((tm, d), lambda i: (i, 0)), pl.BlockSpec((1, d), lambda i: (0, 0))],
        out_specs=pl.BlockSpec((tm, d), lambda i: (i, 0)),
        out_shape=jax.ShapeDtypeStruct((m, d), out_dtype),
        compiler_params=_params("parallel"), name="rmsnorm")(x, w.reshape(1, d))


def _matmul_body(a_ref, w_ref, o_ref, *, sigmoid):
    y = _dot(a_ref[...], w_ref[...])
    if sigmoid:
        y = jax.nn.sigmoid(y)
    o_ref[...] = y.astype(o_ref.dtype)


def matmul(a, w, out_dtype, sigmoid=False):
    m, k = a.shape
    n = w.shape[1]
    tm = min(m, 512)
    tn = min(n, 2048)
    return pl.pallas_call(
        functools.partial(_matmul_body, sigmoid=sigmoid), grid=(m // tm, n // tn),
        in_specs=[pl.BlockSpec((tm, k), lambda i, j: (i, 0)), pl.BlockSpec((k, tn), lambda i, j: (0, j))],
        out_specs=pl.BlockSpec((tm, tn), lambda i, j: (i, j)),
        out_shape=jax.ShapeDtypeStruct((m, n), out_dtype),
        compiler_params=_params("parallel", "arbitrary"), name="matmul")(a, w)


def _mm_res_norm_body(a_ref, w_ref, h_ref, nw_ref, hout_ref, xn_ref):
    h = h_ref[...] + _dot(a_ref[...], w_ref[...])
    hout_ref[...] = h
    xn_ref[...] = _rms(h, nw_ref[...]).astype(xn_ref.dtype)


def matmul_res_norm(a, w, h, nw):
    m, k = a.shape
    d = w.shape[1]
    tm = min(m, 512)
    return pl.pallas_call(
        _mm_res_norm_body, grid=(m // tm,),
        in_specs=[pl.BlockSpec((tm, k), lambda i: (i, 0)), pl.BlockSpec((k, d), lambda i: (0, 0)),
                  pl.BlockSpec((tm, d), lambda i: (i, 0)), pl.BlockSpec((1, d), lambda i: (0, 0))],
        out_specs=[pl.BlockSpec((tm, d), lambda i: (i, 0)), pl.BlockSpec((tm, d), lambda i: (i, 0))],
        out_shape=[jax.ShapeDtypeStruct((m, d), F32), jax.ShapeDtypeStruct((m, d), BF16)],
        compiler_params=_params("parallel"), name="matmul_res_norm")(a, w, h, nw.reshape(1, d))


def _ffn_body(xn_ref, w1_ref, w3_ref, w2_ref, h_ref, nw_ref, hout_ref, yn_ref, acc_ref):
    f = pl.program_id(1)

    @pl.when(f == 0)
    def _():
        acc_ref[...] = jnp.zeros_like(acc_ref)

    x = xn_ref[...]
    g = _dot(x, w1_ref[...])
    u = _dot(x, w3_ref[...])
    mid = (g * jax.nn.sigmoid(g) * u).astype(BF16)
    acc_ref[...] += _dot(mid, w2_ref[...])

    @pl.when(f == pl.num_programs(1) - 1)
    def _():
        h = h_ref[...] + acc_ref[...]
        hout_ref[...] = h
        yn_ref[...] = _rms(h, nw_ref[...]).astype(yn_ref.dtype)


def ffn_res_norm(xn, w1, w3, w2, h, nw, norm_dtype):
    m, d = xn.shape
    dff = w1.shape[1]
    tm = min(m, 512)
    tf = 512
    return pl.pallas_call(
        _ffn_body, grid=(m // tm, dff // tf),
        in_specs=[pl.BlockSpec((tm, d), lambda i, f: (i, 0)),
                  pl.BlockSpec((d, tf), lambda i, f: (0, f)), pl.BlockSpec((d, tf), lambda i, f: (0, f)),
                  pl.BlockSpec((tf, d), lambda i, f: (f, 0)),
                  pl.BlockSpec((tm, d), lambda i, f: (i, 0)), pl.BlockSpec((1, d), lambda i, f: (0, 0))],
        out_specs=[pl.BlockSpec((tm, d), lambda i, f: (i, 0)), pl.BlockSpec((tm, d), lambda i, f: (i, 0))],
        out_shape=[jax.ShapeDtypeStruct((m, d), F32), jax.ShapeDtypeStruct((m, d), norm_dtype)],
        scratch_shapes=[pltpu.VMEM((tm, d), F32)],
        compiler_params=_params("parallel", "arbitrary"), name="ffn")(xn, w1, w3, w2, h, nw.reshape(1, d))


def _bias_body(tab_ref, d_ref, o_ref):
    h = pl.program_id(0)
    d = d_ref[...]
    acc = jnp.full(d.shape, tab_ref[0, h], F32)
    for b in range(1, REL_BUCKETS):
        acc = jnp.where(d >= T5_THR[b], tab_ref[b, h], acc)
    o_ref[...] = acc


def bias_from_dist(rel_bias, dist):
    r = dist.shape[0]
    rt = r
    return pl.pallas_call(
        _bias_body, grid=(N_HEADS, r // rt),
        in_specs=[pl.BlockSpec(memory_space=pltpu.SMEM), pl.BlockSpec((rt, LANES), lambda h, i: (i, 0))],
        out_specs=pl.BlockSpec((None, rt, LANES), lambda h, i: (h, i, 0)),
        out_shape=jax.ShapeDtypeStruct((N_HEADS, r, LANES), F32),
        compiler_params=_params("parallel", "parallel"), name="rel_bias")(rel_bias, dist)


CMP_PAGES = 16


def _cmp1_body(pt_ref, *refs):
    pages = refs[:CMP_PAGES]
    perm = refs[CMP_PAGES][...]
    w_refs = refs[CMP_PAGES + 1:CMP_PAGES + 3]
    o_ref = refs[CMP_PAGES + 3]
    cpp = PAGE_SIZE // CMP_STRIDE
    for kind in range(2):
        w = w_refs[kind][...]
        for kv in range(KV):
            c0 = (kind * KV + kv) * HEAD_DIM
            xs = [_dot(perm, pg[:, c0:c0 + HEAD_DIM].astype(BF16)) for pg in pages]
            pieces = [jnp.concatenate([x[r * cpp:(r + 1) * cpp] for x in xs], axis=0).astype(BF16)
                      for r in range(CMP_STRIDE)]
            o_ref[kind * KV + kv] = _dot(jnp.concatenate(pieces, axis=1), w)


def compress_stage1(rows3, page_table, w1k, w1v):
    nb_seq, n_pages = page_table.shape
    steps = n_pages // CMP_PAGES
    cpp = PAGE_SIZE // CMP_STRIDE
    width = 2 * KV * HEAD_DIM

    def page_spec(i):
        return pl.BlockSpec((None, PAGE_SIZE, width), lambda b, s, pt: (pt[b, s * CMP_PAGES + i], 0, 0))

    wspec = pl.BlockSpec((CMP_STRIDE * HEAD_DIM, 2 * HEAD_DIM), lambda b, s, pt: (0, 0))
    src = np.arange(PAGE_SIZE)
    perm = np.zeros((PAGE_SIZE, PAGE_SIZE), np.float32)
    perm[(src % CMP_STRIDE) * cpp + src // CMP_STRIDE, src] = 1.0
    perm = jnp.asarray(perm, BF16)
    grid_spec = pltpu.PrefetchScalarGridSpec(
        num_scalar_prefetch=1, grid=(nb_seq, steps),
        in_specs=[page_spec(i) for i in range(CMP_PAGES)]
        + [pl.BlockSpec((PAGE_SIZE, PAGE_SIZE), lambda b, s, pt: (0, 0)), wspec, wspec],
        out_specs=pl.BlockSpec((None, 2 * KV, CMP_PAGES * cpp, 2 * HEAD_DIM), lambda b, s, pt: (b, 0, s, 0)))
    return pl.pallas_call(
        _cmp1_body, grid_spec=grid_spec,
        out_shape=jax.ShapeDtypeStruct((nb_seq, 2 * KV, n_pages * cpp, 2 * HEAD_DIM), F32),
        compiler_params=_params("parallel", "arbitrary"), name="compress1")(
            page_table, *([rows3] * CMP_PAGES), perm, w1k, w1v)


def _cmp1_paged_body(pt_ref, *refs):
    pages = refs[:CMP_PAGES]
    w_refs = refs[CMP_PAGES:CMP_PAGES + 2]
    o_ref, sel_ref = refs[CMP_PAGES + 2], refs[CMP_PAGES + 3]
    cpp = PAGE_SIZE // CMP_STRIDE
    rpt = 4 * KV
    for kind in range(2):
        w = w_refs[kind][...]
        for kv in range(KV):
            c = kind * KV + kv
            pieces = [jnp.concatenate([pg[pl.ds(r * rpt + c, cpp, stride=CMP_STRIDE * rpt), :] for pg in pages],
                                      axis=0).astype(BF16) for r in range(CMP_STRIDE)]
            o_ref[c] = _dot(jnp.concatenate(pieces, axis=1), w)
    for c in range(2 * KV):
        for i, pg in enumerate(pages):
            sel_ref[c, i * PAGE_SIZE:(i + 1) * PAGE_SIZE, :] = (
                pg[pl.ds(2 * KV + c, PAGE_SIZE, stride=rpt), :].astype(BF16))


def compress_stage1_paged(cache3, page_table, w1k, w1v):
    nb_seq, n_pages = page_table.shape
    steps = n_pages // CMP_PAGES
    cpp = PAGE_SIZE // CMP_STRIDE
    rows = PAGE_SIZE * 4 * KV

    def page_spec(i):
        return pl.BlockSpec((None, rows, HEAD_DIM), lambda b, s, pt: (pt[b, s * CMP_PAGES + i], 0, 0))

    wspec = pl.BlockSpec((CMP_STRIDE * HEAD_DIM, 2 * HEAD_DIM), lambda b, s, pt: (0, 0))
    grid_spec = pltpu.PrefetchScalarGridSpec(
        num_scalar_prefetch=1, grid=(nb_seq, steps),
        in_specs=[page_spec(i) for i in range(CMP_PAGES)] + [wspec, wspec],
        out_specs=[pl.BlockSpec((None, 2 * KV, CMP_PAGES * cpp, 2 * HEAD_DIM), lambda b, s, pt: (b, 0, s, 0)),
                   pl.BlockSpec((None, 2 * KV, CMP_PAGES * PAGE_SIZE, HEAD_DIM), lambda b, s, pt: (b, 0, s, 0))])
    return pl.pallas_call(
        _cmp1_paged_body, grid_spec=grid_spec,
        out_shape=[jax.ShapeDtypeStruct((nb_seq, 2 * KV, n_pages * cpp, 2 * HEAD_DIM), F32),
                   jax.ShapeDtypeStruct((nb_seq, 2 * KV, n_pages * PAGE_SIZE, HEAD_DIM), BF16)],
        compiler_params=_params("parallel", "arbitrary"), name="compress1_paged")(
            page_table, *([cache3] * CMP_PAGES), w1k, w1v)


def _cmp2_body(ab_ref, pe_ref, w1_ref, w2_ref, o_ref):
    nch = ab_ref.shape[0]
    ab = ab_ref[...]
    c = _dot(pe_ref[...].astype(BF16), w1_ref[...].astype(BF16))
    pre = ab[:, :HEAD_DIM] + pltpu.roll(ab[:, HEAD_DIM:], nch - 1, 0) + c[0:1]
    y = _dot((pre * jax.nn.sigmoid(pre)).astype(BF16), w2_ref[...].astype(BF16))
    rowi = lax.broadcasted_iota(I32, y.shape, 0)
    o_ref[...] = jnp.where(rowi < nch - 1, y, 0.0)


def compress_stage2(ab, pe, w1, w2):
    nb_seq, _, nch, _ = ab.shape
    return pl.pallas_call(
        _cmp2_body, grid=(nb_seq, 2 * KV),
        in_specs=[pl.BlockSpec((None, None, nch, 2 * HEAD_DIM), lambda b, j: (b, j, 0, 0)),
                  pl.BlockSpec((None, SUBLANES, CMP_BLOCK * HEAD_DIM), lambda b, j: (j // KV, 0, 0)),
                  pl.BlockSpec((None, CMP_BLOCK * HEAD_DIM, HEAD_DIM), lambda b, j: (j // KV, 0, 0)),
                  pl.BlockSpec((None, HEAD_DIM, HEAD_DIM), lambda b, j: (j // KV, 0, 0))],
        out_specs=pl.BlockSpec((None, None, nch, HEAD_DIM), lambda b, j: (b, j, 0, 0)),
        out_shape=jax.ShapeDtypeStruct((nb_seq, 2 * KV, nch, HEAD_DIM), F32),
        compiler_params=_params("parallel", "parallel"), name="compress2")(ab, pe, w1, w2)


def compress(rows3, page_table, cw):
    ab = compress_stage1(rows3, page_table, cw["w1k"], cw["w1v"])
    return compress_stage2(ab, cw["pe"], cw["w1"], cw["w2"])


def compress_paged(cache3, page_table, cw):
    ab, sel_rows = compress_stage1_paged(cache3, page_table, cw["w1k"], cw["w1v"])
    return compress_stage2(ab, cw["pe"], cw["w1"], cw["w2"]), sel_rows


def _flash_update(s, mask, v, m_ref, l_ref, acc_ref, rows):
    m_prev = m_ref[rows, :]
    m_new = jnp.maximum(m_prev, jnp.max(jnp.where(mask, s, NEG_INF), axis=1, keepdims=True))
    alpha = jnp.exp(m_prev - m_new)
    p = jnp.where(mask, jnp.exp(s - (m_new if s.shape[1] == LANES else m_new[:, 0:1])), 0.0)
    l_ref[rows, :] = alpha * l_ref[rows, :] + jnp.sum(p, axis=1, keepdims=True)
    acc_ref[rows, :] = alpha * acc_ref[rows, :] + _dot(p.astype(BF16), v)
    m_ref[rows, :] = m_new


def _flash_init(m_ref, l_ref, acc_ref):
    m_ref[...] = jnp.full(m_ref.shape, NEG_INF, F32)
    l_ref[...] = jnp.zeros_like(l_ref)
    acc_ref[...] = jnp.zeros_like(acc_ref)


def _rank_select(score, eligible, blk, n_blocks, n_top, row_of):
    def body(j, cnt):
        rowj = row_of(j)
        beats = (rowj > score) | ((rowj == score) & (blk > j))
        return cnt + beats.astype(I32)
    cnt = lax.fori_loop(0, n_blocks, body, jnp.zeros(score.shape, I32))
    return jnp.where((cnt < n_top) & eligible, 1.0, 0.0)


def _nsa_prompt_body(q_ref, ks_ref, vs_ref, kw_ref, vw_ref, kc_ref, vc_ref, g_ref, bc_ref, bt_ref, ovl_ref, exp_ref,
                     o_ref, m_s, l_s, a_s, m_w, l_w, a_w, mk_ref, sc_ref, *, n_cmp, n_blk):
    qi = pl.program_id(2)
    rows = G * TQ
    q = jnp.concatenate([q_ref[:, g * HEAD_DIM:(g + 1) * HEAD_DIM] for g in range(G)], axis=0)
    row5 = lax.broadcasted_iota(I32, (rows, LANES), 0)
    lane5 = lax.broadcasted_iota(I32, (rows, LANES), 1)
    tpos5 = qi * TQ + (row5 & (TQ - 1))

    s = _dot_nt(q, kc_ref[...].astype(BF16)) * SCALE
    s = s + jnp.concatenate([bc_ref[g] for g in range(G)], axis=0)
    mask_c = (tpos5 >= lane5 * CMP_STRIDE + (CMP_BLOCK - 1)) & (lane5 < n_cmp)
    p_c = _masked_softmax(s, mask_c)
    pb = p_c.astype(BF16)
    o_c = _dot(pb, vc_ref[...].astype(BF16))

    imp = _dot_nt(ovl_ref[...], pb[0:TQ])
    for g in range(1, G):
        imp = imp + _dot_nt(ovl_ref[...], pb[g * TQ:(g + 1) * TQ])
    blk = lax.broadcasted_iota(I32, (LANES, TQ), 0)
    tq_l = qi * TQ + lax.broadcasted_iota(I32, (LANES, TQ), 1)
    cur = lax.shift_right_arithmetic(tq_l, int(math.log2(SEL_BLOCK)))
    eligible = blk * SEL_BLOCK <= tq_l
    forced = (blk == 0) | (blk == cur) | (blk == cur - 1)
    score = jnp.where(eligible, imp + jnp.where(forced, FORCE_BONUS, 0.0), -jnp.inf)
    sc_ref[...] = score
    sel_t = _rank_select(score, eligible, blk, n_blk, min(SEL_TOPK, n_blk), lambda j: sc_ref[pl.ds(j, 1), :])
    mk_ref[...] = _dot(sel_t.T.astype(BF16), exp_ref[...])

    def bias_tile(kt):
        off = pl.multiple_of(jnp.minimum(qi - kt, 2) * TQ, TQ)
        return jnp.concatenate([bt_ref[g, pl.ds(off, TQ), :] for g in range(G)], axis=0)

    _flash_init(m_s, l_s, a_s)
    _flash_init(m_w, l_w, a_w)
    all_rows = pl.ds(0, rows)

    def sel_step(kt, carry):
        off = pl.multiple_of(kt * TQ, TQ)
        k = ks_ref[pl.ds(off, TQ), :].astype(BF16)
        v = vs_ref[pl.ds(off, TQ), :].astype(BF16)
        s = _dot_nt(q, k) * SCALE + bias_tile(kt)
        mk = mk_ref[:, pl.ds(off, TQ)]
        mask = (jnp.concatenate([mk] * G, axis=0) > 0.5) & (off + lane5 <= tpos5)
        _flash_update(s, mask, v, m_s, l_s, a_s, all_rows)
        return carry

    def win_step(kt, carry):
        off = pl.multiple_of(kt * TQ, TQ)
        k = kw_ref[pl.ds(off, TQ), :].astype(BF16)
        v = vw_ref[pl.ds(off, TQ), :].astype(BF16)
        s = _dot_nt(q, k) * SCALE + bias_tile(kt)
        dist = tpos5 - (off + lane5)
        mask = (dist >= 0) & (dist < WINDOW)
        _flash_update(s, mask, v, m_w, l_w, a_w, all_rows)
        return carry

    lo = jnp.maximum(qi - WINDOW // TQ, 0)
    lax.fori_loop(0, lo, sel_step, 0)
    lax.fori_loop(lo, qi + 1, lambda kt, c: win_step(kt, sel_step(kt, c)), 0)

    o_s = a_s[...] * _safe_inv(l_s[...])
    o_w = a_w[...] * _safe_inv(l_w[...])
    gt = g_ref[...]
    for g in range(G):
        r = slice(g * TQ, (g + 1) * TQ)
        out = (gt[:, 3 * g:3 * g + 1] * o_c[r] + gt[:, 3 * g + 1:3 * g + 2] * o_s[r]
               + gt[:, 3 * g + 2:3 * g + 3] * o_w[r])
        o_ref[:, g * HEAD_DIM:(g + 1) * HEAD_DIM] = out.astype(o_ref.dtype)


def nsa_prompt_attention(q, kv4, kwin, gates, kcb, bias_all, b, t):
    nq = t // TQ
    n_chunks = t // CMP_STRIDE
    n_cmp = n_chunks - CMP_BLOCK // CMP_STRIDE + 1
    n_blk = -(-t // SEL_BLOCK)
    assert n_chunks == LANES and n_blk <= LANES and t % TQ == 0 and TQ >= FAR_DIST
    cmp_start = np.arange(LANES) * CMP_STRIDE
    jst = np.arange(LANES) * SEL_BLOCK
    overlap = ((cmp_start[:, None] < jst[None, :] + SEL_BLOCK) & (cmp_start[:, None] + CMP_BLOCK > jst[None, :])
               & (np.arange(LANES)[:, None] < n_cmp) & (np.arange(LANES)[None, :] < n_blk))
    ovl_t = jnp.asarray(overlap.T, BF16)
    expand = jnp.asarray(np.arange(LANES)[:, None] == (np.arange(t)[None, :] // SEL_BLOCK), BF16)
    kvw = KV * HEAD_DIM
    colblk = lambda base: (lambda bb, kv, qi: (bb, base + kv))
    rowblk = lambda bb, kv, qi: (bb * nq + qi, kv)
    toe = 3 * TQ // LANES
    in_specs = [
        pl.BlockSpec((TQ, G * HEAD_DIM), rowblk),
        pl.BlockSpec((t, HEAD_DIM), colblk(2 * KV)), pl.BlockSpec((t, HEAD_DIM), colblk(3 * KV)),
        pl.BlockSpec((t, HEAD_DIM), colblk(0)), pl.BlockSpec((t, HEAD_DIM), colblk(KV)),
        pl.BlockSpec((None, None, n_chunks, HEAD_DIM), lambda bb, kv, qi: (bb, kv, 0, 0)),
        pl.BlockSpec((None, None, n_chunks, HEAD_DIM), lambda bb, kv, qi: (bb, KV + kv, 0, 0)),
        pl.BlockSpec((TQ, LANES), rowblk),
        pl.BlockSpec((G, TQ, LANES), lambda bb, kv, qi: (kv, toe + qi, 0)),
        pl.BlockSpec((G, 3 * TQ, LANES), lambda bb, kv, qi: (kv, 0, 0)),
        pl.BlockSpec((LANES, LANES), lambda bb, kv, qi: (0, 0)),
        pl.BlockSpec((LANES, t), lambda bb, kv, qi: (0, 0)),
    ]
    rows = G * TQ
    scratch = [pltpu.VMEM((rows, LANES), F32) for _ in range(6)] + [pltpu.VMEM((TQ, t), F32),
                                                                   pltpu.VMEM((LANES, TQ), F32)]
    del kvw
    return pl.pallas_call(
        functools.partial(_nsa_prompt_body, n_cmp=n_cmp, n_blk=n_blk), grid=(b, KV, nq),
        in_specs=in_specs, out_specs=pl.BlockSpec((TQ, G * HEAD_DIM), rowblk),
        out_shape=jax.ShapeDtypeStruct((b * t, N_HEADS * HEAD_DIM), BF16),
        scratch_shapes=scratch,
        compiler_params=_params("parallel", "parallel", "arbitrary"), name="nsa_prompt_attn")(
            q, kv4, kv4, kwin, kwin, kcb, kcb, gates, bias_all, bias_all, ovl_t, expand)


def _stick_tile(z, mask, u, carry, along_lanes=True):
    sp = jnp.log1p(jnp.exp(-jnp.abs(z)))
    log_beta = jnp.minimum(z, 0.0) - sp
    log_rest = -jnp.maximum(z, 0.0) - sp
    if mask is not None:
        log_rest = jnp.where(mask, log_rest, 0.0)
    hi = log_rest.astype(BF16)
    r1 = log_rest - hi.astype(F32)
    mid = r1.astype(BF16)
    lo = (r1 - mid.astype(F32)).astype(BF16)
    n = z.shape[0]
    l3 = _dot(jnp.concatenate([hi, mid, lo], axis=0), u)
    later = l3[0:n] + l3[n:2 * n] + l3[2 * n:3 * n] + carry
    a = jnp.exp(log_beta + later)
    if mask is not None:
        a = jnp.where(mask, a, 0.0)
    return a, jnp.sum(log_rest, axis=1, keepdims=True)


SB_HEADS = 4


def _sb_prompt_body(q_ref, k_ref, v_ref, u_ref, o_ref, carry_ref, acc_ref):
    qi = pl.program_id(2)
    row = lax.broadcasted_iota(I32, (TQ, TQ), 0)
    lane = lax.broadcasted_iota(I32, (TQ, TQ), 1)

    def tile(kt, mask, first):
        off = pl.multiple_of(kt * TQ, TQ)
        for h in range(SB_HEADS):
            cols = slice(h * HEAD_DIM, (h + 1) * HEAD_DIM)
            k = k_ref[pl.ds(off, TQ), cols].astype(BF16)
            v = v_ref[pl.ds(off, TQ), cols].astype(BF16)
            z = _dot_nt(q_ref[:, cols], k) * SCALE
            carry = 0.0 if first else carry_ref[h]
            a, tot = _stick_tile(z, mask, u_ref[...], carry)
            pv = _dot(a.astype(BF16), v)
            acc_ref[h] = pv if first else acc_ref[h] + pv
            carry_ref[h] = jnp.broadcast_to(tot, (TQ, TQ)) if first else carry_ref[h] + tot

    tile(qi, lane < row, True)

    def step(i, c):
        tile(qi - 1 - i, None, False)
        return c

    lax.fori_loop(0, qi, step, 0)
    for h in range(SB_HEADS):
        o_ref[:, h * HEAD_DIM:(h + 1) * HEAD_DIM] = acc_ref[h].astype(o_ref.dtype)


def _suffix_matrix():
    j = np.arange(TQ)
    return jnp.asarray(j[:, None] > j[None, :], BF16)


def sb_prompt_attention(q, kvp, b, t):
    nq = t // TQ
    hg = N_HEADS // SB_HEADS
    w = SB_HEADS * HEAD_DIM
    rowblk = lambda bb, h, qi: (bb * nq + qi, h)
    return pl.pallas_call(
        _sb_prompt_body, grid=(b, hg, nq),
        in_specs=[pl.BlockSpec((TQ, w), rowblk),
                  pl.BlockSpec((t, w), lambda bb, h, qi: (bb, h)),
                  pl.BlockSpec((t, w), lambda bb, h, qi: (bb, hg + h)),
                  pl.BlockSpec((TQ, TQ), lambda bb, h, qi: (0, 0))],
        out_specs=pl.BlockSpec((TQ, w), rowblk),
        out_shape=jax.ShapeDtypeStruct((b * t, N_HEADS * HEAD_DIM), BF16),
        scratch_shapes=[pltpu.VMEM((SB_HEADS, TQ, TQ), F32), pltpu.VMEM((SB_HEADS, TQ, HEAD_DIM), F32)],
        compiler_params=_params("parallel", "parallel", "arbitrary"), name="sb_prompt_attn")(
            q, kvp, kvp, _suffix_matrix())


def _nsa_sample_select_body(q_ref, kcb_ref, bc_ref, mc_ref, ovl_ref, fadd_ref, elig_ref, oc_ref, sel_ref, sc_ref,
                            *, n_blk):
    nbp = kcb_ref.shape[1]
    grp = G * T8
    ps = []
    for kv in range(KV):
        qk = jnp.concatenate([q_ref[:, (kv * G + g) * HEAD_DIM:(kv * G + g + 1) * HEAD_DIM] for g in range(G)],
                             axis=0)
        s = _dot_nt(qk, kcb_ref[kv].astype(BF16)) * SCALE + bc_ref[kv * grp:(kv + 1) * grp, :]
        p = _masked_softmax(s, mc_ref[...] > 0.5)
        oc_ref[kv * grp:(kv + 1) * grp, :] = _dot(p.astype(BF16), kcb_ref[KV + kv].astype(BF16))
        ps.append(p)
    pad = jnp.zeros((LANES - KV * T8, nbp), F32)
    imp = None
    for g in range(G):
        xg = jnp.concatenate([ps[kv][g * T8:(g + 1) * T8] for kv in range(KV)] + [pad], axis=0)
        part = _dot_nt(ovl_ref[...], xg.astype(BF16))
        imp = part if imp is None else imp + part
    eligible = elig_ref[...] > 0.5
    score = jnp.where(eligible, imp + fadd_ref[...], -jnp.inf)
    sc_ref[...] = score
    blk = lax.broadcasted_iota(I32, score.shape, 0)
    sel_ref[...] = _rank_select(score, eligible, blk, n_blk, min(SEL_TOPK, n_blk),
                                lambda j: sc_ref[pl.ds(j, 1), :])


SEL_KEYS = 1024


def _nsa_sample_attn_body(q_ref, kv_ref, sel_ref, new_ref, win_ref, wnew_ref, g_ref, oc_ref,
                          bl_ref, bf_ref, bn_ref, bw_ref, ex_ref, o_ref, m_s, l_s, a_s, *, n_steps, t_new, past):
    step = pl.program_id(1)
    grp = G * T8
    kvw = KV * HEAD_DIM
    wb = win_ref.shape[0] // (2 * KV)
    is_last = step == n_steps - 1

    @pl.when(step == 0)
    def _():
        _flash_init(m_s, l_s, a_s)

    def q_of(kv):
        return jnp.concatenate([q_ref[:, (kv * G + g) * HEAD_DIM:(kv * G + g + 1) * HEAD_DIM] for g in range(G)],
                               axis=0)

    def pad_keys(x):
        return jnp.concatenate([x, jnp.zeros((TQ - x.shape[0], x.shape[1]), x.dtype)], axis=0).astype(BF16)

    def rows_of_blocks(piece):
        return jnp.concatenate([piece, jnp.zeros((LANES - piece.shape[0], LANES), F32)], axis=0).T

    bps = SEL_KEYS // SEL_BLOCK
    piece = sel_ref[pl.ds(pl.multiple_of(step * bps, bps), bps), :]
    mk = _dot(rows_of_blocks(piece).astype(BF16), ex_ref[...])

    for kv in range(KV):
        rows = pl.ds(kv * grp, grp)
        bias = jnp.where(is_last, bl_ref[rows, :], bf_ref[rows, 0:1])
        s = _dot_nt(q_of(kv), kv_ref[kv]) * SCALE + bias
        mask = jnp.concatenate([mk[kv * T8:(kv + 1) * T8]] * G, axis=0) > 0.5
        _flash_update(s, mask, kv_ref[KV + kv], m_s, l_s, a_s, rows)

    @pl.when(is_last)
    def _():
        lane = lax.broadcasted_iota(I32, (grp, LANES), 1)
        t_row = lax.broadcasted_iota(I32, (grp, LANES), 0) & (T8 - 1)
        lane_w = lax.broadcasted_iota(I32, (grp, wb), 1)
        t_row_w = lax.broadcasted_iota(I32, (grp, wb), 0) & (T8 - 1)
        new_blk = past // SEL_BLOCK
        picked_all = rows_of_blocks(sel_ref[new_blk:new_blk + SUBLANES, :])
        gt = g_ref[...]
        for kv in range(KV):
            rows = pl.ds(kv * grp, grp)
            qk = q_of(kv)
            kn = pad_keys(new_ref[:, 2 * kvw + kv * HEAD_DIM:2 * kvw + (kv + 1) * HEAD_DIM])
            vn = pad_keys(new_ref[:, 3 * kvw + kv * HEAD_DIM:3 * kvw + (kv + 1) * HEAD_DIM])
            s = _dot_nt(qk, kn) * SCALE + bn_ref[rows, :]
            picked = jnp.concatenate([picked_all[kv * T8:(kv + 1) * T8, 0:1]] * G, axis=0) > 0.5
            mask = (lane <= t_row) & (lane < t_new) & picked
            _flash_update(s, mask, vn, m_s, l_s, a_s, rows)
            o_s = a_s[rows, :] * _safe_inv(l_s[rows, :])
            kw = win_ref[pl.ds(kv, wb, stride=2 * KV), :].astype(BF16)
            vw = win_ref[pl.ds(KV + kv, wb, stride=2 * KV), :].astype(BF16)
            kwn = pad_keys(wnew_ref[:, kv * HEAD_DIM:(kv + 1) * HEAD_DIM])
            vwn = pad_keys(wnew_ref[:, kvw + kv * HEAD_DIM:kvw + (kv + 1) * HEAD_DIM])
            s1 = _dot_nt(qk, kw) * SCALE + bw_ref[rows, 0:wb]
            s2 = _dot_nt(qk, kwn) * SCALE + bw_ref[rows, wb:wb + LANES]
            d1 = wb + t_row_w - lane_w
            mask1 = (d1 >= 0) & (d1 < WINDOW)
            d2 = t_row - lane
            mask2 = (d2 >= 0) & (d2 < WINDOW) & (lane < t_new)
            m = jnp.maximum(jnp.max(jnp.where(mask1, s1, NEG_INF), axis=1, keepdims=True),
                            jnp.max(jnp.where(mask2, s2, NEG_INF), axis=1, keepdims=True))
            e1 = jnp.where(mask1, jnp.exp(s1 - m), 0.0)
            e2 = jnp.where(mask2, jnp.exp(s2 - m), 0.0)
            l = jnp.sum(e1, axis=1, keepdims=True) + jnp.sum(e2, axis=1, keepdims=True)
            o_w = (_dot(e1.astype(BF16), vw) + _dot(e2.astype(BF16), vwn)) * _safe_inv(l)
            o_c = oc_ref[rows, :]
            for g in range(G):
                r = slice(g * T8, (g + 1) * T8)
                c = kv * LANES + 3 * g
                out = gt[:, c:c + 1] * o_c[r] + gt[:, c + 1:c + 2] * o_s[r] + gt[:, c + 2:c + 3] * o_w[r]
                o_ref[:, (kv * G + g) * HEAD_DIM:(kv * G + g + 1) * HEAD_DIM] = out.astype(o_ref.dtype)


def nsa_sample_attention(q8, sel_rows, kcb, new8, win3, wnew8, gates8, bias_s, past, t_new):
    bd = q8.shape[0]
    n_chunks = kcb.shape[2]
    n_cmp = n_chunks - CMP_BLOCK // CMP_STRIDE + 1
    total = past + t_new
    n_blk = -(-total // SEL_BLOCK)
    nblkp = -(-n_blk // LANES) * LANES
    wb = win3.shape[1] // (2 * KV)
    n_steps = past // SEL_KEYS
    assert past % SEL_KEYS == 0 and SEL_KEYS % SEL_BLOCK == 0 and t_new <= T8 and PAGE_SIZE > FAR_DIST
    assert (total // CMP_STRIDE) == n_chunks and wb % LANES == 0 and SEL_KEYS // SEL_BLOCK <= LANES

    pos_q = past + np.arange(T8)
    cmp_start = np.arange(n_chunks) * CMP_STRIDE
    cmp_end = cmp_start + CMP_BLOCK - 1
    mask_c = ((pos_q[:, None] >= cmp_end[None, :]) & (np.arange(n_chunks)[None, :] < n_cmp))
    mask_c = jnp.asarray(np.tile(mask_c, (G, 1)), F32)
    jst = np.arange(nblkp) * SEL_BLOCK
    overlap = ((cmp_start[:, None] < jst[None, :] + SEL_BLOCK) & (cmp_start[:, None] + CMP_BLOCK > jst[None, :])
               & (np.arange(n_chunks)[:, None] < n_cmp) & (np.arange(nblkp)[None, :] < n_blk))
    ovl_t = jnp.asarray(overlap.T, BF16)
    blk = np.arange(nblkp)[:, None]
    pq = np.tile(pos_q, KV)[None, :]
    tvalid = np.tile(np.arange(T8) < t_new, KV)[None, :]
    cur = pq // SEL_BLOCK
    elig = (blk * SEL_BLOCK <= pq) & (blk < n_blk) & tvalid
    forced = (blk == 0) | (blk == cur) | (blk == cur - 1)
    pad_l = LANES - KV * T8
    elig = jnp.asarray(np.pad(elig, ((0, 0), (0, pad_l))), F32)
    fadd = jnp.asarray(np.pad(np.where(forced, FORCE_BONUS, 0.0), ((0, 0), (0, pad_l))), F32)

    grp_all = KV * G * T8
    o_c, sel = pl.pallas_call(
        functools.partial(_nsa_sample_select_body, n_blk=n_blk), grid=(bd,),
        in_specs=[pl.BlockSpec((None, T8, N_HEADS * HEAD_DIM), lambda b: (b, 0, 0)),
                  pl.BlockSpec((None, 2 * KV, n_chunks, HEAD_DIM), lambda b: (b, 0, 0, 0)),
                  pl.BlockSpec((grp_all, n_chunks), lambda b: (0, 0)),
                  pl.BlockSpec((G * T8, n_chunks), lambda b: (0, 0)),
                  pl.BlockSpec((nblkp, n_chunks), lambda b: (0, 0)),
                  pl.BlockSpec((nblkp, LANES), lambda b: (0, 0)),
                  pl.BlockSpec((nblkp, LANES), lambda b: (0, 0))],
        out_specs=[pl.BlockSpec((None, grp_all, HEAD_DIM), lambda b: (b, 0, 0)),
                   pl.BlockSpec((None, nblkp, LANES), lambda b: (b, 0, 0))],
        out_shape=[jax.ShapeDtypeStruct((bd, grp_all, HEAD_DIM), F32),
                   jax.ShapeDtypeStruct((bd, nblkp, LANES), F32)],
        scratch_shapes=[pltpu.VMEM((nblkp, LANES), F32)],
        compiler_params=_params("parallel"), name="nsa_sample_select")(
            q8, kcb, bias_s["cmp"], mask_c, ovl_t, fadd, elig)

    kvw = KV * HEAD_DIM
    const2 = lambda b, s: (0, 0)
    per_b = lambda b, s: (b, 0, 0)
    expand = jnp.asarray(np.arange(LANES)[:, None] == (np.arange(SEL_KEYS)[None, :] // SEL_BLOCK), BF16)
    bias_last = jnp.concatenate([bias_s["far"]] * (SEL_KEYS // LANES - 1) + [bias_s["last"]], axis=1)
    return pl.pallas_call(
        functools.partial(_nsa_sample_attn_body, n_steps=n_steps, t_new=t_new, past=past), grid=(bd, n_steps),
        in_specs=[pl.BlockSpec((None, T8, N_HEADS * HEAD_DIM), per_b),
                  pl.BlockSpec((None, 2 * KV, SEL_KEYS, HEAD_DIM), lambda b, s: (b, 0, s, 0)),
                  pl.BlockSpec((None, nblkp, LANES), per_b),
                  pl.BlockSpec((None, T8, 4 * kvw), per_b),
                  pl.BlockSpec((None, wb * 2 * KV, HEAD_DIM), per_b),
                  pl.BlockSpec((None, T8, 2 * kvw), per_b),
                  pl.BlockSpec((None, T8, KV * LANES), per_b),
                  pl.BlockSpec((None, grp_all, HEAD_DIM), per_b),
                  pl.BlockSpec((grp_all, SEL_KEYS), const2), pl.BlockSpec((grp_all, LANES), const2),
                  pl.BlockSpec((grp_all, LANES), const2), pl.BlockSpec((grp_all, wb + LANES), const2),
                  pl.BlockSpec((LANES, SEL_KEYS), const2)],
        out_specs=pl.BlockSpec((None, T8, N_HEADS * HEAD_DIM), per_b),
        out_shape=jax.ShapeDtypeStruct((bd, T8, N_HEADS * HEAD_DIM), BF16),
        scratch_shapes=[pltpu.VMEM((grp_all, LANES), F32) for _ in range(3)],
        compiler_params=_params("parallel", "arbitrary"), name="nsa_sample_attn")(
            q8, sel_rows, sel, new8, win3, wnew8, gates8, o_c,
            bias_last, bias_s["far"], bias_s["new"], bias_s["win"], expand)


SB_PAGES = 4


def _sb_sample_body(pt_ref, q_ref, *refs, n_steps, t_new):
    pages = refs[:SB_PAGES]
    new_ref, u_ref, o_ref, carry_ref, acc_ref = refs[SB_PAGES:]
    step = pl.program_id(1)
    hw = N_HEADS * HEAD_DIM
    rpt = 2 * N_HEADS
    u = u_ref[...]

    def q_of(h):
        return q_ref[h * T8:(h + 1) * T8, :].astype(BF16)

    def run(n_sub, k_of, v_of, masks):
        zs = [jnp.concatenate([_dot_nt(q_of(h), k_of(j, h)) for h in range(N_HEADS)], axis=0) * SCALE
              for j in range(n_sub)]
        carry = carry_ref[...]
        outs = [None] * N_HEADS
        for j in reversed(range(n_sub)):
            a, tot = _stick_tile(zs[j], masks[j], u, carry)
            for h in range(N_HEADS):
                pv = _dot(a[h * T8:(h + 1) * T8].astype(BF16), v_of(j, h))
                outs[h] = pv if outs[h] is None else outs[h] + pv
            carry = carry + tot
        carry_ref[...] = carry
        for h in range(N_HEADS):
            acc_ref[h * T8:(h + 1) * T8, :] += outs[h]

    @pl.when(step == 0)
    def _():
        carry_ref[...] = jnp.zeros_like(carry_ref)
        acc_ref[...] = jnp.zeros_like(acc_ref)
        pad = jnp.zeros((TQ - T8, 2 * hw), F32)
        kvn = jnp.concatenate([new_ref[...], pad], axis=0)
        lane = lax.broadcasted_iota(I32, (TQ, TQ), 1)
        t_row = lax.broadcasted_iota(I32, (TQ, TQ), 0) & (T8 - 1)
        run(1, lambda j, h: kvn[:, h * HEAD_DIM:(h + 1) * HEAD_DIM].astype(BF16),
            lambda j, h: kvn[:, hw + h * HEAD_DIM:hw + (h + 1) * HEAD_DIM].astype(BF16),
            [(lane < t_row) & (lane < t_new)])

    run(SB_PAGES, lambda j, h: pages[j][pl.ds(h, PAGE_SIZE, stride=rpt), :].astype(BF16),
        lambda j, h: pages[j][pl.ds(N_HEADS + h, PAGE_SIZE, stride=rpt), :].astype(BF16), [None] * SB_PAGES)

    @pl.when(step == n_steps - 1)
    def _():
        o_ref[...] = acc_ref[...].astype(o_ref.dtype)


def sb_sample_attention(q_rows, cache3, page_table, new8, t_new):
    bd, n_pages = page_table.shape
    hw = N_HEADS * HEAD_DIM
    n_steps = n_pages // SB_PAGES
    assert n_pages % SB_PAGES == 0
    per_b = lambda b, s, pt: (b, 0, 0)

    def page_spec(i):
        return pl.BlockSpec((None, PAGE_SIZE * 2 * N_HEADS, HEAD_DIM),
                            lambda b, s, pt: (pt[b, n_pages - (s + 1) * SB_PAGES + i], 0, 0))

    grid_spec = pltpu.PrefetchScalarGridSpec(
        num_scalar_prefetch=1, grid=(bd, n_steps),
        in_specs=[pl.BlockSpec((None, N_HEADS * T8, HEAD_DIM), per_b)] + [page_spec(i) for i in range(SB_PAGES)]
        + [pl.BlockSpec((None, T8, 2 * hw), per_b), pl.BlockSpec((TQ, TQ), lambda b, s, pt: (0, 0))],
        out_specs=pl.BlockSpec((None, N_HEADS * T8, HEAD_DIM), per_b),
        scratch_shapes=[pltpu.VMEM((N_HEADS * T8, LANES), F32), pltpu.VMEM((N_HEADS * T8, HEAD_DIM), F32)])
    return pl.pallas_call(
        functools.partial(_sb_sample_body, n_steps=n_steps, t_new=t_new), grid_spec=grid_spec,
        out_shape=jax.ShapeDtypeStruct((bd, N_HEADS * T8, HEAD_DIM), BF16),
        compiler_params=_params("parallel", "arbitrary"), name="sb_sample_attn")(
            page_table, q_rows, *([cache3] * SB_PAGES), new8, _suffix_matrix())


def _pad_tokens(x, bd, t_new):
    x = x.reshape(bd, t_new, x.shape[-1])
    return jnp.pad(x, ((0, 0), (0, T8 - t_new), (0, 0)))


def _bias_distances(t, past, n_chunks_s, wb):
    r = np.arange(TQ)[:, None]
    c = np.arange(LANES)[None, :]
    toeplitz = np.concatenate([r - c + off for off in (0, TQ, 2 * TQ)], axis=0)
    cmp_p = np.arange(t)[:, None] - (c * CMP_STRIDE + CMP_BLOCK - 1)
    pos_q = past + np.arange(T8)[:, None]
    cmp_s = (pos_q - (np.arange(n_chunks_s)[None, :] * CMP_STRIDE + CMP_BLOCK - 1)).reshape(-1, LANES)
    last = pos_q - (past - PAGE_SIZE + c)
    far = np.full((T8, LANES), 4 * REL_MAX_DIST)
    new = np.arange(T8)[:, None] - c
    win = (wb + np.arange(T8)[:, None] - np.arange(wb + LANES)[None, :]).reshape(-1, LANES)
    parts = [toeplitz, cmp_p, cmp_s, last, far, new, win]
    rows = sum(p.shape[0] for p in parts)
    pad = -rows % LANES
    parts.append(np.zeros((pad, LANES), np.int64))
    offs = np.cumsum([0] + [p.shape[0] for p in parts])
    return jnp.asarray(np.concatenate(parts, axis=0), I32), offs


def _sample_bias(bias_all, offs, n_chunks_s, wb):
    def rows_kgt(x, width):
        return x.reshape(N_HEADS * T8, width)
    seg = lambda i: bias_all[:, offs[i]:offs[i + 1]]
    return {"cmp": rows_kgt(seg(2), n_chunks_s), "last": rows_kgt(seg(3), LANES), "far": rows_kgt(seg(4), LANES),
            "new": rows_kgt(seg(5), LANES), "win": rows_kgt(seg(6), wb + LANES)}


def kernel(x_prompt, x_sample, cache_nsa, state_nsa_win, cache_sb, page_table, rel_bias, norm_w, final_norm_w,
           nsa_w_in, nsa_w_out, nsa_pe_k, nsa_pe_v, nsa_phi_k1, nsa_phi_k2, nsa_phi_v1, nsa_phi_v2,
           sb_w_in, sb_w_out, ffn_w1, ffn_w3, ffn_w2):
    b, t, d = x_prompt.shape
    bd, t_new, _ = x_sample.shape
    n_pages = page_table.shape[1]
    past = n_pages * PAGE_SIZE
    wb = state_nsa_win.shape[2]
    hd = N_HEADS * HEAD_DIM
    kvw = KV * HEAD_DIM
    n_chunks_s = (past + t_new) // CMP_STRIDE
    half = CMP_STRIDE * HEAD_DIM

    dist, offs = _bias_distances(t, past, n_chunks_s, wb)
    bias_all = bias_from_dist(rel_bias, dist)
    bias_s = _sample_bias(bias_all, offs, n_chunks_s, wb)

    w_in = nsa_w_in[0]
    wq = w_in[:, :hd].astype(BF16)
    wkv4 = w_in[:, hd:hd + 4 * kvw].astype(BF16)
    wwin = w_in[:, hd + 4 * kvw:hd + 6 * kvw].astype(BF16)
    wg = w_in[:, hd + 6 * kvw:].reshape(d, KV, G * 3)
    wg = jnp.pad(wg, ((0, 0), (0, 0), (0, LANES - G * 3))).reshape(d, KV * LANES).astype(BF16)
    w_out_nsa = nsa_w_out[0].astype(BF16)
    cw = {
        "w1k": jnp.concatenate([nsa_phi_k1[0, :half], nsa_phi_k1[0, half:]], axis=1).astype(BF16),
        "w1v": jnp.concatenate([nsa_phi_v1[0, :half], nsa_phi_v1[0, half:]], axis=1).astype(BF16),
        "pe": jnp.broadcast_to(jnp.stack([nsa_pe_k[0].reshape(1, -1), nsa_pe_v[0].reshape(1, -1)]),
                               (2, SUBLANES, CMP_BLOCK * HEAD_DIM)),
        "w1": jnp.stack([nsa_phi_k1[0], nsa_phi_v1[0]]),
        "w2": jnp.stack([nsa_phi_k2[0], nsa_phi_v2[0]]),
    }
    wsb = sb_w_in[0]
    wsq = wsb[:, :hd].astype(BF16)
    wskv = wsb[:, hd:].astype(BF16)
    w_out_sb = sb_w_out[0].astype(BF16)
    w1 = ffn_w1.astype(BF16)
    w3 = ffn_w3.astype(BF16)
    w2 = ffn_w2.astype(BF16)

    hp = x_prompt.reshape(b * t, d)
    hs = x_sample.reshape(bd * t_new, d)

    xp = rmsnorm_rows(hp, norm_w[0, 0], BF16)
    xs = rmsnorm_rows(hs, norm_w[0, 0], BF16)
    q_p, q_s = matmul(xp, wq, BF16), matmul(xs, wq, BF16)
    kv4_p, kv4_s = matmul(xp, wkv4, F32), matmul(xs, wkv4, F32)
    win_p, win_s = matmul(xp, wwin, F32), matmul(xs, wwin, F32)
    g_p, g_s = matmul(xp, wg, F32, sigmoid=True), matmul(xs, wg, F32, sigmoid=True)

    pt_prompt = jnp.arange(b * (t // PAGE_SIZE), dtype=I32).reshape(b, t // PAGE_SIZE)
    kcb_p = compress(kv4_p.reshape(b * t // PAGE_SIZE, PAGE_SIZE, 4 * kvw), pt_prompt, cw)
    attn_p = nsa_prompt_attention(q_p, kv4_p, win_p, g_p, kcb_p, bias_all, b, t)

    cache3 = cache_nsa[0].reshape(cache_nsa.shape[1], PAGE_SIZE * 4 * KV, HEAD_DIM)
    kcb_s, sel_rows = compress_paged(cache3, page_table, cw)
    attn_s = nsa_sample_attention(
        _pad_tokens(q_s, bd, t_new), sel_rows, kcb_s, _pad_tokens(kv4_s, bd, t_new),
        state_nsa_win[0].reshape(bd, wb * 2 * KV, HEAD_DIM), _pad_tokens(win_s, bd, t_new),
        _pad_tokens(g_s, bd, t_new), bias_s, past, t_new)
    attn_s = attn_s[:, :t_new].reshape(bd * t_new, hd)

    hp, xp = matmul_res_norm(attn_p, w_out_nsa, hp, norm_w[0, 1])
    hs, xs = matmul_res_norm(attn_s, w_out_nsa, hs, norm_w[0, 1])
    hp, xp = ffn_res_norm(xp, w1[0], w3[0], w2[0], hp, norm_w[1, 0], BF16)
    hs, xs = ffn_res_norm(xs, w1[0], w3[0], w2[0], hs, norm_w[1, 0], BF16)

    sq_p, sq_s = matmul(xp, wsq, BF16), matmul(xs, wsq, BF16)
    skv_p, skv_s = matmul(xp, wskv, F32), matmul(xs, wskv, F32)
    sattn_p = sb_prompt_attention(sq_p, skv_p, b, t)

    q_rows = (_pad_tokens(sq_s, bd, t_new).reshape(bd, T8, N_HEADS, HEAD_DIM).transpose(0, 2, 1, 3)
              .reshape(bd, N_HEADS * T8, HEAD_DIM).astype(F32))
    csb3 = cache_sb[0].reshape(cache_sb.shape[1], PAGE_SIZE * 2 * N_HEADS, HEAD_DIM)
    sattn_s = sb_sample_attention(q_rows, csb3, page_table, _pad_tokens(skv_s, bd, t_new), t_new)
    sattn_s = sattn_s.reshape(bd, N_HEADS, T8, HEAD_DIM)[:, :, :t_new].transpose(0, 2, 1, 3).reshape(bd * t_new, hd)

    hp, xp = matmul_res_norm(sattn_p, w_out_sb, hp, norm_w[1, 1])
    hs, xs = matmul_res_norm(sattn_s, w_out_sb, hs, norm_w[1, 1])
    _, y_p = ffn_res_norm(xp, w1[1], w3[1], w2[1], hp, final_norm_w, F32)
    _, y_s = ffn_res_norm(xs, w1[1], w3[1], w2[1], hs, final_norm_w, F32)

    win_keep = min(WINDOW, t)
    nsa_win_prompt = win_p.reshape(b, t, 2, KV, HEAD_DIM)[:, t - win_keep:]
    win_all = jnp.concatenate([state_nsa_win[0], win_s.reshape(bd, t_new, 2, KV, HEAD_DIM)], axis=1)
    return (y_p.reshape(b, t, d), y_s.reshape(bd, t_new, d),
            kv4_p.reshape(1, b, t, 4, KV, HEAD_DIM), kv4_s.reshape(1, bd, t_new, 4, KV, HEAD_DIM),
            nsa_win_prompt[None], win_all[:, t_new:][None],
            skv_p.reshape(1, b, t, 2, N_HEADS, HEAD_DIM), skv_s.reshape(1, bd, t_new, 2, N_HEADS, HEAD_DIM))
```

```python
import functools
import math

import numpy as np
import jax
import jax.numpy as jnp
from jax import lax
from jax.experimental import pallas as pl
from jax.experimental.pallas import tpu as pltpu

F32 = jnp.float32
BF16 = jnp.bfloat16
I32 = jnp.int32

HEAD_DIM = 128
N_HEADS = 16
NSA_KV_HEADS = 4
NSA_GROUP = N_HEADS // NSA_KV_HEADS
CMP_BLOCK = 32
CMP_STRIDE = 16
SEL_BLOCK = 64
SEL_TOPK = 16
WINDOW = 512
FORCE_BONUS = 1000.0
REL_BUCKETS = 32
REL_MAX_DIST = 128
RMS_EPS = 1e-6
NEG_INF = -1e30
SCALE = HEAD_DIM ** -0.5
PAGE_SIZE = 128

LANES = 128
SUBLANES = 8
TQ = 128
T8 = SUBLANES
VMEM_LIMIT = 56 * 1024 * 1024
KV = NSA_KV_HEADS
G = NSA_GROUP


def _t5_thresholds():
    d = np.arange(0, 4 * REL_MAX_DIST)
    max_exact = REL_BUCKETS // 2
    nf = np.maximum(d, 1).astype(np.float32)
    large = max_exact + (np.log(nf / np.float32(max_exact)) / np.float32(math.log(REL_MAX_DIST / max_exact))
                         * np.float32(REL_BUCKETS - max_exact)).astype(np.int32)
    b = np.where(d < max_exact, d, np.minimum(large, REL_BUCKETS - 1))
    return [int(np.argmax(b >= k)) for k in range(REL_BUCKETS)]


T5_THR = _t5_thresholds()
FAR_DIST = T5_THR[-1]


def _dot(a, b):
    return jnp.dot(a, b, preferred_element_type=F32)


def _dot_nt(a, b):
    return lax.dot_general(a, b, (((1,), (1,)), ((), ())), preferred_element_type=F32)


def _params(*sem):
    return pltpu.CompilerParams(dimension_semantics=sem, vmem_limit_bytes=VMEM_LIMIT)


def _rms(x, w):
    return x * lax.rsqrt(jnp.mean(x * x, axis=-1, keepdims=True) + RMS_EPS) * w


def _masked_softmax(s, mask):
    sm = jnp.where(mask, s, NEG_INF)
    m = jnp.max(sm, axis=1, keepdims=True)
    e = jnp.where(mask, jnp.exp(sm - m), 0.0)
    l = jnp.sum(e, axis=1, keepdims=True)
    return e * jnp.where(l > 0.0, 1.0 / l, 0.0)


def _safe_inv(l):
    return jnp.where(l > 0.0, 1.0 / l, 0.0)


def _rmsnorm_body(x_ref, w_ref, o_ref):
    o_ref[...] = _rms(x_ref[...], w_ref[...]).astype(o_ref.dtype)


def rmsnorm_rows(x, w, out_dtype):
    m, d = x.shape
    tm = min(m, 512)
    return pl.pallas_call(
        _rmsnorm_body, grid=(m // tm,),
        in_specs=[pl.BlockSpec((tm, d), lambda i: (i, 0)), pl.BlockSpec((1, d), lambda i: (0, 0))],
        out_specs=pl.BlockSpec((tm, d), lambda i: (i, 0)),
        out_shape=jax.ShapeDtypeStruct((m, d), out_dtype),
        compiler_params=_params("parallel"), name="rmsnorm")(x, w.reshape(1, d))


def _matmul_body(a_ref, w_ref, *o_refs, sigmoid):
    y = _dot(a_ref[...], w_ref[...])
    if sigmoid:
        y = jax.nn.sigmoid(y)
    for o_ref in o_refs:
        o_ref[...] = y.astype(o_ref.dtype)


def matmul(a, w, out_dtypes, sigmoid=False):
    m, k = a.shape
    n = w.shape[1]
    tm = min(m, 512)
    tn = min(n, 2048)
    outs = pl.pallas_call(
        functools.partial(_matmul_body, sigmoid=sigmoid), grid=(m // tm, n // tn),
        in_specs=[pl.BlockSpec((tm, k), lambda i, j: (i, 0)), pl.BlockSpec((k, tn), lambda i, j: (0, j))],
        out_specs=[pl.BlockSpec((tm, tn), lambda i, j: (i, j)) for _ in out_dtypes],
        out_shape=[jax.ShapeDtypeStruct((m, n), dt) for dt in out_dtypes],
        compiler_params=_params("parallel", "arbitrary"), name="matmul")(a, w)
    return outs if len(outs) > 1 else outs[0]


def _mm_res_norm_body(a_ref, w_ref, h_ref, nw_ref, hout_ref, xn_ref):
    h = h_ref[...] + _dot(a_ref[...], w_ref[...])
    hout_ref[...] = h
    xn_ref[...] = _rms(h, nw_ref[...]).astype(xn_ref.dtype)


def matmul_res_norm(a, w, h, nw):
    m, k = a.shape
    d = w.shape[1]
    tm = min(m, 512)
    return pl.pallas_call(
        _mm_res_norm_body, grid=(m // tm,),
        in_specs=[pl.BlockSpec((tm, k), lambda i: (i, 0)), pl.BlockSpec((k, d), lambda i: (0, 0)),
                  pl.BlockSpec((tm, d), lambda i: (i, 0)), pl.BlockSpec((1, d), lambda i: (0, 0))],
        out_specs=[pl.BlockSpec((tm, d), lambda i: (i, 0)), pl.BlockSpec((tm, d), lambda i: (i, 0))],
        out_shape=[jax.ShapeDtypeStruct((m, d), F32), jax.ShapeDtypeStruct((m, d), BF16)],
        compiler_params=_params("parallel"), name="matmul_res_norm")(a, w, h, nw.reshape(1, d))


def _ffn_body(xn_ref, w1_ref, w3_ref, w2_ref, h_ref, nw_ref, hout_ref, yn_ref, acc_ref):
    f = pl.program_id(1)

    @pl.when(f == 0)
    def _():
        acc_ref[...] = jnp.zeros_like(acc_ref)

    x = xn_ref[...]
    g = _dot(x, w1_ref[...])
    u = _dot(x, w3_ref[...])
    mid = (g * jax.nn.sigmoid(g) * u).astype(BF16)
    acc_ref[...] += _dot(mid, w2_ref[...])

    @pl.when(f == pl.num_programs(1) - 1)
    def _():
        h = h_ref[...] + acc_ref[...]
        hout_ref[...] = h
        yn_ref[...] = _rms(h, nw_ref[...]).astype(yn_ref.dtype)


def ffn_res_norm(xn, w1, w3, w2, h, nw, norm_dtype):
    m, d = xn.shape
    dff = w1.shape[1]
    tm = min(m, 512)
    tf = 512
    return pl.pallas_call(
        _ffn_body, grid=(m // tm, dff // tf),
        in_specs=[pl.BlockSpec((tm, d), lambda i, f: (i, 0)),
                  pl.BlockSpec((d, tf), lambda i, f: (0, f)), pl.BlockSpec((d, tf), lambda i, f: (0, f)),
                  pl.BlockSpec((tf, d), lambda i, f: (f, 0)),
                  pl.BlockSpec((tm, d), lambda i, f: (i, 0)), pl.BlockSpec((1, d), lambda i, f: (0, 0))],
        out_specs=[pl.BlockSpec((tm, d), lambda i, f: (i, 0)), pl.BlockSpec((tm, d), lambda i, f: (i, 0))],
        out_shape=[jax.ShapeDtypeStruct((m, d), F32), jax.ShapeDtypeStruct((m, d), norm_dtype)],
        scratch_shapes=[pltpu.VMEM((tm, d), F32)],
        compiler_params=_params("parallel", "arbitrary"), name="ffn")(xn, w1, w3, w2, h, nw.reshape(1, d))


def _bias_body(tab_ref, d_ref, o_ref):
    h = pl.program_id(0)
    d = d_ref[...]
    acc = jnp.full(d.shape, tab_ref[0, h], F32)
    for b in range(1, REL_BUCKETS):
        acc = jnp.where(d >= T5_THR[b], tab_ref[b, h], acc)
    o_ref[...] = acc


def bias_from_dist(rel_bias, dist):
    r = dist.shape[0]
    rt = r
    return pl.pallas_call(
        _bias_body, grid=(N_HEADS, r // rt),
        in_specs=[pl.BlockSpec(memory_space=pltpu.SMEM), pl.BlockSpec((rt, LANES), lambda h, i: (i, 0))],
        out_specs=pl.BlockSpec((None, rt, LANES), lambda h, i: (h, i, 0)),
        out_shape=jax.ShapeDtypeStruct((N_HEADS, r, LANES), F32),
        compiler_params=_params("parallel", "parallel"), name="rel_bias")(rel_bias, dist)


CMP_PAGES = 16


def _cmp1_body(pt_ref, *refs):
    pages = refs[:CMP_PAGES]
    perm = refs[CMP_PAGES][...]
    w_refs = refs[CMP_PAGES + 1:CMP_PAGES + 3]
    o_ref = refs[CMP_PAGES + 3]
    cpp = PAGE_SIZE // CMP_STRIDE
    for kind in range(2):
        w = w_refs[kind][...]
        for kv in range(KV):
            c0 = (kind * KV + kv) * HEAD_DIM
            xs = [_dot(perm, pg[:, c0:c0 + HEAD_DIM].astype(BF16)) for pg in pages]
            pieces = [jnp.concatenate([x[r * cpp:(r + 1) * cpp] for x in xs], axis=0).astype(BF16)
                      for r in range(CMP_STRIDE)]
            o_ref[kind * KV + kv] = _dot(jnp.concatenate(pieces, axis=1), w)


def compress_stage1(rows3, page_table, w1k, w1v):
    nb_seq, n_pages = page_table.shape
    steps = n_pages // CMP_PAGES
    cpp = PAGE_SIZE // CMP_STRIDE
    width = 2 * KV * HEAD_DIM

    def page_spec(i):
        return pl.BlockSpec((None, PAGE_SIZE, width), lambda b, s, pt: (pt[b, s * CMP_PAGES + i], 0, 0))

    wspec = pl.BlockSpec((CMP_STRIDE * HEAD_DIM, 2 * HEAD_DIM), lambda b, s, pt: (0, 0))
    src = np.arange(PAGE_SIZE)
    perm = np.zeros((PAGE_SIZE, PAGE_SIZE), np.float32)
    perm[(src % CMP_STRIDE) * cpp + src // CMP_STRIDE, src] = 1.0
    perm = jnp.asarray(perm, BF16)
    grid_spec = pltpu.PrefetchScalarGridSpec(
        num_scalar_prefetch=1, grid=(nb_seq, steps),
        in_specs=[page_spec(i) for i in range(CMP_PAGES)]
        + [pl.BlockSpec((PAGE_SIZE, PAGE_SIZE), lambda b, s, pt: (0, 0)), wspec, wspec],
        out_specs=pl.BlockSpec((None, 2 * KV, CMP_PAGES * cpp, 2 * HEAD_DIM), lambda b, s, pt: (b, 0, s, 0)))
    return pl.pallas_call(
        _cmp1_body, grid_spec=grid_spec,
        out_shape=jax.ShapeDtypeStruct((nb_seq, 2 * KV, n_pages * cpp, 2 * HEAD_DIM), F32),
        compiler_params=_params("parallel", "arbitrary"), name="compress1")(
            page_table, *([rows3] * CMP_PAGES), perm, w1k, w1v)


def _cmp1_paged_body(pt_ref, *refs):
    cmp_pages = [r.reshape(PAGE_SIZE * SUBLANES, HEAD_DIM) for r in refs[0:2 * CMP_PAGES:2]]
    sel_pages = [r.reshape(PAGE_SIZE * SUBLANES, HEAD_DIM) for r in refs[1:2 * CMP_PAGES:2]]
    w_refs = refs[2 * CMP_PAGES:2 * CMP_PAGES + 2]
    o_ref, sel_ref = refs[2 * CMP_PAGES + 2], refs[2 * CMP_PAGES + 3]
    cpp = PAGE_SIZE // CMP_STRIDE
    for kind in range(2):
        w = w_refs[kind][...]
        for kv in range(KV):
            c = kind * KV + kv
            pieces = [jnp.concatenate([pg[pl.ds(r * SUBLANES + c, cpp, stride=CMP_STRIDE * SUBLANES), :]
                                       for pg in cmp_pages], axis=0).astype(BF16) for r in range(CMP_STRIDE)]
            o_ref[c] = _dot(jnp.concatenate(pieces, axis=1), w)
    for c in range(2 * KV):
        for i, pg in enumerate(sel_pages):
            sel_ref[c, i * PAGE_SIZE:(i + 1) * PAGE_SIZE, :] = (
                pg[pl.ds(c, PAGE_SIZE, stride=SUBLANES), :].astype(BF16))


def compress_stage1_paged(cache5, page_table, w1k, w1v):
    nb_seq, n_pages = page_table.shape
    steps = n_pages // CMP_PAGES
    cpp = PAGE_SIZE // CMP_STRIDE
    assert cache5.shape[2:] == (2, 2 * KV, HEAD_DIM) and 2 * KV == SUBLANES

    def group_spec(i, g):
        return pl.BlockSpec((None, PAGE_SIZE, None, SUBLANES, HEAD_DIM),
                            lambda b, s, pt: (pt[b, s * CMP_PAGES + i], 0, g, 0, 0))

    wspec = pl.BlockSpec((CMP_STRIDE * HEAD_DIM, 2 * HEAD_DIM), lambda b, s, pt: (0, 0))
    grid_spec = pltpu.PrefetchScalarGridSpec(
        num_scalar_prefetch=1, grid=(nb_seq, steps),
        in_specs=[group_spec(i, g) for i in range(CMP_PAGES) for g in range(2)] + [wspec, wspec],
        out_specs=[pl.BlockSpec((None, 2 * KV, CMP_PAGES * cpp, 2 * HEAD_DIM), lambda b, s, pt: (b, 0, s, 0)),
                   pl.BlockSpec((None, 2 * KV, CMP_PAGES * PAGE_SIZE, HEAD_DIM), lambda b, s, pt: (b, 0, s, 0))])
    return pl.pallas_call(
        _cmp1_paged_body, grid_spec=grid_spec,
        out_shape=[jax.ShapeDtypeStruct((nb_seq, 2 * KV, n_pages * cpp, 2 * HEAD_DIM), F32),
                   jax.ShapeDtypeStruct((nb_seq, 2 * KV, n_pages * PAGE_SIZE, HEAD_DIM), BF16)],
        compiler_params=_params("parallel", "arbitrary"), name="compress1_paged")(
            page_table, *([cache5] * (2 * CMP_PAGES)), w1k, w1v)


def _cmp2_body(ab_ref, pe_ref, w1_ref, w2_ref, o_ref):
    nch = ab_ref.shape[0]
    ab = ab_ref[...]
    c = _dot(pe_ref[...].astype(BF16), w1_ref[...].astype(BF16))
    pre = ab[:, :HEAD_DIM] + pltpu.roll(ab[:, HEAD_DIM:], nch - 1, 0) + c[0:1]
    y = _dot((pre * jax.nn.sigmoid(pre)).astype(BF16), w2_ref[...].astype(BF16))
    rowi = lax.broadcasted_iota(I32, y.shape, 0)
    o_ref[...] = jnp.where(rowi < nch - 1, y, 0.0)


def compress_stage2(ab, pe, w1, w2):
    nb_seq, _, nch, _ = ab.shape
    return pl.pallas_call(
        _cmp2_body, grid=(nb_seq, 2 * KV),
        in_specs=[pl.BlockSpec((None, None, nch, 2 * HEAD_DIM), lambda b, j: (b, j, 0, 0)),
                  pl.BlockSpec((None, SUBLANES, CMP_BLOCK * HEAD_DIM), lambda b, j: (j // KV, 0, 0)),
                  pl.BlockSpec((None, CMP_BLOCK * HEAD_DIM, HEAD_DIM), lambda b, j: (j // KV, 0, 0)),
                  pl.BlockSpec((None, HEAD_DIM, HEAD_DIM), lambda b, j: (j // KV, 0, 0))],
        out_specs=pl.BlockSpec((None, None, nch, HEAD_DIM), lambda b, j: (b, j, 0, 0)),
        out_shape=jax.ShapeDtypeStruct((nb_seq, 2 * KV, nch, HEAD_DIM), F32),
        compiler_params=_params("parallel", "parallel"), name="compress2")(ab, pe, w1, w2)


def compress(rows3, page_table, cw):
    ab = compress_stage1(rows3, page_table, cw["w1k"], cw["w1v"])
    return compress_stage2(ab, cw["pe"], cw["w1"], cw["w2"])


def compress_paged(cache3, page_table, cw):
    ab, sel_rows = compress_stage1_paged(cache3, page_table, cw["w1k"], cw["w1v"])
    return compress_stage2(ab, cw["pe"], cw["w1"], cw["w2"]), sel_rows


def _flash_prob(s, mask, m_ref, l_ref, rows):
    m_prev = m_ref[rows, :]
    m_new = jnp.maximum(m_prev, jnp.max(jnp.where(mask, s, NEG_INF), axis=1, keepdims=True))
    alpha = jnp.exp(m_prev - m_new)
    m_wide = m_new if s.shape[1] == LANES else jnp.concatenate([m_new] * (s.shape[1] // LANES), axis=1)
    p = jnp.where(mask, jnp.exp(s - m_wide), 0.0)
    l_ref[rows, :] = alpha * l_ref[rows, :] + jnp.sum(p, axis=1, keepdims=True)
    m_ref[rows, :] = m_new
    return p.astype(BF16), alpha


def _flash_update(s, mask, v, m_ref, l_ref, acc_ref, rows):
    p, alpha = _flash_prob(s, mask, m_ref, l_ref, rows)
    acc_ref[rows, :] = alpha * acc_ref[rows, :] + _dot(p, v)


def _flash_init(m_ref, l_ref, acc_ref):
    m_ref[...] = jnp.full(m_ref.shape, NEG_INF, F32)
    l_ref[...] = jnp.zeros_like(l_ref)
    acc_ref[...] = jnp.zeros_like(acc_ref)


def _rank_select(score, eligible, blk, n_blocks, n_top, row_of):
    def body(j, cnt):
        rowj = row_of(j)
        beats = (rowj > score) | ((rowj == score) & (blk > j))
        return cnt + beats.astype(I32)
    cnt = lax.fori_loop(0, n_blocks, body, jnp.zeros(score.shape, I32))
    return jnp.where((cnt < n_top) & eligible, 1.0, 0.0)


def _nsa_prompt_body(q_ref, ks_ref, vs_ref, kw_ref, vw_ref, kc_ref, vc_ref, g_ref, bc_ref, bt_ref, ovl_ref, exp_ref,
                     o_ref, m_s, l_s, a_s, m_w, l_w, a_w, mk_ref, sc_ref, *, n_cmp, n_blk):
    qi = pl.program_id(2)
    rows = G * TQ
    q = jnp.concatenate([q_ref[:, g * HEAD_DIM:(g + 1) * HEAD_DIM] for g in range(G)], axis=0)
    row5 = lax.broadcasted_iota(I32, (rows, LANES), 0)
    lane5 = lax.broadcasted_iota(I32, (rows, LANES), 1)
    tpos5 = qi * TQ + (row5 & (TQ - 1))

    s = _dot_nt(q, kc_ref[...].astype(BF16)) * SCALE
    s = s + jnp.concatenate([bc_ref[g] for g in range(G)], axis=0)
    mask_c = (tpos5 >= lane5 * CMP_STRIDE + (CMP_BLOCK - 1)) & (lane5 < n_cmp)
    p_c = _masked_softmax(s, mask_c)
    pb = p_c.astype(BF16)
    o_c = _dot(pb, vc_ref[...].astype(BF16))

    imp = _dot_nt(ovl_ref[...], pb[0:TQ])
    for g in range(1, G):
        imp = imp + _dot_nt(ovl_ref[...], pb[g * TQ:(g + 1) * TQ])
    blk = lax.broadcasted_iota(I32, (LANES, TQ), 0)
    tq_l = qi * TQ + lax.broadcasted_iota(I32, (LANES, TQ), 1)
    cur = lax.shift_right_arithmetic(tq_l, int(math.log2(SEL_BLOCK)))
    eligible = blk * SEL_BLOCK <= tq_l
    forced = (blk == 0) | (blk == cur) | (blk == cur - 1)
    score = jnp.where(eligible, imp + jnp.where(forced, FORCE_BONUS, 0.0), -jnp.inf)
    sc_ref[...] = score
    sel_t = _rank_select(score, eligible, blk, n_blk, min(SEL_TOPK, n_blk), lambda j: sc_ref[pl.ds(j, 1), :])
    t_keys = mk_ref.shape[1]
    causal = (lax.broadcasted_iota(I32, (TQ, t_keys), 1)
              <= qi * TQ + lax.broadcasted_iota(I32, (TQ, t_keys), 0))
    mk_ref[...] = jnp.where(causal, _dot(sel_t.T.astype(BF16), exp_ref[...]), 0.0)

    kt2 = 2 * TQ
    lane_w = lax.broadcasted_iota(I32, (rows, kt2), 1)
    tpos_w = qi * TQ + (lax.broadcasted_iota(I32, (rows, kt2), 0) & (TQ - 1))

    def bias_pair(j):
        tiles = []
        for i in range(2):
            off = pl.multiple_of(jnp.clip(qi - (2 * j + i), 0, 2) * TQ, TQ)
            tiles.append(jnp.concatenate([bt_ref[g, pl.ds(off, TQ), :] for g in range(G)], axis=0))
        return jnp.concatenate(tiles, axis=1)

    _flash_init(m_s, l_s, a_s)
    _flash_init(m_w, l_w, a_w)
    all_rows = pl.ds(0, rows)

    def pair_step(j, with_win):
        off = pl.multiple_of(j * kt2, kt2)
        s_s = _dot_nt(q, ks_ref[pl.ds(off, kt2), :])
        if with_win:
            s_w = _dot_nt(q, kw_ref[pl.ds(off, kt2), :])
        bias = bias_pair(j)
        mask_s = jnp.concatenate([mk_ref[:, pl.ds(off, kt2)]] * G, axis=0) > 0.5
        p_s, alpha_s = _flash_prob(s_s * SCALE + bias, mask_s, m_s, l_s, all_rows)
        if with_win:
            dist = tpos_w - (off + lane_w)
            p_w, alpha_w = _flash_prob(s_w * SCALE + bias, (dist >= 0) & (dist < WINDOW), m_w, l_w, all_rows)
        a_s[...] = alpha_s * a_s[...] + _dot(p_s, vs_ref[pl.ds(off, kt2), :])
        if with_win:
            a_w[...] = alpha_w * a_w[...] + _dot(p_w, vw_ref[pl.ds(off, kt2), :])

    def sel_only(j, c):
        pair_step(j, False)
        return c

    def sel_and_win(j, c):
        pair_step(j, True)
        return c

    lo = lax.shift_right_arithmetic(jnp.maximum(qi - WINDOW // TQ, 0), 1)
    hi = lax.shift_right_arithmetic(qi, 1) + 1
    lax.fori_loop(0, lo, sel_only, 0)
    lax.fori_loop(lo, hi, sel_and_win, 0)

    o_s = a_s[...] * _safe_inv(l_s[...])
    o_w = a_w[...] * _safe_inv(l_w[...])
    gt = g_ref[...]
    for g in range(G):
        r = slice(g * TQ, (g + 1) * TQ)
        out = (gt[:, 3 * g:3 * g + 1] * o_c[r] + gt[:, 3 * g + 1:3 * g + 2] * o_s[r]
               + gt[:, 3 * g + 2:3 * g + 3] * o_w[r])
        o_ref[:, g * HEAD_DIM:(g + 1) * HEAD_DIM] = out.astype(o_ref.dtype)


def nsa_prompt_attention(q, kv4, kwin, gates, kcb, bias_all, b, t):
    nq = t // TQ
    n_chunks = t // CMP_STRIDE
    n_cmp = n_chunks - CMP_BLOCK // CMP_STRIDE + 1
    n_blk = -(-t // SEL_BLOCK)
    assert n_chunks == LANES and n_blk <= LANES and t % TQ == 0 and TQ >= FAR_DIST
    cmp_start = np.arange(LANES) * CMP_STRIDE
    jst = np.arange(LANES) * SEL_BLOCK
    overlap = ((cmp_start[:, None] < jst[None, :] + SEL_BLOCK) & (cmp_start[:, None] + CMP_BLOCK > jst[None, :])
               & (np.arange(LANES)[:, None] < n_cmp) & (np.arange(LANES)[None, :] < n_blk))
    ovl_t = jnp.asarray(overlap.T, BF16)
    expand = jnp.asarray(np.arange(LANES)[:, None] == (np.arange(t)[None, :] // SEL_BLOCK), BF16)
    kvw = KV * HEAD_DIM
    colblk = lambda base: (lambda bb, kv, qi: (bb, base + kv))
    rowblk = lambda bb, kv, qi: (bb * nq + qi, kv)
    toe = 3 * TQ // LANES
    in_specs = [
        pl.BlockSpec((TQ, G * HEAD_DIM), rowblk),
        pl.BlockSpec((t, HEAD_DIM), colblk(2 * KV)), pl.BlockSpec((t, HEAD_DIM), colblk(3 * KV)),
        pl.BlockSpec((t, HEAD_DIM), colblk(0)), pl.BlockSpec((t, HEAD_DIM), colblk(KV)),
        pl.BlockSpec((None, None, n_chunks, HEAD_DIM), lambda bb, kv, qi: (bb, kv, 0, 0)),
        pl.BlockSpec((None, None, n_chunks, HEAD_DIM), lambda bb, kv, qi: (bb, KV + kv, 0, 0)),
        pl.BlockSpec((TQ, LANES), rowblk),
        pl.BlockSpec((G, TQ, LANES), lambda bb, kv, qi: (kv, toe + qi, 0)),
        pl.BlockSpec((G, 3 * TQ, LANES), lambda bb, kv, qi: (kv, 0, 0)),
        pl.BlockSpec((LANES, LANES), lambda bb, kv, qi: (0, 0)),
        pl.BlockSpec((LANES, t), lambda bb, kv, qi: (0, 0)),
    ]
    rows = G * TQ
    scratch = [pltpu.VMEM((rows, LANES), F32) for _ in range(6)] + [pltpu.VMEM((TQ, t), F32),
                                                                   pltpu.VMEM((LANES, TQ), F32)]
    del kvw
    return pl.pallas_call(
        functools.partial(_nsa_prompt_body, n_cmp=n_cmp, n_blk=n_blk), grid=(b, KV, nq),
        in_specs=in_specs, out_specs=pl.BlockSpec((TQ, G * HEAD_DIM), rowblk),
        out_shape=jax.ShapeDtypeStruct((b * t, N_HEADS * HEAD_DIM), BF16),
        scratch_shapes=scratch,
        compiler_params=_params("parallel", "parallel", "arbitrary"), name="nsa_prompt_attn")(
            q, kv4, kv4, kwin, kwin, kcb, kcb, gates, bias_all, bias_all, ovl_t, expand)


def _stick_pre(z, mask):
    sp = jnp.log1p(jnp.exp(-jnp.abs(z)))
    log_beta = jnp.minimum(z, 0.0) - sp
    log_rest = -jnp.maximum(z, 0.0) - sp
    if mask is not None:
        log_rest = jnp.where(mask, log_rest, 0.0)
    hi = log_rest.astype(BF16)
    r1 = log_rest - hi.astype(F32)
    mid = r1.astype(BF16)
    lo = (r1 - mid.astype(F32)).astype(BF16)
    return log_beta, jnp.sum(log_rest, axis=1, keepdims=True), jnp.concatenate([hi, mid, lo], axis=0)


def _stick_post(log_beta, l3, carry, mask):
    n = log_beta.shape[0]
    a = jnp.exp(log_beta + (l3[0:n] + l3[n:2 * n] + l3[2 * n:3 * n] + carry))
    return a if mask is None else jnp.where(mask, a, 0.0)


SB_HEADS = 8


def _sb_prompt_body(q_ref, k_ref, v_ref, u_ref, o_ref, carry_ref, acc_ref):
    qi = pl.program_id(2)
    row = lax.broadcasted_iota(I32, (TQ, TQ), 0)
    lane = lax.broadcasted_iota(I32, (TQ, TQ), 1)

    def tile(kt, mask, first):
        off = pl.multiple_of(kt * TQ, TQ)
        cols = [slice(h * HEAD_DIM, (h + 1) * HEAD_DIM) for h in range(SB_HEADS)]
        zs = [_dot_nt(q_ref[:, c], k_ref[pl.ds(off, TQ), c]) * SCALE for c in cols]
        pre = [_stick_pre(z, mask) for z in zs]
        l3s = [_dot(p[2], u_ref[...]) for p in pre]
        ws = [_stick_post(p[0], l3, 0.0 if first else carry_ref[h], mask).astype(BF16)
              for h, (p, l3) in enumerate(zip(pre, l3s))]
        pvs = [_dot(w, v_ref[pl.ds(off, TQ), c]) for w, c in zip(ws, cols)]
        for h in range(SB_HEADS):
            tot = pre[h][1]
            acc_ref[h] = pvs[h] if first else acc_ref[h] + pvs[h]
            carry_ref[h] = jnp.broadcast_to(tot, (TQ, TQ)) if first else carry_ref[h] + tot

    tile(qi, lane < row, True)

    def step(i, c):
        tile(qi - 1 - i, None, False)
        return c

    lax.fori_loop(0, qi, step, 0)
    for h in range(SB_HEADS):
        o_ref[:, h * HEAD_DIM:(h + 1) * HEAD_DIM] = acc_ref[h].astype(o_ref.dtype)


def _suffix_matrix():
    j = np.arange(TQ)
    return jnp.asarray(j[:, None] > j[None, :], BF16)


def sb_prompt_attention(q, kvp, b, t):
    nq = t // TQ
    hg = N_HEADS // SB_HEADS
    w = SB_HEADS * HEAD_DIM
    rowblk = lambda bb, h, qi: (bb * nq + qi, h)
    return pl.pallas_call(
        _sb_prompt_body, grid=(b, hg, nq),
        in_specs=[pl.BlockSpec((TQ, w), rowblk),
                  pl.BlockSpec((t, w), lambda bb, h, qi: (bb, h)),
                  pl.BlockSpec((t, w), lambda bb, h, qi: (bb, hg + h)),
                  pl.BlockSpec((TQ, TQ), lambda bb, h, qi: (0, 0))],
        out_specs=pl.BlockSpec((TQ, w), rowblk),
        out_shape=jax.ShapeDtypeStruct((b * t, N_HEADS * HEAD_DIM), BF16),
        scratch_shapes=[pltpu.VMEM((SB_HEADS, TQ, TQ), F32), pltpu.VMEM((SB_HEADS, TQ, HEAD_DIM), F32)],
        compiler_params=_params("parallel", "parallel", "arbitrary"), name="sb_prompt_attn")(
            q, kvp, kvp, _suffix_matrix())


def _nsa_sample_select_body(q_ref, kcb_ref, bc_ref, mc_ref, ovl_ref, fadd_ref, elig_ref, oc_ref, sel_ref, sc_ref,
                            *, n_blk):
    nbp = kcb_ref.shape[1]
    grp = G * T8
    ps = []
    for kv in range(KV):
        qk = jnp.concatenate([q_ref[:, (kv * G + g) * HEAD_DIM:(kv * G + g + 1) * HEAD_DIM] for g in range(G)],
                             axis=0)
        s = _dot_nt(qk, kcb_ref[kv].astype(BF16)) * SCALE + bc_ref[kv * grp:(kv + 1) * grp, :]
        p = _masked_softmax(s, mc_ref[...] > 0.5)
        oc_ref[kv * grp:(kv + 1) * grp, :] = _dot(p.astype(BF16), kcb_ref[KV + kv].astype(BF16))
        ps.append(p)
    pad = jnp.zeros((LANES - KV * T8, nbp), F32)
    imp = None
    for g in range(G):
        xg = jnp.concatenate([ps[kv][g * T8:(g + 1) * T8] for kv in range(KV)] + [pad], axis=0)
        part = _dot_nt(ovl_ref[...], xg.astype(BF16))
        imp = part if imp is None else imp + part
    eligible = elig_ref[...] > 0.5
    score = jnp.where(eligible, imp + fadd_ref[...], -jnp.inf)
    sc_ref[...] = score
    blk = lax.broadcasted_iota(I32, score.shape, 0)
    sel_ref[...] = _rank_select(score, eligible, blk, n_blk, min(SEL_TOPK, n_blk),
                                lambda j: sc_ref[pl.ds(j, 1), :])


SEL_KEYS = 1024


def _nsa_sample_attn_body(q_ref, kv_ref, sel_ref, new_ref, win_ref, wnew_ref, g_ref, oc_ref,
                          bl_ref, bf_ref, bn_ref, bw_ref, ex_ref, o_ref, m_s, l_s, a_s, *, n_steps, t_new, past):
    step = pl.program_id(1)
    grp = G * T8
    kvw = KV * HEAD_DIM
    wb = win_ref.shape[0] // (2 * KV)
    is_last = step == n_steps - 1

    @pl.when(step == 0)
    def _():
        _flash_init(m_s, l_s, a_s)

    def q_of(kv):
        return jnp.concatenate([q_ref[:, (kv * G + g) * HEAD_DIM:(kv * G + g + 1) * HEAD_DIM] for g in range(G)],
                               axis=0)

    def pad_keys(x):
        return jnp.concatenate([x, jnp.zeros((TQ - x.shape[0], x.shape[1]), x.dtype)], axis=0).astype(BF16)

    def rows_of_blocks(piece):
        return jnp.concatenate([piece, jnp.zeros((LANES - piece.shape[0], LANES), F32)], axis=0).T

    bps = SEL_KEYS // SEL_BLOCK
    piece = sel_ref[pl.ds(pl.multiple_of(step * bps, bps), bps), :]
    mk = _dot(rows_of_blocks(piece).astype(BF16), ex_ref[...])

    kv_rows = [pl.ds(kv * grp, grp) for kv in range(KV)]
    scores = [_dot_nt(q_of(kv), kv_ref[kv]) for kv in range(KV)]
    probs = []
    for kv in range(KV):
        bias = jnp.where(is_last, bl_ref[kv_rows[kv], :], bf_ref[kv_rows[kv], 0:1])
        mask = jnp.concatenate([mk[kv * T8:(kv + 1) * T8]] * G, axis=0) > 0.5
        probs.append(_flash_prob(scores[kv] * SCALE + bias, mask, m_s, l_s, kv_rows[kv]))
    for kv in range(KV):
        p, alpha = probs[kv]
        a_s[kv_rows[kv], :] = alpha * a_s[kv_rows[kv], :] + _dot(p, kv_ref[KV + kv])

    @pl.when(is_last)
    def _():
        lane = lax.broadcasted_iota(I32, (grp, LANES), 1)
        t_row = lax.broadcasted_iota(I32, (grp, LANES), 0) & (T8 - 1)
        lane_w = lax.broadcasted_iota(I32, (grp, wb), 1)
        t_row_w = lax.broadcasted_iota(I32, (grp, wb), 0) & (T8 - 1)
        new_blk = past // SEL_BLOCK
        picked_all = rows_of_blocks(sel_ref[new_blk:new_blk + SUBLANES, :])
        gt = g_ref[...]
        for kv in range(KV):
            rows = pl.ds(kv * grp, grp)
            qk = q_of(kv)
            kn = pad_keys(new_ref[:, 2 * kvw + kv * HEAD_DIM:2 * kvw + (kv + 1) * HEAD_DIM])
            vn = pad_keys(new_ref[:, 3 * kvw + kv * HEAD_DIM:3 * kvw + (kv + 1) * HEAD_DIM])
            s = _dot_nt(qk, kn) * SCALE + bn_ref[rows, :]
            picked = jnp.concatenate([picked_all[kv * T8:(kv + 1) * T8, 0:1]] * G, axis=0) > 0.5
            mask = (lane <= t_row) & (lane < t_new) & picked
            _flash_update(s, mask, vn, m_s, l_s, a_s, rows)
            o_s = a_s[rows, :] * _safe_inv(l_s[rows, :])
            kw = win_ref[pl.ds(kv, wb, stride=2 * KV), :].astype(BF16)
            vw = win_ref[pl.ds(KV + kv, wb, stride=2 * KV), :].astype(BF16)
            kwn = pad_keys(wnew_ref[:, kv * HEAD_DIM:(kv + 1) * HEAD_DIM])
            vwn = pad_keys(wnew_ref[:, kvw + kv * HEAD_DIM:kvw + (kv + 1) * HEAD_DIM])
            s1 = _dot_nt(qk, kw) * SCALE + bw_ref[rows, 0:wb]
            s2 = _dot_nt(qk, kwn) * SCALE + bw_ref[rows, wb:wb + LANES]
            d1 = wb + t_row_w - lane_w
            mask1 = (d1 >= 0) & (d1 < WINDOW)
            d2 = t_row - lane
            mask2 = (d2 >= 0) & (d2 < WINDOW) & (lane < t_new)
            m = jnp.maximum(jnp.max(jnp.where(mask1, s1, NEG_INF), axis=1, keepdims=True),
                            jnp.max(jnp.where(mask2, s2, NEG_INF), axis=1, keepdims=True))
            e1 = jnp.where(mask1, jnp.exp(s1 - m), 0.0)
            e2 = jnp.where(mask2, jnp.exp(s2 - m), 0.0)
            l = jnp.sum(e1, axis=1, keepdims=True) + jnp.sum(e2, axis=1, keepdims=True)
            o_w = (_dot(e1.astype(BF16), vw) + _dot(e2.astype(BF16), vwn)) * _safe_inv(l)
            o_c = oc_ref[rows, :]
            for g in range(G):
                r = slice(g * T8, (g + 1) * T8)
                c = kv * LANES + 3 * g
                out = gt[:, c:c + 1] * o_c[r] + gt[:, c + 1:c + 2] * o_s[r] + gt[:, c + 2:c + 3] * o_w[r]
                o_ref[:, (kv * G + g) * HEAD_DIM:(kv * G + g + 1) * HEAD_DIM] = out.astype(o_ref.dtype)


def nsa_sample_attention(q8, sel_rows, kcb, new8, win3, wnew8, gates8, bias_s, past, t_new):
    bd = q8.shape[0]
    n_chunks = kcb.shape[2]
    n_cmp = n_chunks - CMP_BLOCK // CMP_STRIDE + 1
    total = past + t_new
    n_blk = -(-total // SEL_BLOCK)
    nblkp = -(-n_blk // LANES) * LANES
    wb = win3.shape[1] // (2 * KV)
    n_steps = past // SEL_KEYS
    assert past % SEL_KEYS == 0 and SEL_KEYS % SEL_BLOCK == 0 and t_new <= T8 and PAGE_SIZE > FAR_DIST
    assert (total // CMP_STRIDE) == n_chunks and wb % LANES == 0 and SEL_KEYS // SEL_BLOCK <= LANES

    pos_q = past + np.arange(T8)
    cmp_start = np.arange(n_chunks) * CMP_STRIDE
    cmp_end = cmp_start + CMP_BLOCK - 1
    mask_c = ((pos_q[:, None] >= cmp_end[None, :]) & (np.arange(n_chunks)[None, :] < n_cmp))
    mask_c = jnp.asarray(np.tile(mask_c, (G, 1)), F32)
    jst = np.arange(nblkp) * SEL_BLOCK
    overlap = ((cmp_start[:, None] < jst[None, :] + SEL_BLOCK) & (cmp_start[:, None] + CMP_BLOCK > jst[None, :])
               & (np.arange(n_chunks)[:, None] < n_cmp) & (np.arange(nblkp)[None, :] < n_blk))
    ovl_t = jnp.asarray(overlap.T, BF16)
    blk = np.arange(nblkp)[:, None]
    pq = np.tile(pos_q, KV)[None, :]
    tvalid = np.tile(np.arange(T8) < t_new, KV)[None, :]
    cur = pq // SEL_BLOCK
    elig = (blk * SEL_BLOCK <= pq) & (blk < n_blk) & tvalid
    forced = (blk == 0) | (blk == cur) | (blk == cur - 1)
    pad_l = LANES - KV * T8
    elig = jnp.asarray(np.pad(elig, ((0, 0), (0, pad_l))), F32)
    fadd = jnp.asarray(np.pad(np.where(forced, FORCE_BONUS, 0.0), ((0, 0), (0, pad_l))), F32)

    grp_all = KV * G * T8
    o_c, sel = pl.pallas_call(
        functools.partial(_nsa_sample_select_body, n_blk=n_blk), grid=(bd,),
        in_specs=[pl.BlockSpec((None, T8, N_HEADS * HEAD_DIM), lambda b: (b, 0, 0)),
                  pl.BlockSpec((None, 2 * KV, n_chunks, HEAD_DIM), lambda b: (b, 0, 0, 0)),
                  pl.BlockSpec((grp_all, n_chunks), lambda b: (0, 0)),
                  pl.BlockSpec((G * T8, n_chunks), lambda b: (0, 0)),
                  pl.BlockSpec((nblkp, n_chunks), lambda b: (0, 0)),
                  pl.BlockSpec((nblkp, LANES), lambda b: (0, 0)),
                  pl.BlockSpec((nblkp, LANES), lambda b: (0, 0))],
        out_specs=[pl.BlockSpec((None, grp_all, HEAD_DIM), lambda b: (b, 0, 0)),
                   pl.BlockSpec((None, nblkp, LANES), lambda b: (b, 0, 0))],
        out_shape=[jax.ShapeDtypeStruct((bd, grp_all, HEAD_DIM), F32),
                   jax.ShapeDtypeStruct((bd, nblkp, LANES), F32)],
        scratch_shapes=[pltpu.VMEM((nblkp, LANES), F32)],
        compiler_params=_params("parallel"), name="nsa_sample_select")(
            q8, kcb, bias_s["cmp"], mask_c, ovl_t, fadd, elig)

    kvw = KV * HEAD_DIM
    const2 = lambda b, s: (0, 0)
    per_b = lambda b, s: (b, 0, 0)
    expand = jnp.asarray(np.arange(LANES)[:, None] == (np.arange(SEL_KEYS)[None, :] // SEL_BLOCK), BF16)
    bias_last = jnp.concatenate([bias_s["far"]] * (SEL_KEYS // LANES - 1) + [bias_s["last"]], axis=1)
    return pl.pallas_call(
        functools.partial(_nsa_sample_attn_body, n_steps=n_steps, t_new=t_new, past=past), grid=(bd, n_steps),
        in_specs=[pl.BlockSpec((None, T8, N_HEADS * HEAD_DIM), per_b),
                  pl.BlockSpec((None, 2 * KV, SEL_KEYS, HEAD_DIM), lambda b, s: (b, 0, s, 0)),
                  pl.BlockSpec((None, nblkp, LANES), per_b),
                  pl.BlockSpec((None, T8, 4 * kvw), per_b),
                  pl.BlockSpec((None, wb * 2 * KV, HEAD_DIM), per_b),
                  pl.BlockSpec((None, T8, 2 * kvw), per_b),
                  pl.BlockSpec((None, T8, KV * LANES), per_b),
                  pl.BlockSpec((None, grp_all, HEAD_DIM), per_b),
                  pl.BlockSpec((grp_all, SEL_KEYS), const2), pl.BlockSpec((grp_all, LANES), const2),
                  pl.BlockSpec((grp_all, LANES), const2), pl.BlockSpec((grp_all, wb + LANES), const2),
                  pl.BlockSpec((LANES, SEL_KEYS), const2)],
        out_specs=pl.BlockSpec((None, T8, N_HEADS * HEAD_DIM), per_b),
        out_shape=jax.ShapeDtypeStruct((bd, T8, N_HEADS * HEAD_DIM), BF16),
        scratch_shapes=[pltpu.VMEM((grp_all, LANES), F32) for _ in range(3)],
        compiler_params=_params("parallel", "arbitrary"), name="nsa_sample_attn")(
            q8, sel_rows, sel, new8, win3, wnew8, gates8, o_c,
            bias_last, bias_s["far"], bias_s["new"], bias_s["win"], expand)


SB_PAGES = 4
SB_GROUPS = 2 * N_HEADS // SUBLANES


def _sb_sample_body(pt_ref, q_ref, *refs, n_steps, t_new):
    groups = [r.reshape(PAGE_SIZE * SUBLANES, HEAD_DIM) for r in refs[:SB_PAGES * SB_GROUPS]]
    new_ref, u_ref, o_ref, carry_ref, acc_ref = refs[SB_PAGES * SB_GROUPS:]
    step = pl.program_id(1)
    hw = N_HEADS * HEAD_DIM

    def head_rows(j, row):
        g = groups[j * SB_GROUPS + row // SUBLANES]
        return g[pl.ds(row % SUBLANES, PAGE_SIZE, stride=SUBLANES), :].astype(BF16)
    u = u_ref[...]

    def q_of(h):
        return q_ref[h * T8:(h + 1) * T8, :].astype(BF16)

    def run(n_sub, k_of, v_of, masks):
        zs = [jnp.concatenate([_dot_nt(q_of(h), k_of(j, h)) for h in range(N_HEADS)], axis=0) * SCALE
              for j in range(n_sub)]
        pre = [_stick_pre(zs[j], masks[j]) for j in range(n_sub)]
        l3s = [_dot(pre[j][2], u) for j in range(n_sub)]
        carry = carry_ref[...]
        ws = [None] * n_sub
        for j in reversed(range(n_sub)):
            ws[j] = _stick_post(pre[j][0], l3s[j], carry, masks[j])
            carry = carry + pre[j][1]
        carry_ref[...] = carry
        for h in range(N_HEADS):
            r = slice(h * T8, (h + 1) * T8)
            out = _dot(ws[0][r].astype(BF16), v_of(0, h))
            for j in range(1, n_sub):
                out = out + _dot(ws[j][r].astype(BF16), v_of(j, h))
            acc_ref[r, :] += out

    @pl.when(step == 0)
    def _():
        carry_ref[...] = jnp.zeros_like(carry_ref)
        acc_ref[...] = jnp.zeros_like(acc_ref)
        pad = jnp.zeros((TQ - T8, 2 * hw), F32)
        kvn = jnp.concatenate([new_ref[...], pad], axis=0)
        lane = lax.broadcasted_iota(I32, (TQ, TQ), 1)
        t_row = lax.broadcasted_iota(I32, (TQ, TQ), 0) & (T8 - 1)
        run(1, lambda j, h: kvn[:, h * HEAD_DIM:(h + 1) * HEAD_DIM].astype(BF16),
            lambda j, h: kvn[:, hw + h * HEAD_DIM:hw + (h + 1) * HEAD_DIM].astype(BF16),
            [(lane < t_row) & (lane < t_new)])

    run(SB_PAGES, lambda j, h: head_rows(j, h), lambda j, h: head_rows(j, N_HEADS + h), [None] * SB_PAGES)

    @pl.when(step == n_steps - 1)
    def _():
        o_ref[...] = acc_ref[...].astype(o_ref.dtype)


def sb_sample_attention(q_rows, cache5, page_table, new8, t_new):
    bd, n_pages = page_table.shape
    hw = N_HEADS * HEAD_DIM
    n_steps = n_pages // SB_PAGES
    assert n_pages % SB_PAGES == 0 and cache5.shape[2] == SB_GROUPS
    per_b = lambda b, s, pt: (b, 0, 0)

    def group_spec(i, g):
        return pl.BlockSpec((None, PAGE_SIZE, None, SUBLANES, HEAD_DIM),
                            lambda b, s, pt: (pt[b, n_pages - (s + 1) * SB_PAGES + i], 0, g, 0, 0))

    grid_spec = pltpu.PrefetchScalarGridSpec(
        num_scalar_prefetch=1, grid=(bd, n_steps),
        in_specs=[pl.BlockSpec((None, N_HEADS * T8, HEAD_DIM), per_b)]
        + [group_spec(i, g) for i in range(SB_PAGES) for g in range(SB_GROUPS)]
        + [pl.BlockSpec((None, T8, 2 * hw), per_b), pl.BlockSpec((TQ, TQ), lambda b, s, pt: (0, 0))],
        out_specs=pl.BlockSpec((None, N_HEADS * T8, HEAD_DIM), per_b),
        scratch_shapes=[pltpu.VMEM((N_HEADS * T8, LANES), F32), pltpu.VMEM((N_HEADS * T8, HEAD_DIM), F32)])
    return pl.pallas_call(
        functools.partial(_sb_sample_body, n_steps=n_steps, t_new=t_new), grid_spec=grid_spec,
        out_shape=jax.ShapeDtypeStruct((bd, N_HEADS * T8, HEAD_DIM), BF16),
        compiler_params=_params("parallel", "arbitrary"), name="sb_sample_attn")(
            page_table, q_rows, *([cache5] * (SB_PAGES * SB_GROUPS)), new8, _suffix_matrix())


def _pad_tokens(x, bd, t_new):
    x = x.reshape(bd, t_new, x.shape[-1])
    return jnp.pad(x, ((0, 0), (0, T8 - t_new), (0, 0)))


def _bias_distances(t, past, n_chunks_s, wb):
    r = np.arange(TQ)[:, None]
    c = np.arange(LANES)[None, :]
    toeplitz = np.concatenate([r - c + off for off in (0, TQ, 2 * TQ)], axis=0)
    cmp_p = np.arange(t)[:, None] - (c * CMP_STRIDE + CMP_BLOCK - 1)
    pos_q = past + np.arange(T8)[:, None]
    cmp_s = (pos_q - (np.arange(n_chunks_s)[None, :] * CMP_STRIDE + CMP_BLOCK - 1)).reshape(-1, LANES)
    last = pos_q - (past - PAGE_SIZE + c)
    far = np.full((T8, LANES), 4 * REL_MAX_DIST)
    new = np.arange(T8)[:, None] - c
    win = (wb + np.arange(T8)[:, None] - np.arange(wb + LANES)[None, :]).reshape(-1, LANES)
    parts = [toeplitz, cmp_p, cmp_s, last, far, new, win]
    rows = sum(p.shape[0] for p in parts)
    pad = -rows % LANES
    parts.append(np.zeros((pad, LANES), np.int64))
    offs = np.cumsum([0] + [p.shape[0] for p in parts])
    return jnp.asarray(np.concatenate(parts, axis=0), I32), offs


def _sample_bias(bias_all, offs, n_chunks_s, wb):
    def rows_kgt(x, width):
        return x.reshape(N_HEADS * T8, width)
    seg = lambda i: bias_all[:, offs[i]:offs[i + 1]]
    return {"cmp": rows_kgt(seg(2), n_chunks_s), "last": rows_kgt(seg(3), LANES), "far": rows_kgt(seg(4), LANES),
            "new": rows_kgt(seg(5), LANES), "win": rows_kgt(seg(6), wb + LANES)}


def kernel(x_prompt, x_sample, cache_nsa, state_nsa_win, cache_sb, page_table, rel_bias, norm_w, final_norm_w,
           nsa_w_in, nsa_w_out, nsa_pe_k, nsa_pe_v, nsa_phi_k1, nsa_phi_k2, nsa_phi_v1, nsa_phi_v2,
           sb_w_in, sb_w_out, ffn_w1, ffn_w3, ffn_w2):
    b, t, d = x_prompt.shape
    bd, t_new, _ = x_sample.shape
    n_pages = page_table.shape[1]
    past = n_pages * PAGE_SIZE
    wb = state_nsa_win.shape[2]
    hd = N_HEADS * HEAD_DIM
    kvw = KV * HEAD_DIM
    n_chunks_s = (past + t_new) // CMP_STRIDE
    half = CMP_STRIDE * HEAD_DIM

    dist, offs = _bias_distances(t, past, n_chunks_s, wb)
    bias_all = bias_from_dist(rel_bias, dist)
    bias_s = _sample_bias(bias_all, offs, n_chunks_s, wb)

    w_in = nsa_w_in[0]
    wq = w_in[:, :hd].astype(BF16)
    wkv4 = w_in[:, hd:hd + 4 * kvw].astype(BF16)
    wwin = w_in[:, hd + 4 * kvw:hd + 6 * kvw].astype(BF16)
    wg = w_in[:, hd + 6 * kvw:].reshape(d, KV, G * 3)
    wg = jnp.pad(wg, ((0, 0), (0, 0), (0, LANES - G * 3))).reshape(d, KV * LANES).astype(BF16)
    w_out_nsa = nsa_w_out[0].astype(BF16)
    cw = {
        "w1k": jnp.concatenate([nsa_phi_k1[0, :half], nsa_phi_k1[0, half:]], axis=1).astype(BF16),
        "w1v": jnp.concatenate([nsa_phi_v1[0, :half], nsa_phi_v1[0, half:]], axis=1).astype(BF16),
        "pe": jnp.broadcast_to(jnp.stack([nsa_pe_k[0].reshape(1, -1), nsa_pe_v[0].reshape(1, -1)]),
                               (2, SUBLANES, CMP_BLOCK * HEAD_DIM)),
        "w1": jnp.stack([nsa_phi_k1[0], nsa_phi_v1[0]]),
        "w2": jnp.stack([nsa_phi_k2[0], nsa_phi_v2[0]]),
    }
    wsb = sb_w_in[0]
    wsq = wsb[:, :hd].astype(BF16)
    wskv = wsb[:, hd:].astype(BF16)
    w_out_sb = sb_w_out[0].astype(BF16)
    w1 = ffn_w1.astype(BF16)
    w3 = ffn_w3.astype(BF16)
    w2 = ffn_w2.astype(BF16)

    hp = x_prompt.reshape(b * t, d)
    hs = x_sample.reshape(bd * t_new, d)

    xp = rmsnorm_rows(hp, norm_w[0, 0], BF16)
    xs = rmsnorm_rows(hs, norm_w[0, 0], BF16)
    q_p, q_s = matmul(xp, wq, (BF16,)), matmul(xs, wq, (BF16,))
    (kv4_p, kv4b_p), kv4_s = matmul(xp, wkv4, (F32, BF16)), matmul(xs, wkv4, (F32,))
    (win_p, winb_p), win_s = matmul(xp, wwin, (F32, BF16)), matmul(xs, wwin, (F32,))
    g_p, g_s = matmul(xp, wg, (F32,), sigmoid=True), matmul(xs, wg, (F32,), sigmoid=True)

    pt_prompt = jnp.arange(b * (t // PAGE_SIZE), dtype=I32).reshape(b, t // PAGE_SIZE)
    kcb_p = compress(kv4_p.reshape(b * t // PAGE_SIZE, PAGE_SIZE, 4 * kvw), pt_prompt, cw)
    attn_p = nsa_prompt_attention(q_p, kv4b_p, winb_p, g_p, kcb_p, bias_all, b, t)

    cache5 = cache_nsa[0].reshape(cache_nsa.shape[1], PAGE_SIZE, 2, 2 * KV, HEAD_DIM)
    kcb_s, sel_rows = compress_paged(cache5, page_table, cw)
    attn_s = nsa_sample_attention(
        _pad_tokens(q_s, bd, t_new), sel_rows, kcb_s, _pad_tokens(kv4_s, bd, t_new),
        state_nsa_win[0].reshape(bd, wb * 2 * KV, HEAD_DIM), _pad_tokens(win_s, bd, t_new),
        _pad_tokens(g_s, bd, t_new), bias_s, past, t_new)
    attn_s = attn_s[:, :t_new].reshape(bd * t_new, hd)

    hp, xp = matmul_res_norm(attn_p, w_out_nsa, hp, norm_w[0, 1])
    hs, xs = matmul_res_norm(attn_s, w_out_nsa, hs, norm_w[0, 1])
    hp, xp = ffn_res_norm(xp, w1[0], w3[0], w2[0], hp, norm_w[1, 0], BF16)
    hs, xs = ffn_res_norm(xs, w1[0], w3[0], w2[0], hs, norm_w[1, 0], BF16)

    sq_p, sq_s = matmul(xp, wsq, (BF16,)), matmul(xs, wsq, (BF16,))
    (skv_p, skvb_p), skv_s = matmul(xp, wskv, (F32, BF16)), matmul(xs, wskv, (F32,))
    sattn_p = sb_prompt_attention(sq_p, skvb_p, b, t)

    q_rows = (_pad_tokens(sq_s, bd, t_new).reshape(bd, T8, N_HEADS, HEAD_DIM).transpose(0, 2, 1, 3)
              .reshape(bd, N_HEADS * T8, HEAD_DIM).astype(F32))
    csb5 = cache_sb[0].reshape(cache_sb.shape[1], PAGE_SIZE, SB_GROUPS, SUBLANES, HEAD_DIM)
    sattn_s = sb_sample_attention(q_rows, csb5, page_table, _pad_tokens(skv_s, bd, t_new), t_new)
    sattn_s = sattn_s.reshape(bd, N_HEADS, T8, HEAD_DIM)[:, :, :t_new].transpose(0, 2, 1, 3).reshape(bd * t_new, hd)

    hp, xp = matmul_res_norm(sattn_p, w_out_sb, hp, norm_w[1, 1])
    hs, xs = matmul_res_norm(sattn_s, w_out_sb, hs, norm_w[1, 1])
    _, y_p = ffn_res_norm(xp, w1[1], w3[1], w2[1], hp, final_norm_w, F32)
    _, y_s = ffn_res_norm(xs, w1[1], w3[1], w2[1], hs, final_norm_w, F32)

    win_keep = min(WINDOW, t)
    nsa_win_prompt = win_p.reshape(b, t, 2, KV, HEAD_DIM)[:, t - win_keep:]
    win_all = jnp.concatenate([state_nsa_win[0], win_s.reshape(bd, t_new, 2, KV, HEAD_DIM)], axis=1)
    return (y_p.reshape(b, t, d), y_s.reshape(bd, t_new, d),
            kv4_p.reshape(1, b, t, 4, KV, HEAD_DIM), kv4_s.reshape(1, bd, t_new, 4, KV, HEAD_DIM),
            nsa_win_prompt[None], win_all[:, t_new:][None],
            skv_p.reshape(1, b, t, 2, N_HEADS, HEAD_DIM), skv_s.reshape(1, bd, t_new, 2, N_HEADS, HEAD_DIM))
```

```python
import functools
import math

import numpy as np
import jax
import jax.numpy as jnp
from jax import lax
from jax.experimental import pallas as pl
from jax.experimental.pallas import tpu as pltpu

F32 = jnp.float32
BF16 = jnp.bfloat16
I32 = jnp.int32

HEAD_DIM = 128
N_HEADS = 16
NSA_KV_HEADS = 4
NSA_GROUP = N_HEADS // NSA_KV_HEADS
CMP_BLOCK = 32
CMP_STRIDE = 16
SEL_BLOCK = 64
SEL_TOPK = 16
WINDOW = 512
FORCE_BONUS = 1000.0
REL_BUCKETS = 32
REL_MAX_DIST = 128
RMS_EPS = 1e-6
NEG_INF = -1e30
MASK_NEG = 2 * NEG_INF
SCALE = HEAD_DIM ** -0.5
PAGE_SIZE = 128

LANES = 128
SUBLANES = 8
TQ = 128
WIN_TILES = WINDOW // TQ + 2
T8 = SUBLANES
VMEM_LIMIT = 56 * 1024 * 1024
KV = NSA_KV_HEADS
G = NSA_GROUP


def _t5_thresholds():
    d = np.arange(0, 4 * REL_MAX_DIST)
    max_exact = REL_BUCKETS // 2
    nf = np.maximum(d, 1).astype(np.float32)
    large = max_exact + (np.log(nf / np.float32(max_exact)) / np.float32(math.log(REL_MAX_DIST / max_exact))
                         * np.float32(REL_BUCKETS - max_exact)).astype(np.int32)
    b = np.where(d < max_exact, d, np.minimum(large, REL_BUCKETS - 1))
    return [int(np.argmax(b >= k)) for k in range(REL_BUCKETS)]


T5_THR = _t5_thresholds()
FAR_DIST = T5_THR[-1]


def _dot(a, b):
    return jnp.dot(a, b, preferred_element_type=F32)


def _dot_nt(a, b):
    return lax.dot_general(a, b, (((1,), (1,)), ((), ())), preferred_element_type=F32)


def _params(*sem):
    return pltpu.CompilerParams(dimension_semantics=sem, vmem_limit_bytes=VMEM_LIMIT)


def _rms(x, w):
    return x * lax.rsqrt(jnp.mean(x * x, axis=-1, keepdims=True) + RMS_EPS) * w


def _masked_softmax(s, mask):
    sm = jnp.where(mask, s, NEG_INF)
    m = jnp.max(sm, axis=1, keepdims=True)
    e = jnp.where(mask, jnp.exp(sm - m), 0.0)
    l = jnp.sum(e, axis=1, keepdims=True)
    return e * jnp.where(l > 0.0, 1.0 / l, 0.0)


def _safe_inv(l):
    return jnp.where(l > 0.0, 1.0 / l, 0.0)


def _rmsnorm_body(x_ref, w_ref, o_ref):
    o_ref[...] = _rms(x_ref[...], w_ref[...]).astype(o_ref.dtype)


def rmsnorm_rows(x, w, out_dtype):
    m, d = x.shape
    tm = min(m, 512)
    return pl.pallas_call(
        _rmsnorm_body, grid=(m // tm,),
        in_specs=[pl.BlockSpec((tm, d), lambda i: (i, 0)), pl.BlockSpec((1, d), lambda i: (0, 0))],
        out_specs=pl.BlockSpec((tm, d), lambda i: (i, 0)),
        out_shape=jax.ShapeDtypeStruct((m, d), out_dtype),
        compiler_params=_params("parallel"), name="rmsnorm")(x, w.reshape(1, d))


def _matmul_body(a_ref, w_ref, *o_refs, sigmoid):
    y = _dot(a_ref[...], w_ref[...])
    if sigmoid:
        y = jax.nn.sigmoid(y)
    for o_ref in o_refs:
        o_ref[...] = y.astype(o_ref.dtype)


def matmul(a, w, out_dtypes, sigmoid=False):
    m, k = a.shape
    n = w.shape[1]
    tm = min(m, 512)
    tn = min(n, 2048)
    outs = pl.pallas_call(
        functools.partial(_matmul_body, sigmoid=sigmoid), grid=(m // tm, n // tn),
        in_specs=[pl.BlockSpec((tm, k), lambda i, j: (i, 0)), pl.BlockSpec((k, tn), lambda i, j: (0, j))],
        out_specs=[pl.BlockSpec((tm, tn), lambda i, j: (i, j)) for _ in out_dtypes],
        out_shape=[jax.ShapeDtypeStruct((m, n), dt) for dt in out_dtypes],
        compiler_params=_params("parallel", "arbitrary"), name="matmul")(a, w)
    return outs if len(outs) > 1 else outs[0]


def _mm_res_norm_body(a_ref, w_ref, h_ref, nw_ref, hout_ref, xn_ref):
    h = h_ref[...] + _dot(a_ref[...], w_ref[...])
    hout_ref[...] = h
    xn_ref[...] = _rms(h, nw_ref[...]).astype(xn_ref.dtype)


def matmul_res_norm(a, w, h, nw):
    m, k = a.shape
    d = w.shape[1]
    tm = min(m, 512)
    return pl.pallas_call(
        _mm_res_norm_body, grid=(m // tm,),
        in_specs=[pl.BlockSpec((tm, k), lambda i: (i, 0)), pl.BlockSpec((k, d), lambda i: (0, 0)),
                  pl.BlockSpec((tm, d), lambda i: (i, 0)), pl.BlockSpec((1, d), lambda i: (0, 0))],
        out_specs=[pl.BlockSpec((tm, d), lambda i: (i, 0)), pl.BlockSpec((tm, d), lambda i: (i, 0))],
        out_shape=[jax.ShapeDtypeStruct((m, d), F32), jax.ShapeDtypeStruct((m, d), BF16)],
        compiler_params=_params("parallel"), name="matmul_res_norm")(a, w, h, nw.reshape(1, d))


def _ffn_body(xn_ref, w1_ref, w3_ref, w2_ref, h_ref, nw_ref, hout_ref, yn_ref, acc_ref):
    f = pl.program_id(1)

    @pl.when(f == 0)
    def _():
        acc_ref[...] = jnp.zeros_like(acc_ref)

    x = xn_ref[...]
    g = _dot(x, w1_ref[...])
    u = _dot(x, w3_ref[...])
    mid = (g * jax.nn.sigmoid(g) * u).astype(BF16)
    acc_ref[...] += _dot(mid, w2_ref[...])

    @pl.when(f == pl.num_programs(1) - 1)
    def _():
        h = h_ref[...] + acc_ref[...]
        hout_ref[...] = h
        yn_ref[...] = _rms(h, nw_ref[...]).astype(yn_ref.dtype)


def ffn_res_norm(xn, w1, w3, w2, h, nw, norm_dtype):
    m, d = xn.shape
    dff = w1.shape[1]
    tm = min(m, 512)
    tf = 512
    return pl.pallas_call(
        _ffn_body, grid=(m // tm, dff // tf),
        in_specs=[pl.BlockSpec((tm, d), lambda i, f: (i, 0)),
                  pl.BlockSpec((d, tf), lambda i, f: (0, f)), pl.BlockSpec((d, tf), lambda i, f: (0, f)),
                  pl.BlockSpec((tf, d), lambda i, f: (f, 0)),
                  pl.BlockSpec((tm, d), lambda i, f: (i, 0)), pl.BlockSpec((1, d), lambda i, f: (0, 0))],
        out_specs=[pl.BlockSpec((tm, d), lambda i, f: (i, 0)), pl.BlockSpec((tm, d), lambda i, f: (i, 0))],
        out_shape=[jax.ShapeDtypeStruct((m, d), F32), jax.ShapeDtypeStruct((m, d), norm_dtype)],
        scratch_shapes=[pltpu.VMEM((tm, d), F32)],
        compiler_params=_params("parallel", "arbitrary"), name="ffn")(xn, w1, w3, w2, h, nw.reshape(1, d))


def _bias_body(tab_ref, d_ref, m_ref, o_ref):
    h = pl.program_id(0)
    d = d_ref[...]
    acc = jnp.full(d.shape, tab_ref[0, h], F32)
    for b in range(1, REL_BUCKETS):
        acc = jnp.where(d >= T5_THR[b], tab_ref[b, h], acc)
    o_ref[...] = acc + m_ref[...]


def bias_from_dist(rel_bias, dist, addmask):
    r = dist.shape[0]
    return pl.pallas_call(
        _bias_body, grid=(N_HEADS,),
        in_specs=[pl.BlockSpec(memory_space=pltpu.SMEM), pl.BlockSpec((r, LANES), lambda h: (0, 0)),
                  pl.BlockSpec((r, LANES), lambda h: (0, 0))],
        out_specs=pl.BlockSpec((None, r, LANES), lambda h: (h, 0, 0)),
        out_shape=jax.ShapeDtypeStruct((N_HEADS, r, LANES), F32),
        compiler_params=_params("parallel"), name="rel_bias")(rel_bias, dist, addmask)


CMP_PAGES = 16


def _cmp1_body(pt_ref, *refs):
    pages = refs[:CMP_PAGES]
    perm = refs[CMP_PAGES][...]
    w_refs = refs[CMP_PAGES + 1:CMP_PAGES + 3]
    o_ref = refs[CMP_PAGES + 3]
    cpp = PAGE_SIZE // CMP_STRIDE
    for kind in range(2):
        w = w_refs[kind][...]
        for kv in range(KV):
            c0 = (kind * KV + kv) * HEAD_DIM
            xs = [_dot(perm, pg[:, c0:c0 + HEAD_DIM].astype(BF16)) for pg in pages]
            pieces = [jnp.concatenate([x[r * cpp:(r + 1) * cpp] for x in xs], axis=0).astype(BF16)
                      for r in range(CMP_STRIDE)]
            o_ref[kind * KV + kv] = _dot(jnp.concatenate(pieces, axis=1), w)


def compress_stage1(rows3, page_table, w1k, w1v):
    nb_seq, n_pages = page_table.shape
    steps = n_pages // CMP_PAGES
    cpp = PAGE_SIZE // CMP_STRIDE
    width = 2 * KV * HEAD_DIM

    def page_spec(i):
        return pl.BlockSpec((None, PAGE_SIZE, width), lambda b, s, pt: (pt[b, s * CMP_PAGES + i], 0, 0))

    wspec = pl.BlockSpec((CMP_STRIDE * HEAD_DIM, 2 * HEAD_DIM), lambda b, s, pt: (0, 0))
    src = np.arange(PAGE_SIZE)
    perm = np.zeros((PAGE_SIZE, PAGE_SIZE), np.float32)
    perm[(src % CMP_STRIDE) * cpp + src // CMP_STRIDE, src] = 1.0
    perm = jnp.asarray(perm, BF16)
    grid_spec = pltpu.PrefetchScalarGridSpec(
        num_scalar_prefetch=1, grid=(nb_seq, steps),
        in_specs=[page_spec(i) for i in range(CMP_PAGES)]
        + [pl.BlockSpec((PAGE_SIZE, PAGE_SIZE), lambda b, s, pt: (0, 0)), wspec, wspec],
        out_specs=pl.BlockSpec((None, 2 * KV, CMP_PAGES * cpp, 2 * HEAD_DIM), lambda b, s, pt: (b, 0, s, 0)))
    return pl.pallas_call(
        _cmp1_body, grid_spec=grid_spec,
        out_shape=jax.ShapeDtypeStruct((nb_seq, 2 * KV, n_pages * cpp, 2 * HEAD_DIM), F32),
        compiler_params=_params("parallel", "arbitrary"), name="compress1")(
            page_table, *([rows3] * CMP_PAGES), perm, w1k, w1v)


def _cmp1_paged_body(pt_ref, *refs):
    cmp_pages = [r.reshape(PAGE_SIZE * SUBLANES, HEAD_DIM) for r in refs[0:2 * CMP_PAGES:2]]
    sel_pages = [r.reshape(PAGE_SIZE * SUBLANES, HEAD_DIM) for r in refs[1:2 * CMP_PAGES:2]]
    w_refs = refs[2 * CMP_PAGES:2 * CMP_PAGES + 2]
    o_ref, sel_ref = refs[2 * CMP_PAGES + 2], refs[2 * CMP_PAGES + 3]
    cpp = PAGE_SIZE // CMP_STRIDE
    for kind in range(2):
        w = w_refs[kind][...]
        for kv in range(KV):
            c = kind * KV + kv
            pieces = [jnp.concatenate([pg[pl.ds(r * SUBLANES + c, cpp, stride=CMP_STRIDE * SUBLANES), :]
                                       for pg in cmp_pages], axis=0).astype(BF16) for r in range(CMP_STRIDE)]
            o_ref[c] = _dot(jnp.concatenate(pieces, axis=1), w)
    for c in range(2 * KV):
        for i, pg in enumerate(sel_pages):
            sel_ref[c, i * PAGE_SIZE:(i + 1) * PAGE_SIZE, :] = (
                pg[pl.ds(c, PAGE_SIZE, stride=SUBLANES), :].astype(BF16))


def compress_stage1_paged(cache5, page_table, w1k, w1v):
    nb_seq, n_pages = page_table.shape
    steps = n_pages // CMP_PAGES
    cpp = PAGE_SIZE // CMP_STRIDE
    assert cache5.shape[2:] == (2, 2 * KV, HEAD_DIM) and 2 * KV == SUBLANES

    def group_spec(i, g):
        return pl.BlockSpec((None, PAGE_SIZE, None, SUBLANES, HEAD_DIM),
                            lambda b, s, pt: (pt[b, s * CMP_PAGES + i], 0, g, 0, 0))

    wspec = pl.BlockSpec((CMP_STRIDE * HEAD_DIM, 2 * HEAD_DIM), lambda b, s, pt: (0, 0))
    grid_spec = pltpu.PrefetchScalarGridSpec(
        num_scalar_prefetch=1, grid=(nb_seq, steps),
        in_specs=[group_spec(i, g) for i in range(CMP_PAGES) for g in range(2)] + [wspec, wspec],
        out_specs=[pl.BlockSpec((None, 2 * KV, CMP_PAGES * cpp, 2 * HEAD_DIM), lambda b, s, pt: (b, 0, s, 0)),
                   pl.BlockSpec((None, 2 * KV, CMP_PAGES * PAGE_SIZE, HEAD_DIM), lambda b, s, pt: (b, 0, s, 0))])
    return pl.pallas_call(
        _cmp1_paged_body, grid_spec=grid_spec,
        out_shape=[jax.ShapeDtypeStruct((nb_seq, 2 * KV, n_pages * cpp, 2 * HEAD_DIM), F32),
                   jax.ShapeDtypeStruct((nb_seq, 2 * KV, n_pages * PAGE_SIZE, HEAD_DIM), BF16)],
        compiler_params=_params("parallel", "arbitrary"), name="compress1_paged")(
            page_table, *([cache5] * (2 * CMP_PAGES)), w1k, w1v)


def _cmp2_body(ab_ref, pe_ref, w1_ref, w2_ref, o_ref):
    nch = ab_ref.shape[0]
    ab = ab_ref[...]
    c = _dot(pe_ref[...].astype(BF16), w1_ref[...].astype(BF16))
    pre = ab[:, :HEAD_DIM] + pltpu.roll(ab[:, HEAD_DIM:], nch - 1, 0) + c[0:1]
    y = _dot((pre * jax.nn.sigmoid(pre)).astype(BF16), w2_ref[...].astype(BF16))
    rowi = lax.broadcasted_iota(I32, y.shape, 0)
    o_ref[...] = jnp.where(rowi < nch - 1, y, 0.0)


def compress_stage2(ab, pe, w1, w2):
    nb_seq, _, nch, _ = ab.shape
    return pl.pallas_call(
        _cmp2_body, grid=(nb_seq, 2 * KV),
        in_specs=[pl.BlockSpec((None, None, nch, 2 * HEAD_DIM), lambda b, j: (b, j, 0, 0)),
                  pl.BlockSpec((None, SUBLANES, CMP_BLOCK * HEAD_DIM), lambda b, j: (j // KV, 0, 0)),
                  pl.BlockSpec((None, CMP_BLOCK * HEAD_DIM, HEAD_DIM), lambda b, j: (j // KV, 0, 0)),
                  pl.BlockSpec((None, HEAD_DIM, HEAD_DIM), lambda b, j: (j // KV, 0, 0))],
        out_specs=pl.BlockSpec((None, None, nch, HEAD_DIM), lambda b, j: (b, j, 0, 0)),
        out_shape=jax.ShapeDtypeStruct((nb_seq, 2 * KV, nch, HEAD_DIM), F32),
        compiler_params=_params("parallel", "parallel"), name="compress2")(ab, pe, w1, w2)


def compress(rows3, page_table, cw):
    ab = compress_stage1(rows3, page_table, cw["w1k"], cw["w1v"])
    return compress_stage2(ab, cw["pe"], cw["w1"], cw["w2"])


def compress_paged(cache3, page_table, cw):
    ab, sel_rows = compress_stage1_paged(cache3, page_table, cw["w1k"], cw["w1v"])
    return compress_stage2(ab, cw["pe"], cw["w1"], cw["w2"]), sel_rows


def _flash_prob(s, mask, m_ref, l_ref, rows):
    m_prev = m_ref[rows, :]
    if mask is not None:
        s = jnp.where(mask, s, MASK_NEG)
    m_new = jnp.maximum(m_prev, jnp.max(s, axis=1, keepdims=True))
    alpha = jnp.exp(m_prev - m_new)
    m_wide = m_new if s.shape[1] == LANES else jnp.concatenate([m_new] * (s.shape[1] // LANES), axis=1)
    p = jnp.exp(s - m_wide)
    l_ref[rows, :] = alpha * l_ref[rows, :] + jnp.sum(p, axis=1, keepdims=True)
    m_ref[rows, :] = m_new
    return p.astype(BF16), alpha


def _flash_update(s, mask, v, m_ref, l_ref, acc_ref, rows):
    p, alpha = _flash_prob(s, mask, m_ref, l_ref, rows)
    acc_ref[rows, :] = alpha * acc_ref[rows, :] + _dot(p, v)


def _flash_init(m_ref, l_ref, acc_ref):
    m_ref[...] = jnp.full(m_ref.shape, NEG_INF, F32)
    l_ref[...] = jnp.zeros_like(l_ref)
    acc_ref[...] = jnp.zeros_like(acc_ref)


def _rank_select(score, eligible, blk, n_blocks, n_top, row_of):
    def body(j, cnt):
        rowj = row_of(j)
        beats = (rowj > score) | ((rowj == score) & (blk > j))
        return cnt + beats.astype(I32)
    cnt = lax.fori_loop(0, n_blocks, body, jnp.zeros(score.shape, I32))
    return jnp.where((cnt < n_top) & eligible, 1.0, 0.0)


def _nsa_prompt_body(q_ref, ks_ref, vs_ref, kw_ref, vw_ref, kc_ref, vc_ref, g_ref, bc_ref, bt_ref, wbt_ref,
                     ovl_ref, exp_ref,
                     o_ref, m_s, l_s, a_s, m_w, l_w, a_w, mk_ref, sc_ref, *, n_cmp, n_blk):
    qi = pl.program_id(2)
    rows = G * TQ
    q = jnp.concatenate([q_ref[:, g * HEAD_DIM:(g + 1) * HEAD_DIM] for g in range(G)], axis=0)
    row5 = lax.broadcasted_iota(I32, (rows, LANES), 0)
    lane5 = lax.broadcasted_iota(I32, (rows, LANES), 1)
    tpos5 = qi * TQ + (row5 & (TQ - 1))

    s = _dot_nt(q, kc_ref[...].astype(BF16)) * SCALE
    s = s + jnp.concatenate([bc_ref[g] for g in range(G)], axis=0)
    mask_c = (tpos5 >= lane5 * CMP_STRIDE + (CMP_BLOCK - 1)) & (lane5 < n_cmp)
    p_c = _masked_softmax(s, mask_c)
    pb = p_c.astype(BF16)
    o_c = _dot(pb, vc_ref[...].astype(BF16))

    nbr = ovl_ref.shape[0]
    imp = _dot_nt(ovl_ref[...], pb[0:TQ])
    for g in range(1, G):
        imp = imp + _dot_nt(ovl_ref[...], pb[g * TQ:(g + 1) * TQ])
    blk = lax.broadcasted_iota(I32, (nbr, TQ), 0)
    tq_l = qi * TQ + lax.broadcasted_iota(I32, (nbr, TQ), 1)
    cur = lax.shift_right_arithmetic(tq_l, int(math.log2(SEL_BLOCK)))
    eligible = (blk * SEL_BLOCK <= tq_l) & (blk < n_blk)
    forced = (blk == 0) | (blk == cur) | (blk == cur - 1)
    score = jnp.where(eligible, imp + jnp.where(forced, FORCE_BONUS, 0.0), -jnp.inf)
    sc_ref[...] = score
    sel_t = _rank_select(score, eligible, blk, n_blk, min(SEL_TOPK, n_blk), lambda j: sc_ref[pl.ds(j, 1), :])
    unsel = jnp.concatenate([1.0 - sel_t, jnp.zeros((LANES - nbr, TQ), F32)], axis=0)
    mk_ref[...] = _dot(unsel.T.astype(BF16), exp_ref[...])
    row1 = lax.broadcasted_iota(I32, (TQ, TQ), 0)
    lane1 = lax.broadcasted_iota(I32, (TQ, TQ), 1)
    off_d = pl.multiple_of(qi * TQ, TQ)
    mk_ref[:, pl.ds(off_d, TQ)] = jnp.where(lane1 <= row1, mk_ref[:, pl.ds(off_d, TQ)], MASK_NEG)

    @pl.when(qi + 1 < pl.num_programs(2))
    def _():
        mk_ref[:, pl.ds(pl.multiple_of((qi + 1) * TQ, TQ), TQ)] = jnp.full((TQ, TQ), MASK_NEG, F32)

    kt2 = 2 * TQ

    def tile_pair(ref, j, index_of):
        tiles = []
        for i in range(2):
            off = pl.multiple_of(index_of(qi - (2 * j + i)) * TQ, TQ)
            tiles.append(jnp.concatenate([ref[g, pl.ds(off, TQ), :] for g in range(G)], axis=0))
        return jnp.concatenate(tiles, axis=1)

    _flash_init(m_s, l_s, a_s)
    _flash_init(m_w, l_w, a_w)
    all_rows = pl.ds(0, rows)

    def pair_step(j, with_win):
        off = pl.multiple_of(j * kt2, kt2)
        s_s = _dot_nt(q, ks_ref[pl.ds(off, kt2), :])
        if with_win:
            s_w = _dot_nt(q, kw_ref[pl.ds(off, kt2), :])
        bias_s = tile_pair(bt_ref, j, lambda d: jnp.clip(d, 0, 2))
        key_mask = jnp.concatenate([mk_ref[:, pl.ds(off, kt2)]] * G, axis=0)
        p_s, alpha_s = _flash_prob(s_s * SCALE + bias_s + key_mask, None, m_s, l_s, all_rows)
        if with_win:
            bias_w = tile_pair(wbt_ref, j, lambda d: jnp.where(d < 0, WIN_TILES - 1, jnp.minimum(d, WIN_TILES - 1)))
            p_w, alpha_w = _flash_prob(s_w * SCALE + bias_w, None, m_w, l_w, all_rows)
        a_s[...] = alpha_s * a_s[...] + _dot(p_s, vs_ref[pl.ds(off, kt2), :])
        if with_win:
            a_w[...] = alpha_w * a_w[...] + _dot(p_w, vw_ref[pl.ds(off, kt2), :])

    def sel_only(j, c):
        pair_step(j, False)
        return c

    def sel_and_win(j, c):
        pair_step(j, True)
        return c

    lo = lax.shift_right_arithmetic(jnp.maximum(qi - WINDOW // TQ, 0), 1)
    hi = lax.shift_right_arithmetic(qi, 1) + 1
    lax.fori_loop(0, lo, sel_only, 0)
    lax.fori_loop(lo, hi, sel_and_win, 0)

    o_s = a_s[...] * _safe_inv(l_s[...])
    o_w = a_w[...] * _safe_inv(l_w[...])
    gt = g_ref[...]
    for g in range(G):
        r = slice(g * TQ, (g + 1) * TQ)
        out = (gt[:, 3 * g:3 * g + 1] * o_c[r] + gt[:, 3 * g + 1:3 * g + 2] * o_s[r]
               + gt[:, 3 * g + 2:3 * g + 3] * o_w[r])
        o_ref[:, g * HEAD_DIM:(g + 1) * HEAD_DIM] = out.astype(o_ref.dtype)


def nsa_prompt_attention(q, kv4, kwin, gates, kcb, bias_all, b, t):
    nq = t // TQ
    n_chunks = t // CMP_STRIDE
    n_cmp = n_chunks - CMP_BLOCK // CMP_STRIDE + 1
    n_blk = -(-t // SEL_BLOCK)
    assert n_chunks == LANES and n_blk <= LANES and t % TQ == 0 and TQ >= FAR_DIST
    cmp_start = np.arange(LANES) * CMP_STRIDE
    jst = np.arange(LANES) * SEL_BLOCK
    overlap = ((cmp_start[:, None] < jst[None, :] + SEL_BLOCK) & (cmp_start[:, None] + CMP_BLOCK > jst[None, :])
               & (np.arange(LANES)[:, None] < n_cmp) & (np.arange(LANES)[None, :] < n_blk))
    nbr = -(-n_blk // (2 * SUBLANES)) * 2 * SUBLANES
    ovl_t = jnp.asarray(overlap.T[:nbr], BF16)
    expand = jnp.asarray(np.where(np.arange(LANES)[:, None] == (np.arange(t)[None, :] // SEL_BLOCK), MASK_NEG, 0.0),
                         BF16)
    colblk = lambda base: (lambda bb, kv, qi: (bb, base + kv))
    rowblk = lambda bb, kv, qi: (bb * nq + qi, kv)
    in_specs = [
        pl.BlockSpec((TQ, G * HEAD_DIM), rowblk),
        pl.BlockSpec((t, HEAD_DIM), colblk(2 * KV)), pl.BlockSpec((t, HEAD_DIM), colblk(3 * KV)),
        pl.BlockSpec((t, HEAD_DIM), colblk(0)), pl.BlockSpec((t, HEAD_DIM), colblk(KV)),
        pl.BlockSpec((None, None, n_chunks, HEAD_DIM), lambda bb, kv, qi: (bb, kv, 0, 0)),
        pl.BlockSpec((None, None, n_chunks, HEAD_DIM), lambda bb, kv, qi: (bb, KV + kv, 0, 0)),
        pl.BlockSpec((TQ, LANES), rowblk),
        pl.BlockSpec((G, TQ, LANES), lambda bb, kv, qi: (kv, WIN_TILES + 3 + qi, 0)),
        pl.BlockSpec((G, 3 * TQ, LANES), lambda bb, kv, qi: (kv, WIN_TILES // 3, 0)),
        pl.BlockSpec((G, WIN_TILES * TQ, LANES), lambda bb, kv, qi: (kv, 0, 0)),
        pl.BlockSpec((nbr, LANES), lambda bb, kv, qi: (0, 0)),
        pl.BlockSpec((LANES, t), lambda bb, kv, qi: (0, 0)),
    ]
    assert WIN_TILES % 3 == 0 and TQ == LANES
    rows = G * TQ
    scratch = [pltpu.VMEM((rows, LANES), F32) for _ in range(6)] + [pltpu.VMEM((TQ, t), F32),
                                                                   pltpu.VMEM((nbr, TQ), F32)]
    return pl.pallas_call(
        functools.partial(_nsa_prompt_body, n_cmp=n_cmp, n_blk=n_blk), grid=(b, KV, nq),
        in_specs=in_specs, out_specs=pl.BlockSpec((TQ, G * HEAD_DIM), rowblk),
        out_shape=jax.ShapeDtypeStruct((b * t, N_HEADS * HEAD_DIM), BF16),
        scratch_shapes=scratch,
        compiler_params=_params("parallel", "parallel", "arbitrary"), name="nsa_prompt_attn")(
            q, kv4, kv4, kwin, kwin, kcb, kcb, gates, bias_all, bias_all, bias_all, ovl_t, expand)


def _stick_pre(z, mask):
    sp = jnp.log(1.0 + jnp.exp(-jnp.abs(z)))
    log_beta = jnp.minimum(z, 0.0) - sp
    rest = jnp.maximum(z, 0.0) + sp
    if mask is not None:
        rest = jnp.where(mask, rest, 0.0)
    hi = rest.astype(BF16)
    lo = (rest - hi.astype(F32)).astype(BF16)
    return log_beta, jnp.sum(rest, axis=1, keepdims=True), jnp.concatenate([hi, lo], axis=0)


def _stick_post(log_beta, l2, carry, mask):
    n = log_beta.shape[0]
    a = jnp.exp(log_beta - (l2[0:n] + l2[n:2 * n] + carry))
    return a if mask is None else jnp.where(mask, a, 0.0)


SB_HEADS = 8


def _sb_prompt_body(q_ref, k_ref, v_ref, u_ref, o_ref, carry_ref, acc_ref):
    qi = pl.program_id(2)
    row = lax.broadcasted_iota(I32, (TQ, TQ), 0)
    lane = lax.broadcasted_iota(I32, (TQ, TQ), 1)

    def tile(kt, mask, first):
        off = pl.multiple_of(kt * TQ, TQ)
        cols = [slice(h * HEAD_DIM, (h + 1) * HEAD_DIM) for h in range(SB_HEADS)]
        zs = [_dot_nt(q_ref[:, c], k_ref[pl.ds(off, TQ), c]) * SCALE for c in cols]
        pre = [_stick_pre(z, mask) for z in zs]
        l3s = [_dot(p[2], u_ref[...]) for p in pre]
        ws = [_stick_post(p[0], l3, 0.0 if first else carry_ref[h], mask).astype(BF16)
              for h, (p, l3) in enumerate(zip(pre, l3s))]
        pvs = [_dot(w, v_ref[pl.ds(off, TQ), c]) for w, c in zip(ws, cols)]
        for h in range(SB_HEADS):
            tot = pre[h][1]
            acc_ref[h] = pvs[h] if first else acc_ref[h] + pvs[h]
            carry_ref[h] = jnp.broadcast_to(tot, (TQ, TQ)) if first else carry_ref[h] + tot

    tile(qi, lane < row, True)

    def step(i, c):
        tile(qi - 1 - i, None, False)
        return c

    lax.fori_loop(0, qi, step, 0)
    for h in range(SB_HEADS):
        o_ref[:, h * HEAD_DIM:(h + 1) * HEAD_DIM] = acc_ref[h].astype(o_ref.dtype)


def _suffix_matrix():
    j = np.arange(TQ)
    return jnp.asarray(j[:, None] > j[None, :], BF16)


def sb_prompt_attention(q, kvp, b, t):
    nq = t // TQ
    hg = N_HEADS // SB_HEADS
    w = SB_HEADS * HEAD_DIM
    rowblk = lambda bb, h, qi: (bb * nq + qi, h)
    return pl.pallas_call(
        _sb_prompt_body, grid=(b, hg, nq),
        in_specs=[pl.BlockSpec((TQ, w), rowblk),
                  pl.BlockSpec((t, w), lambda bb, h, qi: (bb, h)),
                  pl.BlockSpec((t, w), lambda bb, h, qi: (bb, hg + h)),
                  pl.BlockSpec((TQ, TQ), lambda bb, h, qi: (0, 0))],
        out_specs=pl.BlockSpec((TQ, w), rowblk),
        out_shape=jax.ShapeDtypeStruct((b * t, N_HEADS * HEAD_DIM), BF16),
        scratch_shapes=[pltpu.VMEM((SB_HEADS, TQ, TQ), F32), pltpu.VMEM((SB_HEADS, TQ, HEAD_DIM), F32)],
        compiler_params=_params("parallel", "parallel", "arbitrary"), name="sb_prompt_attn")(
            q, kvp, kvp, _suffix_matrix())


def _nsa_sample_select_body(q_ref, kcb_ref, bc_ref, mc_ref, ovl_ref, fadd_ref, elig_ref, oc_ref, sel_ref, sc_ref,
                            *, n_blk):
    nbp = kcb_ref.shape[1]
    grp = G * T8
    ps = []
    for kv in range(KV):
        qk = jnp.concatenate([q_ref[:, (kv * G + g) * HEAD_DIM:(kv * G + g + 1) * HEAD_DIM] for g in range(G)],
                             axis=0)
        s = _dot_nt(qk, kcb_ref[kv].astype(BF16)) * SCALE + bc_ref[kv * grp:(kv + 1) * grp, :]
        p = _masked_softmax(s, mc_ref[...] > 0.5)
        oc_ref[kv * grp:(kv + 1) * grp, :] = _dot(p.astype(BF16), kcb_ref[KV + kv].astype(BF16))
        ps.append(p)
    pad = jnp.zeros((LANES - KV * T8, nbp), F32)
    imp = None
    for g in range(G):
        xg = jnp.concatenate([ps[kv][g * T8:(g + 1) * T8] for kv in range(KV)] + [pad], axis=0)
        part = _dot_nt(ovl_ref[...], xg.astype(BF16))
        imp = part if imp is None else imp + part
    eligible = elig_ref[...] > 0.5
    score = jnp.where(eligible, imp + fadd_ref[...], -jnp.inf)
    sc_ref[...] = score
    blk = lax.broadcasted_iota(I32, score.shape, 0)
    sel_ref[...] = _rank_select(score, eligible, blk, n_blk, min(SEL_TOPK, n_blk),
                                lambda j: sc_ref[pl.ds(j, 1), :])


SEL_KEYS = 1024


def _nsa_sample_attn_body(q_ref, kv_ref, sel_ref, new_ref, win_ref, wnew_ref, g_ref, oc_ref,
                          bl_ref, bf_ref, bn_ref, bw_ref, ex_ref, o_ref, m_s, l_s, a_s, *, n_steps, t_new, past):
    step = pl.program_id(1)
    grp = G * T8
    kvw = KV * HEAD_DIM
    wb = win_ref.shape[0] // (2 * KV)
    is_last = step == n_steps - 1

    @pl.when(step == 0)
    def _():
        _flash_init(m_s, l_s, a_s)

    def q_of(kv):
        return jnp.concatenate([q_ref[:, (kv * G + g) * HEAD_DIM:(kv * G + g + 1) * HEAD_DIM] for g in range(G)],
                               axis=0)

    def pad_keys(x):
        return jnp.concatenate([x, jnp.zeros((TQ - x.shape[0], x.shape[1]), x.dtype)], axis=0).astype(BF16)

    def rows_of_blocks(piece):
        return jnp.concatenate([piece, jnp.zeros((LANES - piece.shape[0], LANES), F32)], axis=0).T

    bps = SEL_KEYS // SEL_BLOCK
    piece = sel_ref[pl.ds(pl.multiple_of(step * bps, bps), bps), :]
    mk = _dot(rows_of_blocks(piece).astype(BF16), ex_ref[...])

    kv_rows = [pl.ds(kv * grp, grp) for kv in range(KV)]
    scores = [_dot_nt(q_of(kv), kv_ref[kv]) for kv in range(KV)]
    probs = []
    for kv in range(KV):
        bias = jnp.where(is_last, bl_ref[kv_rows[kv], :], bf_ref[kv_rows[kv], 0:1])
        mask = jnp.concatenate([mk[kv * T8:(kv + 1) * T8]] * G, axis=0) > 0.5
        probs.append(_flash_prob(scores[kv] * SCALE + bias, mask, m_s, l_s, kv_rows[kv]))
    for kv in range(KV):
        p, alpha = probs[kv]
        a_s[kv_rows[kv], :] = alpha * a_s[kv_rows[kv], :] + _dot(p, kv_ref[KV + kv])

    @pl.when(is_last)
    def _():
        lane = lax.broadcasted_iota(I32, (grp, LANES), 1)
        t_row = lax.broadcasted_iota(I32, (grp, LANES), 0) & (T8 - 1)
        lane_w = lax.broadcasted_iota(I32, (grp, wb), 1)
        t_row_w = lax.broadcasted_iota(I32, (grp, wb), 0) & (T8 - 1)
        new_blk = past // SEL_BLOCK
        picked_all = rows_of_blocks(sel_ref[new_blk:new_blk + SUBLANES, :])
        gt = g_ref[...]
        for kv in range(KV):
            rows = pl.ds(kv * grp, grp)
            qk = q_of(kv)
            kn = pad_keys(new_ref[:, 2 * kvw + kv * HEAD_DIM:2 * kvw + (kv + 1) * HEAD_DIM])
            vn = pad_keys(new_ref[:, 3 * kvw + kv * HEAD_DIM:3 * kvw + (kv + 1) * HEAD_DIM])
            s = _dot_nt(qk, kn) * SCALE + bn_ref[rows, :]
            picked = jnp.concatenate([picked_all[kv * T8:(kv + 1) * T8, 0:1]] * G, axis=0) > 0.5
            mask = (lane <= t_row) & (lane < t_new) & picked
            _flash_update(s, mask, vn, m_s, l_s, a_s, rows)
            o_s = a_s[rows, :] * _safe_inv(l_s[rows, :])
            kw = win_ref[pl.ds(kv, wb, stride=2 * KV), :].astype(BF16)
            vw = win_ref[pl.ds(KV + kv, wb, stride=2 * KV), :].astype(BF16)
            kwn = pad_keys(wnew_ref[:, kv * HEAD_DIM:(kv + 1) * HEAD_DIM])
            vwn = pad_keys(wnew_ref[:, kvw + kv * HEAD_DIM:kvw + (kv + 1) * HEAD_DIM])
            s1 = _dot_nt(qk, kw) * SCALE + bw_ref[rows, 0:wb]
            s2 = _dot_nt(qk, kwn) * SCALE + bw_ref[rows, wb:wb + LANES]
            d1 = wb + t_row_w - lane_w
            mask1 = (d1 >= 0) & (d1 < WINDOW)
            d2 = t_row - lane
            mask2 = (d2 >= 0) & (d2 < WINDOW) & (lane < t_new)
            m = jnp.maximum(jnp.max(jnp.where(mask1, s1, NEG_INF), axis=1, keepdims=True),
                            jnp.max(jnp.where(mask2, s2, NEG_INF), axis=1, keepdims=True))
            e1 = jnp.where(mask1, jnp.exp(s1 - m), 0.0)
            e2 = jnp.where(mask2, jnp.exp(s2 - m), 0.0)
            l = jnp.sum(e1, axis=1, keepdims=True) + jnp.sum(e2, axis=1, keepdims=True)
            o_w = (_dot(e1.astype(BF16), vw) + _dot(e2.astype(BF16), vwn)) * _safe_inv(l)
            o_c = oc_ref[rows, :]
            for g in range(G):
                r = slice(g * T8, (g + 1) * T8)
                c = kv * LANES + 3 * g
                out = gt[:, c:c + 1] * o_c[r] + gt[:, c + 1:c + 2] * o_s[r] + gt[:, c + 2:c + 3] * o_w[r]
                o_ref[:, (kv * G + g) * HEAD_DIM:(kv * G + g + 1) * HEAD_DIM] = out.astype(o_ref.dtype)


def nsa_sample_attention(q8, sel_rows, kcb, new8, win3, wnew8, gates8, bias_s, past, t_new):
    bd = q8.shape[0]
    n_chunks = kcb.shape[2]
    n_cmp = n_chunks - CMP_BLOCK // CMP_STRIDE + 1
    total = past + t_new
    n_blk = -(-total // SEL_BLOCK)
    nblkp = -(-(n_blk + SUBLANES) // (2 * SUBLANES)) * 2 * SUBLANES
    wb = win3.shape[1] // (2 * KV)
    n_steps = past // SEL_KEYS
    assert past % SEL_KEYS == 0 and SEL_KEYS % SEL_BLOCK == 0 and t_new <= T8 and PAGE_SIZE > FAR_DIST
    assert (total // CMP_STRIDE) == n_chunks and wb % LANES == 0 and SEL_KEYS // SEL_BLOCK <= LANES

    pos_q = past + np.arange(T8)
    cmp_start = np.arange(n_chunks) * CMP_STRIDE
    cmp_end = cmp_start + CMP_BLOCK - 1
    mask_c = ((pos_q[:, None] >= cmp_end[None, :]) & (np.arange(n_chunks)[None, :] < n_cmp))
    mask_c = jnp.asarray(np.tile(mask_c, (G, 1)), F32)
    jst = np.arange(nblkp) * SEL_BLOCK
    overlap = ((cmp_start[:, None] < jst[None, :] + SEL_BLOCK) & (cmp_start[:, None] + CMP_BLOCK > jst[None, :])
               & (np.arange(n_chunks)[:, None] < n_cmp) & (np.arange(nblkp)[None, :] < n_blk))
    ovl_t = jnp.asarray(overlap.T, BF16)
    blk = np.arange(nblkp)[:, None]
    pq = np.tile(pos_q, KV)[None, :]
    tvalid = np.tile(np.arange(T8) < t_new, KV)[None, :]
    cur = pq // SEL_BLOCK
    elig = (blk * SEL_BLOCK <= pq) & (blk < n_blk) & tvalid
    forced = (blk == 0) | (blk == cur) | (blk == cur - 1)
    pad_l = LANES - KV * T8
    elig = jnp.asarray(np.pad(elig, ((0, 0), (0, pad_l))), F32)
    fadd = jnp.asarray(np.pad(np.where(forced, FORCE_BONUS, 0.0), ((0, 0), (0, pad_l))), F32)

    grp_all = KV * G * T8
    o_c, sel = pl.pallas_call(
        functools.partial(_nsa_sample_select_body, n_blk=n_blk), grid=(bd,),
        in_specs=[pl.BlockSpec((None, T8, N_HEADS * HEAD_DIM), lambda b: (b, 0, 0)),
                  pl.BlockSpec((None, 2 * KV, n_chunks, HEAD_DIM), lambda b: (b, 0, 0, 0)),
                  pl.BlockSpec((grp_all, n_chunks), lambda b: (0, 0)),
                  pl.BlockSpec((G * T8, n_chunks), lambda b: (0, 0)),
                  pl.BlockSpec((nblkp, n_chunks), lambda b: (0, 0)),
                  pl.BlockSpec((nblkp, LANES), lambda b: (0, 0)),
                  pl.BlockSpec((nblkp, LANES), lambda b: (0, 0))],
        out_specs=[pl.BlockSpec((None, grp_all, HEAD_DIM), lambda b: (b, 0, 0)),
                   pl.BlockSpec((None, nblkp, LANES), lambda b: (b, 0, 0))],
        out_shape=[jax.ShapeDtypeStruct((bd, grp_all, HEAD_DIM), F32),
                   jax.ShapeDtypeStruct((bd, nblkp, LANES), F32)],
        scratch_shapes=[pltpu.VMEM((nblkp, LANES), F32)],
        compiler_params=_params("parallel"), name="nsa_sample_select")(
            q8, kcb, bias_s["cmp"], mask_c, ovl_t, fadd, elig)

    kvw = KV * HEAD_DIM
    const2 = lambda b, s: (0, 0)
    per_b = lambda b, s: (b, 0, 0)
    expand = jnp.asarray(np.arange(LANES)[:, None] == (np.arange(SEL_KEYS)[None, :] // SEL_BLOCK), BF16)
    bias_last = jnp.concatenate([bias_s["far"]] * (SEL_KEYS // LANES - 1) + [bias_s["last"]], axis=1)
    return pl.pallas_call(
        functools.partial(_nsa_sample_attn_body, n_steps=n_steps, t_new=t_new, past=past), grid=(bd, n_steps),
        in_specs=[pl.BlockSpec((None, T8, N_HEADS * HEAD_DIM), per_b),
                  pl.BlockSpec((None, 2 * KV, SEL_KEYS, HEAD_DIM), lambda b, s: (b, 0, s, 0)),
                  pl.BlockSpec((None, nblkp, LANES), per_b),
                  pl.BlockSpec((None, T8, 4 * kvw), per_b),
                  pl.BlockSpec((None, wb * 2 * KV, HEAD_DIM), per_b),
                  pl.BlockSpec((None, T8, 2 * kvw), per_b),
                  pl.BlockSpec((None, T8, KV * LANES), per_b),
                  pl.BlockSpec((None, grp_all, HEAD_DIM), per_b),
                  pl.BlockSpec((grp_all, SEL_KEYS), const2), pl.BlockSpec((grp_all, LANES), const2),
                  pl.BlockSpec((grp_all, LANES), const2), pl.BlockSpec((grp_all, wb + LANES), const2),
                  pl.BlockSpec((LANES, SEL_KEYS), const2)],
        out_specs=pl.BlockSpec((None, T8, N_HEADS * HEAD_DIM), per_b),
        out_shape=jax.ShapeDtypeStruct((bd, T8, N_HEADS * HEAD_DIM), BF16),
        scratch_shapes=[pltpu.VMEM((grp_all, LANES), F32) for _ in range(3)],
        compiler_params=_params("parallel", "arbitrary"), name="nsa_sample_attn")(
            q8, sel_rows, sel, new8, win3, wnew8, gates8, o_c,
            bias_last, bias_s["far"], bias_s["new"], bias_s["win"], expand)


SB_PAGES = 4
SB_GROUPS = 2 * N_HEADS // SUBLANES


def _sb_sample_body(pt_ref, q_ref, *refs, n_steps, t_new):
    groups = [r.reshape(PAGE_SIZE * SUBLANES, HEAD_DIM) for r in refs[:SB_PAGES * SB_GROUPS]]
    new_ref, u_ref, o_ref, carry_ref, acc_ref = refs[SB_PAGES * SB_GROUPS:]
    step = pl.program_id(1)
    hw = N_HEADS * HEAD_DIM

    def head_rows(j, row):
        g = groups[j * SB_GROUPS + row // SUBLANES]
        return g[pl.ds(row % SUBLANES, PAGE_SIZE, stride=SUBLANES), :].astype(BF16)
    u = u_ref[...]

    def q_of(h):
        return q_ref[h * T8:(h + 1) * T8, :].astype(BF16)

    def run(n_sub, k_of, v_of, masks):
        zs = [jnp.concatenate([_dot_nt(q_of(h), k_of(j, h)) for h in range(N_HEADS)], axis=0) * SCALE
              for j in range(n_sub)]
        pre = [_stick_pre(zs[j], masks[j]) for j in range(n_sub)]
        l3s = [_dot(pre[j][2], u) for j in range(n_sub)]
        carry = carry_ref[...]
        ws = [None] * n_sub
        for j in reversed(range(n_sub)):
            ws[j] = _stick_post(pre[j][0], l3s[j], carry, masks[j])
            carry = carry + pre[j][1]
        carry_ref[...] = carry
        for h in range(N_HEADS):
            r = slice(h * T8, (h + 1) * T8)
            out = _dot(ws[0][r].astype(BF16), v_of(0, h))
            for j in range(1, n_sub):
                out = out + _dot(ws[j][r].astype(BF16), v_of(j, h))
            acc_ref[r, :] += out

    @pl.when(step == 0)
    def _():
        carry_ref[...] = jnp.zeros_like(carry_ref)
        acc_ref[...] = jnp.zeros_like(acc_ref)
        pad = jnp.zeros((TQ - T8, 2 * hw), F32)
        kvn = jnp.concatenate([new_ref[...], pad], axis=0)
        lane = lax.broadcasted_iota(I32, (TQ, TQ), 1)
        t_row = lax.broadcasted_iota(I32, (TQ, TQ), 0) & (T8 - 1)
        run(1, lambda j, h: kvn[:, h * HEAD_DIM:(h + 1) * HEAD_DIM].astype(BF16),
            lambda j, h: kvn[:, hw + h * HEAD_DIM:hw + (h + 1) * HEAD_DIM].astype(BF16),
            [(lane < t_row) & (lane < t_new)])

    run(SB_PAGES, lambda j, h: head_rows(j, h), lambda j, h: head_rows(j, N_HEADS + h), [None] * SB_PAGES)

    @pl.when(step == n_steps - 1)
    def _():
        o_ref[...] = acc_ref[...].astype(o_ref.dtype)


def sb_sample_attention(q_rows, cache5, page_table, new8, t_new):
    bd, n_pages = page_table.shape
    hw = N_HEADS * HEAD_DIM
    n_steps = n_pages // SB_PAGES
    assert n_pages % SB_PAGES == 0 and cache5.shape[2] == SB_GROUPS
    per_b = lambda b, s, pt: (b, 0, 0)

    def group_spec(i, g):
        return pl.BlockSpec((None, PAGE_SIZE, None, SUBLANES, HEAD_DIM),
                            lambda b, s, pt: (pt[b, n_pages - (s + 1) * SB_PAGES + i], 0, g, 0, 0))

    grid_spec = pltpu.PrefetchScalarGridSpec(
        num_scalar_prefetch=1, grid=(bd, n_steps),
        in_specs=[pl.BlockSpec((None, N_HEADS * T8, HEAD_DIM), per_b)]
        + [group_spec(i, g) for i in range(SB_PAGES) for g in range(SB_GROUPS)]
        + [pl.BlockSpec((None, T8, 2 * hw), per_b), pl.BlockSpec((TQ, TQ), lambda b, s, pt: (0, 0))],
        out_specs=pl.BlockSpec((None, N_HEADS * T8, HEAD_DIM), per_b),
        scratch_shapes=[pltpu.VMEM((N_HEADS * T8, LANES), F32), pltpu.VMEM((N_HEADS * T8, HEAD_DIM), F32)])
    return pl.pallas_call(
        functools.partial(_sb_sample_body, n_steps=n_steps, t_new=t_new), grid_spec=grid_spec,
        out_shape=jax.ShapeDtypeStruct((bd, N_HEADS * T8, HEAD_DIM), BF16),
        compiler_params=_params("parallel", "arbitrary"), name="sb_sample_attn")(
            page_table, q_rows, *([cache5] * (SB_PAGES * SB_GROUPS)), new8, _suffix_matrix())


def _pad_tokens(x, bd, t_new):
    x = x.reshape(bd, t_new, x.shape[-1])
    return jnp.pad(x, ((0, 0), (0, T8 - t_new), (0, 0)))


def _bias_distances(t, past, n_chunks_s, wb):
    r = np.arange(TQ)[:, None]
    c = np.arange(LANES)[None, :]
    win_tiles = np.concatenate([r - c + off * TQ for off in range(WIN_TILES)], axis=0)
    toeplitz = np.concatenate([r - c + off * TQ for off in range(3)], axis=0)
    cmp_p = np.arange(t)[:, None] - (c * CMP_STRIDE + CMP_BLOCK - 1)
    pos_q = past + np.arange(T8)[:, None]
    cmp_s = (pos_q - (np.arange(n_chunks_s)[None, :] * CMP_STRIDE + CMP_BLOCK - 1)).reshape(-1, LANES)
    last = pos_q - (past - PAGE_SIZE + c)
    far = np.full((T8, LANES), 4 * REL_MAX_DIST)
    new = np.arange(T8)[:, None] - c
    win = (wb + np.arange(T8)[:, None] - np.arange(wb + LANES)[None, :]).reshape(-1, LANES)
    parts = [win_tiles, toeplitz, cmp_p, cmp_s, last, far, new, win]
    rows = sum(p.shape[0] for p in parts)
    pad = -rows % LANES
    parts.append(np.zeros((pad, LANES), np.int64))
    offs = np.cumsum([0] + [p.shape[0] for p in parts])
    dist = np.concatenate(parts, axis=0)
    addmask = np.zeros(dist.shape, np.float32)
    addmask[:win_tiles.shape[0]] = np.where((win_tiles >= 0) & (win_tiles < WINDOW), 0.0, NEG_INF)
    return jnp.asarray(dist, I32), jnp.asarray(addmask), offs


def _sample_bias(bias_all, offs, n_chunks_s, wb):
    def rows_kgt(x, width):
        return x.reshape(N_HEADS * T8, width)
    seg = lambda i: bias_all[:, offs[i]:offs[i + 1]]
    return {"cmp": rows_kgt(seg(3), n_chunks_s), "last": rows_kgt(seg(4), LANES), "far": rows_kgt(seg(5), LANES),
            "new": rows_kgt(seg(6), LANES), "win": rows_kgt(seg(7), wb + LANES)}


def kernel(x_prompt, x_sample, cache_nsa, state_nsa_win, cache_sb, page_table, rel_bias, norm_w, final_norm_w,
           nsa_w_in, nsa_w_out, nsa_pe_k, nsa_pe_v, nsa_phi_k1, nsa_phi_k2, nsa_phi_v1, nsa_phi_v2,
           sb_w_in, sb_w_out, ffn_w1, ffn_w3, ffn_w2):
    b, t, d = x_prompt.shape
    bd, t_new, _ = x_sample.shape
    n_pages = page_table.shape[1]
    past = n_pages * PAGE_SIZE
    wb = state_nsa_win.shape[2]
    hd = N_HEADS * HEAD_DIM
    kvw = KV * HEAD_DIM
    n_chunks_s = (past + t_new) // CMP_STRIDE
    half = CMP_STRIDE * HEAD_DIM

    dist, addmask, offs = _bias_distances(t, past, n_chunks_s, wb)
    bias_all = bias_from_dist(rel_bias, dist, addmask)
    bias_s = _sample_bias(bias_all, offs, n_chunks_s, wb)

    w_in = nsa_w_in[0]
    wq = w_in[:, :hd].astype(BF16)
    wkv4 = w_in[:, hd:hd + 4 * kvw].astype(BF16)
    wwin = w_in[:, hd + 4 * kvw:hd + 6 * kvw].astype(BF16)
    wg = w_in[:, hd + 6 * kvw:].reshape(d, KV, G * 3)
    wg = jnp.pad(wg, ((0, 0), (0, 0), (0, LANES - G * 3))).reshape(d, KV * LANES).astype(BF16)
    w_out_nsa = nsa_w_out[0].astype(BF16)
    cw = {
        "w1k": jnp.concatenate([nsa_phi_k1[0, :half], nsa_phi_k1[0, half:]], axis=1).astype(BF16),
        "w1v": jnp.concatenate([nsa_phi_v1[0, :half], nsa_phi_v1[0, half:]], axis=1).astype(BF16),
        "pe": jnp.broadcast_to(jnp.stack([nsa_pe_k[0].reshape(1, -1), nsa_pe_v[0].reshape(1, -1)]),
                               (2, SUBLANES, CMP_BLOCK * HEAD_DIM)),
        "w1": jnp.stack([nsa_phi_k1[0], nsa_phi_v1[0]]),
        "w2": jnp.stack([nsa_phi_k2[0], nsa_phi_v2[0]]),
    }
    wsb = sb_w_in[0]
    wsq = wsb[:, :hd].astype(BF16)
    wskv = wsb[:, hd:].astype(BF16)
    w_out_sb = sb_w_out[0].astype(BF16)
    w1 = ffn_w1.astype(BF16)
    w3 = ffn_w3.astype(BF16)
    w2 = ffn_w2.astype(BF16)

    hp = x_prompt.reshape(b * t, d)
    hs = x_sample.reshape(bd * t_new, d)

    xp = rmsnorm_rows(hp, norm_w[0, 0], BF16)
    xs = rmsnorm_rows(hs, norm_w[0, 0], BF16)
    q_p, q_s = matmul(xp, wq, (BF16,)), matmul(xs, wq, (BF16,))
    (kv4_p, kv4b_p), kv4_s = matmul(xp, wkv4, (F32, BF16)), matmul(xs, wkv4, (F32,))
    (win_p, winb_p), win_s = matmul(xp, wwin, (F32, BF16)), matmul(xs, wwin, (F32,))
    g_p, g_s = matmul(xp, wg, (F32,), sigmoid=True), matmul(xs, wg, (F32,), sigmoid=True)

    pt_prompt = jnp.arange(b * (t // PAGE_SIZE), dtype=I32).reshape(b, t // PAGE_SIZE)
    kcb_p = compress(kv4_p.reshape(b * t // PAGE_SIZE, PAGE_SIZE, 4 * kvw), pt_prompt, cw)
    attn_p = nsa_prompt_attention(q_p, kv4b_p, winb_p, g_p, kcb_p, bias_all, b, t)

    cache5 = cache_nsa[0].reshape(cache_nsa.shape[1], PAGE_SIZE, 2, 2 * KV, HEAD_DIM)
    kcb_s, sel_rows = compress_paged(cache5, page_table, cw)
    attn_s = nsa_sample_attention(
        _pad_tokens(q_s, bd, t_new), sel_rows, kcb_s, _pad_tokens(kv4_s, bd, t_new),
        state_nsa_win[0].reshape(bd, wb * 2 * KV, HEAD_DIM), _pad_tokens(win_s, bd, t_new),
        _pad_tokens(g_s, bd, t_new), bias_s, past, t_new)
    attn_s = attn_s[:, :t_new].reshape(bd * t_new, hd)

    hp, xp = matmul_res_norm(attn_p, w_out_nsa, hp, norm_w[0, 1])
    hs, xs = matmul_res_norm(attn_s, w_out_nsa, hs, norm_w[0, 1])
    hp, xp = ffn_res_norm(xp, w1[0], w3[0], w2[0], hp, norm_w[1, 0], BF16)
    hs, xs = ffn_res_norm(xs, w1[0], w3[0], w2[0], hs, norm_w[1, 0], BF16)

    sq_p, sq_s = matmul(xp, wsq, (BF16,)), matmul(xs, wsq, (BF16,))
    (skv_p, skvb_p), skv_s = matmul(xp, wskv, (F32, BF16)), matmul(xs, wskv, (F32,))
    sattn_p = sb_prompt_attention(sq_p, skvb_p, b, t)

    q_rows = (_pad_tokens(sq_s, bd, t_new).reshape(bd, T8, N_HEADS, HEAD_DIM).transpose(0, 2, 1, 3)
              .reshape(bd, N_HEADS * T8, HEAD_DIM).astype(F32))
    csb5 = cache_sb[0].reshape(cache_sb.shape[1], PAGE_SIZE, SB_GROUPS, SUBLANES, HEAD_DIM)
    sattn_s = sb_sample_attention(q_rows, csb5, page_table, _pad_tokens(skv_s, bd, t_new), t_new)
    sattn_s = sattn_s.reshape(bd, N_HEADS, T8, HEAD_DIM)[:, :, :t_new].transpose(0, 2, 1, 3).reshape(bd * t_new, hd)

    hp, xp = matmul_res_norm(sattn_p, w_out_sb, hp, norm_w[1, 1])
    hs, xs = matmul_res_norm(sattn_s, w_out_sb, hs, norm_w[1, 1])
    _, y_p = ffn_res_norm(xp, w1[1], w3[1], w2[1], hp, final_norm_w, F32)
    _, y_s = ffn_res_norm(xs, w1[1], w3[1], w2[1], hs, final_norm_w, F32)

    win_keep = min(WINDOW, t)
    nsa_win_prompt = win_p.reshape(b, t, 2, KV, HEAD_DIM)[:, t - win_keep:]
    win_all = jnp.concatenate([state_nsa_win[0], win_s.reshape(bd, t_new, 2, KV, HEAD_DIM)], axis=1)
    return (y_p.reshape(b, t, d), y_s.reshape(bd, t_new, d),
            kv4_p.reshape(1, b, t, 4, KV, HEAD_DIM), kv4_s.reshape(1, bd, t_new, 4, KV, HEAD_DIM),
            nsa_win_prompt[None], win_all[:, t_new:][None],
            skv_p.reshape(1, b, t, 2, N_HEADS, HEAD_DIM), skv_s.reshape(1, bd, t_new, 2, N_HEADS, HEAD_DIM))
```

```python
import functools
import math

import numpy as np
import jax
import jax.numpy as jnp
from jax import lax
from jax.experimental import pallas as pl
from jax.experimental.pallas import tpu as pltpu

F32 = jnp.float32
BF16 = jnp.bfloat16
I32 = jnp.int32

HEAD_DIM = 128
N_HEADS = 16
NSA_KV_HEADS = 4
NSA_GROUP = N_HEADS // NSA_KV_HEADS
CMP_BLOCK = 32
CMP_STRIDE = 16
SEL_BLOCK = 64
SEL_TOPK = 16
WINDOW = 512
FORCE_BONUS = 1000.0
REL_BUCKETS = 32
REL_MAX_DIST = 128
RMS_EPS = 1e-6
NEG_INF = -1e30
MASK_NEG = 2 * NEG_INF
SCALE = HEAD_DIM ** -0.5
PAGE_SIZE = 128

LANES = 128
SUBLANES = 8
TQ = 128
WIN_TILES = WINDOW // TQ + 2
T8 = SUBLANES
VMEM_LIMIT = 56 * 1024 * 1024
KV = NSA_KV_HEADS
G = NSA_GROUP


def _t5_thresholds():
    d = np.arange(0, 4 * REL_MAX_DIST)
    max_exact = REL_BUCKETS // 2
    nf = np.maximum(d, 1).astype(np.float32)
    large = max_exact + (np.log(nf / np.float32(max_exact)) / np.float32(math.log(REL_MAX_DIST / max_exact))
                         * np.float32(REL_BUCKETS - max_exact)).astype(np.int32)
    b = np.where(d < max_exact, d, np.minimum(large, REL_BUCKETS - 1))
    return [int(np.argmax(b >= k)) for k in range(REL_BUCKETS)]


T5_THR = _t5_thresholds()
FAR_DIST = T5_THR[-1]


def _dot(a, b):
    return jnp.dot(a, b, preferred_element_type=F32)


def _dot_nt(a, b):
    return lax.dot_general(a, b, (((1,), (1,)), ((), ())), preferred_element_type=F32)


def _params(*sem):
    return pltpu.CompilerParams(dimension_semantics=sem, vmem_limit_bytes=VMEM_LIMIT)


def _rms(x, w):
    return x * lax.rsqrt(jnp.mean(x * x, axis=-1, keepdims=True) + RMS_EPS) * w


def _masked_softmax(s, mask):
    sm = jnp.where(mask, s, NEG_INF)
    m = jnp.max(sm, axis=1, keepdims=True)
    e = jnp.where(mask, jnp.exp(sm - m), 0.0)
    l = jnp.sum(e, axis=1, keepdims=True)
    return e * jnp.where(l > 0.0, 1.0 / l, 0.0)


def _safe_inv(l):
    return jnp.where(l > 0.0, 1.0 / l, 0.0)


def _rmsnorm_body(x_ref, w_ref, o_ref):
    o_ref[...] = _rms(x_ref[...], w_ref[...]).astype(o_ref.dtype)


def rmsnorm_rows(x, w, out_dtype):
    m, d = x.shape
    tm = min(m, 512)
    return pl.pallas_call(
        _rmsnorm_body, grid=(m // tm,),
        in_specs=[pl.BlockSpec((tm, d), lambda i: (i, 0)), pl.BlockSpec((1, d), lambda i: (0, 0))],
        out_specs=pl.BlockSpec((tm, d), lambda i: (i, 0)),
        out_shape=jax.ShapeDtypeStruct((m, d), out_dtype),
        compiler_params=_params("parallel"), name="rmsnorm")(x, w.reshape(1, d))


def _matmul_body(a_ref, w_ref, *o_refs, sigmoid):
    y = _dot(a_ref[...], w_ref[...])
    if sigmoid:
        y = jax.nn.sigmoid(y)
    for o_ref in o_refs:
        o_ref[...] = y.astype(o_ref.dtype)


def matmul(a, w, out_dtypes, sigmoid=False):
    m, k = a.shape
    n = w.shape[1]
    tm = min(m, 512)
    tn = min(n, 2048)
    outs = pl.pallas_call(
        functools.partial(_matmul_body, sigmoid=sigmoid), grid=(m // tm, n // tn),
        in_specs=[pl.BlockSpec((tm, k), lambda i, j: (i, 0)), pl.BlockSpec((k, tn), lambda i, j: (0, j))],
        out_specs=[pl.BlockSpec((tm, tn), lambda i, j: (i, j)) for _ in out_dtypes],
        out_shape=[jax.ShapeDtypeStruct((m, n), dt) for dt in out_dtypes],
        compiler_params=_params("parallel", "arbitrary"), name="matmul")(a, w)
    return outs if len(outs) > 1 else outs[0]


def _mm_res_norm_body(a_ref, w_ref, h_ref, nw_ref, hout_ref, xn_ref):
    h = h_ref[...] + _dot(a_ref[...], w_ref[...])
    hout_ref[...] = h
    xn_ref[...] = _rms(h, nw_ref[...]).astype(xn_ref.dtype)


def matmul_res_norm(a, w, h, nw):
    m, k = a.shape
    d = w.shape[1]
    tm = min(m, 512)
    return pl.pallas_call(
        _mm_res_norm_body, grid=(m // tm,),
        in_specs=[pl.BlockSpec((tm, k), lambda i: (i, 0)), pl.BlockSpec((k, d), lambda i: (0, 0)),
                  pl.BlockSpec((tm, d), lambda i: (i, 0)), pl.BlockSpec((1, d), lambda i: (0, 0))],
        out_specs=[pl.BlockSpec((tm, d), lambda i: (i, 0)), pl.BlockSpec((tm, d), lambda i: (i, 0))],
        out_shape=[jax.ShapeDtypeStruct((m, d), F32), jax.ShapeDtypeStruct((m, d), BF16)],
        compiler_params=_params("parallel"), name="matmul_res_norm")(a, w, h, nw.reshape(1, d))


def _ffn_body(xn_ref, w1_ref, w3_ref, w2_ref, h_ref, nw_ref, hout_ref, yn_ref, acc_ref):
    f = pl.program_id(1)

    @pl.when(f == 0)
    def _():
        acc_ref[...] = jnp.zeros_like(acc_ref)

    x = xn_ref[...]
    g = _dot(x, w1_ref[...])
    u = _dot(x, w3_ref[...])
    mid = (g * jax.nn.sigmoid(g) * u).astype(BF16)
    acc_ref[...] += _dot(mid, w2_ref[...])

    @pl.when(f == pl.num_programs(1) - 1)
    def _():
        h = h_ref[...] + acc_ref[...]
        hout_ref[...] = h
        yn_ref[...] = _rms(h, nw_ref[...]).astype(yn_ref.dtype)


def ffn_res_norm(xn, w1, w3, w2, h, nw, norm_dtype):
    m, d = xn.shape
    dff = w1.shape[1]
    tm = min(m, 512)
    tf = 512
    return pl.pallas_call(
        _ffn_body, grid=(m // tm, dff // tf),
        in_specs=[pl.BlockSpec((tm, d), lambda i, f: (i, 0)),
                  pl.BlockSpec((d, tf), lambda i, f: (0, f)), pl.BlockSpec((d, tf), lambda i, f: (0, f)),
                  pl.BlockSpec((tf, d), lambda i, f: (f, 0)),
                  pl.BlockSpec((tm, d), lambda i, f: (i, 0)), pl.BlockSpec((1, d), lambda i, f: (0, 0))],
        out_specs=[pl.BlockSpec((tm, d), lambda i, f: (i, 0)), pl.BlockSpec((tm, d), lambda i, f: (i, 0))],
        out_shape=[jax.ShapeDtypeStruct((m, d), F32), jax.ShapeDtypeStruct((m, d), norm_dtype)],
        scratch_shapes=[pltpu.VMEM((tm, d), F32)],
        compiler_params=_params("parallel", "arbitrary"), name="ffn")(xn, w1, w3, w2, h, nw.reshape(1, d))


def _bias_body(tab_ref, d_ref, m_ref, o_ref):
    h = pl.program_id(0)
    d = d_ref[...]
    acc = jnp.full(d.shape, tab_ref[0, h], F32)
    for b in range(1, REL_BUCKETS):
        acc = jnp.where(d >= T5_THR[b], tab_ref[b, h], acc)
    o_ref[...] = acc + m_ref[...]


def bias_from_dist(rel_bias, dist, addmask):
    r = dist.shape[0]
    return pl.pallas_call(
        _bias_body, grid=(N_HEADS,),
        in_specs=[pl.BlockSpec(memory_space=pltpu.SMEM), pl.BlockSpec((r, LANES), lambda h: (0, 0)),
                  pl.BlockSpec((r, LANES), lambda h: (0, 0))],
        out_specs=pl.BlockSpec((None, r, LANES), lambda h: (h, 0, 0)),
        out_shape=jax.ShapeDtypeStruct((N_HEADS, r, LANES), F32),
        compiler_params=_params("parallel"), name="rel_bias")(rel_bias, dist, addmask)


CMP_PAGES = 16


def _cmp1_body(pt_ref, *refs):
    pages = refs[:CMP_PAGES]
    perm = refs[CMP_PAGES][...]
    w_refs = refs[CMP_PAGES + 1:CMP_PAGES + 3]
    o_ref = refs[CMP_PAGES + 3]
    cpp = PAGE_SIZE // CMP_STRIDE
    for kind in range(2):
        w = w_refs[kind][...]
        for kv in range(KV):
            c0 = (kind * KV + kv) * HEAD_DIM
            xs = [_dot(perm, pg[:, c0:c0 + HEAD_DIM].astype(BF16)) for pg in pages]
            pieces = [jnp.concatenate([x[r * cpp:(r + 1) * cpp] for x in xs], axis=0).astype(BF16)
                      for r in range(CMP_STRIDE)]
            o_ref[kind * KV + kv] = _dot(jnp.concatenate(pieces, axis=1), w)


def compress_stage1(rows3, page_table, w1k, w1v):
    nb_seq, n_pages = page_table.shape
    steps = n_pages // CMP_PAGES
    cpp = PAGE_SIZE // CMP_STRIDE
    width = 2 * KV * HEAD_DIM

    def page_spec(i):
        return pl.BlockSpec((None, PAGE_SIZE, width), lambda b, s, pt: (pt[b, s * CMP_PAGES + i], 0, 0))

    wspec = pl.BlockSpec((CMP_STRIDE * HEAD_DIM, 2 * HEAD_DIM), lambda b, s, pt: (0, 0))
    src = np.arange(PAGE_SIZE)
    perm = np.zeros((PAGE_SIZE, PAGE_SIZE), np.float32)
    perm[(src % CMP_STRIDE) * cpp + src // CMP_STRIDE, src] = 1.0
    perm = jnp.asarray(perm, BF16)
    grid_spec = pltpu.PrefetchScalarGridSpec(
        num_scalar_prefetch=1, grid=(nb_seq, steps),
        in_specs=[page_spec(i) for i in range(CMP_PAGES)]
        + [pl.BlockSpec((PAGE_SIZE, PAGE_SIZE), lambda b, s, pt: (0, 0)), wspec, wspec],
        out_specs=pl.BlockSpec((None, 2 * KV, CMP_PAGES * cpp, 2 * HEAD_DIM), lambda b, s, pt: (b, 0, s, 0)))
    return pl.pallas_call(
        _cmp1_body, grid_spec=grid_spec,
        out_shape=jax.ShapeDtypeStruct((nb_seq, 2 * KV, n_pages * cpp, 2 * HEAD_DIM), F32),
        compiler_params=_params("parallel", "arbitrary"), name="compress1")(
            page_table, *([rows3] * CMP_PAGES), perm, w1k, w1v)


def _cmp1_paged_body(pt_ref, *refs):
    cmp_pages = [r.reshape(PAGE_SIZE * SUBLANES, HEAD_DIM) for r in refs[0:2 * CMP_PAGES:2]]
    sel_pages = [r.reshape(PAGE_SIZE * SUBLANES, HEAD_DIM) for r in refs[1:2 * CMP_PAGES:2]]
    w_refs = refs[2 * CMP_PAGES:2 * CMP_PAGES + 2]
    o_ref, sel_ref = refs[2 * CMP_PAGES + 2], refs[2 * CMP_PAGES + 3]
    cpp = PAGE_SIZE // CMP_STRIDE
    for kind in range(2):
        w = w_refs[kind][...]
        for kv in range(KV):
            c = kind * KV + kv
            pieces = [jnp.concatenate([pg[pl.ds(r * SUBLANES + c, cpp, stride=CMP_STRIDE * SUBLANES), :]
                                       for pg in cmp_pages], axis=0).astype(BF16) for r in range(CMP_STRIDE)]
            o_ref[c] = _dot(jnp.concatenate(pieces, axis=1), w)
    for c in range(2 * KV):
        for i, pg in enumerate(sel_pages):
            sel_ref[c, i * PAGE_SIZE:(i + 1) * PAGE_SIZE, :] = (
                pg[pl.ds(c, PAGE_SIZE, stride=SUBLANES), :].astype(BF16))


def compress_stage1_paged(cache5, page_table, w1k, w1v):
    nb_seq, n_pages = page_table.shape
    steps = n_pages // CMP_PAGES
    cpp = PAGE_SIZE // CMP_STRIDE
    assert cache5.shape[2:] == (2, 2 * KV, HEAD_DIM) and 2 * KV == SUBLANES

    def group_spec(i, g):
        return pl.BlockSpec((None, PAGE_SIZE, None, SUBLANES, HEAD_DIM),
                            lambda b, s, pt: (pt[b, s * CMP_PAGES + i], 0, g, 0, 0))

    wspec = pl.BlockSpec((CMP_STRIDE * HEAD_DIM, 2 * HEAD_DIM), lambda b, s, pt: (0, 0))
    grid_spec = pltpu.PrefetchScalarGridSpec(
        num_scalar_prefetch=1, grid=(nb_seq, steps),
        in_specs=[group_spec(i, g) for i in range(CMP_PAGES) for g in range(2)] + [wspec, wspec],
        out_specs=[pl.BlockSpec((None, 2 * KV, CMP_PAGES * cpp, 2 * HEAD_DIM), lambda b, s, pt: (b, 0, s, 0)),
                   pl.BlockSpec((None, 2 * KV, CMP_PAGES * PAGE_SIZE, HEAD_DIM), lambda b, s, pt: (b, 0, s, 0))])
    return pl.pallas_call(
        _cmp1_paged_body, grid_spec=grid_spec,
        out_shape=[jax.ShapeDtypeStruct((nb_seq, 2 * KV, n_pages * cpp, 2 * HEAD_DIM), F32),
                   jax.ShapeDtypeStruct((nb_seq, 2 * KV, n_pages * PAGE_SIZE, HEAD_DIM), BF16)],
        compiler_params=_params("parallel", "arbitrary"), name="compress1_paged")(
            page_table, *([cache5] * (2 * CMP_PAGES)), w1k, w1v)


def _cmp2_body(ab_ref, pe_ref, w1_ref, w2_ref, o_ref):
    nch = ab_ref.shape[0]
    ab = ab_ref[...]
    c = _dot(pe_ref[...].astype(BF16), w1_ref[...].astype(BF16))
    pre = ab[:, :HEAD_DIM] + pltpu.roll(ab[:, HEAD_DIM:], nch - 1, 0) + c[0:1]
    y = _dot((pre * jax.nn.sigmoid(pre)).astype(BF16), w2_ref[...].astype(BF16))
    rowi = lax.broadcasted_iota(I32, y.shape, 0)
    o_ref[...] = jnp.where(rowi < nch - 1, y, 0.0)


def compress_stage2(ab, pe, w1, w2):
    nb_seq, _, nch, _ = ab.shape
    return pl.pallas_call(
        _cmp2_body, grid=(nb_seq, 2 * KV),
        in_specs=[pl.BlockSpec((None, None, nch, 2 * HEAD_DIM), lambda b, j: (b, j, 0, 0)),
                  pl.BlockSpec((None, SUBLANES, CMP_BLOCK * HEAD_DIM), lambda b, j: (j // KV, 0, 0)),
                  pl.BlockSpec((None, CMP_BLOCK * HEAD_DIM, HEAD_DIM), lambda b, j: (j // KV, 0, 0)),
                  pl.BlockSpec((None, HEAD_DIM, HEAD_DIM), lambda b, j: (j // KV, 0, 0))],
        out_specs=pl.BlockSpec((None, None, nch, HEAD_DIM), lambda b, j: (b, j, 0, 0)),
        out_shape=jax.ShapeDtypeStruct((nb_seq, 2 * KV, nch, HEAD_DIM), F32),
        compiler_params=_params("parallel", "parallel"), name="compress2")(ab, pe, w1, w2)


def compress(rows3, page_table, cw):
    ab = compress_stage1(rows3, page_table, cw["w1k"], cw["w1v"])
    return compress_stage2(ab, cw["pe"], cw["w1"], cw["w2"])


def compress_paged(cache3, page_table, cw):
    ab, sel_rows = compress_stage1_paged(cache3, page_table, cw["w1k"], cw["w1v"])
    return compress_stage2(ab, cw["pe"], cw["w1"], cw["w2"]), sel_rows


def _flash_prob(s, mask, m_ref, l_ref, rows):
    m_prev = m_ref[rows, :]
    if mask is not None:
        s = jnp.where(mask, s, MASK_NEG)
    m_new = jnp.maximum(m_prev, jnp.max(s, axis=1, keepdims=True))
    alpha = jnp.exp(m_prev - m_new)
    m_wide = m_new if s.shape[1] == LANES else jnp.concatenate([m_new] * (s.shape[1] // LANES), axis=1)
    p = jnp.exp(s - m_wide)
    l_ref[rows, :] = alpha * l_ref[rows, :] + jnp.sum(p, axis=1, keepdims=True)
    m_ref[rows, :] = m_new
    return p.astype(BF16), alpha


def _flash_update(s, mask, v, m_ref, l_ref, acc_ref, rows):
    p, alpha = _flash_prob(s, mask, m_ref, l_ref, rows)
    acc_ref[rows, :] = alpha * acc_ref[rows, :] + _dot(p, v)


def _flash_init(m_ref, l_ref, acc_ref):
    m_ref[...] = jnp.full(m_ref.shape, NEG_INF, F32)
    l_ref[...] = jnp.zeros_like(l_ref)
    acc_ref[...] = jnp.zeros_like(acc_ref)


def _rank_select(score, eligible, blk, n_blocks, n_top, row_of):
    def body(j, cnt):
        rowj = row_of(j)
        beats = (rowj > score) | ((rowj == score) & (blk > j))
        return cnt + beats.astype(I32)
    cnt = lax.fori_loop(0, n_blocks, body, jnp.zeros(score.shape, I32))
    return jnp.where((cnt < n_top) & eligible, 1.0, 0.0)


def _nsa_prompt_body(q_ref, ks_ref, vs_ref, kw_ref, vw_ref, kc_ref, vc_ref, g_ref, bc_ref, bt_ref, wbt_ref,
                     ovl_ref, exp_ref,
                     o_ref, m_s, l_s, a_s, m_w, l_w, a_w, mk_ref, sc_ref, *, n_cmp, n_blk):
    qi = pl.program_id(2)
    rows = G * TQ
    q = jnp.concatenate([q_ref[:, g * HEAD_DIM:(g + 1) * HEAD_DIM] for g in range(G)], axis=0)
    row5 = lax.broadcasted_iota(I32, (rows, LANES), 0)
    lane5 = lax.broadcasted_iota(I32, (rows, LANES), 1)
    tpos5 = qi * TQ + (row5 & (TQ - 1))

    s = _dot_nt(q, kc_ref[...].astype(BF16)) * SCALE
    s = s + jnp.concatenate([bc_ref[g] for g in range(G)], axis=0)
    mask_c = (tpos5 >= lane5 * CMP_STRIDE + (CMP_BLOCK - 1)) & (lane5 < n_cmp)
    p_c = _masked_softmax(s, mask_c)
    pb = p_c.astype(BF16)
    o_c = _dot(pb, vc_ref[...].astype(BF16))

    nbr = ovl_ref.shape[0]
    imp = _dot_nt(ovl_ref[...], pb[0:TQ])
    for g in range(1, G):
        imp = imp + _dot_nt(ovl_ref[...], pb[g * TQ:(g + 1) * TQ])
    blk = lax.broadcasted_iota(I32, (nbr, TQ), 0)
    tq_l = qi * TQ + lax.broadcasted_iota(I32, (nbr, TQ), 1)
    cur = lax.shift_right_arithmetic(tq_l, int(math.log2(SEL_BLOCK)))
    eligible = (blk * SEL_BLOCK <= tq_l) & (blk < n_blk)
    forced = (blk == 0) | (blk == cur) | (blk == cur - 1)
    score = jnp.where(eligible, imp + jnp.where(forced, FORCE_BONUS, 0.0), -jnp.inf)
    sc_ref[...] = score
    sel_t = _rank_select(score, eligible, blk, n_blk, min(SEL_TOPK, n_blk), lambda j: sc_ref[pl.ds(j, 1), :])
    unsel = jnp.concatenate([1.0 - sel_t, jnp.zeros((LANES - nbr, TQ), F32)], axis=0)
    mk_ref[...] = _dot(unsel.T.astype(BF16), exp_ref[...])
    row1 = lax.broadcasted_iota(I32, (TQ, TQ), 0)
    lane1 = lax.broadcasted_iota(I32, (TQ, TQ), 1)
    off_d = pl.multiple_of(qi * TQ, TQ)
    mk_ref[:, pl.ds(off_d, TQ)] = jnp.where(lane1 <= row1, mk_ref[:, pl.ds(off_d, TQ)], MASK_NEG)

    @pl.when(qi + 1 < pl.num_programs(2))
    def _():
        mk_ref[:, pl.ds(pl.multiple_of((qi + 1) * TQ, TQ), TQ)] = jnp.full((TQ, TQ), MASK_NEG, F32)

    kt2 = 2 * TQ

    def tile_pair(ref, j, index_of):
        tiles = []
        for i in range(2):
            off = pl.multiple_of(index_of(qi - (2 * j + i)) * TQ, TQ)
            tiles.append(jnp.concatenate([ref[g, pl.ds(off, TQ), :] for g in range(G)], axis=0))
        return jnp.concatenate(tiles, axis=1)

    _flash_init(m_s, l_s, a_s)
    _flash_init(m_w, l_w, a_w)
    all_rows = pl.ds(0, rows)

    def pair_step(j, with_win):
        off = pl.multiple_of(j * kt2, kt2)
        s_s = _dot_nt(q, ks_ref[pl.ds(off, kt2), :])
        if with_win:
            s_w = _dot_nt(q, kw_ref[pl.ds(off, kt2), :])
        bias_s = tile_pair(bt_ref, j, lambda d: jnp.clip(d, 0, 2))
        key_mask = jnp.concatenate([mk_ref[:, pl.ds(off, kt2)]] * G, axis=0)
        p_s, alpha_s = _flash_prob(s_s * SCALE + bias_s + key_mask, None, m_s, l_s, all_rows)
        if with_win:
            bias_w = tile_pair(wbt_ref, j, lambda d: jnp.where(d < 0, WIN_TILES - 1, jnp.minimum(d, WIN_TILES - 1)))
            p_w, alpha_w = _flash_prob(s_w * SCALE + bias_w, None, m_w, l_w, all_rows)
        a_s[...] = alpha_s * a_s[...] + _dot(p_s, vs_ref[pl.ds(off, kt2), :])
        if with_win:
            a_w[...] = alpha_w * a_w[...] + _dot(p_w, vw_ref[pl.ds(off, kt2), :])

    def sel_only(j, c):
        pair_step(j, False)
        return c

    def sel_and_win(j, c):
        pair_step(j, True)
        return c

    lo = lax.shift_right_arithmetic(jnp.maximum(qi - WINDOW // TQ, 0), 1)
    hi = lax.shift_right_arithmetic(qi, 1) + 1
    lax.fori_loop(0, lo, sel_only, 0)
    lax.fori_loop(lo, hi, sel_and_win, 0)

    o_s = a_s[...] * _safe_inv(l_s[...])
    o_w = a_w[...] * _safe_inv(l_w[...])
    gt = g_ref[...]
    for g in range(G):
        r = slice(g * TQ, (g + 1) * TQ)
        out = (gt[:, 3 * g:3 * g + 1] * o_c[r] + gt[:, 3 * g + 1:3 * g + 2] * o_s[r]
               + gt[:, 3 * g + 2:3 * g + 3] * o_w[r])
        o_ref[:, g * HEAD_DIM:(g + 1) * HEAD_DIM] = out.astype(o_ref.dtype)


def nsa_prompt_attention(q, kv4, kwin, gates, kcb, bias_all, b, t):
    nq = t // TQ
    n_chunks = t // CMP_STRIDE
    n_cmp = n_chunks - CMP_BLOCK // CMP_STRIDE + 1
    n_blk = -(-t // SEL_BLOCK)
    assert n_chunks == LANES and n_blk <= LANES and t % TQ == 0 and TQ >= FAR_DIST
    cmp_start = np.arange(LANES) * CMP_STRIDE
    jst = np.arange(LANES) * SEL_BLOCK
    overlap = ((cmp_start[:, None] < jst[None, :] + SEL_BLOCK) & (cmp_start[:, None] + CMP_BLOCK > jst[None, :])
               & (np.arange(LANES)[:, None] < n_cmp) & (np.arange(LANES)[None, :] < n_blk))
    nbr = -(-n_blk // (2 * SUBLANES)) * 2 * SUBLANES
    ovl_t = jnp.asarray(overlap.T[:nbr], BF16)
    expand = jnp.asarray(np.where(np.arange(LANES)[:, None] == (np.arange(t)[None, :] // SEL_BLOCK), MASK_NEG, 0.0),
                         BF16)
    colblk = lambda base: (lambda bb, kv, qi: (bb, base + kv))
    rowblk = lambda bb, kv, qi: (bb * nq + qi, kv)
    in_specs = [
        pl.BlockSpec((TQ, G * HEAD_DIM), rowblk),
        pl.BlockSpec((t, HEAD_DIM), colblk(2 * KV)), pl.BlockSpec((t, HEAD_DIM), colblk(3 * KV)),
        pl.BlockSpec((t, HEAD_DIM), colblk(0)), pl.BlockSpec((t, HEAD_DIM), colblk(KV)),
        pl.BlockSpec((None, None, n_chunks, HEAD_DIM), lambda bb, kv, qi: (bb, kv, 0, 0)),
        pl.BlockSpec((None, None, n_chunks, HEAD_DIM), lambda bb, kv, qi: (bb, KV + kv, 0, 0)),
        pl.BlockSpec((TQ, LANES), rowblk),
        pl.BlockSpec((G, TQ, LANES), lambda bb, kv, qi: (kv, WIN_TILES + 3 + qi, 0)),
        pl.BlockSpec((G, 3 * TQ, LANES), lambda bb, kv, qi: (kv, WIN_TILES // 3, 0)),
        pl.BlockSpec((G, WIN_TILES * TQ, LANES), lambda bb, kv, qi: (kv, 0, 0)),
        pl.BlockSpec((nbr, LANES), lambda bb, kv, qi: (0, 0)),
        pl.BlockSpec((LANES, t), lambda bb, kv, qi: (0, 0)),
    ]
    assert WIN_TILES % 3 == 0 and TQ == LANES
    rows = G * TQ
    scratch = [pltpu.VMEM((rows, LANES), F32) for _ in range(6)] + [pltpu.VMEM((TQ, t), F32),
                                                                   pltpu.VMEM((nbr, TQ), F32)]
    return pl.pallas_call(
        functools.partial(_nsa_prompt_body, n_cmp=n_cmp, n_blk=n_blk), grid=(b, KV, nq),
        in_specs=in_specs, out_specs=pl.BlockSpec((TQ, G * HEAD_DIM), rowblk),
        out_shape=jax.ShapeDtypeStruct((b * t, N_HEADS * HEAD_DIM), BF16),
        scratch_shapes=scratch,
        compiler_params=_params("parallel", "parallel", "arbitrary"), name="nsa_prompt_attn")(
            q, kv4, kv4, kwin, kwin, kcb, kcb, gates, bias_all, bias_all, bias_all, ovl_t, expand)


def _stick_pre(z, mask):
    sp = jnp.log(1.0 + jnp.exp(-jnp.abs(z)))
    log_beta = jnp.minimum(z, 0.0) - sp
    rest = jnp.maximum(z, 0.0) + sp
    if mask is not None:
        rest = jnp.where(mask, rest, 0.0)
    hi = rest.astype(BF16)
    lo = (rest - hi.astype(F32)).astype(BF16)
    return log_beta, jnp.sum(rest, axis=1, keepdims=True), jnp.concatenate([hi, lo], axis=0)


def _stick_post(log_beta, l2, carry, mask):
    n = log_beta.shape[0]
    a = jnp.exp(log_beta - (l2[0:n] + l2[n:2 * n] + carry))
    return a if mask is None else jnp.where(mask, a, 0.0)


SB_HEADS = 8
SB_DONE = 110.0


def _sb_prompt_body(q_ref, k_ref, v_ref, u_ref, o_ref, carry_ref, acc_ref):
    qi = pl.program_id(2)
    row = lax.broadcasted_iota(I32, (TQ, TQ), 0)
    lane = lax.broadcasted_iota(I32, (TQ, TQ), 1)

    def tile(kt, mask, first):
        off = pl.multiple_of(kt * TQ, TQ)
        cols = [slice(h * HEAD_DIM, (h + 1) * HEAD_DIM) for h in range(SB_HEADS)]
        zs = [_dot_nt(q_ref[:, c], k_ref[pl.ds(off, TQ), c]) * SCALE for c in cols]
        pre = [_stick_pre(z, mask) for z in zs]
        l3s = [_dot(p[2], u_ref[...]) for p in pre]
        ws = [_stick_post(p[0], l3, 0.0 if first else carry_ref[h], mask).astype(BF16)
              for h, (p, l3) in enumerate(zip(pre, l3s))]
        pvs = [_dot(w, v_ref[pl.ds(off, TQ), c]) for w, c in zip(ws, cols)]
        for h in range(SB_HEADS):
            tot = pre[h][1]
            acc_ref[h] = pvs[h] if first else acc_ref[h] + pvs[h]
            carry_ref[h] = jnp.broadcast_to(tot, (TQ, TQ)) if first else carry_ref[h] + tot

    tile(qi, lane < row, True)

    def unfinished():
        c = carry_ref[0]
        for h in range(1, SB_HEADS):
            c = jnp.minimum(c, carry_ref[h])
        return (jnp.min(c) < SB_DONE).astype(I32)

    def step(state):
        i, _ = state
        tile(qi - 1 - i, None, False)
        return i + 1, unfinished()

    lax.while_loop(lambda state: (state[0] < qi) & (state[1] > 0), step, (jnp.int32(0), unfinished()))
    for h in range(SB_HEADS):
        o_ref[:, h * HEAD_DIM:(h + 1) * HEAD_DIM] = acc_ref[h].astype(o_ref.dtype)


def _suffix_matrix():
    j = np.arange(TQ)
    return jnp.asarray(j[:, None] > j[None, :], BF16)


def sb_prompt_attention(q, kvp, b, t):
    nq = t // TQ
    hg = N_HEADS // SB_HEADS
    w = SB_HEADS * HEAD_DIM
    rowblk = lambda bb, h, qi: (bb * nq + qi, h)
    return pl.pallas_call(
        _sb_prompt_body, grid=(b, hg, nq),
        in_specs=[pl.BlockSpec((TQ, w), rowblk),
                  pl.BlockSpec((t, w), lambda bb, h, qi: (bb, h)),
                  pl.BlockSpec((t, w), lambda bb, h, qi: (bb, hg + h)),
                  pl.BlockSpec((TQ, TQ), lambda bb, h, qi: (0, 0))],
        out_specs=pl.BlockSpec((TQ, w), rowblk),
        out_shape=jax.ShapeDtypeStruct((b * t, N_HEADS * HEAD_DIM), BF16),
        scratch_shapes=[pltpu.VMEM((SB_HEADS, TQ, TQ), F32), pltpu.VMEM((SB_HEADS, TQ, HEAD_DIM), F32)],
        compiler_params=_params("parallel", "parallel", "arbitrary"), name="sb_prompt_attn")(
            q, kvp, kvp, _suffix_matrix())


def _nsa_sample_select_body(q_ref, kcb_ref, bc_ref, mc_ref, ovl_ref, fadd_ref, elig_ref, oc_ref, sel_ref, sc_ref,
                            *, n_blk):
    nbp = kcb_ref.shape[1]
    grp = G * T8
    ps = []
    for kv in range(KV):
        qk = jnp.concatenate([q_ref[:, (kv * G + g) * HEAD_DIM:(kv * G + g + 1) * HEAD_DIM] for g in range(G)],
                             axis=0)
        s = _dot_nt(qk, kcb_ref[kv].astype(BF16)) * SCALE + bc_ref[kv * grp:(kv + 1) * grp, :]
        p = _masked_softmax(s, mc_ref[...] > 0.5)
        oc_ref[kv * grp:(kv + 1) * grp, :] = _dot(p.astype(BF16), kcb_ref[KV + kv].astype(BF16))
        ps.append(p)
    pad = jnp.zeros((LANES - KV * T8, nbp), F32)
    imp = None
    for g in range(G):
        xg = jnp.concatenate([ps[kv][g * T8:(g + 1) * T8] for kv in range(KV)] + [pad], axis=0)
        part = _dot_nt(ovl_ref[...], xg.astype(BF16))
        imp = part if imp is None else imp + part
    eligible = elig_ref[...] > 0.5
    score = jnp.where(eligible, imp + fadd_ref[...], -jnp.inf)
    sc_ref[...] = score
    blk = lax.broadcasted_iota(I32, score.shape, 0)
    sel_ref[...] = _rank_select(score, eligible, blk, n_blk, min(SEL_TOPK, n_blk),
                                lambda j: sc_ref[pl.ds(j, 1), :])


SEL_KEYS = 1024


def _nsa_sample_attn_body(q_ref, kv_ref, sel_ref, new_ref, win_ref, wnew_ref, g_ref, oc_ref,
                          bl_ref, bf_ref, bn_ref, bw_ref, ex_ref, o_ref, m_s, l_s, a_s, *, n_steps, t_new, past):
    step = pl.program_id(1)
    grp = G * T8
    kvw = KV * HEAD_DIM
    wb = win_ref.shape[0] // (2 * KV)
    is_last = step == n_steps - 1

    @pl.when(step == 0)
    def _():
        _flash_init(m_s, l_s, a_s)

    def q_of(kv):
        return jnp.concatenate([q_ref[:, (kv * G + g) * HEAD_DIM:(kv * G + g + 1) * HEAD_DIM] for g in range(G)],
                               axis=0)

    def pad_keys(x):
        return jnp.concatenate([x, jnp.zeros((TQ - x.shape[0], x.shape[1]), x.dtype)], axis=0).astype(BF16)

    def rows_of_blocks(piece):
        return jnp.concatenate([piece, jnp.zeros((LANES - piece.shape[0], LANES), F32)], axis=0).T

    bps = SEL_KEYS // SEL_BLOCK
    piece = sel_ref[pl.ds(pl.multiple_of(step * bps, bps), bps), :]
    mk = _dot(rows_of_blocks(piece).astype(BF16), ex_ref[...])

    kv_rows = [pl.ds(kv * grp, grp) for kv in range(KV)]
    scores = [_dot_nt(q_of(kv), kv_ref[kv]) for kv in range(KV)]
    probs = []
    for kv in range(KV):
        bias = jnp.where(is_last, bl_ref[kv_rows[kv], :], bf_ref[kv_rows[kv], 0:1])
        mask = jnp.concatenate([mk[kv * T8:(kv + 1) * T8]] * G, axis=0) > 0.5
        probs.append(_flash_prob(scores[kv] * SCALE + bias, mask, m_s, l_s, kv_rows[kv]))
    for kv in range(KV):
        p, alpha = probs[kv]
        a_s[kv_rows[kv], :] = alpha * a_s[kv_rows[kv], :] + _dot(p, kv_ref[KV + kv])

    @pl.when(is_last)
    def _():
        lane = lax.broadcasted_iota(I32, (grp, LANES), 1)
        t_row = lax.broadcasted_iota(I32, (grp, LANES), 0) & (T8 - 1)
        lane_w = lax.broadcasted_iota(I32, (grp, wb), 1)
        t_row_w = lax.broadcasted_iota(I32, (grp, wb), 0) & (T8 - 1)
        new_blk = past // SEL_BLOCK
        picked_all = rows_of_blocks(sel_ref[new_blk:new_blk + SUBLANES, :])
        gt = g_ref[...]
        for kv in range(KV):
            rows = pl.ds(kv * grp, grp)
            qk = q_of(kv)
            kn = pad_keys(new_ref[:, 2 * kvw + kv * HEAD_DIM:2 * kvw + (kv + 1) * HEAD_DIM])
            vn = pad_keys(new_ref[:, 3 * kvw + kv * HEAD_DIM:3 * kvw + (kv + 1) * HEAD_DIM])
            s = _dot_nt(qk, kn) * SCALE + bn_ref[rows, :]
            picked = jnp.concatenate([picked_all[kv * T8:(kv + 1) * T8, 0:1]] * G, axis=0) > 0.5
            mask = (lane <= t_row) & (lane < t_new) & picked
            _flash_update(s, mask, vn, m_s, l_s, a_s, rows)
            o_s = a_s[rows, :] * _safe_inv(l_s[rows, :])
            kw = win_ref[pl.ds(kv, wb, stride=2 * KV), :].astype(BF16)
            vw = win_ref[pl.ds(KV + kv, wb, stride=2 * KV), :].astype(BF16)
            kwn = pad_keys(wnew_ref[:, kv * HEAD_DIM:(kv + 1) * HEAD_DIM])
            vwn = pad_keys(wnew_ref[:, kvw + kv * HEAD_DIM:kvw + (kv + 1) * HEAD_DIM])
            s1 = _dot_nt(qk, kw) * SCALE + bw_ref[rows, 0:wb]
            s2 = _dot_nt(qk, kwn) * SCALE + bw_ref[rows, wb:wb + LANES]
            d1 = wb + t_row_w - lane_w
            mask1 = (d1 >= 0) & (d1 < WINDOW)
            d2 = t_row - lane
            mask2 = (d2 >= 0) & (d2 < WINDOW) & (lane < t_new)
            m = jnp.maximum(jnp.max(jnp.where(mask1, s1, NEG_INF), axis=1, keepdims=True),
                            jnp.max(jnp.where(mask2, s2, NEG_INF), axis=1, keepdims=True))
            e1 = jnp.where(mask1, jnp.exp(s1 - m), 0.0)
            e2 = jnp.where(mask2, jnp.exp(s2 - m), 0.0)
            l = jnp.sum(e1, axis=1, keepdims=True) + jnp.sum(e2, axis=1, keepdims=True)
            o_w = (_dot(e1.astype(BF16), vw) + _dot(e2.astype(BF16), vwn)) * _safe_inv(l)
            o_c = oc_ref[rows, :]
            for g in range(G):
                r = slice(g * T8, (g + 1) * T8)
                c = kv * LANES + 3 * g
                out = gt[:, c:c + 1] * o_c[r] + gt[:, c + 1:c + 2] * o_s[r] + gt[:, c + 2:c + 3] * o_w[r]
                o_ref[:, (kv * G + g) * HEAD_DIM:(kv * G + g + 1) * HEAD_DIM] = out.astype(o_ref.dtype)


def nsa_sample_attention(q8, sel_rows, kcb, new8, win3, wnew8, gates8, bias_s, past, t_new):
    bd = q8.shape[0]
    n_chunks = kcb.shape[2]
    n_cmp = n_chunks - CMP_BLOCK // CMP_STRIDE + 1
    total = past + t_new
    n_blk = -(-total // SEL_BLOCK)
    nblkp = -(-(n_blk + SUBLANES) // (2 * SUBLANES)) * 2 * SUBLANES
    wb = win3.shape[1] // (2 * KV)
    n_steps = past // SEL_KEYS
    assert past % SEL_KEYS == 0 and SEL_KEYS % SEL_BLOCK == 0 and t_new <= T8 and PAGE_SIZE > FAR_DIST
    assert (total // CMP_STRIDE) == n_chunks and wb % LANES == 0 and SEL_KEYS // SEL_BLOCK <= LANES

    pos_q = past + np.arange(T8)
    cmp_start = np.arange(n_chunks) * CMP_STRIDE
    cmp_end = cmp_start + CMP_BLOCK - 1
    mask_c = ((pos_q[:, None] >= cmp_end[None, :]) & (np.arange(n_chunks)[None, :] < n_cmp))
    mask_c = jnp.asarray(np.tile(mask_c, (G, 1)), F32)
    jst = np.arange(nblkp) * SEL_BLOCK
    overlap = ((cmp_start[:, None] < jst[None, :] + SEL_BLOCK) & (cmp_start[:, None] + CMP_BLOCK > jst[None, :])
               & (np.arange(n_chunks)[:, None] < n_cmp) & (np.arange(nblkp)[None, :] < n_blk))
    ovl_t = jnp.asarray(overlap.T, BF16)
    blk = np.arange(nblkp)[:, None]
    pq = np.tile(pos_q, KV)[None, :]
    tvalid = np.tile(np.arange(T8) < t_new, KV)[None, :]
    cur = pq // SEL_BLOCK
    elig = (blk * SEL_BLOCK <= pq) & (blk < n_blk) & tvalid
    forced = (blk == 0) | (blk == cur) | (blk == cur - 1)
    pad_l = LANES - KV * T8
    elig = jnp.asarray(np.pad(elig, ((0, 0), (0, pad_l))), F32)
    fadd = jnp.asarray(np.pad(np.where(forced, FORCE_BONUS, 0.0), ((0, 0), (0, pad_l))), F32)

    grp_all = KV * G * T8
    o_c, sel = pl.pallas_call(
        functools.partial(_nsa_sample_select_body, n_blk=n_blk), grid=(bd,),
        in_specs=[pl.BlockSpec((None, T8, N_HEADS * HEAD_DIM), lambda b: (b, 0, 0)),
                  pl.BlockSpec((None, 2 * KV, n_chunks, HEAD_DIM), lambda b: (b, 0, 0, 0)),
                  pl.BlockSpec((grp_all, n_chunks), lambda b: (0, 0)),
                  pl.BlockSpec((G * T8, n_chunks), lambda b: (0, 0)),
                  pl.BlockSpec((nblkp, n_chunks), lambda b: (0, 0)),
                  pl.BlockSpec((nblkp, LANES), lambda b: (0, 0)),
                  pl.BlockSpec((nblkp, LANES), lambda b: (0, 0))],
        out_specs=[pl.BlockSpec((None, grp_all, HEAD_DIM), lambda b: (b, 0, 0)),
                   pl.BlockSpec((None, nblkp, LANES), lambda b: (b, 0, 0))],
        out_shape=[jax.ShapeDtypeStruct((bd, grp_all, HEAD_DIM), F32),
                   jax.ShapeDtypeStruct((bd, nblkp, LANES), F32)],
        scratch_shapes=[pltpu.VMEM((nblkp, LANES), F32)],
        compiler_params=_params("parallel"), name="nsa_sample_select")(
            q8, kcb, bias_s["cmp"], mask_c, ovl_t, fadd, elig)

    kvw = KV * HEAD_DIM
    const2 = lambda b, s: (0, 0)
    per_b = lambda b, s: (b, 0, 0)
    expand = jnp.asarray(np.arange(LANES)[:, None] == (np.arange(SEL_KEYS)[None, :] // SEL_BLOCK), BF16)
    bias_last = jnp.concatenate([bias_s["far"]] * (SEL_KEYS // LANES - 1) + [bias_s["last"]], axis=1)
    return pl.pallas_call(
        functools.partial(_nsa_sample_attn_body, n_steps=n_steps, t_new=t_new, past=past), grid=(bd, n_steps),
        in_specs=[pl.BlockSpec((None, T8, N_HEADS * HEAD_DIM), per_b),
                  pl.BlockSpec((None, 2 * KV, SEL_KEYS, HEAD_DIM), lambda b, s: (b, 0, s, 0)),
                  pl.BlockSpec((None, nblkp, LANES), per_b),
                  pl.BlockSpec((None, T8, 4 * kvw), per_b),
                  pl.BlockSpec((None, wb * 2 * KV, HEAD_DIM), per_b),
                  pl.BlockSpec((None, T8, 2 * kvw), per_b),
                  pl.BlockSpec((None, T8, KV * LANES), per_b),
                  pl.BlockSpec((None, grp_all, HEAD_DIM), per_b),
                  pl.BlockSpec((grp_all, SEL_KEYS), const2), pl.BlockSpec((grp_all, LANES), const2),
                  pl.BlockSpec((grp_all, LANES), const2), pl.BlockSpec((grp_all, wb + LANES), const2),
                  pl.BlockSpec((LANES, SEL_KEYS), const2)],
        out_specs=pl.BlockSpec((None, T8, N_HEADS * HEAD_DIM), per_b),
        out_shape=jax.ShapeDtypeStruct((bd, T8, N_HEADS * HEAD_DIM), BF16),
        scratch_shapes=[pltpu.VMEM((grp_all, LANES), F32) for _ in range(3)],
        compiler_params=_params("parallel", "arbitrary"), name="nsa_sample_attn")(
            q8, sel_rows, sel, new8, win3, wnew8, gates8, o_c,
            bias_last, bias_s["far"], bias_s["new"], bias_s["win"], expand)


SB_PAGES = 4
SB_GROUPS = 2 * N_HEADS // SUBLANES


def _sb_sample_run(q_ref, u, carry_ref, acc_ref, n_sub, k_of, v_of, masks):
    def q_of(h):
        return q_ref[h * T8:(h + 1) * T8, :].astype(BF16)

    zs = [jnp.concatenate([_dot_nt(q_of(h), k_of(j, h)) for h in range(N_HEADS)], axis=0) * SCALE
          for j in range(n_sub)]
    pre = [_stick_pre(zs[j], masks[j]) for j in range(n_sub)]
    l2s = [_dot(pre[j][2], u) for j in range(n_sub)]
    carry = carry_ref[...]
    ws = [None] * n_sub
    for j in reversed(range(n_sub)):
        ws[j] = _stick_post(pre[j][0], l2s[j], carry, masks[j])
        carry = carry + pre[j][1]
    carry_ref[...] = carry
    for h in range(N_HEADS):
        r = slice(h * T8, (h + 1) * T8)
        out = _dot(ws[0][r].astype(BF16), v_of(0, h))
        for j in range(1, n_sub):
            out = out + _dot(ws[j][r].astype(BF16), v_of(j, h))
        acc_ref[r, :] += out


def _sb_page_rows(refs):
    groups = [r.reshape(PAGE_SIZE * SUBLANES, HEAD_DIM) for r in refs]

    def head_rows(j, row):
        g = groups[j * SB_GROUPS + row // SUBLANES]
        return g[pl.ds(row % SUBLANES, PAGE_SIZE, stride=SUBLANES), :].astype(BF16)
    return head_rows


def _sb_sample_head_body(pt_ref, q_ref, *refs, t_new):
    n_grp = SB_PAGES * SB_GROUPS
    head_rows = _sb_page_rows(refs[:n_grp])
    new_ref, u_ref, acc_ref, carry_ref = refs[n_grp:]
    hw = N_HEADS * HEAD_DIM
    carry_ref[...] = jnp.zeros_like(carry_ref)
    acc_ref[...] = jnp.zeros_like(acc_ref)
    pad = jnp.zeros((TQ - T8, 2 * hw), F32)
    kvn = jnp.concatenate([new_ref[...], pad], axis=0)
    lane = lax.broadcasted_iota(I32, (TQ, TQ), 1)
    t_row = lax.broadcasted_iota(I32, (TQ, TQ), 0) & (T8 - 1)
    _sb_sample_run(q_ref, u_ref[...], carry_ref, acc_ref, 1,
                   lambda j, h: kvn[:, h * HEAD_DIM:(h + 1) * HEAD_DIM].astype(BF16),
                   lambda j, h: kvn[:, hw + h * HEAD_DIM:hw + (h + 1) * HEAD_DIM].astype(BF16),
                   [(lane < t_row) & (lane < t_new)])
    _sb_sample_run(q_ref, u_ref[...], carry_ref, acc_ref, SB_PAGES, lambda j, h: head_rows(j, h),
                   lambda j, h: head_rows(j, N_HEADS + h), [None] * SB_PAGES)


def _sb_sample_tail_body(pt_ref, need_ref, q_ref, *refs, n_steps):
    n_grp = SB_PAGES * SB_GROUPS
    head_rows = _sb_page_rows(refs[:n_grp])
    u_ref, acc_in, carry_in, o_ref, carry_ref, acc_ref = refs[n_grp:]
    step = pl.program_id(1)

    @pl.when(step == 0)
    def _():
        carry_ref[...] = carry_in[...]
        acc_ref[...] = acc_in[...]

    @pl.when((need_ref[pl.program_id(0)] > 0) & (jnp.min(carry_ref[...]) < SB_DONE))
    def _():
        _sb_sample_run(q_ref, u_ref[...], carry_ref, acc_ref, SB_PAGES, lambda j, h: head_rows(j, h),
                       lambda j, h: head_rows(j, N_HEADS + h), [None] * SB_PAGES)

    @pl.when(step == n_steps - 1)
    def _():
        o_ref[...] = acc_ref[...].astype(o_ref.dtype)


def sb_sample_attention(q_rows, cache5, page_table, new8, t_new):
    bd, n_pages = page_table.shape
    hw = N_HEADS * HEAD_DIM
    n_steps = n_pages // SB_PAGES - 1
    assert n_pages % SB_PAGES == 0 and n_steps >= 1 and cache5.shape[2] == SB_GROUPS
    rows = N_HEADS * T8
    grp_block = (None, PAGE_SIZE, None, SUBLANES, HEAD_DIM)
    n_grp = SB_PAGES * SB_GROUPS
    u = _suffix_matrix()

    def head_spec(i, g):
        return pl.BlockSpec(grp_block, lambda b, pt: (pt[b, n_pages - SB_PAGES + i], 0, g, 0, 0))

    acc, carry = pl.pallas_call(
        functools.partial(_sb_sample_head_body, t_new=t_new),
        grid_spec=pltpu.PrefetchScalarGridSpec(
            num_scalar_prefetch=1, grid=(bd,),
            in_specs=[pl.BlockSpec((None, rows, HEAD_DIM), lambda b, pt: (b, 0, 0))]
            + [head_spec(i, g) for i in range(SB_PAGES) for g in range(SB_GROUPS)]
            + [pl.BlockSpec((None, T8, 2 * hw), lambda b, pt: (b, 0, 0)),
               pl.BlockSpec((TQ, TQ), lambda b, pt: (0, 0))],
            out_specs=[pl.BlockSpec((None, rows, HEAD_DIM), lambda b, pt: (b, 0, 0)),
                       pl.BlockSpec((None, rows, LANES), lambda b, pt: (b, 0, 0))]),
        out_shape=[jax.ShapeDtypeStruct((bd, rows, HEAD_DIM), F32), jax.ShapeDtypeStruct((bd, rows, LANES), F32)],
        compiler_params=_params("parallel"), name="sb_sample_head")(
            page_table, q_rows, *([cache5] * n_grp), new8, u)

    need = (jnp.min(carry, axis=(1, 2)) < SB_DONE).astype(I32)

    def tail_spec(i, g):
        def index(b, s, pt, nd):
            page = jnp.where(nd[b] > 0, n_pages - (s + 2) * SB_PAGES + i, n_pages - SB_PAGES + i)
            return (pt[b, page], 0, g, 0, 0)
        return pl.BlockSpec(grp_block, index)

    per_b = lambda b, s, pt, nd: (b, 0, 0)
    return pl.pallas_call(
        functools.partial(_sb_sample_tail_body, n_steps=n_steps),
        grid_spec=pltpu.PrefetchScalarGridSpec(
            num_scalar_prefetch=2, grid=(bd, n_steps),
            in_specs=[pl.BlockSpec((None, rows, HEAD_DIM), per_b)]
            + [tail_spec(i, g) for i in range(SB_PAGES) for g in range(SB_GROUPS)]
            + [pl.BlockSpec((TQ, TQ), lambda b, s, pt, nd: (0, 0)),
               pl.BlockSpec((None, rows, HEAD_DIM), per_b), pl.BlockSpec((None, rows, LANES), per_b)],
            out_specs=pl.BlockSpec((None, rows, HEAD_DIM), per_b),
            scratch_shapes=[pltpu.VMEM((rows, LANES), F32), pltpu.VMEM((rows, HEAD_DIM), F32)]),
        out_shape=jax.ShapeDtypeStruct((bd, rows, HEAD_DIM), BF16),
        compiler_params=_params("parallel", "arbitrary"), name="sb_sample_tail")(
            page_table, need, q_rows, *([cache5] * n_grp), u, acc, carry)


def _pad_tokens(x, bd, t_new):
    x = x.reshape(bd, t_new, x.shape[-1])
    return jnp.pad(x, ((0, 0), (0, T8 - t_new), (0, 0)))


def _bias_distances(t, past, n_chunks_s, wb):
    r = np.arange(TQ)[:, None]
    c = np.arange(LANES)[None, :]
    win_tiles = np.concatenate([r - c + off * TQ for off in range(WIN_TILES)], axis=0)
    toeplitz = np.concatenate([r - c + off * TQ for off in range(3)], axis=0)
    cmp_p = np.arange(t)[:, None] - (c * CMP_STRIDE + CMP_BLOCK - 1)
    pos_q = past + np.arange(T8)[:, None]
    cmp_s = (pos_q - (np.arange(n_chunks_s)[None, :] * CMP_STRIDE + CMP_BLOCK - 1)).reshape(-1, LANES)
    last = pos_q - (past - PAGE_SIZE + c)
    far = np.full((T8, LANES), 4 * REL_MAX_DIST)
    new = np.arange(T8)[:, None] - c
    win = (wb + np.arange(T8)[:, None] - np.arange(wb + LANES)[None, :]).reshape(-1, LANES)
    parts = [win_tiles, toeplitz, cmp_p, cmp_s, last, far, new, win]
    rows = sum(p.shape[0] for p in parts)
    pad = -rows % LANES
    parts.append(np.zeros((pad, LANES), np.int64))
    offs = np.cumsum([0] + [p.shape[0] for p in parts])
    dist = np.concatenate(parts, axis=0)
    addmask = np.zeros(dist.shape, np.float32)
    addmask[:win_tiles.shape[0]] = np.where((win_tiles >= 0) & (win_tiles < WINDOW), 0.0, NEG_INF)
    return jnp.asarray(dist, I32), jnp.asarray(addmask), offs


def _sample_bias(bias_all, offs, n_chunks_s, wb):
    def rows_kgt(x, width):
        return x.reshape(N_HEADS * T8, width)
    seg = lambda i: bias_all[:, offs[i]:offs[i + 1]]
    return {"cmp": rows_kgt(seg(3), n_chunks_s), "last": rows_kgt(seg(4), LANES), "far": rows_kgt(seg(5), LANES),
            "new": rows_kgt(seg(6), LANES), "win": rows_kgt(seg(7), wb + LANES)}


def kernel(x_prompt, x_sample, cache_nsa, state_nsa_win, cache_sb, page_table, rel_bias, norm_w, final_norm_w,
           nsa_w_in, nsa_w_out, nsa_pe_k, nsa_pe_v, nsa_phi_k1, nsa_phi_k2, nsa_phi_v1, nsa_phi_v2,
           sb_w_in, sb_w_out, ffn_w1, ffn_w3, ffn_w2):
    b, t, d = x_prompt.shape
    bd, t_new, _ = x_sample.shape
    n_pages = page_table.shape[1]
    past = n_pages * PAGE_SIZE
    wb = state_nsa_win.shape[2]
    hd = N_HEADS * HEAD_DIM
    kvw = KV * HEAD_DIM
    n_chunks_s = (past + t_new) // CMP_STRIDE
    half = CMP_STRIDE * HEAD_DIM

    dist, addmask, offs = _bias_distances(t, past, n_chunks_s, wb)
    bias_all = bias_from_dist(rel_bias, dist, addmask)
    bias_s = _sample_bias(bias_all, offs, n_chunks_s, wb)

    w_in = nsa_w_in[0]
    wq = w_in[:, :hd].astype(BF16)
    wkv4 = w_in[:, hd:hd + 4 * kvw].astype(BF16)
    wwin = w_in[:, hd + 4 * kvw:hd + 6 * kvw].astype(BF16)
    wg = w_in[:, hd + 6 * kvw:].reshape(d, KV, G * 3)
    wg = jnp.pad(wg, ((0, 0), (0, 0), (0, LANES - G * 3))).reshape(d, KV * LANES).astype(BF16)
    w_out_nsa = nsa_w_out[0].astype(BF16)
    cw = {
        "w1k": jnp.concatenate([nsa_phi_k1[0, :half], nsa_phi_k1[0, half:]], axis=1).astype(BF16),
        "w1v": jnp.concatenate([nsa_phi_v1[0, :half], nsa_phi_v1[0, half:]], axis=1).astype(BF16),
        "pe": jnp.broadcast_to(jnp.stack([nsa_pe_k[0].reshape(1, -1), nsa_pe_v[0].reshape(1, -1)]),
                               (2, SUBLANES, CMP_BLOCK * HEAD_DIM)),
        "w1": jnp.stack([nsa_phi_k1[0], nsa_phi_v1[0]]),
        "w2": jnp.stack([nsa_phi_k2[0], nsa_phi_v2[0]]),
    }
    wsb = sb_w_in[0]
    wsq = wsb[:, :hd].astype(BF16)
    wskv = wsb[:, hd:].astype(BF16)
    w_out_sb = sb_w_out[0].astype(BF16)
    w1 = ffn_w1.astype(BF16)
    w3 = ffn_w3.astype(BF16)
    w2 = ffn_w2.astype(BF16)

    hp = x_prompt.reshape(b * t, d)
    hs = x_sample.reshape(bd * t_new, d)

    xp = rmsnorm_rows(hp, norm_w[0, 0], BF16)
    xs = rmsnorm_rows(hs, norm_w[0, 0], BF16)
    q_p, q_s = matmul(xp, wq, (BF16,)), matmul(xs, wq, (BF16,))
    (kv4_p, kv4b_p), kv4_s = matmul(xp, wkv4, (F32, BF16)), matmul(xs, wkv4, (F32,))
    (win_p, winb_p), win_s = matmul(xp, wwin, (F32, BF16)), matmul(xs, wwin, (F32,))
    g_p, g_s = matmul(xp, wg, (F32,), sigmoid=True), matmul(xs, wg, (F32,), sigmoid=True)

    pt_prompt = jnp.arange(b * (t // PAGE_SIZE), dtype=I32).reshape(b, t // PAGE_SIZE)
    kcb_p = compress(kv4_p.reshape(b * t // PAGE_SIZE, PAGE_SIZE, 4 * kvw), pt_prompt, cw)
    attn_p = nsa_prompt_attention(q_p, kv4b_p, winb_p, g_p, kcb_p, bias_all, b, t)

    cache5 = cache_nsa[0].reshape(cache_nsa.shape[1], PAGE_SIZE, 2, 2 * KV, HEAD_DIM)
    kcb_s, sel_rows = compress_paged(cache5, page_table, cw)
    attn_s = nsa_sample_attention(
        _pad_tokens(q_s, bd, t_new), sel_rows, kcb_s, _pad_tokens(kv4_s, bd, t_new),
        state_nsa_win[0].reshape(bd, wb * 2 * KV, HEAD_DIM), _pad_tokens(win_s, bd, t_new),
        _pad_tokens(g_s, bd, t_new), bias_s, past, t_new)
    attn_s = attn_s[:, :t_new].reshape(bd * t_new, hd)

    hp, xp = matmul_res_norm(attn_p, w_out_nsa, hp, norm_w[0, 1])
    hs, xs = matmul_res_norm(attn_s, w_out_nsa, hs, norm_w[0, 1])
    hp, xp = ffn_res_norm(xp, w1[0], w3[0], w2[0], hp, norm_w[1, 0], BF16)
    hs, xs = ffn_res_norm(xs, w1[0], w3[0], w2[0], hs, norm_w[1, 0], BF16)

    sq_p, sq_s = matmul(xp, wsq, (BF16,)), matmul(xs, wsq, (BF16,))
    (skv_p, skvb_p), skv_s = matmul(xp, wskv, (F32, BF16)), matmul(xs, wskv, (F32,))
    sattn_p = sb_prompt_attention(sq_p, skvb_p, b, t)

    q_rows = (_pad_tokens(sq_s, bd, t_new).reshape(bd, T8, N_HEADS, HEAD_DIM).transpose(0, 2, 1, 3)
              .reshape(bd, N_HEADS * T8, HEAD_DIM).astype(F32))
    csb5 = cache_sb[0].reshape(cache_sb.shape[1], PAGE_SIZE, SB_GROUPS, SUBLANES, HEAD_DIM)
    sattn_s = sb_sample_attention(q_rows, csb5, page_table, _pad_tokens(skv_s, bd, t_new), t_new)
    sattn_s = sattn_s.reshape(bd, N_HEADS, T8, HEAD_DIM)[:, :, :t_new].transpose(0, 2, 1, 3).reshape(bd * t_new, hd)

    hp, xp = matmul_res_norm(sattn_p, w_out_sb, hp, norm_w[1, 1])
    hs, xs = matmul_res_norm(sattn_s, w_out_sb, hs, norm_w[1, 1])
    _, y_p = ffn_res_norm(xp, w1[1], w3[1], w2[1], hp, final_norm_w, F32)
    _, y_s = ffn_res_norm(xs, w1[1], w3[1], w2[1], hs, final_norm_w, F32)

    win_keep = min(WINDOW, t)
    nsa_win_prompt = win_p.reshape(b, t, 2, KV, HEAD_DIM)[:, t - win_keep:]
    win_all = jnp.concatenate([state_nsa_win[0], win_s.reshape(bd, t_new, 2, KV, HEAD_DIM)], axis=1)
    return (y_p.reshape(b, t, d), y_s.reshape(bd, t_new, d),
            kv4_p.reshape(1, b, t, 4, KV, HEAD_DIM), kv4_s.reshape(1, bd, t_new, 4, KV, HEAD_DIM),
            nsa_win_prompt[None], win_all[:, t_new:][None],
            skv_p.reshape(1, b, t, 2, N_HEADS, HEAD_DIM), skv_s.reshape(1, bd, t_new, 2, N_HEADS, HEAD_DIM))
```

```python
import functools
import math

import numpy as np
import jax
import jax.numpy as jnp
from jax import lax
from jax.experimental import pallas as pl
from jax.experimental.pallas import tpu as pltpu

F32 = jnp.float32
BF16 = jnp.bfloat16
I32 = jnp.int32

HEAD_DIM = 128
N_HEADS = 16
NSA_KV_HEADS = 4
NSA_GROUP = N_HEADS // NSA_KV_HEADS
CMP_BLOCK = 32
CMP_STRIDE = 16
SEL_BLOCK = 64
SEL_TOPK = 16
WINDOW = 512
FORCE_BONUS = 1000.0
REL_BUCKETS = 32
REL_MAX_DIST = 128
RMS_EPS = 1e-6
NEG_INF = -1e30
MASK_NEG = 2 * NEG_INF
SCALE = HEAD_DIM ** -0.5
PAGE_SIZE = 128

LANES = 128
SUBLANES = 8
TQ = 128
WIN_TILES = WINDOW // TQ + 2
T8 = SUBLANES
VMEM_LIMIT = 56 * 1024 * 1024
KV = NSA_KV_HEADS
G = NSA_GROUP


def _t5_thresholds():
    d = np.arange(0, 4 * REL_MAX_DIST)
    max_exact = REL_BUCKETS // 2
    nf = np.maximum(d, 1).astype(np.float32)
    large = max_exact + (np.log(nf / np.float32(max_exact)) / np.float32(math.log(REL_MAX_DIST / max_exact))
                         * np.float32(REL_BUCKETS - max_exact)).astype(np.int32)
    b = np.where(d < max_exact, d, np.minimum(large, REL_BUCKETS - 1))
    return [int(np.argmax(b >= k)) for k in range(REL_BUCKETS)]


T5_THR = _t5_thresholds()
FAR_DIST = T5_THR[-1]


def _dot(a, b):
    return jnp.dot(a, b, preferred_element_type=F32)


def _dot_nt(a, b):
    return lax.dot_general(a, b, (((1,), (1,)), ((), ())), preferred_element_type=F32)


def _params(*sem):
    return pltpu.CompilerParams(dimension_semantics=sem, vmem_limit_bytes=VMEM_LIMIT)


def _rms(x, w):
    return x * lax.rsqrt(jnp.mean(x * x, axis=-1, keepdims=True) + RMS_EPS) * w


def _masked_softmax(s, mask):
    sm = jnp.where(mask, s, NEG_INF)
    m = jnp.max(sm, axis=1, keepdims=True)
    e = jnp.where(mask, jnp.exp(sm - m), 0.0)
    l = jnp.sum(e, axis=1, keepdims=True)
    return e * jnp.where(l > 0.0, 1.0 / l, 0.0)


def _safe_inv(l):
    return jnp.where(l > 0.0, 1.0 / l, 0.0)


def _rmsnorm_body(x_ref, w_ref, o_ref):
    o_ref[...] = _rms(x_ref[...], w_ref[...]).astype(o_ref.dtype)


def rmsnorm_rows(x, w, out_dtype):
    m, d = x.shape
    tm = min(m, 512)
    return pl.pallas_call(
        _rmsnorm_body, grid=(m // tm,),
        in_specs=[pl.BlockSpec((tm, d), lambda i: (i, 0)), pl.BlockSpec((1, d), lambda i: (0, 0))],
        out_specs=pl.BlockSpec((tm, d), lambda i: (i, 0)),
        out_shape=jax.ShapeDtypeStruct((m, d), out_dtype),
        compiler_params=_params("parallel"), name="rmsnorm")(x, w.reshape(1, d))


def _matmul_body(a_ref, w_ref, *o_refs, sigmoid):
    y = _dot(a_ref[...], w_ref[...])
    if sigmoid:
        y = jax.nn.sigmoid(y)
    for o_ref in o_refs:
        o_ref[...] = y.astype(o_ref.dtype)


def matmul(a, w, out_dtypes, sigmoid=False):
    m, k = a.shape
    n = w.shape[1]
    tm = min(m, 512)
    tn = min(n, 2048)
    outs = pl.pallas_call(
        functools.partial(_matmul_body, sigmoid=sigmoid), grid=(m // tm, n // tn),
        in_specs=[pl.BlockSpec((tm, k), lambda i, j: (i, 0)), pl.BlockSpec((k, tn), lambda i, j: (0, j))],
        out_specs=[pl.BlockSpec((tm, tn), lambda i, j: (i, j)) for _ in out_dtypes],
        out_shape=[jax.ShapeDtypeStruct((m, n), dt) for dt in out_dtypes],
        compiler_params=_params("parallel", "arbitrary"), name="matmul")(a, w)
    return outs if len(outs) > 1 else outs[0]


def _mm_res_norm_body(a_ref, w_ref, h_ref, nw_ref, hout_ref, xn_ref):
    h = h_ref[...] + _dot(a_ref[...], w_ref[...])
    hout_ref[...] = h
    xn_ref[...] = _rms(h, nw_ref[...]).astype(xn_ref.dtype)


def matmul_res_norm(a, w, h, nw):
    m, k = a.shape
    d = w.shape[1]
    tm = min(m, 512)
    return pl.pallas_call(
        _mm_res_norm_body, grid=(m // tm,),
        in_specs=[pl.BlockSpec((tm, k), lambda i: (i, 0)), pl.BlockSpec((k, d), lambda i: (0, 0)),
                  pl.BlockSpec((tm, d), lambda i: (i, 0)), pl.BlockSpec((1, d), lambda i: (0, 0))],
        out_specs=[pl.BlockSpec((tm, d), lambda i: (i, 0)), pl.BlockSpec((tm, d), lambda i: (i, 0))],
        out_shape=[jax.ShapeDtypeStruct((m, d), F32), jax.ShapeDtypeStruct((m, d), BF16)],
        compiler_params=_params("parallel"), name="matmul_res_norm")(a, w, h, nw.reshape(1, d))


def _ffn_body(xn_ref, w1_ref, w3_ref, w2_ref, h_ref, nw_ref, hout_ref, yn_ref, acc_ref):
    f = pl.program_id(1)

    @pl.when(f == 0)
    def _():
        acc_ref[...] = jnp.zeros_like(acc_ref)

    x = xn_ref[...]
    g = _dot(x, w1_ref[...])
    u = _dot(x, w3_ref[...])
    mid = (g * jax.nn.sigmoid(g) * u).astype(BF16)
    acc_ref[...] += _dot(mid, w2_ref[...])

    @pl.when(f == pl.num_programs(1) - 1)
    def _():
        h = h_ref[...] + acc_ref[...]
        hout_ref[...] = h
        yn_ref[...] = _rms(h, nw_ref[...]).astype(yn_ref.dtype)


def ffn_res_norm(xn, w1, w3, w2, h, nw, norm_dtype):
    m, d = xn.shape
    dff = w1.shape[1]
    tm = min(m, 512)
    tf = 512
    return pl.pallas_call(
        _ffn_body, grid=(m // tm, dff // tf),
        in_specs=[pl.BlockSpec((tm, d), lambda i, f: (i, 0)),
                  pl.BlockSpec((d, tf), lambda i, f: (0, f)), pl.BlockSpec((d, tf), lambda i, f: (0, f)),
                  pl.BlockSpec((tf, d), lambda i, f: (f, 0)),
                  pl.BlockSpec((tm, d), lambda i, f: (i, 0)), pl.BlockSpec((1, d), lambda i, f: (0, 0))],
        out_specs=[pl.BlockSpec((tm, d), lambda i, f: (i, 0)), pl.BlockSpec((tm, d), lambda i, f: (i, 0))],
        out_shape=[jax.ShapeDtypeStruct((m, d), F32), jax.ShapeDtypeStruct((m, d), norm_dtype)],
        scratch_shapes=[pltpu.VMEM((tm, d), F32)],
        compiler_params=_params("parallel", "arbitrary"), name="ffn")(xn, w1, w3, w2, h, nw.reshape(1, d))


def _bias_body(tab_ref, d_ref, m_ref, o_ref):
    h = pl.program_id(0)
    d = d_ref[...]
    acc = jnp.full(d.shape, tab_ref[0, h], F32)
    for b in range(1, REL_BUCKETS):
        acc = jnp.where(d >= T5_THR[b], tab_ref[b, h], acc)
    o_ref[...] = acc + m_ref[...]


def bias_from_dist(rel_bias, dist, addmask):
    r = dist.shape[0]
    return pl.pallas_call(
        _bias_body, grid=(N_HEADS,),
        in_specs=[pl.BlockSpec(memory_space=pltpu.SMEM), pl.BlockSpec((r, LANES), lambda h: (0, 0)),
                  pl.BlockSpec((r, LANES), lambda h: (0, 0))],
        out_specs=pl.BlockSpec((None, r, LANES), lambda h: (h, 0, 0)),
        out_shape=jax.ShapeDtypeStruct((N_HEADS, r, LANES), F32),
        compiler_params=_params("parallel"), name="rel_bias")(rel_bias, dist, addmask)


CMP_PAGES = 16


def _cmp1_body(pt_ref, *refs):
    pages = refs[:CMP_PAGES]
    perm = refs[CMP_PAGES][...]
    w_refs = refs[CMP_PAGES + 1:CMP_PAGES + 3]
    o_ref = refs[CMP_PAGES + 3]
    cpp = PAGE_SIZE // CMP_STRIDE
    for kind in range(2):
        w = w_refs[kind][...]
        for kv in range(KV):
            c0 = (kind * KV + kv) * HEAD_DIM
            xs = [_dot(perm, pg[:, c0:c0 + HEAD_DIM].astype(BF16)) for pg in pages]
            pieces = [jnp.concatenate([x[r * cpp:(r + 1) * cpp] for x in xs], axis=0).astype(BF16)
                      for r in range(CMP_STRIDE)]
            o_ref[kind * KV + kv] = _dot(jnp.concatenate(pieces, axis=1), w)


def compress_stage1(rows3, page_table, w1k, w1v):
    nb_seq, n_pages = page_table.shape
    steps = n_pages // CMP_PAGES
    cpp = PAGE_SIZE // CMP_STRIDE
    width = 2 * KV * HEAD_DIM

    def page_spec(i):
        return pl.BlockSpec((None, PAGE_SIZE, width), lambda b, s, pt: (pt[b, s * CMP_PAGES + i], 0, 0))

    wspec = pl.BlockSpec((CMP_STRIDE * HEAD_DIM, 2 * HEAD_DIM), lambda b, s, pt: (0, 0))
    src = np.arange(PAGE_SIZE)
    perm = np.zeros((PAGE_SIZE, PAGE_SIZE), np.float32)
    perm[(src % CMP_STRIDE) * cpp + src // CMP_STRIDE, src] = 1.0
    perm = jnp.asarray(perm, BF16)
    grid_spec = pltpu.PrefetchScalarGridSpec(
        num_scalar_prefetch=1, grid=(nb_seq, steps),
        in_specs=[page_spec(i) for i in range(CMP_PAGES)]
        + [pl.BlockSpec((PAGE_SIZE, PAGE_SIZE), lambda b, s, pt: (0, 0)), wspec, wspec],
        out_specs=pl.BlockSpec((None, 2 * KV, CMP_PAGES * cpp, 2 * HEAD_DIM), lambda b, s, pt: (b, 0, s, 0)))
    return pl.pallas_call(
        _cmp1_body, grid_spec=grid_spec,
        out_shape=jax.ShapeDtypeStruct((nb_seq, 2 * KV, n_pages * cpp, 2 * HEAD_DIM), F32),
        compiler_params=_params("parallel", "arbitrary"), name="compress1")(
            page_table, *([rows3] * CMP_PAGES), perm, w1k, w1v)


def _cmp1_paged_body(pt_ref, *refs):
    cmp_pages = [r.reshape(PAGE_SIZE * SUBLANES, HEAD_DIM) for r in refs[0:2 * CMP_PAGES:2]]
    sel_pages = [r.reshape(PAGE_SIZE * SUBLANES, HEAD_DIM) for r in refs[1:2 * CMP_PAGES:2]]
    w_refs = refs[2 * CMP_PAGES:2 * CMP_PAGES + 2]
    o_ref, sel_ref = refs[2 * CMP_PAGES + 2], refs[2 * CMP_PAGES + 3]
    cpp = PAGE_SIZE // CMP_STRIDE
    for kind in range(2):
        w = w_refs[kind][...]
        for kv in range(KV):
            c = kind * KV + kv
            xs = [jnp.swapaxes(pg[pl.ds(c, PAGE_SIZE, stride=SUBLANES), :].reshape(cpp, CMP_STRIDE, HEAD_DIM), 0, 1)
                  for pg in cmp_pages]
            pieces = [jnp.concatenate([x[r] for x in xs], axis=0).astype(BF16) for r in range(CMP_STRIDE)]
            o_ref[c] = _dot(jnp.concatenate(pieces, axis=1), w)
    for c in range(2 * KV):
        for i, pg in enumerate(sel_pages):
            sel_ref[c, i * PAGE_SIZE:(i + 1) * PAGE_SIZE, :] = (
                pg[pl.ds(c, PAGE_SIZE, stride=SUBLANES), :].astype(BF16))


def compress_stage1_paged(cache5, page_table, w1k, w1v):
    nb_seq, n_pages = page_table.shape
    steps = n_pages // CMP_PAGES
    cpp = PAGE_SIZE // CMP_STRIDE
    assert cache5.shape[2:] == (2, 2 * KV, HEAD_DIM) and 2 * KV == SUBLANES

    def group_spec(i, g):
        return pl.BlockSpec((None, PAGE_SIZE, None, SUBLANES, HEAD_DIM),
                            lambda b, s, pt: (pt[b, s * CMP_PAGES + i], 0, g, 0, 0))

    wspec = pl.BlockSpec((CMP_STRIDE * HEAD_DIM, 2 * HEAD_DIM), lambda b, s, pt: (0, 0))
    grid_spec = pltpu.PrefetchScalarGridSpec(
        num_scalar_prefetch=1, grid=(nb_seq, steps),
        in_specs=[group_spec(i, g) for i in range(CMP_PAGES) for g in range(2)] + [wspec, wspec],
        out_specs=[pl.BlockSpec((None, 2 * KV, CMP_PAGES * cpp, 2 * HEAD_DIM), lambda b, s, pt: (b, 0, s, 0)),
                   pl.BlockSpec((None, 2 * KV, CMP_PAGES * PAGE_SIZE, HEAD_DIM), lambda b, s, pt: (b, 0, s, 0))])
    return pl.pallas_call(
        _cmp1_paged_body, grid_spec=grid_spec,
        out_shape=[jax.ShapeDtypeStruct((nb_seq, 2 * KV, n_pages * cpp, 2 * HEAD_DIM), F32),
                   jax.ShapeDtypeStruct((nb_seq, 2 * KV, n_pages * PAGE_SIZE, HEAD_DIM), BF16)],
        compiler_params=_params("parallel", "arbitrary"), name="compress1_paged")(
            page_table, *([cache5] * (2 * CMP_PAGES)), w1k, w1v)


def _cmp2_body(ab_ref, pe_ref, w1_ref, w2_ref, o_ref):
    nch = ab_ref.shape[0]
    ab = ab_ref[...]
    c = _dot(pe_ref[...].astype(BF16), w1_ref[...].astype(BF16))
    pre = ab[:, :HEAD_DIM] + pltpu.roll(ab[:, HEAD_DIM:], nch - 1, 0) + c[0:1]
    y = _dot((pre * jax.nn.sigmoid(pre)).astype(BF16), w2_ref[...].astype(BF16))
    rowi = lax.broadcasted_iota(I32, y.shape, 0)
    o_ref[...] = jnp.where(rowi < nch - 1, y, 0.0)


def compress_stage2(ab, pe, w1, w2):
    nb_seq, _, nch, _ = ab.shape
    return pl.pallas_call(
        _cmp2_body, grid=(nb_seq, 2 * KV),
        in_specs=[pl.BlockSpec((None, None, nch, 2 * HEAD_DIM), lambda b, j: (b, j, 0, 0)),
                  pl.BlockSpec((None, SUBLANES, CMP_BLOCK * HEAD_DIM), lambda b, j: (j // KV, 0, 0)),
                  pl.BlockSpec((None, CMP_BLOCK * HEAD_DIM, HEAD_DIM), lambda b, j: (j // KV, 0, 0)),
                  pl.BlockSpec((None, HEAD_DIM, HEAD_DIM), lambda b, j: (j // KV, 0, 0))],
        out_specs=pl.BlockSpec((None, None, nch, HEAD_DIM), lambda b, j: (b, j, 0, 0)),
        out_shape=jax.ShapeDtypeStruct((nb_seq, 2 * KV, nch, HEAD_DIM), F32),
        compiler_params=_params("parallel", "parallel"), name="compress2")(ab, pe, w1, w2)


def compress(rows3, page_table, cw):
    ab = compress_stage1(rows3, page_table, cw["w1k"], cw["w1v"])
    return compress_stage2(ab, cw["pe"], cw["w1"], cw["w2"])


def compress_paged(cache3, page_table, cw):
    ab, sel_rows = compress_stage1_paged(cache3, page_table, cw["w1k"], cw["w1v"])
    return compress_stage2(ab, cw["pe"], cw["w1"], cw["w2"]), sel_rows


def _flash_prob(s, mask, m_ref, l_ref, rows):
    m_prev = m_ref[rows, :]
    if mask is not None:
        s = jnp.where(mask, s, MASK_NEG)
    m_new = jnp.maximum(m_prev, jnp.max(s, axis=1, keepdims=True))
    alpha = jnp.exp(m_prev - m_new)
    m_wide = m_new if s.shape[1] == LANES else jnp.concatenate([m_new] * (s.shape[1] // LANES), axis=1)
    p = jnp.exp(s - m_wide)
    l_ref[rows, :] = alpha * l_ref[rows, :] + jnp.sum(p, axis=1, keepdims=True)
    m_ref[rows, :] = m_new
    return p.astype(BF16), alpha


def _flash_update(s, mask, v, m_ref, l_ref, acc_ref, rows):
    p, alpha = _flash_prob(s, mask, m_ref, l_ref, rows)
    acc_ref[rows, :] = alpha * acc_ref[rows, :] + _dot(p, v)


def _flash_init(m_ref, l_ref, acc_ref):
    m_ref[...] = jnp.full(m_ref.shape, NEG_INF, F32)
    l_ref[...] = jnp.zeros_like(l_ref)
    acc_ref[...] = jnp.zeros_like(acc_ref)


def _rank_select(score, eligible, blk, n_blocks, n_top, row_of):
    def body(j, cnt):
        rowj = row_of(j)
        beats = (rowj > score) | ((rowj == score) & (blk > j))
        return cnt + beats.astype(I32)
    cnt = lax.fori_loop(0, n_blocks, body, jnp.zeros(score.shape, I32))
    return jnp.where((cnt < n_top) & eligible, 1.0, 0.0)


def _nsa_prompt_body(q_ref, ks_ref, vs_ref, kw_ref, vw_ref, kc_ref, vc_ref, g_ref, bc_ref, bt_ref, wbt_ref,
                     ovl_ref, exp_ref,
                     o_ref, m_s, l_s, a_s, m_w, l_w, a_w, mk_ref, sc_ref, *, n_cmp, n_blk):
    qi = pl.program_id(2)
    rows = G * TQ
    q = jnp.concatenate([q_ref[:, g * HEAD_DIM:(g + 1) * HEAD_DIM] for g in range(G)], axis=0)
    row5 = lax.broadcasted_iota(I32, (rows, LANES), 0)
    lane5 = lax.broadcasted_iota(I32, (rows, LANES), 1)
    tpos5 = qi * TQ + (row5 & (TQ - 1))

    s = _dot_nt(q, kc_ref[...].astype(BF16))
    s = s + jnp.concatenate([bc_ref[g] for g in range(G)], axis=0)
    mask_c = (tpos5 >= lane5 * CMP_STRIDE + (CMP_BLOCK - 1)) & (lane5 < n_cmp)
    p_c = _masked_softmax(s, mask_c)
    pb = p_c.astype(BF16)
    o_c = _dot(pb, vc_ref[...].astype(BF16))

    nbr = ovl_ref.shape[0]
    imp = _dot_nt(ovl_ref[...], pb[0:TQ])
    for g in range(1, G):
        imp = imp + _dot_nt(ovl_ref[...], pb[g * TQ:(g + 1) * TQ])
    blk = lax.broadcasted_iota(I32, (nbr, TQ), 0)
    tq_l = qi * TQ + lax.broadcasted_iota(I32, (nbr, TQ), 1)
    cur = lax.shift_right_arithmetic(tq_l, int(math.log2(SEL_BLOCK)))
    eligible = (blk * SEL_BLOCK <= tq_l) & (blk < n_blk)
    forced = (blk == 0) | (blk == cur) | (blk == cur - 1)
    score = jnp.where(eligible, imp + jnp.where(forced, FORCE_BONUS, 0.0), -jnp.inf)
    sc_ref[...] = score
    sel_t = _rank_select(score, eligible, blk, n_blk, min(SEL_TOPK, n_blk), lambda j: sc_ref[pl.ds(j, 1), :])
    unsel = jnp.concatenate([1.0 - sel_t, jnp.zeros((LANES - nbr, TQ), F32)], axis=0)
    mk_ref[...] = _dot(unsel.T.astype(BF16), exp_ref[...])
    row1 = lax.broadcasted_iota(I32, (TQ, TQ), 0)
    lane1 = lax.broadcasted_iota(I32, (TQ, TQ), 1)
    off_d = pl.multiple_of(qi * TQ, TQ)
    mk_ref[:, pl.ds(off_d, TQ)] = jnp.where(lane1 <= row1, mk_ref[:, pl.ds(off_d, TQ)], MASK_NEG)

    @pl.when(qi + 1 < pl.num_programs(2))
    def _():
        mk_ref[:, pl.ds(pl.multiple_of((qi + 1) * TQ, TQ), TQ)] = jnp.full((TQ, TQ), MASK_NEG, F32)

    kt2 = 2 * TQ

    def tile_pair(ref, j, index_of):
        tiles = []
        for i in range(2):
            off = pl.multiple_of(index_of(qi - (2 * j + i)) * TQ, TQ)
            tiles.append(jnp.concatenate([ref[g, pl.ds(off, TQ), :] for g in range(G)], axis=0))
        return jnp.concatenate(tiles, axis=1)

    _flash_init(m_s, l_s, a_s)
    _flash_init(m_w, l_w, a_w)
    all_rows = pl.ds(0, rows)

    def pair_step(j, with_win):
        off = pl.multiple_of(j * kt2, kt2)
        s_s = _dot_nt(q, ks_ref[pl.ds(off, kt2), :])
        if with_win:
            s_w = _dot_nt(q, kw_ref[pl.ds(off, kt2), :])
        bias_s = tile_pair(bt_ref, j, lambda d: jnp.clip(d, 0, 2))
        key_mask = jnp.concatenate([mk_ref[:, pl.ds(off, kt2)]] * G, axis=0)
        p_s, alpha_s = _flash_prob(s_s + bias_s + key_mask, None, m_s, l_s, all_rows)
        if with_win:
            bias_w = tile_pair(wbt_ref, j, lambda d: jnp.where(d < 0, WIN_TILES - 1, jnp.minimum(d, WIN_TILES - 1)))
            p_w, alpha_w = _flash_prob(s_w + bias_w, None, m_w, l_w, all_rows)
        a_s[...] = alpha_s * a_s[...] + _dot(p_s, vs_ref[pl.ds(off, kt2), :])
        if with_win:
            a_w[...] = alpha_w * a_w[...] + _dot(p_w, vw_ref[pl.ds(off, kt2), :])

    def sel_only(j, c):
        pair_step(j, False)
        return c

    def sel_and_win(j, c):
        pair_step(j, True)
        return c

    lo = lax.shift_right_arithmetic(jnp.maximum(qi - WINDOW // TQ, 0), 1)
    hi = lax.shift_right_arithmetic(qi, 1) + 1
    lax.fori_loop(0, lo, sel_only, 0)
    lax.fori_loop(lo, hi, sel_and_win, 0)

    o_s = a_s[...] * _safe_inv(l_s[...])
    o_w = a_w[...] * _safe_inv(l_w[...])
    gt = g_ref[...]
    for g in range(G):
        r = slice(g * TQ, (g + 1) * TQ)
        out = (gt[:, 3 * g:3 * g + 1] * o_c[r] + gt[:, 3 * g + 1:3 * g + 2] * o_s[r]
               + gt[:, 3 * g + 2:3 * g + 3] * o_w[r])
        o_ref[:, g * HEAD_DIM:(g + 1) * HEAD_DIM] = out.astype(o_ref.dtype)


def nsa_prompt_attention(q, kv4, kwin, gates, kcb, bias_all, b, t):
    nq = t // TQ
    n_chunks = t // CMP_STRIDE
    n_cmp = n_chunks - CMP_BLOCK // CMP_STRIDE + 1
    n_blk = -(-t // SEL_BLOCK)
    assert n_chunks == LANES and n_blk <= LANES and t % TQ == 0 and TQ >= FAR_DIST
    cmp_start = np.arange(LANES) * CMP_STRIDE
    jst = np.arange(LANES) * SEL_BLOCK
    overlap = ((cmp_start[:, None] < jst[None, :] + SEL_BLOCK) & (cmp_start[:, None] + CMP_BLOCK > jst[None, :])
               & (np.arange(LANES)[:, None] < n_cmp) & (np.arange(LANES)[None, :] < n_blk))
    nbr = -(-n_blk // (2 * SUBLANES)) * 2 * SUBLANES
    ovl_t = jnp.asarray(overlap.T[:nbr], BF16)
    expand = jnp.asarray(np.where(np.arange(LANES)[:, None] == (np.arange(t)[None, :] // SEL_BLOCK), MASK_NEG, 0.0),
                         BF16)
    colblk = lambda base: (lambda bb, kv, qi: (bb, base + kv))
    rowblk = lambda bb, kv, qi: (bb * nq + qi, kv)
    in_specs = [
        pl.BlockSpec((TQ, G * HEAD_DIM), rowblk),
        pl.BlockSpec((t, HEAD_DIM), colblk(2 * KV)), pl.BlockSpec((t, HEAD_DIM), colblk(3 * KV)),
        pl.BlockSpec((t, HEAD_DIM), colblk(0)), pl.BlockSpec((t, HEAD_DIM), colblk(KV)),
        pl.BlockSpec((None, None, n_chunks, HEAD_DIM), lambda bb, kv, qi: (bb, kv, 0, 0)),
        pl.BlockSpec((None, None, n_chunks, HEAD_DIM), lambda bb, kv, qi: (bb, KV + kv, 0, 0)),
        pl.BlockSpec((TQ, LANES), rowblk),
        pl.BlockSpec((G, TQ, LANES), lambda bb, kv, qi: (kv, WIN_TILES + 3 + qi, 0)),
        pl.BlockSpec((G, 3 * TQ, LANES), lambda bb, kv, qi: (kv, WIN_TILES // 3, 0)),
        pl.BlockSpec((G, WIN_TILES * TQ, LANES), lambda bb, kv, qi: (kv, 0, 0)),
        pl.BlockSpec((nbr, LANES), lambda bb, kv, qi: (0, 0)),
        pl.BlockSpec((LANES, t), lambda bb, kv, qi: (0, 0)),
    ]
    assert WIN_TILES % 3 == 0 and TQ == LANES
    rows = G * TQ
    scratch = [pltpu.VMEM((rows, LANES), F32) for _ in range(6)] + [pltpu.VMEM((TQ, t), F32),
                                                                   pltpu.VMEM((nbr, TQ), F32)]
    return pl.pallas_call(
        functools.partial(_nsa_prompt_body, n_cmp=n_cmp, n_blk=n_blk), grid=(b, KV, nq),
        in_specs=in_specs, out_specs=pl.BlockSpec((TQ, G * HEAD_DIM), rowblk),
        out_shape=jax.ShapeDtypeStruct((b * t, N_HEADS * HEAD_DIM), BF16),
        scratch_shapes=scratch,
        compiler_params=_params("parallel", "parallel", "arbitrary"), name="nsa_prompt_attn")(
            q, kv4, kv4, kwin, kwin, kcb, kcb, gates, bias_all, bias_all, bias_all, ovl_t, expand)


def _stick_pre(z, mask):
    sp = jnp.log(1.0 + jnp.exp(-jnp.abs(z)))
    log_beta = jnp.minimum(z, 0.0) - sp
    rest = jnp.maximum(z, 0.0) + sp
    if mask is not None:
        rest = jnp.where(mask, rest, 0.0)
    hi = rest.astype(BF16)
    lo = (rest - hi.astype(F32)).astype(BF16)
    return log_beta, jnp.sum(rest, axis=1, keepdims=True), jnp.concatenate([hi, lo], axis=0)


def _stick_post(log_beta, l2, carry, mask):
    n = log_beta.shape[0]
    a = jnp.exp(log_beta - (l2[0:n] + l2[n:2 * n] + carry))
    return a if mask is None else jnp.where(mask, a, 0.0)


SB_HEADS = 8
SB_DONE = 110.0


def _sb_prompt_body(q_ref, k_ref, v_ref, u_ref, o_ref, carry_ref, acc_ref):
    qi = pl.program_id(2)
    row = lax.broadcasted_iota(I32, (TQ, TQ), 0)
    lane = lax.broadcasted_iota(I32, (TQ, TQ), 1)

    def tile(kt, mask, first):
        off = pl.multiple_of(kt * TQ, TQ)
        cols = [slice(h * HEAD_DIM, (h + 1) * HEAD_DIM) for h in range(SB_HEADS)]
        zs = [_dot_nt(q_ref[:, c], k_ref[pl.ds(off, TQ), c]) for c in cols]
        pre = [_stick_pre(z, mask) for z in zs]
        l3s = [_dot(p[2], u_ref[...]) for p in pre]
        ws = [_stick_post(p[0], l3, 0.0 if first else carry_ref[h], mask).astype(BF16)
              for h, (p, l3) in enumerate(zip(pre, l3s))]
        pvs = [_dot(w, v_ref[pl.ds(off, TQ), c]) for w, c in zip(ws, cols)]
        for h in range(SB_HEADS):
            tot = pre[h][1]
            acc_ref[h] = pvs[h] if first else acc_ref[h] + pvs[h]
            carry_ref[h] = jnp.broadcast_to(tot, (TQ, TQ)) if first else carry_ref[h] + tot

    tile(qi, lane < row, True)

    def unfinished():
        c = carry_ref[0]
        for h in range(1, SB_HEADS):
            c = jnp.minimum(c, carry_ref[h])
        return (jnp.min(c) < SB_DONE).astype(I32)

    def step(state):
        i, _ = state
        tile(qi - 1 - i, None, False)
        return i + 1, unfinished()

    lax.while_loop(lambda state: (state[0] < qi) & (state[1] > 0), step, (jnp.int32(0), unfinished()))
    for h in range(SB_HEADS):
        o_ref[:, h * HEAD_DIM:(h + 1) * HEAD_DIM] = acc_ref[h].astype(o_ref.dtype)


def _suffix_matrix():
    j = np.arange(TQ)
    return jnp.asarray(j[:, None] > j[None, :], BF16)


def sb_prompt_attention(q, kvp, b, t):
    nq = t // TQ
    hg = N_HEADS // SB_HEADS
    w = SB_HEADS * HEAD_DIM
    rowblk = lambda bb, h, qi: (bb * nq + qi, h)
    return pl.pallas_call(
        _sb_prompt_body, grid=(b, hg, nq),
        in_specs=[pl.BlockSpec((TQ, w), rowblk),
                  pl.BlockSpec((t, w), lambda bb, h, qi: (bb, h)),
                  pl.BlockSpec((t, w), lambda bb, h, qi: (bb, hg + h)),
                  pl.BlockSpec((TQ, TQ), lambda bb, h, qi: (0, 0))],
        out_specs=pl.BlockSpec((TQ, w), rowblk),
        out_shape=jax.ShapeDtypeStruct((b * t, N_HEADS * HEAD_DIM), BF16),
        scratch_shapes=[pltpu.VMEM((SB_HEADS, TQ, TQ), F32), pltpu.VMEM((SB_HEADS, TQ, HEAD_DIM), F32)],
        compiler_params=_params("parallel", "parallel", "arbitrary"), name="sb_prompt_attn")(
            q, kvp, kvp, _suffix_matrix())


def _nsa_sample_select_body(q_ref, kcb_ref, bc_ref, mc_ref, ovl_ref, fadd_ref, elig_ref, oc_ref, sel_ref, sc_ref,
                            *, n_blk):
    nbp = kcb_ref.shape[1]
    grp = G * T8
    ps = []
    for kv in range(KV):
        qk = jnp.concatenate([q_ref[:, (kv * G + g) * HEAD_DIM:(kv * G + g + 1) * HEAD_DIM] for g in range(G)],
                             axis=0)
        s = _dot_nt(qk, kcb_ref[kv].astype(BF16)) + bc_ref[kv * grp:(kv + 1) * grp, :]
        p = _masked_softmax(s, mc_ref[...] > 0.5)
        oc_ref[kv * grp:(kv + 1) * grp, :] = _dot(p.astype(BF16), kcb_ref[KV + kv].astype(BF16))
        ps.append(p)
    pad = jnp.zeros((LANES - KV * T8, nbp), F32)
    imp = None
    for g in range(G):
        xg = jnp.concatenate([ps[kv][g * T8:(g + 1) * T8] for kv in range(KV)] + [pad], axis=0)
        part = _dot_nt(ovl_ref[...], xg.astype(BF16))
        imp = part if imp is None else imp + part
    eligible = elig_ref[...] > 0.5
    score = jnp.where(eligible, imp + fadd_ref[...], -jnp.inf)
    sc_ref[...] = score
    blk = lax.broadcasted_iota(I32, score.shape, 0)
    sel_ref[...] = _rank_select(score, eligible, blk, n_blk, min(SEL_TOPK, n_blk),
                                lambda j: sc_ref[pl.ds(j, 1), :])


SEL_KEYS = 1024


def _nsa_sample_attn_body(q_ref, kv_ref, sel_ref, new_ref, win_ref, wnew_ref, g_ref, oc_ref,
                          bl_ref, bf_ref, bn_ref, bw_ref, ex_ref, o_ref, m_s, l_s, a_s, *, n_steps, t_new, past):
    step = pl.program_id(1)
    grp = G * T8
    kvw = KV * HEAD_DIM
    wb = win_ref.shape[0] // (2 * KV)
    is_last = step == n_steps - 1

    @pl.when(step == 0)
    def _():
        _flash_init(m_s, l_s, a_s)

    def q_of(kv):
        return jnp.concatenate([q_ref[:, (kv * G + g) * HEAD_DIM:(kv * G + g + 1) * HEAD_DIM] for g in range(G)],
                               axis=0)

    def pad_keys(x):
        return jnp.concatenate([x, jnp.zeros((TQ - x.shape[0], x.shape[1]), x.dtype)], axis=0).astype(BF16)

    def rows_of_blocks(piece):
        return jnp.concatenate([piece, jnp.zeros((LANES - piece.shape[0], LANES), F32)], axis=0).T

    bps = SEL_KEYS // SEL_BLOCK
    piece = sel_ref[pl.ds(pl.multiple_of(step * bps, bps), bps), :]
    mk = _dot(rows_of_blocks(piece).astype(BF16), ex_ref[...])

    kv_rows = [pl.ds(kv * grp, grp) for kv in range(KV)]
    scores = [_dot_nt(q_of(kv), kv_ref[kv]) for kv in range(KV)]
    probs = []
    for kv in range(KV):
        bias = jnp.where(is_last, bl_ref[kv_rows[kv], :], bf_ref[kv_rows[kv], 0:1])
        mask = jnp.concatenate([mk[kv * T8:(kv + 1) * T8]] * G, axis=0) > 0.5
        probs.append(_flash_prob(scores[kv] + bias, mask, m_s, l_s, kv_rows[kv]))
    for kv in range(KV):
        p, alpha = probs[kv]
        a_s[kv_rows[kv], :] = alpha * a_s[kv_rows[kv], :] + _dot(p, kv_ref[KV + kv])

    @pl.when(is_last)
    def _():
        lane = lax.broadcasted_iota(I32, (grp, LANES), 1)
        t_row = lax.broadcasted_iota(I32, (grp, LANES), 0) & (T8 - 1)
        lane_w = lax.broadcasted_iota(I32, (grp, wb), 1)
        t_row_w = lax.broadcasted_iota(I32, (grp, wb), 0) & (T8 - 1)
        new_blk = past // SEL_BLOCK
        picked_all = rows_of_blocks(sel_ref[new_blk:new_blk + SUBLANES, :])
        gt = g_ref[...]
        for kv in range(KV):
            rows = pl.ds(kv * grp, grp)
            qk = q_of(kv)
            kn = pad_keys(new_ref[:, 2 * kvw + kv * HEAD_DIM:2 * kvw + (kv + 1) * HEAD_DIM])
            vn = pad_keys(new_ref[:, 3 * kvw + kv * HEAD_DIM:3 * kvw + (kv + 1) * HEAD_DIM])
            s = _dot_nt(qk, kn) + bn_ref[rows, :]
            picked = jnp.concatenate([picked_all[kv * T8:(kv + 1) * T8, 0:1]] * G, axis=0) > 0.5
            mask = (lane <= t_row) & (lane < t_new) & picked
            _flash_update(s, mask, vn, m_s, l_s, a_s, rows)
            o_s = a_s[rows, :] * _safe_inv(l_s[rows, :])
            kw = win_ref[pl.ds(kv, wb, stride=2 * KV), :].astype(BF16)
            vw = win_ref[pl.ds(KV + kv, wb, stride=2 * KV), :].astype(BF16)
            kwn = pad_keys(wnew_ref[:, kv * HEAD_DIM:(kv + 1) * HEAD_DIM])
            vwn = pad_keys(wnew_ref[:, kvw + kv * HEAD_DIM:kvw + (kv + 1) * HEAD_DIM])
            s1 = _dot_nt(qk, kw) + bw_ref[rows, 0:wb]
            s2 = _dot_nt(qk, kwn) + bw_ref[rows, wb:wb + LANES]
            d1 = wb + t_row_w - lane_w
            mask1 = (d1 >= 0) & (d1 < WINDOW)
            d2 = t_row - lane
            mask2 = (d2 >= 0) & (d2 < WINDOW) & (lane < t_new)
            m = jnp.maximum(jnp.max(jnp.where(mask1, s1, NEG_INF), axis=1, keepdims=True),
                            jnp.max(jnp.where(mask2, s2, NEG_INF), axis=1, keepdims=True))
            e1 = jnp.where(mask1, jnp.exp(s1 - m), 0.0)
            e2 = jnp.where(mask2, jnp.exp(s2 - m), 0.0)
            l = jnp.sum(e1, axis=1, keepdims=True) + jnp.sum(e2, axis=1, keepdims=True)
            o_w = (_dot(e1.astype(BF16), vw) + _dot(e2.astype(BF16), vwn)) * _safe_inv(l)
            o_c = oc_ref[rows, :]
            for g in range(G):
                r = slice(g * T8, (g + 1) * T8)
                c = kv * LANES + 3 * g
                out = gt[:, c:c + 1] * o_c[r] + gt[:, c + 1:c + 2] * o_s[r] + gt[:, c + 2:c + 3] * o_w[r]
                o_ref[:, (kv * G + g) * HEAD_DIM:(kv * G + g + 1) * HEAD_DIM] = out.astype(o_ref.dtype)


def nsa_sample_attention(q8, sel_rows, kcb, new8, win3, wnew8, gates8, bias_s, past, t_new):
    bd = q8.shape[0]
    n_chunks = kcb.shape[2]
    n_cmp = n_chunks - CMP_BLOCK // CMP_STRIDE + 1
    total = past + t_new
    n_blk = -(-total // SEL_BLOCK)
    nblkp = -(-(n_blk + SUBLANES) // (2 * SUBLANES)) * 2 * SUBLANES
    wb = win3.shape[1] // (2 * KV)
    n_steps = past // SEL_KEYS
    assert past % SEL_KEYS == 0 and SEL_KEYS % SEL_BLOCK == 0 and t_new <= T8 and PAGE_SIZE > FAR_DIST
    assert (total // CMP_STRIDE) == n_chunks and wb % LANES == 0 and SEL_KEYS // SEL_BLOCK <= LANES

    pos_q = past + np.arange(T8)
    cmp_start = np.arange(n_chunks) * CMP_STRIDE
    cmp_end = cmp_start + CMP_BLOCK - 1
    mask_c = ((pos_q[:, None] >= cmp_end[None, :]) & (np.arange(n_chunks)[None, :] < n_cmp))
    mask_c = jnp.asarray(np.tile(mask_c, (G, 1)), F32)
    jst = np.arange(nblkp) * SEL_BLOCK
    overlap = ((cmp_start[:, None] < jst[None, :] + SEL_BLOCK) & (cmp_start[:, None] + CMP_BLOCK > jst[None, :])
               & (np.arange(n_chunks)[:, None] < n_cmp) & (np.arange(nblkp)[None, :] < n_blk))
    ovl_t = jnp.asarray(overlap.T, BF16)
    blk = np.arange(nblkp)[:, None]
    pq = np.tile(pos_q, KV)[None, :]
    tvalid = np.tile(np.arange(T8) < t_new, KV)[None, :]
    cur = pq // SEL_BLOCK
    elig = (blk * SEL_BLOCK <= pq) & (blk < n_blk) & tvalid
    forced = (blk == 0) | (blk == cur) | (blk == cur - 1)
    pad_l = LANES - KV * T8
    elig = jnp.asarray(np.pad(elig, ((0, 0), (0, pad_l))), F32)
    fadd = jnp.asarray(np.pad(np.where(forced, FORCE_BONUS, 0.0), ((0, 0), (0, pad_l))), F32)

    grp_all = KV * G * T8
    o_c, sel = pl.pallas_call(
        functools.partial(_nsa_sample_select_body, n_blk=n_blk), grid=(bd,),
        in_specs=[pl.BlockSpec((None, T8, N_HEADS * HEAD_DIM), lambda b: (b, 0, 0)),
                  pl.BlockSpec((None, 2 * KV, n_chunks, HEAD_DIM), lambda b: (b, 0, 0, 0)),
                  pl.BlockSpec((grp_all, n_chunks), lambda b: (0, 0)),
                  pl.BlockSpec((G * T8, n_chunks), lambda b: (0, 0)),
                  pl.BlockSpec((nblkp, n_chunks), lambda b: (0, 0)),
                  pl.BlockSpec((nblkp, LANES), lambda b: (0, 0)),
                  pl.BlockSpec((nblkp, LANES), lambda b: (0, 0))],
        out_specs=[pl.BlockSpec((None, grp_all, HEAD_DIM), lambda b: (b, 0, 0)),
                   pl.BlockSpec((None, nblkp, LANES), lambda b: (b, 0, 0))],
        out_shape=[jax.ShapeDtypeStruct((bd, grp_all, HEAD_DIM), F32),
                   jax.ShapeDtypeStruct((bd, nblkp, LANES), F32)],
        scratch_shapes=[pltpu.VMEM((nblkp, LANES), F32)],
        compiler_params=_params("parallel"), name="nsa_sample_select")(
            q8, kcb, bias_s["cmp"], mask_c, ovl_t, fadd, elig)

    kvw = KV * HEAD_DIM
    const2 = lambda b, s: (0, 0)
    per_b = lambda b, s: (b, 0, 0)
    expand = jnp.asarray(np.arange(LANES)[:, None] == (np.arange(SEL_KEYS)[None, :] // SEL_BLOCK), BF16)
    bias_last = jnp.concatenate([bias_s["far"]] * (SEL_KEYS // LANES - 1) + [bias_s["last"]], axis=1)
    return pl.pallas_call(
        functools.partial(_nsa_sample_attn_body, n_steps=n_steps, t_new=t_new, past=past), grid=(bd, n_steps),
        in_specs=[pl.BlockSpec((None, T8, N_HEADS * HEAD_DIM), per_b),
                  pl.BlockSpec((None, 2 * KV, SEL_KEYS, HEAD_DIM), lambda b, s: (b, 0, s, 0)),
                  pl.BlockSpec((None, nblkp, LANES), per_b),
                  pl.BlockSpec((None, T8, 4 * kvw), per_b),
                  pl.BlockSpec((None, wb * 2 * KV, HEAD_DIM), per_b),
                  pl.BlockSpec((None, T8, 2 * kvw), per_b),
                  pl.BlockSpec((None, T8, KV * LANES), per_b),
                  pl.BlockSpec((None, grp_all, HEAD_DIM), per_b),
                  pl.BlockSpec((grp_all, SEL_KEYS), const2), pl.BlockSpec((grp_all, LANES), const2),
                  pl.BlockSpec((grp_all, LANES), const2), pl.BlockSpec((grp_all, wb + LANES), const2),
                  pl.BlockSpec((LANES, SEL_KEYS), const2)],
        out_specs=pl.BlockSpec((None, T8, N_HEADS * HEAD_DIM), per_b),
        out_shape=jax.ShapeDtypeStruct((bd, T8, N_HEADS * HEAD_DIM), BF16),
        scratch_shapes=[pltpu.VMEM((grp_all, LANES), F32) for _ in range(3)],
        compiler_params=_params("parallel", "arbitrary"), name="nsa_sample_attn")(
            q8, sel_rows, sel, new8, win3, wnew8, gates8, o_c,
            bias_last, bias_s["far"], bias_s["new"], bias_s["win"], expand)


SB_PAGES = 8
SB_GROUPS = 2 * N_HEADS // SUBLANES


def _sb_sample_run(q_ref, u, carry_ref, acc_ref, n_sub, k_of, v_of, masks):
    def q_of(h):
        return q_ref[h * T8:(h + 1) * T8, :].astype(BF16)

    zs = [jnp.concatenate([_dot_nt(q_of(h), k_of(j, h)) for h in range(N_HEADS)], axis=0)
          for j in range(n_sub)]
    pre = [_stick_pre(zs[j], masks[j]) for j in range(n_sub)]
    l2s = [_dot(pre[j][2], u) for j in range(n_sub)]
    carry = carry_ref[...]
    ws = [None] * n_sub
    for j in reversed(range(n_sub)):
        ws[j] = _stick_post(pre[j][0], l2s[j], carry, masks[j])
        carry = carry + pre[j][1]
    carry_ref[...] = carry
    for h in range(N_HEADS):
        r = slice(h * T8, (h + 1) * T8)
        out = _dot(ws[0][r].astype(BF16), v_of(0, h))
        for j in range(1, n_sub):
            out = out + _dot(ws[j][r].astype(BF16), v_of(j, h))
        acc_ref[r, :] += out


def _sb_page_rows(refs):
    groups = [r.reshape(PAGE_SIZE * SUBLANES, HEAD_DIM) for r in refs]

    def head_rows(j, row):
        g = groups[j * SB_GROUPS + row // SUBLANES]
        return g[pl.ds(row % SUBLANES, PAGE_SIZE, stride=SUBLANES), :].astype(BF16)
    return head_rows


def _sb_sample_head_body(pt_ref, q_ref, *refs, t_new):
    n_grp = SB_PAGES * SB_GROUPS
    head_rows = _sb_page_rows(refs[:n_grp])
    new_ref, u_ref, acc_ref, carry_ref = refs[n_grp:]
    hw = N_HEADS * HEAD_DIM
    carry_ref[...] = jnp.zeros_like(carry_ref)
    acc_ref[...] = jnp.zeros_like(acc_ref)
    pad = jnp.zeros((TQ - T8, 2 * hw), F32)
    kvn = jnp.concatenate([new_ref[...], pad], axis=0)
    lane = lax.broadcasted_iota(I32, (TQ, TQ), 1)
    t_row = lax.broadcasted_iota(I32, (TQ, TQ), 0) & (T8 - 1)
    _sb_sample_run(q_ref, u_ref[...], carry_ref, acc_ref, 1,
                   lambda j, h: kvn[:, h * HEAD_DIM:(h + 1) * HEAD_DIM].astype(BF16),
                   lambda j, h: kvn[:, hw + h * HEAD_DIM:hw + (h + 1) * HEAD_DIM].astype(BF16),
                   [(lane < t_row) & (lane < t_new)])
    _sb_sample_run(q_ref, u_ref[...], carry_ref, acc_ref, SB_PAGES, lambda j, h: head_rows(j, h),
                   lambda j, h: head_rows(j, N_HEADS + h), [None] * SB_PAGES)


def _sb_sample_tail_body(pt_ref, need_ref, q_ref, *refs, n_steps):
    n_grp = SB_PAGES * SB_GROUPS
    head_rows = _sb_page_rows(refs[:n_grp])
    u_ref, acc_in, carry_in, o_ref, carry_ref, acc_ref = refs[n_grp:]
    step = pl.program_id(1)

    @pl.when(step == 0)
    def _():
        carry_ref[...] = carry_in[...]
        acc_ref[...] = acc_in[...]

    @pl.when((need_ref[pl.program_id(0)] > 0) & (jnp.min(carry_ref[...]) < SB_DONE))
    def _():
        _sb_sample_run(q_ref, u_ref[...], carry_ref, acc_ref, SB_PAGES, lambda j, h: head_rows(j, h),
                       lambda j, h: head_rows(j, N_HEADS + h), [None] * SB_PAGES)

    @pl.when(step == n_steps - 1)
    def _():
        o_ref[...] = acc_ref[...].astype(o_ref.dtype)


def sb_sample_attention(q_rows, cache5, page_table, new8, t_new):
    bd, n_pages = page_table.shape
    hw = N_HEADS * HEAD_DIM
    n_steps = n_pages // SB_PAGES - 1
    assert n_pages % SB_PAGES == 0 and n_steps >= 1 and cache5.shape[2] == SB_GROUPS
    rows = N_HEADS * T8
    grp_block = (None, PAGE_SIZE, None, SUBLANES, HEAD_DIM)
    n_grp = SB_PAGES * SB_GROUPS
    u = _suffix_matrix()

    def head_spec(i, g):
        return pl.BlockSpec(grp_block, lambda b, pt: (pt[b, n_pages - SB_PAGES + i], 0, g, 0, 0))

    acc, carry = pl.pallas_call(
        functools.partial(_sb_sample_head_body, t_new=t_new),
        grid_spec=pltpu.PrefetchScalarGridSpec(
            num_scalar_prefetch=1, grid=(bd,),
            in_specs=[pl.BlockSpec((None, rows, HEAD_DIM), lambda b, pt: (b, 0, 0))]
            + [head_spec(i, g) for i in range(SB_PAGES) for g in range(SB_GROUPS)]
            + [pl.BlockSpec((None, T8, 2 * hw), lambda b, pt: (b, 0, 0)),
               pl.BlockSpec((TQ, TQ), lambda b, pt: (0, 0))],
            out_specs=[pl.BlockSpec((None, rows, HEAD_DIM), lambda b, pt: (b, 0, 0)),
                       pl.BlockSpec((None, rows, LANES), lambda b, pt: (b, 0, 0))]),
        out_shape=[jax.ShapeDtypeStruct((bd, rows, HEAD_DIM), F32), jax.ShapeDtypeStruct((bd, rows, LANES), F32)],
        compiler_params=_params("parallel"), name="sb_sample_head")(
            page_table, q_rows, *([cache5] * n_grp), new8, u)

    need = (jnp.min(carry, axis=(1, 2)) < SB_DONE).astype(I32)

    def tail_spec(i, g):
        def index(b, s, pt, nd):
            page = jnp.where(nd[b] > 0, n_pages - (s + 2) * SB_PAGES + i, n_pages - SB_PAGES + i)
            return (pt[b, page], 0, g, 0, 0)
        return pl.BlockSpec(grp_block, index)

    per_b = lambda b, s, pt, nd: (b, 0, 0)
    return pl.pallas_call(
        functools.partial(_sb_sample_tail_body, n_steps=n_steps),
        grid_spec=pltpu.PrefetchScalarGridSpec(
            num_scalar_prefetch=2, grid=(bd, n_steps),
            in_specs=[pl.BlockSpec((None, rows, HEAD_DIM), per_b)]
            + [tail_spec(i, g) for i in range(SB_PAGES) for g in range(SB_GROUPS)]
            + [pl.BlockSpec((TQ, TQ), lambda b, s, pt, nd: (0, 0)),
               pl.BlockSpec((None, rows, HEAD_DIM), per_b), pl.BlockSpec((None, rows, LANES), per_b)],
            out_specs=pl.BlockSpec((None, rows, HEAD_DIM), per_b),
            scratch_shapes=[pltpu.VMEM((rows, LANES), F32), pltpu.VMEM((rows, HEAD_DIM), F32)]),
        out_shape=jax.ShapeDtypeStruct((bd, rows, HEAD_DIM), BF16),
        compiler_params=_params("parallel", "arbitrary"), name="sb_sample_tail")(
            page_table, need, q_rows, *([cache5] * n_grp), u, acc, carry)


def _pad_tokens(x, bd, t_new):
    x = x.reshape(bd, t_new, x.shape[-1])
    return jnp.pad(x, ((0, 0), (0, T8 - t_new), (0, 0)))


def _bias_distances(t, past, n_chunks_s, wb):
    r = np.arange(TQ)[:, None]
    c = np.arange(LANES)[None, :]
    win_tiles = np.concatenate([r - c + off * TQ for off in range(WIN_TILES)], axis=0)
    toeplitz = np.concatenate([r - c + off * TQ for off in range(3)], axis=0)
    cmp_p = np.arange(t)[:, None] - (c * CMP_STRIDE + CMP_BLOCK - 1)
    pos_q = past + np.arange(T8)[:, None]
    cmp_s = (pos_q - (np.arange(n_chunks_s)[None, :] * CMP_STRIDE + CMP_BLOCK - 1)).reshape(-1, LANES)
    last = pos_q - (past - PAGE_SIZE + c)
    far = np.full((T8, LANES), 4 * REL_MAX_DIST)
    new = np.arange(T8)[:, None] - c
    win = (wb + np.arange(T8)[:, None] - np.arange(wb + LANES)[None, :]).reshape(-1, LANES)
    parts = [win_tiles, toeplitz, cmp_p, cmp_s, last, far, new, win]
    rows = sum(p.shape[0] for p in parts)
    pad = -rows % LANES
    parts.append(np.zeros((pad, LANES), np.int64))
    offs = np.cumsum([0] + [p.shape[0] for p in parts])
    dist = np.concatenate(parts, axis=0)
    addmask = np.zeros(dist.shape, np.float32)
    addmask[:win_tiles.shape[0]] = np.where((win_tiles >= 0) & (win_tiles < WINDOW), 0.0, NEG_INF)
    return jnp.asarray(dist, I32), jnp.asarray(addmask), offs


def _sample_bias(bias_all, offs, n_chunks_s, wb):
    def rows_kgt(x, width):
        return x.reshape(N_HEADS * T8, width)
    seg = lambda i: bias_all[:, offs[i]:offs[i + 1]]
    return {"cmp": rows_kgt(seg(3), n_chunks_s), "last": rows_kgt(seg(4), LANES), "far": rows_kgt(seg(5), LANES),
            "new": rows_kgt(seg(6), LANES), "win": rows_kgt(seg(7), wb + LANES)}


def kernel(x_prompt, x_sample, cache_nsa, state_nsa_win, cache_sb, page_table, rel_bias, norm_w, final_norm_w,
           nsa_w_in, nsa_w_out, nsa_pe_k, nsa_pe_v, nsa_phi_k1, nsa_phi_k2, nsa_phi_v1, nsa_phi_v2,
           sb_w_in, sb_w_out, ffn_w1, ffn_w3, ffn_w2):
    b, t, d = x_prompt.shape
    bd, t_new, _ = x_sample.shape
    n_pages = page_table.shape[1]
    past = n_pages * PAGE_SIZE
    wb = state_nsa_win.shape[2]
    hd = N_HEADS * HEAD_DIM
    kvw = KV * HEAD_DIM
    n_chunks_s = (past + t_new) // CMP_STRIDE
    half = CMP_STRIDE * HEAD_DIM

    dist, addmask, offs = _bias_distances(t, past, n_chunks_s, wb)
    bias_all = bias_from_dist(rel_bias, dist, addmask)
    bias_s = _sample_bias(bias_all, offs, n_chunks_s, wb)

    w_in = nsa_w_in[0]
    wq = (w_in[:, :hd] * SCALE).astype(BF16)
    wkv4 = w_in[:, hd:hd + 4 * kvw].astype(BF16)
    wwin = w_in[:, hd + 4 * kvw:hd + 6 * kvw].astype(BF16)
    wg = w_in[:, hd + 6 * kvw:].reshape(d, KV, G * 3)
    wg = jnp.pad(wg, ((0, 0), (0, 0), (0, LANES - G * 3))).reshape(d, KV * LANES).astype(BF16)
    w_out_nsa = nsa_w_out[0].astype(BF16)
    cw = {
        "w1k": jnp.concatenate([nsa_phi_k1[0, :half], nsa_phi_k1[0, half:]], axis=1).astype(BF16),
        "w1v": jnp.concatenate([nsa_phi_v1[0, :half], nsa_phi_v1[0, half:]], axis=1).astype(BF16),
        "pe": jnp.broadcast_to(jnp.stack([nsa_pe_k[0].reshape(1, -1), nsa_pe_v[0].reshape(1, -1)]),
                               (2, SUBLANES, CMP_BLOCK * HEAD_DIM)),
        "w1": jnp.stack([nsa_phi_k1[0], nsa_phi_v1[0]]),
        "w2": jnp.stack([nsa_phi_k2[0], nsa_phi_v2[0]]),
    }
    wsb = sb_w_in[0]
    wsq = (wsb[:, :hd] * SCALE).astype(BF16)
    wskv = wsb[:, hd:].astype(BF16)
    w_out_sb = sb_w_out[0].astype(BF16)
    w1 = ffn_w1.astype(BF16)
    w3 = ffn_w3.astype(BF16)
    w2 = ffn_w2.astype(BF16)

    hp = x_prompt.reshape(b * t, d)
    hs = x_sample.reshape(bd * t_new, d)

    xp = rmsnorm_rows(hp, norm_w[0, 0], BF16)
    xs = rmsnorm_rows(hs, norm_w[0, 0], BF16)
    q_p, q_s = matmul(xp, wq, (BF16,)), matmul(xs, wq, (BF16,))
    (kv4_p, kv4b_p), kv4_s = matmul(xp, wkv4, (F32, BF16)), matmul(xs, wkv4, (F32,))
    (win_p, winb_p), win_s = matmul(xp, wwin, (F32, BF16)), matmul(xs, wwin, (F32,))
    g_p, g_s = matmul(xp, wg, (F32,), sigmoid=True), matmul(xs, wg, (F32,), sigmoid=True)

    pt_prompt = jnp.arange(b * (t // PAGE_SIZE), dtype=I32).reshape(b, t // PAGE_SIZE)
    kcb_p = compress(kv4_p.reshape(b * t // PAGE_SIZE, PAGE_SIZE, 4 * kvw), pt_prompt, cw)
    attn_p = nsa_prompt_attention(q_p, kv4b_p, winb_p, g_p, kcb_p, bias_all, b, t)

    cache5 = cache_nsa[0].reshape(cache_nsa.shape[1], PAGE_SIZE, 2, 2 * KV, HEAD_DIM)
    kcb_s, sel_rows = compress_paged(cache5, page_table, cw)
    attn_s = nsa_sample_attention(
        _pad_tokens(q_s, bd, t_new), sel_rows, kcb_s, _pad_tokens(kv4_s, bd, t_new),
        state_nsa_win[0].reshape(bd, wb * 2 * KV, HEAD_DIM), _pad_tokens(win_s, bd, t_new),
        _pad_tokens(g_s, bd, t_new), bias_s, past, t_new)
    attn_s = attn_s[:, :t_new].reshape(bd * t_new, hd)

    hp, xp = matmul_res_norm(attn_p, w_out_nsa, hp, norm_w[0, 1])
    hs, xs = matmul_res_norm(attn_s, w_out_nsa, hs, norm_w[0, 1])
    hp, xp = ffn_res_norm(xp, w1[0], w3[0], w2[0], hp, norm_w[1, 0], BF16)
    hs, xs = ffn_res_norm(xs, w1[0], w3[0], w2[0], hs, norm_w[1, 0], BF16)

    sq_p, sq_s = matmul(xp, wsq, (BF16,)), matmul(xs, wsq, (BF16,))
    (skv_p, skvb_p), skv_s = matmul(xp, wskv, (F32, BF16)), matmul(xs, wskv, (F32,))
    sattn_p = sb_prompt_attention(sq_p, skvb_p, b, t)

    q_rows = (_pad_tokens(sq_s, bd, t_new).reshape(bd, T8, N_HEADS, HEAD_DIM).transpose(0, 2, 1, 3)
              .reshape(bd, N_HEADS * T8, HEAD_DIM).astype(F32))
    csb5 = cache_sb[0].reshape(cache_sb.shape[1], PAGE_SIZE, SB_GROUPS, SUBLANES, HEAD_DIM)
    sattn_s = sb_sample_attention(q_rows, csb5, page_table, _pad_tokens(skv_s, bd, t_new), t_new)
    sattn_s = sattn_s.reshape(bd, N_HEADS, T8, HEAD_DIM)[:, :, :t_new].transpose(0, 2, 1, 3).reshape(bd * t_new, hd)

    hp, xp = matmul_res_norm(sattn_p, w_out_sb, hp, norm_w[1, 1])
    hs, xs = matmul_res_norm(sattn_s, w_out_sb, hs, norm_w[1, 1])
    _, y_p = ffn_res_norm(xp, w1[1], w3[1], w2[1], hp, final_norm_w, F32)
    _, y_s = ffn_res_norm(xs, w1[1], w3[1], w2[1], hs, final_norm_w, F32)

    win_keep = min(WINDOW, t)
    nsa_win_prompt = win_p.reshape(b, t, 2, KV, HEAD_DIM)[:, t - win_keep:]
    win_all = jnp.concatenate([state_nsa_win[0], win_s.reshape(bd, t_new, 2, KV, HEAD_DIM)], axis=1)
    return (y_p.reshape(b, t, d), y_s.reshape(bd, t_new, d),
            kv4_p.reshape(1, b, t, 4, KV, HEAD_DIM), kv4_s.reshape(1, bd, t_new, 4, KV, HEAD_DIM),
            nsa_win_prompt[None], win_all[:, t_new:][None],
            skv_p.reshape(1, b, t, 2, N_HEADS, HEAD_DIM), skv_s.reshape(1, bd, t_new, 2, N_HEADS, HEAD_DIM))
```

```python
import functools
import math

import numpy as np
import jax
import jax.numpy as jnp
from jax import lax
from jax.experimental import pallas as pl
from jax.experimental.pallas import tpu as pltpu

F32 = jnp.float32
BF16 = jnp.bfloat16
I32 = jnp.int32

HEAD_DIM = 128
N_HEADS = 16
NSA_KV_HEADS = 4
NSA_GROUP = N_HEADS // NSA_KV_HEADS
CMP_BLOCK = 32
CMP_STRIDE = 16
SEL_BLOCK = 64
SEL_TOPK = 16
WINDOW = 512
FORCE_BONUS = 1000.0
REL_BUCKETS = 32
REL_MAX_DIST = 128
RMS_EPS = 1e-6
NEG_INF = -1e30
MASK_NEG = 2 * NEG_INF
SCALE = HEAD_DIM ** -0.5
PAGE_SIZE = 128

LANES = 128
SUBLANES = 8
TQ = 128
WIN_TILES = WINDOW // TQ + 2
T8 = SUBLANES
VMEM_LIMIT = 56 * 1024 * 1024
KV = NSA_KV_HEADS
G = NSA_GROUP


def _t5_thresholds():
    d = np.arange(0, 4 * REL_MAX_DIST)
    max_exact = REL_BUCKETS // 2
    nf = np.maximum(d, 1).astype(np.float32)
    large = max_exact + (np.log(nf / np.float32(max_exact)) / np.float32(math.log(REL_MAX_DIST / max_exact))
                         * np.float32(REL_BUCKETS - max_exact)).astype(np.int32)
    b = np.where(d < max_exact, d, np.minimum(large, REL_BUCKETS - 1))
    return [int(np.argmax(b >= k)) for k in range(REL_BUCKETS)]


T5_THR = _t5_thresholds()
FAR_DIST = T5_THR[-1]


def _dot(a, b):
    return jnp.dot(a, b, preferred_element_type=F32)


def _dot_nt(a, b):
    return lax.dot_general(a, b, (((1,), (1,)), ((), ())), preferred_element_type=F32)


def _params(*sem):
    return pltpu.CompilerParams(dimension_semantics=sem, vmem_limit_bytes=VMEM_LIMIT)


def _rms(x, w):
    return x * lax.rsqrt(jnp.mean(x * x, axis=-1, keepdims=True) + RMS_EPS) * w


def _masked_softmax(s, mask):
    sm = jnp.where(mask, s, NEG_INF)
    m = jnp.max(sm, axis=1, keepdims=True)
    e = jnp.where(mask, jnp.exp(sm - m), 0.0)
    l = jnp.sum(e, axis=1, keepdims=True)
    return e * jnp.where(l > 0.0, 1.0 / l, 0.0)


def _safe_inv(l):
    return jnp.where(l > 0.0, 1.0 / l, 0.0)


def _rmsnorm_body(x_ref, w_ref, o_ref):
    o_ref[...] = _rms(x_ref[...], w_ref[...]).astype(o_ref.dtype)


def rmsnorm_rows(x, w, out_dtype):
    m, d = x.shape
    tm = min(m, 512)
    return pl.pallas_call(
        _rmsnorm_body, grid=(m // tm,),
        in_specs=[pl.BlockSpec((tm, d), lambda i: (i, 0)), pl.BlockSpec((1, d), lambda i: (0, 0))],
        out_specs=pl.BlockSpec((tm, d), lambda i: (i, 0)),
        out_shape=jax.ShapeDtypeStruct((m, d), out_dtype),
        compiler_params=_params("parallel"), name="rmsnorm")(x, w.reshape(1, d))


def _matmul_body(a_ref, w_ref, *o_refs, sigmoid):
    y = _dot(a_ref[...], w_ref[...])
    if sigmoid:
        y = jax.nn.sigmoid(y)
    for o_ref in o_refs:
        o_ref[...] = y.astype(o_ref.dtype)


def matmul(a, w, out_dtypes, sigmoid=False):
    m, k = a.shape
    n = w.shape[1]
    tm = min(m, 512)
    tn = min(n, 2048)
    outs = pl.pallas_call(
        functools.partial(_matmul_body, sigmoid=sigmoid), grid=(m // tm, n // tn),
        in_specs=[pl.BlockSpec((tm, k), lambda i, j: (i, 0)), pl.BlockSpec((k, tn), lambda i, j: (0, j))],
        out_specs=[pl.BlockSpec((tm, tn), lambda i, j: (i, j)) for _ in out_dtypes],
        out_shape=[jax.ShapeDtypeStruct((m, n), dt) for dt in out_dtypes],
        compiler_params=_params("parallel", "arbitrary"), name="matmul")(a, w)
    return outs if len(outs) > 1 else outs[0]


def _mm_res_norm_body(a_ref, w_ref, h_ref, nw_ref, hout_ref, xn_ref):
    h = h_ref[...] + _dot(a_ref[...], w_ref[...])
    hout_ref[...] = h
    xn_ref[...] = _rms(h, nw_ref[...]).astype(xn_ref.dtype)


def matmul_res_norm(a, w, h, nw):
    m, k = a.shape
    d = w.shape[1]
    tm = min(m, 512)
    return pl.pallas_call(
        _mm_res_norm_body, grid=(m // tm,),
        in_specs=[pl.BlockSpec((tm, k), lambda i: (i, 0)), pl.BlockSpec((k, d), lambda i: (0, 0)),
                  pl.BlockSpec((tm, d), lambda i: (i, 0)), pl.BlockSpec((1, d), lambda i: (0, 0))],
        out_specs=[pl.BlockSpec((tm, d), lambda i: (i, 0)), pl.BlockSpec((tm, d), lambda i: (i, 0))],
        out_shape=[jax.ShapeDtypeStruct((m, d), F32), jax.ShapeDtypeStruct((m, d), BF16)],
        compiler_params=_params("parallel"), name="matmul_res_norm")(a, w, h, nw.reshape(1, d))


def _ffn_body(xn_ref, w1_ref, w3_ref, w2_ref, h_ref, nw_ref, hout_ref, yn_ref, acc_ref):
    f = pl.program_id(1)

    @pl.when(f == 0)
    def _():
        acc_ref[...] = jnp.zeros_like(acc_ref)

    x = xn_ref[...]
    g = _dot(x, w1_ref[...])
    u = _dot(x, w3_ref[...])
    mid = (g * jax.nn.sigmoid(g) * u).astype(BF16)
    acc_ref[...] += _dot(mid, w2_ref[...])

    @pl.when(f == pl.num_programs(1) - 1)
    def _():
        h = h_ref[...] + acc_ref[...]
        hout_ref[...] = h
        yn_ref[...] = _rms(h, nw_ref[...]).astype(yn_ref.dtype)


def ffn_res_norm(xn, w1, w3, w2, h, nw, norm_dtype):
    m, d = xn.shape
    dff = w1.shape[1]
    tm = min(m, 512)
    tf = 512
    return pl.pallas_call(
        _ffn_body, grid=(m // tm, dff // tf),
        in_specs=[pl.BlockSpec((tm, d), lambda i, f: (i, 0)),
                  pl.BlockSpec((d, tf), lambda i, f: (0, f)), pl.BlockSpec((d, tf), lambda i, f: (0, f)),
                  pl.BlockSpec((tf, d), lambda i, f: (f, 0)),
                  pl.BlockSpec((tm, d), lambda i, f: (i, 0)), pl.BlockSpec((1, d), lambda i, f: (0, 0))],
        out_specs=[pl.BlockSpec((tm, d), lambda i, f: (i, 0)), pl.BlockSpec((tm, d), lambda i, f: (i, 0))],
        out_shape=[jax.ShapeDtypeStruct((m, d), F32), jax.ShapeDtypeStruct((m, d), norm_dtype)],
        scratch_shapes=[pltpu.VMEM((tm, d), F32)],
        compiler_params=_params("parallel", "arbitrary"), name="ffn")(xn, w1, w3, w2, h, nw.reshape(1, d))


def _bias_body(tab_ref, d_ref, m_ref, o_ref):
    h = pl.program_id(0)
    d = d_ref[...]
    acc = jnp.full(d.shape, tab_ref[0, h], F32)
    for b in range(1, REL_BUCKETS):
        acc = jnp.where(d >= T5_THR[b], tab_ref[b, h], acc)
    o_ref[...] = acc + m_ref[...]


def bias_from_dist(rel_bias, dist, addmask):
    r = dist.shape[0]
    return pl.pallas_call(
        _bias_body, grid=(N_HEADS,),
        in_specs=[pl.BlockSpec(memory_space=pltpu.SMEM), pl.BlockSpec((r, LANES), lambda h: (0, 0)),
                  pl.BlockSpec((r, LANES), lambda h: (0, 0))],
        out_specs=pl.BlockSpec((None, r, LANES), lambda h: (h, 0, 0)),
        out_shape=jax.ShapeDtypeStruct((N_HEADS, r, LANES), F32),
        compiler_params=_params("parallel"), name="rel_bias")(rel_bias, dist, addmask)


CMP_PAGES = 16


def _cmp1_body(pt_ref, *refs):
    pages = refs[:CMP_PAGES]
    perm = refs[CMP_PAGES][...]
    w_refs = refs[CMP_PAGES + 1:CMP_PAGES + 3]
    o_ref = refs[CMP_PAGES + 3]
    cpp = PAGE_SIZE // CMP_STRIDE
    for kind in range(2):
        w = w_refs[kind][...]
        for kv in range(KV):
            c0 = (kind * KV + kv) * HEAD_DIM
            xs = [_dot(perm, pg[:, c0:c0 + HEAD_DIM].astype(BF16)) for pg in pages]
            pieces = [jnp.concatenate([x[r * cpp:(r + 1) * cpp] for x in xs], axis=0).astype(BF16)
                      for r in range(CMP_STRIDE)]
            o_ref[kind * KV + kv] = _dot(jnp.concatenate(pieces, axis=1), w)


def compress_stage1(rows3, page_table, w1k, w1v):
    nb_seq, n_pages = page_table.shape
    steps = n_pages // CMP_PAGES
    cpp = PAGE_SIZE // CMP_STRIDE
    width = 2 * KV * HEAD_DIM

    def page_spec(i):
        return pl.BlockSpec((None, PAGE_SIZE, width), lambda b, s, pt: (pt[b, s * CMP_PAGES + i], 0, 0))

    wspec = pl.BlockSpec((CMP_STRIDE * HEAD_DIM, 2 * HEAD_DIM), lambda b, s, pt: (0, 0))
    src = np.arange(PAGE_SIZE)
    perm = np.zeros((PAGE_SIZE, PAGE_SIZE), np.float32)
    perm[(src % CMP_STRIDE) * cpp + src // CMP_STRIDE, src] = 1.0
    perm = jnp.asarray(perm, BF16)
    grid_spec = pltpu.PrefetchScalarGridSpec(
        num_scalar_prefetch=1, grid=(nb_seq, steps),
        in_specs=[page_spec(i) for i in range(CMP_PAGES)]
        + [pl.BlockSpec((PAGE_SIZE, PAGE_SIZE), lambda b, s, pt: (0, 0)), wspec, wspec],
        out_specs=pl.BlockSpec((None, 2 * KV, CMP_PAGES * cpp, 2 * HEAD_DIM), lambda b, s, pt: (b, 0, s, 0)))
    return pl.pallas_call(
        _cmp1_body, grid_spec=grid_spec,
        out_shape=jax.ShapeDtypeStruct((nb_seq, 2 * KV, n_pages * cpp, 2 * HEAD_DIM), F32),
        compiler_params=_params("parallel", "arbitrary"), name="compress1")(
            page_table, *([rows3] * CMP_PAGES), perm, w1k, w1v)


def _cmp1_paged_body(pt_ref, *refs):
    cmp_pages = [r.reshape(PAGE_SIZE * SUBLANES, HEAD_DIM) for r in refs[:CMP_PAGES]]
    w_refs = refs[CMP_PAGES:CMP_PAGES + 2]
    o_ref = refs[CMP_PAGES + 2]
    cpp = PAGE_SIZE // CMP_STRIDE
    for kind in range(2):
        w = w_refs[kind][...]
        for kv in range(KV):
            c = kind * KV + kv
            xs = [jnp.swapaxes(pg[pl.ds(c, PAGE_SIZE, stride=SUBLANES), :].reshape(cpp, CMP_STRIDE, HEAD_DIM), 0, 1)
                  for pg in cmp_pages]
            pieces = [jnp.concatenate([x[r] for x in xs], axis=0).astype(BF16) for r in range(CMP_STRIDE)]
            o_ref[c] = _dot(jnp.concatenate(pieces, axis=1), w)


def compress_stage1_paged(cache5, page_table, w1k, w1v):
    nb_seq, n_pages = page_table.shape
    steps = n_pages // CMP_PAGES
    cpp = PAGE_SIZE // CMP_STRIDE
    assert cache5.shape[2:] == (2, 2 * KV, HEAD_DIM) and 2 * KV == SUBLANES

    def group_spec(i):
        return pl.BlockSpec((None, PAGE_SIZE, None, SUBLANES, HEAD_DIM),
                            lambda b, s, pt: (pt[b, s * CMP_PAGES + i], 0, 0, 0, 0))

    wspec = pl.BlockSpec((CMP_STRIDE * HEAD_DIM, 2 * HEAD_DIM), lambda b, s, pt: (0, 0))
    grid_spec = pltpu.PrefetchScalarGridSpec(
        num_scalar_prefetch=1, grid=(nb_seq, steps),
        in_specs=[group_spec(i) for i in range(CMP_PAGES)] + [wspec, wspec],
        out_specs=pl.BlockSpec((None, 2 * KV, CMP_PAGES * cpp, 2 * HEAD_DIM), lambda b, s, pt: (b, 0, s, 0)))
    return pl.pallas_call(
        _cmp1_paged_body, grid_spec=grid_spec,
        out_shape=jax.ShapeDtypeStruct((nb_seq, 2 * KV, n_pages * cpp, 2 * HEAD_DIM), F32),
        compiler_params=_params("parallel", "arbitrary"), name="compress1_paged")(
            page_table, *([cache5] * CMP_PAGES), w1k, w1v)


def _cmp2_body(ab_ref, pe_ref, w1_ref, w2_ref, o_ref):
    nch = ab_ref.shape[0]
    ab = ab_ref[...]
    c = _dot(pe_ref[...].astype(BF16), w1_ref[...].astype(BF16))
    pre = ab[:, :HEAD_DIM] + pltpu.roll(ab[:, HEAD_DIM:], nch - 1, 0) + c[0:1]
    y = _dot((pre * jax.nn.sigmoid(pre)).astype(BF16), w2_ref[...].astype(BF16))
    rowi = lax.broadcasted_iota(I32, y.shape, 0)
    o_ref[...] = jnp.where(rowi < nch - 1, y, 0.0)


def compress_stage2(ab, pe, w1, w2):
    nb_seq, _, nch, _ = ab.shape
    return pl.pallas_call(
        _cmp2_body, grid=(nb_seq, 2 * KV),
        in_specs=[pl.BlockSpec((None, None, nch, 2 * HEAD_DIM), lambda b, j: (b, j, 0, 0)),
                  pl.BlockSpec((None, SUBLANES, CMP_BLOCK * HEAD_DIM), lambda b, j: (j // KV, 0, 0)),
                  pl.BlockSpec((None, CMP_BLOCK * HEAD_DIM, HEAD_DIM), lambda b, j: (j // KV, 0, 0)),
                  pl.BlockSpec((None, HEAD_DIM, HEAD_DIM), lambda b, j: (j // KV, 0, 0))],
        out_specs=pl.BlockSpec((None, None, nch, HEAD_DIM), lambda b, j: (b, j, 0, 0)),
        out_shape=jax.ShapeDtypeStruct((nb_seq, 2 * KV, nch, HEAD_DIM), F32),
        compiler_params=_params("parallel", "parallel"), name="compress2")(ab, pe, w1, w2)


def compress(rows3, page_table, cw):
    ab = compress_stage1(rows3, page_table, cw["w1k"], cw["w1v"])
    return compress_stage2(ab, cw["pe"], cw["w1"], cw["w2"])


def compress_paged(cache3, page_table, cw):
    ab = compress_stage1_paged(cache3, page_table, cw["w1k"], cw["w1v"])
    return compress_stage2(ab, cw["pe"], cw["w1"], cw["w2"])


def _flash_prob(s, mask, m_ref, l_ref, rows):
    m_prev = m_ref[rows, :]
    if mask is not None:
        s = jnp.where(mask, s, MASK_NEG)
    m_new = jnp.maximum(m_prev, jnp.max(s, axis=1, keepdims=True))
    alpha = jnp.exp(m_prev - m_new)
    m_wide = m_new if s.shape[1] == LANES else jnp.concatenate([m_new] * (s.shape[1] // LANES), axis=1)
    p = jnp.exp(s - m_wide)
    l_ref[rows, :] = alpha * l_ref[rows, :] + jnp.sum(p, axis=1, keepdims=True)
    m_ref[rows, :] = m_new
    return p.astype(BF16), alpha


def _flash_update(s, mask, v, m_ref, l_ref, acc_ref, rows):
    p, alpha = _flash_prob(s, mask, m_ref, l_ref, rows)
    acc_ref[rows, :] = alpha * acc_ref[rows, :] + _dot(p, v)


def _flash_init(m_ref, l_ref, acc_ref):
    m_ref[...] = jnp.full(m_ref.shape, NEG_INF, F32)
    l_ref[...] = jnp.zeros_like(l_ref)
    acc_ref[...] = jnp.zeros_like(acc_ref)


def _rank_select(score, eligible, blk, n_blocks, n_top, row_of):
    def body(j, cnt):
        rowj = row_of(j)
        beats = (rowj > score) | ((rowj == score) & (blk > j))
        return cnt + beats.astype(I32)
    cnt = lax.fori_loop(0, n_blocks, body, jnp.zeros(score.shape, I32))
    return jnp.where((cnt < n_top) & eligible, 1.0, 0.0)


def _nsa_prompt_body(q_ref, ks_ref, vs_ref, kw_ref, vw_ref, kc_ref, vc_ref, g_ref, bc_ref, bt_ref, wbt_ref,
                     ovl_ref, exp_ref,
                     o_ref, m_s, l_s, a_s, m_w, l_w, a_w, mk_ref, sc_ref, *, n_cmp, n_blk):
    qi = pl.program_id(2)
    rows = G * TQ
    q = jnp.concatenate([q_ref[:, g * HEAD_DIM:(g + 1) * HEAD_DIM] for g in range(G)], axis=0)
    row5 = lax.broadcasted_iota(I32, (rows, LANES), 0)
    lane5 = lax.broadcasted_iota(I32, (rows, LANES), 1)
    tpos5 = qi * TQ + (row5 & (TQ - 1))

    s = _dot_nt(q, kc_ref[...].astype(BF16))
    s = s + jnp.concatenate([bc_ref[g] for g in range(G)], axis=0)
    mask_c = (tpos5 >= lane5 * CMP_STRIDE + (CMP_BLOCK - 1)) & (lane5 < n_cmp)
    p_c = _masked_softmax(s, mask_c)
    pb = p_c.astype(BF16)
    o_c = _dot(pb, vc_ref[...].astype(BF16))

    nbr = ovl_ref.shape[0]
    imp = _dot_nt(ovl_ref[...], pb[0:TQ])
    for g in range(1, G):
        imp = imp + _dot_nt(ovl_ref[...], pb[g * TQ:(g + 1) * TQ])
    blk = lax.broadcasted_iota(I32, (nbr, TQ), 0)
    tq_l = qi * TQ + lax.broadcasted_iota(I32, (nbr, TQ), 1)
    cur = lax.shift_right_arithmetic(tq_l, int(math.log2(SEL_BLOCK)))
    eligible = (blk * SEL_BLOCK <= tq_l) & (blk < n_blk)
    forced = (blk == 0) | (blk == cur) | (blk == cur - 1)
    score = jnp.where(eligible, imp + jnp.where(forced, FORCE_BONUS, 0.0), -jnp.inf)
    sc_ref[...] = score
    sel_t = _rank_select(score, eligible, blk, n_blk, min(SEL_TOPK, n_blk), lambda j: sc_ref[pl.ds(j, 1), :])
    unsel = jnp.concatenate([1.0 - sel_t, jnp.zeros((LANES - nbr, TQ), F32)], axis=0)
    mk_ref[...] = _dot(unsel.T.astype(BF16), exp_ref[...])
    row1 = lax.broadcasted_iota(I32, (TQ, TQ), 0)
    lane1 = lax.broadcasted_iota(I32, (TQ, TQ), 1)
    off_d = pl.multiple_of(qi * TQ, TQ)
    mk_ref[:, pl.ds(off_d, TQ)] = jnp.where(lane1 <= row1, mk_ref[:, pl.ds(off_d, TQ)], MASK_NEG)

    @pl.when(qi + 1 < pl.num_programs(2))
    def _():
        mk_ref[:, pl.ds(pl.multiple_of((qi + 1) * TQ, TQ), TQ)] = jnp.full((TQ, TQ), MASK_NEG, F32)

    kt2 = 2 * TQ

    def tile_pair(ref, j, index_of):
        tiles = []
        for i in range(2):
            off = pl.multiple_of(index_of(qi - (2 * j + i)) * TQ, TQ)
            tiles.append(jnp.concatenate([ref[g, pl.ds(off, TQ), :] for g in range(G)], axis=0))
        return jnp.concatenate(tiles, axis=1)

    _flash_init(m_s, l_s, a_s)
    _flash_init(m_w, l_w, a_w)
    all_rows = pl.ds(0, rows)

    def pair_step(j, with_win):
        off = pl.multiple_of(j * kt2, kt2)
        s_s = _dot_nt(q, ks_ref[pl.ds(off, kt2), :])
        if with_win:
            s_w = _dot_nt(q, kw_ref[pl.ds(off, kt2), :])
        bias_s = tile_pair(bt_ref, j, lambda d: jnp.clip(d, 0, 2))
        key_mask = jnp.concatenate([mk_ref[:, pl.ds(off, kt2)]] * G, axis=0)
        p_s, alpha_s = _flash_prob(s_s + bias_s + key_mask, None, m_s, l_s, all_rows)
        if with_win:
            bias_w = tile_pair(wbt_ref, j, lambda d: jnp.where(d < 0, WIN_TILES - 1, jnp.minimum(d, WIN_TILES - 1)))
            p_w, alpha_w = _flash_prob(s_w + bias_w, None, m_w, l_w, all_rows)
        a_s[...] = alpha_s * a_s[...] + _dot(p_s, vs_ref[pl.ds(off, kt2), :])
        if with_win:
            a_w[...] = alpha_w * a_w[...] + _dot(p_w, vw_ref[pl.ds(off, kt2), :])

    def sel_only(j, c):
        pair_step(j, False)
        return c

    def sel_and_win(j, c):
        pair_step(j, True)
        return c

    lo = lax.shift_right_arithmetic(jnp.maximum(qi - WINDOW // TQ, 0), 1)
    hi = lax.shift_right_arithmetic(qi, 1) + 1
    lax.fori_loop(0, lo, sel_only, 0)
    lax.fori_loop(lo, hi, sel_and_win, 0)

    o_s = a_s[...] * _safe_inv(l_s[...])
    o_w = a_w[...] * _safe_inv(l_w[...])
    gt = g_ref[...]
    for g in range(G):
        r = slice(g * TQ, (g + 1) * TQ)
        out = (gt[:, 3 * g:3 * g + 1] * o_c[r] + gt[:, 3 * g + 1:3 * g + 2] * o_s[r]
               + gt[:, 3 * g + 2:3 * g + 3] * o_w[r])
        o_ref[:, g * HEAD_DIM:(g + 1) * HEAD_DIM] = out.astype(o_ref.dtype)


def nsa_prompt_attention(q, kv4, kwin, gates, kcb, bias_all, b, t):
    nq = t // TQ
    n_chunks = t // CMP_STRIDE
    n_cmp = n_chunks - CMP_BLOCK // CMP_STRIDE + 1
    n_blk = -(-t // SEL_BLOCK)
    assert n_chunks == LANES and n_blk <= LANES and t % TQ == 0 and TQ >= FAR_DIST
    cmp_start = np.arange(LANES) * CMP_STRIDE
    jst = np.arange(LANES) * SEL_BLOCK
    overlap = ((cmp_start[:, None] < jst[None, :] + SEL_BLOCK) & (cmp_start[:, None] + CMP_BLOCK > jst[None, :])
               & (np.arange(LANES)[:, None] < n_cmp) & (np.arange(LANES)[None, :] < n_blk))
    nbr = -(-n_blk // (2 * SUBLANES)) * 2 * SUBLANES
    ovl_t = jnp.asarray(overlap.T[:nbr], BF16)
    expand = jnp.asarray(np.where(np.arange(LANES)[:, None] == (np.arange(t)[None, :] // SEL_BLOCK), MASK_NEG, 0.0),
                         BF16)
    colblk = lambda base: (lambda bb, kv, qi: (bb, base + kv))
    rowblk = lambda bb, kv, qi: (bb * nq + qi, kv)
    in_specs = [
        pl.BlockSpec((TQ, G * HEAD_DIM), rowblk),
        pl.BlockSpec((t, HEAD_DIM), colblk(2 * KV)), pl.BlockSpec((t, HEAD_DIM), colblk(3 * KV)),
        pl.BlockSpec((t, HEAD_DIM), colblk(0)), pl.BlockSpec((t, HEAD_DIM), colblk(KV)),
        pl.BlockSpec((None, None, n_chunks, HEAD_DIM), lambda bb, kv, qi: (bb, kv, 0, 0)),
        pl.BlockSpec((None, None, n_chunks, HEAD_DIM), lambda bb, kv, qi: (bb, KV + kv, 0, 0)),
        pl.BlockSpec((TQ, LANES), rowblk),
        pl.BlockSpec((G, TQ, LANES), lambda bb, kv, qi: (kv, WIN_TILES + 3 + qi, 0)),
        pl.BlockSpec((G, 3 * TQ, LANES), lambda bb, kv, qi: (kv, WIN_TILES // 3, 0)),
        pl.BlockSpec((G, WIN_TILES * TQ, LANES), lambda bb, kv, qi: (kv, 0, 0)),
        pl.BlockSpec((nbr, LANES), lambda bb, kv, qi: (0, 0)),
        pl.BlockSpec((LANES, t), lambda bb, kv, qi: (0, 0)),
    ]
    assert WIN_TILES % 3 == 0 and TQ == LANES
    rows = G * TQ
    scratch = [pltpu.VMEM((rows, LANES), F32) for _ in range(6)] + [pltpu.VMEM((TQ, t), F32),
                                                                   pltpu.VMEM((nbr, TQ), F32)]
    return pl.pallas_call(
        functools.partial(_nsa_prompt_body, n_cmp=n_cmp, n_blk=n_blk), grid=(b, KV, nq),
        in_specs=in_specs, out_specs=pl.BlockSpec((TQ, G * HEAD_DIM), rowblk),
        out_shape=jax.ShapeDtypeStruct((b * t, N_HEADS * HEAD_DIM), BF16),
        scratch_shapes=scratch,
        compiler_params=_params("parallel", "parallel", "arbitrary"), name="nsa_prompt_attn")(
            q, kv4, kv4, kwin, kwin, kcb, kcb, gates, bias_all, bias_all, bias_all, ovl_t, expand)


def _stick_pre(z, mask):
    sp = jnp.log(1.0 + jnp.exp(-jnp.abs(z)))
    log_beta = jnp.minimum(z, 0.0) - sp
    rest = jnp.maximum(z, 0.0) + sp
    if mask is not None:
        rest = jnp.where(mask, rest, 0.0)
    hi = rest.astype(BF16)
    lo = (rest - hi.astype(F32)).astype(BF16)
    return log_beta, jnp.sum(rest, axis=1, keepdims=True), jnp.concatenate([hi, lo], axis=0)


def _stick_post(log_beta, l2, carry, mask):
    n = log_beta.shape[0]
    a = jnp.exp(log_beta - (l2[0:n] + l2[n:2 * n] + carry))
    return a if mask is None else jnp.where(mask, a, 0.0)


SB_HEADS = 8
SB_DONE = 110.0


def _sb_prompt_body(q_ref, k_ref, v_ref, u_ref, o_ref, carry_ref, acc_ref):
    qi = pl.program_id(2)
    row = lax.broadcasted_iota(I32, (TQ, TQ), 0)
    lane = lax.broadcasted_iota(I32, (TQ, TQ), 1)

    def tile(kt, mask, first):
        off = pl.multiple_of(kt * TQ, TQ)
        cols = [slice(h * HEAD_DIM, (h + 1) * HEAD_DIM) for h in range(SB_HEADS)]
        zs = [_dot_nt(q_ref[:, c], k_ref[pl.ds(off, TQ), c]) for c in cols]
        pre = [_stick_pre(z, mask) for z in zs]
        l3s = [_dot(p[2], u_ref[...]) for p in pre]
        ws = [_stick_post(p[0], l3, 0.0 if first else carry_ref[h], mask).astype(BF16)
              for h, (p, l3) in enumerate(zip(pre, l3s))]
        pvs = [_dot(w, v_ref[pl.ds(off, TQ), c]) for w, c in zip(ws, cols)]
        for h in range(SB_HEADS):
            tot = pre[h][1]
            acc_ref[h] = pvs[h] if first else acc_ref[h] + pvs[h]
            carry_ref[h] = jnp.broadcast_to(tot, (TQ, TQ)) if first else carry_ref[h] + tot

    tile(qi, lane < row, True)

    def unfinished():
        c = carry_ref[0]
        for h in range(1, SB_HEADS):
            c = jnp.minimum(c, carry_ref[h])
        return (jnp.min(c) < SB_DONE).astype(I32)

    def step(state):
        i, _ = state
        tile(qi - 1 - i, None, False)
        return i + 1, unfinished()

    lax.while_loop(lambda state: (state[0] < qi) & (state[1] > 0), step, (jnp.int32(0), unfinished()))
    for h in range(SB_HEADS):
        o_ref[:, h * HEAD_DIM:(h + 1) * HEAD_DIM] = acc_ref[h].astype(o_ref.dtype)


def _suffix_matrix():
    j = np.arange(TQ)
    return jnp.asarray(j[:, None] > j[None, :], BF16)


def sb_prompt_attention(q, kvp, b, t):
    nq = t // TQ
    hg = N_HEADS // SB_HEADS
    w = SB_HEADS * HEAD_DIM
    rowblk = lambda bb, h, qi: (bb * nq + qi, h)
    return pl.pallas_call(
        _sb_prompt_body, grid=(b, hg, nq),
        in_specs=[pl.BlockSpec((TQ, w), rowblk),
                  pl.BlockSpec((t, w), lambda bb, h, qi: (bb, h)),
                  pl.BlockSpec((t, w), lambda bb, h, qi: (bb, hg + h)),
                  pl.BlockSpec((TQ, TQ), lambda bb, h, qi: (0, 0))],
        out_specs=pl.BlockSpec((TQ, w), rowblk),
        out_shape=jax.ShapeDtypeStruct((b * t, N_HEADS * HEAD_DIM), BF16),
        scratch_shapes=[pltpu.VMEM((SB_HEADS, TQ, TQ), F32), pltpu.VMEM((SB_HEADS, TQ, HEAD_DIM), F32)],
        compiler_params=_params("parallel", "parallel", "arbitrary"), name="sb_prompt_attn")(
            q, kvp, kvp, _suffix_matrix())


def _nsa_sample_select_body(q_ref, kcb_ref, bc_ref, mc_ref, ovl_ref, fadd_ref, elig_ref, oc_ref, sel_ref, sc_ref,
                            *, n_blk):
    nbp = kcb_ref.shape[1]
    grp = G * T8
    ps = []
    for kv in range(KV):
        qk = jnp.concatenate([q_ref[:, (kv * G + g) * HEAD_DIM:(kv * G + g + 1) * HEAD_DIM] for g in range(G)],
                             axis=0)
        s = _dot_nt(qk, kcb_ref[kv].astype(BF16)) + bc_ref[kv * grp:(kv + 1) * grp, :]
        p = _masked_softmax(s, mc_ref[...] > 0.5)
        oc_ref[kv * grp:(kv + 1) * grp, :] = _dot(p.astype(BF16), kcb_ref[KV + kv].astype(BF16))
        ps.append(p)
    pad = jnp.zeros((LANES - KV * T8, nbp), F32)
    imp = None
    for g in range(G):
        xg = jnp.concatenate([ps[kv][g * T8:(g + 1) * T8] for kv in range(KV)] + [pad], axis=0)
        part = _dot_nt(ovl_ref[...], xg.astype(BF16))
        imp = part if imp is None else imp + part
    eligible = elig_ref[...] > 0.5
    score = jnp.where(eligible, imp + fadd_ref[...], -jnp.inf)
    sc_ref[...] = score
    blk = lax.broadcasted_iota(I32, score.shape, 0)
    sel_ref[...] = _rank_select(score, eligible, blk, n_blk, min(SEL_TOPK, n_blk),
                                lambda j: sc_ref[pl.ds(j, 1), :])


SEL_KEYS = 1024
SEL_PAGES = SEL_KEYS // PAGE_SIZE


def _nsa_sample_attn_body(pt_ref, q_ref, *refs, n_steps, t_new, past):
    pages = [r.reshape(PAGE_SIZE * SUBLANES, HEAD_DIM) for r in refs[:SEL_PAGES]]
    (sel_ref, new_ref, win_ref, wnew_ref, g_ref, oc_ref, bl_ref, bf_ref, bn_ref, bw_ref, ex_ref,
     o_ref, m_s, l_s, a_s) = refs[SEL_PAGES:]

    def page_rows(c):
        return jnp.concatenate([pg[pl.ds(c, PAGE_SIZE, stride=SUBLANES), :] for pg in pages], axis=0).astype(BF16)

    step = pl.program_id(1)
    grp = G * T8
    kvw = KV * HEAD_DIM
    wb = win_ref.shape[0] // (2 * KV)
    is_last = step == n_steps - 1

    @pl.when(step == 0)
    def _():
        _flash_init(m_s, l_s, a_s)

    def q_of(kv):
        return jnp.concatenate([q_ref[:, (kv * G + g) * HEAD_DIM:(kv * G + g + 1) * HEAD_DIM] for g in range(G)],
                               axis=0)

    def pad_keys(x):
        return jnp.concatenate([x, jnp.zeros((TQ - x.shape[0], x.shape[1]), x.dtype)], axis=0).astype(BF16)

    def rows_of_blocks(piece):
        return jnp.concatenate([piece, jnp.zeros((LANES - piece.shape[0], LANES), F32)], axis=0).T

    bps = SEL_KEYS // SEL_BLOCK
    piece = sel_ref[pl.ds(pl.multiple_of(step * bps, bps), bps), :]
    mk = _dot(rows_of_blocks(piece).astype(BF16), ex_ref[...])

    kv_rows = [pl.ds(kv * grp, grp) for kv in range(KV)]
    scores = [_dot_nt(q_of(kv), page_rows(kv)) for kv in range(KV)]
    probs = []
    for kv in range(KV):
        bias = jnp.where(is_last, bl_ref[kv_rows[kv], :], bf_ref[kv_rows[kv], 0:1])
        mask = jnp.concatenate([mk[kv * T8:(kv + 1) * T8]] * G, axis=0) > 0.5
        probs.append(_flash_prob(scores[kv] + bias, mask, m_s, l_s, kv_rows[kv]))
    for kv in range(KV):
        p, alpha = probs[kv]
        a_s[kv_rows[kv], :] = alpha * a_s[kv_rows[kv], :] + _dot(p, page_rows(KV + kv))

    @pl.when(is_last)
    def _():
        lane = lax.broadcasted_iota(I32, (grp, LANES), 1)
        t_row = lax.broadcasted_iota(I32, (grp, LANES), 0) & (T8 - 1)
        lane_w = lax.broadcasted_iota(I32, (grp, wb), 1)
        t_row_w = lax.broadcasted_iota(I32, (grp, wb), 0) & (T8 - 1)
        new_blk = past // SEL_BLOCK
        picked_all = rows_of_blocks(sel_ref[new_blk:new_blk + SUBLANES, :])
        gt = g_ref[...]
        for kv in range(KV):
            rows = pl.ds(kv * grp, grp)
            qk = q_of(kv)
            kn = pad_keys(new_ref[:, 2 * kvw + kv * HEAD_DIM:2 * kvw + (kv + 1) * HEAD_DIM])
            vn = pad_keys(new_ref[:, 3 * kvw + kv * HEAD_DIM:3 * kvw + (kv + 1) * HEAD_DIM])
            s = _dot_nt(qk, kn) + bn_ref[rows, :]
            picked = jnp.concatenate([picked_all[kv * T8:(kv + 1) * T8, 0:1]] * G, axis=0) > 0.5
            mask = (lane <= t_row) & (lane < t_new) & picked
            _flash_update(s, mask, vn, m_s, l_s, a_s, rows)
            o_s = a_s[rows, :] * _safe_inv(l_s[rows, :])
            kw = win_ref[pl.ds(kv, wb, stride=2 * KV), :].astype(BF16)
            vw = win_ref[pl.ds(KV + kv, wb, stride=2 * KV), :].astype(BF16)
            kwn = pad_keys(wnew_ref[:, kv * HEAD_DIM:(kv + 1) * HEAD_DIM])
            vwn = pad_keys(wnew_ref[:, kvw + kv * HEAD_DIM:kvw + (kv + 1) * HEAD_DIM])
            s1 = _dot_nt(qk, kw) + bw_ref[rows, 0:wb]
            s2 = _dot_nt(qk, kwn) + bw_ref[rows, wb:wb + LANES]
            d1 = wb + t_row_w - lane_w
            mask1 = (d1 >= 0) & (d1 < WINDOW)
            d2 = t_row - lane
            mask2 = (d2 >= 0) & (d2 < WINDOW) & (lane < t_new)
            m = jnp.maximum(jnp.max(jnp.where(mask1, s1, NEG_INF), axis=1, keepdims=True),
                            jnp.max(jnp.where(mask2, s2, NEG_INF), axis=1, keepdims=True))
            e1 = jnp.where(mask1, jnp.exp(s1 - m), 0.0)
            e2 = jnp.where(mask2, jnp.exp(s2 - m), 0.0)
            l = jnp.sum(e1, axis=1, keepdims=True) + jnp.sum(e2, axis=1, keepdims=True)
            o_w = (_dot(e1.astype(BF16), vw) + _dot(e2.astype(BF16), vwn)) * _safe_inv(l)
            o_c = oc_ref[rows, :]
            for g in range(G):
                r = slice(g * T8, (g + 1) * T8)
                c = kv * LANES + 3 * g
                out = gt[:, c:c + 1] * o_c[r] + gt[:, c + 1:c + 2] * o_s[r] + gt[:, c + 2:c + 3] * o_w[r]
                o_ref[:, (kv * G + g) * HEAD_DIM:(kv * G + g + 1) * HEAD_DIM] = out.astype(o_ref.dtype)


def nsa_sample_attention(q8, cache5, page_table, kcb, new8, win3, wnew8, gates8, bias_s, past, t_new):
    bd = q8.shape[0]
    n_chunks = kcb.shape[2]
    n_cmp = n_chunks - CMP_BLOCK // CMP_STRIDE + 1
    total = past + t_new
    n_blk = -(-total // SEL_BLOCK)
    nblkp = -(-(n_blk + SUBLANES) // (2 * SUBLANES)) * 2 * SUBLANES
    wb = win3.shape[1] // (2 * KV)
    n_steps = past // SEL_KEYS
    assert past % SEL_KEYS == 0 and SEL_KEYS % SEL_BLOCK == 0 and t_new <= T8 and PAGE_SIZE > FAR_DIST
    assert (total // CMP_STRIDE) == n_chunks and wb % LANES == 0 and SEL_KEYS // SEL_BLOCK <= LANES

    pos_q = past + np.arange(T8)
    cmp_start = np.arange(n_chunks) * CMP_STRIDE
    cmp_end = cmp_start + CMP_BLOCK - 1
    mask_c = ((pos_q[:, None] >= cmp_end[None, :]) & (np.arange(n_chunks)[None, :] < n_cmp))
    mask_c = jnp.asarray(np.tile(mask_c, (G, 1)), F32)
    jst = np.arange(nblkp) * SEL_BLOCK
    overlap = ((cmp_start[:, None] < jst[None, :] + SEL_BLOCK) & (cmp_start[:, None] + CMP_BLOCK > jst[None, :])
               & (np.arange(n_chunks)[:, None] < n_cmp) & (np.arange(nblkp)[None, :] < n_blk))
    ovl_t = jnp.asarray(overlap.T, BF16)
    blk = np.arange(nblkp)[:, None]
    pq = np.tile(pos_q, KV)[None, :]
    tvalid = np.tile(np.arange(T8) < t_new, KV)[None, :]
    cur = pq // SEL_BLOCK
    elig = (blk * SEL_BLOCK <= pq) & (blk < n_blk) & tvalid
    forced = (blk == 0) | (blk == cur) | (blk == cur - 1)
    pad_l = LANES - KV * T8
    elig = jnp.asarray(np.pad(elig, ((0, 0), (0, pad_l))), F32)
    fadd = jnp.asarray(np.pad(np.where(forced, FORCE_BONUS, 0.0), ((0, 0), (0, pad_l))), F32)

    grp_all = KV * G * T8
    o_c, sel = pl.pallas_call(
        functools.partial(_nsa_sample_select_body, n_blk=n_blk), grid=(bd,),
        in_specs=[pl.BlockSpec((None, T8, N_HEADS * HEAD_DIM), lambda b: (b, 0, 0)),
                  pl.BlockSpec((None, 2 * KV, n_chunks, HEAD_DIM), lambda b: (b, 0, 0, 0)),
                  pl.BlockSpec((grp_all, n_chunks), lambda b: (0, 0)),
                  pl.BlockSpec((G * T8, n_chunks), lambda b: (0, 0)),
                  pl.BlockSpec((nblkp, n_chunks), lambda b: (0, 0)),
                  pl.BlockSpec((nblkp, LANES), lambda b: (0, 0)),
                  pl.BlockSpec((nblkp, LANES), lambda b: (0, 0))],
        out_specs=[pl.BlockSpec((None, grp_all, HEAD_DIM), lambda b: (b, 0, 0)),
                   pl.BlockSpec((None, nblkp, LANES), lambda b: (b, 0, 0))],
        out_shape=[jax.ShapeDtypeStruct((bd, grp_all, HEAD_DIM), F32),
                   jax.ShapeDtypeStruct((bd, nblkp, LANES), F32)],
        scratch_shapes=[pltpu.VMEM((nblkp, LANES), F32)],
        compiler_params=_params("parallel"), name="nsa_sample_select")(
            q8, kcb, bias_s["cmp"], mask_c, ovl_t, fadd, elig)

    kvw = KV * HEAD_DIM
    const2 = lambda b, s, pt: (0, 0)
    per_b = lambda b, s, pt: (b, 0, 0)
    expand = jnp.asarray(np.arange(LANES)[:, None] == (np.arange(SEL_KEYS)[None, :] // SEL_BLOCK), BF16)
    bias_last = jnp.concatenate([bias_s["far"]] * (SEL_KEYS // LANES - 1) + [bias_s["last"]], axis=1)

    def page_spec(i):
        return pl.BlockSpec((None, PAGE_SIZE, None, SUBLANES, HEAD_DIM),
                            lambda b, s, pt: (pt[b, s * SEL_PAGES + i], 0, 1, 0, 0))

    grid_spec = pltpu.PrefetchScalarGridSpec(
        num_scalar_prefetch=1, grid=(bd, n_steps),
        in_specs=[pl.BlockSpec((None, T8, N_HEADS * HEAD_DIM), per_b)] + [page_spec(i) for i in range(SEL_PAGES)]
        + [pl.BlockSpec((None, nblkp, LANES), per_b),
                  pl.BlockSpec((None, T8, 4 * kvw), per_b),
                  pl.BlockSpec((None, wb * 2 * KV, HEAD_DIM), per_b),
                  pl.BlockSpec((None, T8, 2 * kvw), per_b),
                  pl.BlockSpec((None, T8, KV * LANES), per_b),
                  pl.BlockSpec((None, grp_all, HEAD_DIM), per_b),
                  pl.BlockSpec((grp_all, SEL_KEYS), const2), pl.BlockSpec((grp_all, LANES), const2),
                  pl.BlockSpec((grp_all, LANES), const2), pl.BlockSpec((grp_all, wb + LANES), const2),
                  pl.BlockSpec((LANES, SEL_KEYS), const2)],
        out_specs=pl.BlockSpec((None, T8, N_HEADS * HEAD_DIM), per_b),
        scratch_shapes=[pltpu.VMEM((grp_all, LANES), F32) for _ in range(3)])
    return pl.pallas_call(
        functools.partial(_nsa_sample_attn_body, n_steps=n_steps, t_new=t_new, past=past), grid_spec=grid_spec,
        out_shape=jax.ShapeDtypeStruct((bd, T8, N_HEADS * HEAD_DIM), BF16),
        compiler_params=_params("parallel", "arbitrary"), name="nsa_sample_attn")(
            page_table, q8, *([cache5] * SEL_PAGES), sel, new8, win3, wnew8, gates8, o_c,
            bias_last, bias_s["far"], bias_s["new"], bias_s["win"], expand)


SB_PAGES = 8
SB_GROUPS = 2 * N_HEADS // SUBLANES


def _sb_sample_run(q_ref, u, carry_ref, acc_ref, n_sub, k_of, v_of, masks):
    def q_of(h):
        return q_ref[h * T8:(h + 1) * T8, :].astype(BF16)

    zs = [jnp.concatenate([_dot_nt(q_of(h), k_of(j, h)) for h in range(N_HEADS)], axis=0)
          for j in range(n_sub)]
    pre = [_stick_pre(zs[j], masks[j]) for j in range(n_sub)]
    l2s = [_dot(pre[j][2], u) for j in range(n_sub)]
    carry = carry_ref[...]
    ws = [None] * n_sub
    for j in reversed(range(n_sub)):
        ws[j] = _stick_post(pre[j][0], l2s[j], carry, masks[j])
        carry = carry + pre[j][1]
    carry_ref[...] = carry
    for h in range(N_HEADS):
        r = slice(h * T8, (h + 1) * T8)
        out = _dot(ws[0][r].astype(BF16), v_of(0, h))
        for j in range(1, n_sub):
            out = out + _dot(ws[j][r].astype(BF16), v_of(j, h))
        acc_ref[r, :] += out


def _sb_page_rows(refs, rows_per_block):
    blocks = [r.reshape(PAGE_SIZE * rows_per_block, HEAD_DIM) for r in refs]
    per_page = 2 * N_HEADS // rows_per_block

    def head_rows(j, row):
        blk = blocks[j * per_page + row // rows_per_block]
        return blk[pl.ds(row % rows_per_block, PAGE_SIZE, stride=rows_per_block), :].astype(BF16)
    return head_rows


def _sb_sample_head_body(pt_ref, q_ref, *refs, t_new):
    n_grp = SB_PAGES * SB_GROUPS
    head_rows = _sb_page_rows(refs[:n_grp], SUBLANES)
    new_ref, u_ref, acc_ref, carry_ref = refs[n_grp:]
    hw = N_HEADS * HEAD_DIM
    carry_ref[...] = jnp.zeros_like(carry_ref)
    acc_ref[...] = jnp.zeros_like(acc_ref)
    pad = jnp.zeros((TQ - T8, 2 * hw), F32)
    kvn = jnp.concatenate([new_ref[...], pad], axis=0)
    lane = lax.broadcasted_iota(I32, (TQ, TQ), 1)
    t_row = lax.broadcasted_iota(I32, (TQ, TQ), 0) & (T8 - 1)
    _sb_sample_run(q_ref, u_ref[...], carry_ref, acc_ref, 1,
                   lambda j, h: kvn[:, h * HEAD_DIM:(h + 1) * HEAD_DIM].astype(BF16),
                   lambda j, h: kvn[:, hw + h * HEAD_DIM:hw + (h + 1) * HEAD_DIM].astype(BF16),
                   [(lane < t_row) & (lane < t_new)])
    _sb_sample_run(q_ref, u_ref[...], carry_ref, acc_ref, SB_PAGES, lambda j, h: head_rows(j, h),
                   lambda j, h: head_rows(j, N_HEADS + h), [None] * SB_PAGES)


def _sb_sample_tail_body(pt_ref, need_ref, q_ref, *refs, n_steps):
    head_rows = _sb_page_rows(refs[:SB_PAGES], 2 * N_HEADS)
    u_ref, acc_in, carry_in, o_ref, carry_ref, acc_ref = refs[SB_PAGES:]
    step = pl.program_id(1)

    @pl.when(step == 0)
    def _():
        carry_ref[...] = carry_in[...]
        acc_ref[...] = acc_in[...]

    @pl.when((need_ref[pl.program_id(0)] > 0) & (jnp.min(carry_ref[...]) < SB_DONE))
    def _():
        _sb_sample_run(q_ref, u_ref[...], carry_ref, acc_ref, SB_PAGES, lambda j, h: head_rows(j, h),
                       lambda j, h: head_rows(j, N_HEADS + h), [None] * SB_PAGES)

    @pl.when(step == n_steps - 1)
    def _():
        o_ref[...] = acc_ref[...].astype(o_ref.dtype)


def sb_sample_attention(q_rows, cache5, page_table, new8, t_new):
    bd, n_pages = page_table.shape
    hw = N_HEADS * HEAD_DIM
    n_steps = n_pages // SB_PAGES - 1
    assert n_pages % SB_PAGES == 0 and n_steps >= 1 and cache5.shape[2] == SB_GROUPS
    rows = N_HEADS * T8
    grp_block = (None, PAGE_SIZE, None, SUBLANES, HEAD_DIM)
    n_grp = SB_PAGES * SB_GROUPS
    u = _suffix_matrix()

    def head_spec(i, g):
        return pl.BlockSpec(grp_block, lambda b, pt: (pt[b, n_pages - SB_PAGES + i], 0, g, 0, 0))

    acc, carry = pl.pallas_call(
        functools.partial(_sb_sample_head_body, t_new=t_new),
        grid_spec=pltpu.PrefetchScalarGridSpec(
            num_scalar_prefetch=1, grid=(bd,),
            in_specs=[pl.BlockSpec((None, rows, HEAD_DIM), lambda b, pt: (b, 0, 0))]
            + [head_spec(i, g) for i in range(SB_PAGES) for g in range(SB_GROUPS)]
            + [pl.BlockSpec((None, T8, 2 * hw), lambda b, pt: (b, 0, 0)),
               pl.BlockSpec((TQ, TQ), lambda b, pt: (0, 0))],
            out_specs=[pl.BlockSpec((None, rows, HEAD_DIM), lambda b, pt: (b, 0, 0)),
                       pl.BlockSpec((None, rows, LANES), lambda b, pt: (b, 0, 0))]),
        out_shape=[jax.ShapeDtypeStruct((bd, rows, HEAD_DIM), F32), jax.ShapeDtypeStruct((bd, rows, LANES), F32)],
        compiler_params=_params("parallel"), name="sb_sample_head")(
            page_table, q_rows, *([cache5] * n_grp), new8, u)

    need = (jnp.min(carry, axis=(1, 2)) < SB_DONE).astype(I32)

    def tail_spec(i):
        def index(b, s, pt, nd):
            page = jnp.where(nd[b] > 0, n_pages - (s + 2) * SB_PAGES + i, n_pages - SB_PAGES + i)
            return (pt[b, page], 0, 0, 0, 0)
        return pl.BlockSpec((None, PAGE_SIZE, SB_GROUPS, SUBLANES, HEAD_DIM), index)

    per_b = lambda b, s, pt, nd: (b, 0, 0)
    return pl.pallas_call(
        functools.partial(_sb_sample_tail_body, n_steps=n_steps),
        grid_spec=pltpu.PrefetchScalarGridSpec(
            num_scalar_prefetch=2, grid=(bd, n_steps),
            in_specs=[pl.BlockSpec((None, rows, HEAD_DIM), per_b)]
            + [tail_spec(i) for i in range(SB_PAGES)]
            + [pl.BlockSpec((TQ, TQ), lambda b, s, pt, nd: (0, 0)),
               pl.BlockSpec((None, rows, HEAD_DIM), per_b), pl.BlockSpec((None, rows, LANES), per_b)],
            out_specs=pl.BlockSpec((None, rows, HEAD_DIM), per_b),
            scratch_shapes=[pltpu.VMEM((rows, LANES), F32), pltpu.VMEM((rows, HEAD_DIM), F32)]),
        out_shape=jax.ShapeDtypeStruct((bd, rows, HEAD_DIM), BF16),
        compiler_params=_params("parallel", "arbitrary"), name="sb_sample_tail")(
            page_table, need, q_rows, *([cache5] * SB_PAGES), u, acc, carry)


def _pad_tokens(x, bd, t_new):
    x = x.reshape(bd, t_new, x.shape[-1])
    return jnp.pad(x, ((0, 0), (0, T8 - t_new), (0, 0)))


def _bias_distances(t, past, n_chunks_s, wb):
    r = np.arange(TQ)[:, None]
    c = np.arange(LANES)[None, :]
    win_tiles = np.concatenate([r - c + off * TQ for off in range(WIN_TILES)], axis=0)
    toeplitz = np.concatenate([r - c + off * TQ for off in range(3)], axis=0)
    cmp_p = np.arange(t)[:, None] - (c * CMP_STRIDE + CMP_BLOCK - 1)
    pos_q = past + np.arange(T8)[:, None]
    cmp_s = (pos_q - (np.arange(n_chunks_s)[None, :] * CMP_STRIDE + CMP_BLOCK - 1)).reshape(-1, LANES)
    last = pos_q - (past - PAGE_SIZE + c)
    far = np.full((T8, LANES), 4 * REL_MAX_DIST)
    new = np.arange(T8)[:, None] - c
    win = (wb + np.arange(T8)[:, None] - np.arange(wb + LANES)[None, :]).reshape(-1, LANES)
    parts = [win_tiles, toeplitz, cmp_p, cmp_s, last, far, new, win]
    rows = sum(p.shape[0] for p in parts)
    pad = -rows % LANES
    parts.append(np.zeros((pad, LANES), np.int64))
    offs = np.cumsum([0] + [p.shape[0] for p in parts])
    dist = np.concatenate(parts, axis=0)
    addmask = np.zeros(dist.shape, np.float32)
    addmask[:win_tiles.shape[0]] = np.where((win_tiles >= 0) & (win_tiles < WINDOW), 0.0, NEG_INF)
    return jnp.asarray(dist, I32), jnp.asarray(addmask), offs


def _sample_bias(bias_all, offs, n_chunks_s, wb):
    def rows_kgt(x, width):
        return x.reshape(N_HEADS * T8, width)
    seg = lambda i: bias_all[:, offs[i]:offs[i + 1]]
    return {"cmp": rows_kgt(seg(3), n_chunks_s), "last": rows_kgt(seg(4), LANES), "far": rows_kgt(seg(5), LANES),
            "new": rows_kgt(seg(6), LANES), "win": rows_kgt(seg(7), wb + LANES)}


def kernel(x_prompt, x_sample, cache_nsa, state_nsa_win, cache_sb, page_table, rel_bias, norm_w, final_norm_w,
           nsa_w_in, nsa_w_out, nsa_pe_k, nsa_pe_v, nsa_phi_k1, nsa_phi_k2, nsa_phi_v1, nsa_phi_v2,
           sb_w_in, sb_w_out, ffn_w1, ffn_w3, ffn_w2):
    b, t, d = x_prompt.shape
    bd, t_new, _ = x_sample.shape
    n_pages = page_table.shape[1]
    past = n_pages * PAGE_SIZE
    wb = state_nsa_win.shape[2]
    hd = N_HEADS * HEAD_DIM
    kvw = KV * HEAD_DIM
    n_chunks_s = (past + t_new) // CMP_STRIDE
    half = CMP_STRIDE * HEAD_DIM

    dist, addmask, offs = _bias_distances(t, past, n_chunks_s, wb)
    bias_all = bias_from_dist(rel_bias, dist, addmask)
    bias_s = _sample_bias(bias_all, offs, n_chunks_s, wb)

    w_in = nsa_w_in[0]
    wq = (w_in[:, :hd] * SCALE).astype(BF16)
    wkv4 = w_in[:, hd:hd + 4 * kvw].astype(BF16)
    wwin = w_in[:, hd + 4 * kvw:hd + 6 * kvw].astype(BF16)
    wg = w_in[:, hd + 6 * kvw:].reshape(d, KV, G * 3)
    wg = jnp.pad(wg, ((0, 0), (0, 0), (0, LANES - G * 3))).reshape(d, KV * LANES).astype(BF16)
    w_out_nsa = nsa_w_out[0].astype(BF16)
    cw = {
        "w1k": jnp.concatenate([nsa_phi_k1[0, :half], nsa_phi_k1[0, half:]], axis=1).astype(BF16),
        "w1v": jnp.concatenate([nsa_phi_v1[0, :half], nsa_phi_v1[0, half:]], axis=1).astype(BF16),
        "pe": jnp.broadcast_to(jnp.stack([nsa_pe_k[0].reshape(1, -1), nsa_pe_v[0].reshape(1, -1)]),
                               (2, SUBLANES, CMP_BLOCK * HEAD_DIM)),
        "w1": jnp.stack([nsa_phi_k1[0], nsa_phi_v1[0]]),
        "w2": jnp.stack([nsa_phi_k2[0], nsa_phi_v2[0]]),
    }
    wsb = sb_w_in[0]
    wsq = (wsb[:, :hd] * SCALE).astype(BF16)
    wskv = wsb[:, hd:].astype(BF16)
    w_out_sb = sb_w_out[0].astype(BF16)
    w1 = ffn_w1.astype(BF16)
    w3 = ffn_w3.astype(BF16)
    w2 = ffn_w2.astype(BF16)

    hp = x_prompt.reshape(b * t, d)
    hs = x_sample.reshape(bd * t_new, d)

    xp = rmsnorm_rows(hp, norm_w[0, 0], BF16)
    xs = rmsnorm_rows(hs, norm_w[0, 0], BF16)
    q_p, q_s = matmul(xp, wq, (BF16,)), matmul(xs, wq, (BF16,))
    (kv4_p, kv4b_p), kv4_s = matmul(xp, wkv4, (F32, BF16)), matmul(xs, wkv4, (F32,))
    (win_p, winb_p), win_s = matmul(xp, wwin, (F32, BF16)), matmul(xs, wwin, (F32,))
    g_p, g_s = matmul(xp, wg, (F32,), sigmoid=True), matmul(xs, wg, (F32,), sigmoid=True)

    pt_prompt = jnp.arange(b * (t // PAGE_SIZE), dtype=I32).reshape(b, t // PAGE_SIZE)
    kcb_p = compress(kv4_p.reshape(b * t // PAGE_SIZE, PAGE_SIZE, 4 * kvw), pt_prompt, cw)
    attn_p = nsa_prompt_attention(q_p, kv4b_p, winb_p, g_p, kcb_p, bias_all, b, t)

    cache5 = cache_nsa[0].reshape(cache_nsa.shape[1], PAGE_SIZE, 2, 2 * KV, HEAD_DIM)
    kcb_s = compress_paged(cache5, page_table, cw)
    attn_s = nsa_sample_attention(
        _pad_tokens(q_s, bd, t_new), cache5, page_table, kcb_s, _pad_tokens(kv4_s, bd, t_new),
        state_nsa_win[0].reshape(bd, wb * 2 * KV, HEAD_DIM), _pad_tokens(win_s, bd, t_new),
        _pad_tokens(g_s, bd, t_new), bias_s, past, t_new)
    attn_s = attn_s[:, :t_new].reshape(bd * t_new, hd)

    hp, xp = matmul_res_norm(attn_p, w_out_nsa, hp, norm_w[0, 1])
    hs, xs = matmul_res_norm(attn_s, w_out_nsa, hs, norm_w[0, 1])
    hp, xp = ffn_res_norm(xp, w1[0], w3[0], w2[0], hp, norm_w[1, 0], BF16)
    hs, xs = ffn_res_norm(xs, w1[0], w3[0], w2[0], hs, norm_w[1, 0], BF16)

    sq_p, sq_s = matmul(xp, wsq, (BF16,)), matmul(xs, wsq, (BF16,))
    (skv_p, skvb_p), skv_s = matmul(xp, wskv, (F32, BF16)), matmul(xs, wskv, (F32,))
    sattn_p = sb_prompt_attention(sq_p, skvb_p, b, t)

    q_rows = (_pad_tokens(sq_s, bd, t_new).reshape(bd, T8, N_HEADS, HEAD_DIM).transpose(0, 2, 1, 3)
              .reshape(bd, N_HEADS * T8, HEAD_DIM).astype(F32))
    csb5 = cache_sb[0].reshape(cache_sb.shape[1], PAGE_SIZE, SB_GROUPS, SUBLANES, HEAD_DIM)
    sattn_s = sb_sample_attention(q_rows, csb5, page_table, _pad_tokens(skv_s, bd, t_new), t_new)
    sattn_s = sattn_s.reshape(bd, N_HEADS, T8, HEAD_DIM)[:, :, :t_new].transpose(0, 2, 1, 3).reshape(bd * t_new, hd)

    hp, xp = matmul_res_norm(sattn_p, w_out_sb, hp, norm_w[1, 1])
    hs, xs = matmul_res_norm(sattn_s, w_out_sb, hs, norm_w[1, 1])
    _, y_p = ffn_res_norm(xp, w1[1], w3[1], w2[1], hp, final_norm_w, F32)
    _, y_s = ffn_res_norm(xs, w1[1], w3[1], w2[1], hs, final_norm_w, F32)

    win_keep = min(WINDOW, t)
    nsa_win_prompt = win_p.reshape(b, t, 2, KV, HEAD_DIM)[:, t - win_keep:]
    win_all = jnp.concatenate([state_nsa_win[0], win_s.reshape(bd, t_new, 2, KV, HEAD_DIM)], axis=1)
    return (y_p.reshape(b, t, d), y_s.reshape(bd, t_new, d),
            kv4_p.reshape(1, b, t, 4, KV, HEAD_DIM), kv4_s.reshape(1, bd, t_new, 4, KV, HEAD_DIM),
            nsa_win_prompt[None], win_all[:, t_new:][None],
            skv_p.reshape(1, b, t, 2, N_HEADS, HEAD_DIM), skv_s.reshape(1, bd, t_new, 2, N_HEADS, HEAD_DIM))
```

```python
import functools
import math

import numpy as np
import jax
import jax.numpy as jnp
from jax import lax
from jax.experimental import pallas as pl
from jax.experimental.pallas import tpu as pltpu

F32 = jnp.float32
BF16 = jnp.bfloat16
I32 = jnp.int32

HEAD_DIM = 128
N_HEADS = 16
NSA_KV_HEADS = 4
NSA_GROUP = N_HEADS // NSA_KV_HEADS
CMP_BLOCK = 32
CMP_STRIDE = 16
SEL_BLOCK = 64
SEL_TOPK = 16
WINDOW = 512
FORCE_BONUS = 1000.0
REL_BUCKETS = 32
REL_MAX_DIST = 128
RMS_EPS = 1e-6
NEG_INF = -1e30
MASK_NEG = 2 * NEG_INF
SCALE = HEAD_DIM ** -0.5
PAGE_SIZE = 128

LANES = 128
SUBLANES = 8
TQ = 128
WIN_TILES = WINDOW // TQ + 2
T8 = SUBLANES
VMEM_LIMIT = 56 * 1024 * 1024
KV = NSA_KV_HEADS
G = NSA_GROUP


def _t5_thresholds():
    d = np.arange(0, 4 * REL_MAX_DIST)
    max_exact = REL_BUCKETS // 2
    nf = np.maximum(d, 1).astype(np.float32)
    large = max_exact + (np.log(nf / np.float32(max_exact)) / np.float32(math.log(REL_MAX_DIST / max_exact))
                         * np.float32(REL_BUCKETS - max_exact)).astype(np.int32)
    b = np.where(d < max_exact, d, np.minimum(large, REL_BUCKETS - 1))
    return [int(np.argmax(b >= k)) for k in range(REL_BUCKETS)]


T5_THR = _t5_thresholds()
FAR_DIST = T5_THR[-1]


def _dot(a, b):
    return jnp.dot(a, b, preferred_element_type=F32)


def _dot_nt(a, b):
    return lax.dot_general(a, b, (((1,), (1,)), ((), ())), preferred_element_type=F32)


def _params(*sem):
    return pltpu.CompilerParams(dimension_semantics=sem, vmem_limit_bytes=VMEM_LIMIT)


def _rms(x, w):
    return x * lax.rsqrt(jnp.mean(x * x, axis=-1, keepdims=True) + RMS_EPS) * w


def _masked_softmax(s, mask):
    sm = jnp.where(mask, s, NEG_INF)
    m = jnp.max(sm, axis=1, keepdims=True)
    e = jnp.where(mask, jnp.exp(sm - m), 0.0)
    l = jnp.sum(e, axis=1, keepdims=True)
    return e * jnp.where(l > 0.0, 1.0 / l, 0.0)


def _safe_inv(l):
    return jnp.where(l > 0.0, 1.0 / l, 0.0)


def _rmsnorm_body(x_ref, w_ref, o_ref):
    o_ref[...] = _rms(x_ref[...], w_ref[...]).astype(o_ref.dtype)


def rmsnorm_rows(x, w, out_dtype):
    m, d = x.shape
    tm = min(m, 512)
    return pl.pallas_call(
        _rmsnorm_body, grid=(m // tm,),
        in_specs=[pl.BlockSpec((tm, d), lambda i: (i, 0)), pl.BlockSpec((1, d), lambda i: (0, 0))],
        out_specs=pl.BlockSpec((tm, d), lambda i: (i, 0)),
        out_shape=jax.ShapeDtypeStruct((m, d), out_dtype),
        compiler_params=_params("parallel"), name="rmsnorm")(x, w.reshape(1, d))


def _matmul_body(a_ref, w_ref, *o_refs, sigmoid):
    y = _dot(a_ref[...], w_ref[...])
    if sigmoid:
        y = jax.nn.sigmoid(y)
    for o_ref in o_refs:
        o_ref[...] = y.astype(o_ref.dtype)


def matmul(a, w, out_dtypes, sigmoid=False):
    m, k = a.shape
    n = w.shape[1]
    tm = min(m, 512)
    tn = min(n, 2048)
    outs = pl.pallas_call(
        functools.partial(_matmul_body, sigmoid=sigmoid), grid=(m // tm, n // tn),
        in_specs=[pl.BlockSpec((tm, k), lambda i, j: (i, 0)), pl.BlockSpec((k, tn), lambda i, j: (0, j))],
        out_specs=[pl.BlockSpec((tm, tn), lambda i, j: (i, j)) for _ in out_dtypes],
        out_shape=[jax.ShapeDtypeStruct((m, n), dt) for dt in out_dtypes],
        compiler_params=_params("parallel", "arbitrary"), name="matmul")(a, w)
    return outs if len(outs) > 1 else outs[0]


def _mm_res_norm_body(a_ref, w_ref, h_ref, nw_ref, hout_ref, xn_ref):
    h = h_ref[...] + _dot(a_ref[...], w_ref[...])
    hout_ref[...] = h
    xn_ref[...] = _rms(h, nw_ref[...]).astype(xn_ref.dtype)


def matmul_res_norm(a, w, h, nw):
    m, k = a.shape
    d = w.shape[1]
    tm = min(m, 512)
    return pl.pallas_call(
        _mm_res_norm_body, grid=(m // tm,),
        in_specs=[pl.BlockSpec((tm, k), lambda i: (i, 0)), pl.BlockSpec((k, d), lambda i: (0, 0)),
                  pl.BlockSpec((tm, d), lambda i: (i, 0)), pl.BlockSpec((1, d), lambda i: (0, 0))],
        out_specs=[pl.BlockSpec((tm, d), lambda i: (i, 0)), pl.BlockSpec((tm, d), lambda i: (i, 0))],
        out_shape=[jax.ShapeDtypeStruct((m, d), F32), jax.ShapeDtypeStruct((m, d), BF16)],
        compiler_params=_params("parallel"), name="matmul_res_norm")(a, w, h, nw.reshape(1, d))


def _ffn_body(xn_ref, w1_ref, w3_ref, w2_ref, h_ref, nw_ref, *out_refs):
    hout_ref = out_refs[0] if len(out_refs) == 3 else None
    yn_ref, acc_ref = out_refs[-2:]
    f = pl.program_id(1)

    @pl.when(f == 0)
    def _():
        acc_ref[...] = jnp.zeros_like(acc_ref)

    x = xn_ref[...]
    g = _dot(x, w1_ref[...])
    u = _dot(x, w3_ref[...])
    mid = (g * jax.nn.sigmoid(g) * u).astype(BF16)
    acc_ref[...] += _dot(mid, w2_ref[...])

    @pl.when(f == pl.num_programs(1) - 1)
    def _():
        h = h_ref[...] + acc_ref[...]
        if hout_ref is not None:
            hout_ref[...] = h
        yn_ref[...] = _rms(h, nw_ref[...]).astype(yn_ref.dtype)


def ffn_res_norm(xn, w1, w3, w2, h, nw, norm_dtype, emit_h=True):
    m, d = xn.shape
    dff = w1.shape[1]
    tm = min(m, 512)
    tf = 512
    n_out = 2 if emit_h else 1
    outs = pl.pallas_call(
        _ffn_body, grid=(m // tm, dff // tf),
        in_specs=[pl.BlockSpec((tm, d), lambda i, f: (i, 0)),
                  pl.BlockSpec((d, tf), lambda i, f: (0, f)), pl.BlockSpec((d, tf), lambda i, f: (0, f)),
                  pl.BlockSpec((tf, d), lambda i, f: (f, 0)),
                  pl.BlockSpec((tm, d), lambda i, f: (i, 0)), pl.BlockSpec((1, d), lambda i, f: (0, 0))],
        out_specs=[pl.BlockSpec((tm, d), lambda i, f: (i, 0)) for _ in range(n_out)],
        out_shape=[jax.ShapeDtypeStruct((m, d), F32), jax.ShapeDtypeStruct((m, d), norm_dtype)][-n_out:],
        scratch_shapes=[pltpu.VMEM((tm, d), F32)],
        compiler_params=_params("parallel", "arbitrary"), name="ffn")(xn, w1, w3, w2, h, nw.reshape(1, d))
    return outs if emit_h else outs[0]


def _bias_body(tab_ref, d_ref, m_ref, o_ref):
    h = pl.program_id(0)
    d = d_ref[...]
    acc = jnp.full(d.shape, tab_ref[0, h], F32)
    for b in range(1, REL_BUCKETS):
        acc = jnp.where(d >= T5_THR[b], tab_ref[b, h], acc)
    o_ref[...] = acc + m_ref[...]


def bias_from_dist(rel_bias, dist, addmask):
    r = dist.shape[0]
    return pl.pallas_call(
        _bias_body, grid=(N_HEADS,),
        in_specs=[pl.BlockSpec(memory_space=pltpu.SMEM), pl.BlockSpec((r, LANES), lambda h: (0, 0)),
                  pl.BlockSpec((r, LANES), lambda h: (0, 0))],
        out_specs=pl.BlockSpec((None, r, LANES), lambda h: (h, 0, 0)),
        out_shape=jax.ShapeDtypeStruct((N_HEADS, r, LANES), F32),
        compiler_params=_params("parallel"), name="rel_bias")(rel_bias, dist, addmask)


CMP_PAGES = 16


def _cmp1_body(pt_ref, *refs):
    pages = refs[:CMP_PAGES]
    perm = refs[CMP_PAGES][...]
    w_refs = refs[CMP_PAGES + 1:CMP_PAGES + 3]
    o_ref = refs[CMP_PAGES + 3]
    cpp = PAGE_SIZE // CMP_STRIDE
    for kind in range(2):
        w = w_refs[kind][...]
        for kv in range(KV):
            c0 = (kind * KV + kv) * HEAD_DIM
            xs = [_dot(perm, pg[:, c0:c0 + HEAD_DIM].astype(BF16)) for pg in pages]
            pieces = [jnp.concatenate([x[r * cpp:(r + 1) * cpp] for x in xs], axis=0).astype(BF16)
                      for r in range(CMP_STRIDE)]
            o_ref[kind * KV + kv] = _dot(jnp.concatenate(pieces, axis=1), w)


def compress_stage1(rows3, page_table, w1k, w1v):
    nb_seq, n_pages = page_table.shape
    steps = n_pages // CMP_PAGES
    cpp = PAGE_SIZE // CMP_STRIDE
    width = 2 * KV * HEAD_DIM

    def page_spec(i):
        return pl.BlockSpec((None, PAGE_SIZE, width), lambda b, s, pt: (pt[b, s * CMP_PAGES + i], 0, 0))

    wspec = pl.BlockSpec((CMP_STRIDE * HEAD_DIM, 2 * HEAD_DIM), lambda b, s, pt: (0, 0))
    src = np.arange(PAGE_SIZE)
    perm = np.zeros((PAGE_SIZE, PAGE_SIZE), np.float32)
    perm[(src % CMP_STRIDE) * cpp + src // CMP_STRIDE, src] = 1.0
    perm = jnp.asarray(perm, BF16)
    grid_spec = pltpu.PrefetchScalarGridSpec(
        num_scalar_prefetch=1, grid=(nb_seq, steps),
        in_specs=[page_spec(i) for i in range(CMP_PAGES)]
        + [pl.BlockSpec((PAGE_SIZE, PAGE_SIZE), lambda b, s, pt: (0, 0)), wspec, wspec],
        out_specs=pl.BlockSpec((None, 2 * KV, CMP_PAGES * cpp, 2 * HEAD_DIM), lambda b, s, pt: (b, 0, s, 0)))
    return pl.pallas_call(
        _cmp1_body, grid_spec=grid_spec,
        out_shape=jax.ShapeDtypeStruct((nb_seq, 2 * KV, n_pages * cpp, 2 * HEAD_DIM), F32),
        compiler_params=_params("parallel", "arbitrary"), name="compress1")(
            page_table, *([rows3] * CMP_PAGES), perm, w1k, w1v)


def _cmp1_paged_body(pt_ref, *refs):
    cmp_pages = [r.reshape(PAGE_SIZE * SUBLANES, HEAD_DIM) for r in refs[:CMP_PAGES]]
    w_refs = refs[CMP_PAGES:CMP_PAGES + 2]
    o_ref = refs[CMP_PAGES + 2]
    cpp = PAGE_SIZE // CMP_STRIDE
    for kind in range(2):
        w = w_refs[kind][...]
        for kv in range(KV):
            c = kind * KV + kv
            xs = [jnp.swapaxes(pg[pl.ds(c, PAGE_SIZE, stride=SUBLANES), :].reshape(cpp, CMP_STRIDE, HEAD_DIM), 0, 1)
                  for pg in cmp_pages]
            pieces = [jnp.concatenate([x[r] for x in xs], axis=0).astype(BF16) for r in range(CMP_STRIDE)]
            o_ref[c] = _dot(jnp.concatenate(pieces, axis=1), w)


def compress_stage1_paged(cache5, page_table, w1k, w1v):
    nb_seq, n_pages = page_table.shape
    steps = n_pages // CMP_PAGES
    cpp = PAGE_SIZE // CMP_STRIDE
    assert cache5.shape[2:] == (2, 2 * KV, HEAD_DIM) and 2 * KV == SUBLANES

    def group_spec(i):
        return pl.BlockSpec((None, PAGE_SIZE, None, SUBLANES, HEAD_DIM),
                            lambda b, s, pt: (pt[b, s * CMP_PAGES + i], 0, 0, 0, 0))

    wspec = pl.BlockSpec((CMP_STRIDE * HEAD_DIM, 2 * HEAD_DIM), lambda b, s, pt: (0, 0))
    grid_spec = pltpu.PrefetchScalarGridSpec(
        num_scalar_prefetch=1, grid=(nb_seq, steps),
        in_specs=[group_spec(i) for i in range(CMP_PAGES)] + [wspec, wspec],
        out_specs=pl.BlockSpec((None, 2 * KV, CMP_PAGES * cpp, 2 * HEAD_DIM), lambda b, s, pt: (b, 0, s, 0)))
    return pl.pallas_call(
        _cmp1_paged_body, grid_spec=grid_spec,
        out_shape=jax.ShapeDtypeStruct((nb_seq, 2 * KV, n_pages * cpp, 2 * HEAD_DIM), F32),
        compiler_params=_params("parallel", "arbitrary"), name="compress1_paged")(
            page_table, *([cache5] * CMP_PAGES), w1k, w1v)


def _cmp2_body(ab_ref, pe_ref, w1_ref, w2_ref, o_ref):
    nch = ab_ref.shape[0]
    ab = ab_ref[...]
    c = _dot(pe_ref[...].astype(BF16), w1_ref[...].astype(BF16))
    pre = ab[:, :HEAD_DIM] + pltpu.roll(ab[:, HEAD_DIM:], nch - 1, 0) + c[0:1]
    y = _dot((pre * jax.nn.sigmoid(pre)).astype(BF16), w2_ref[...].astype(BF16))
    rowi = lax.broadcasted_iota(I32, y.shape, 0)
    o_ref[...] = jnp.where(rowi < nch - 1, y, 0.0)


def compress_stage2(ab, pe, w1, w2):
    nb_seq, _, nch, _ = ab.shape
    return pl.pallas_call(
        _cmp2_body, grid=(nb_seq, 2 * KV),
        in_specs=[pl.BlockSpec((None, None, nch, 2 * HEAD_DIM), lambda b, j: (b, j, 0, 0)),
                  pl.BlockSpec((None, SUBLANES, CMP_BLOCK * HEAD_DIM), lambda b, j: (j // KV, 0, 0)),
                  pl.BlockSpec((None, CMP_BLOCK * HEAD_DIM, HEAD_DIM), lambda b, j: (j // KV, 0, 0)),
                  pl.BlockSpec((None, HEAD_DIM, HEAD_DIM), lambda b, j: (j // KV, 0, 0))],
        out_specs=pl.BlockSpec((None, None, nch, HEAD_DIM), lambda b, j: (b, j, 0, 0)),
        out_shape=jax.ShapeDtypeStruct((nb_seq, 2 * KV, nch, HEAD_DIM), F32),
        compiler_params=_params("parallel", "parallel"), name="compress2")(ab, pe, w1, w2)


def compress(rows3, page_table, cw):
    ab = compress_stage1(rows3, page_table, cw["w1k"], cw["w1v"])
    return compress_stage2(ab, cw["pe"], cw["w1"], cw["w2"])


def compress_paged(cache3, page_table, cw):
    ab = compress_stage1_paged(cache3, page_table, cw["w1k"], cw["w1v"])
    return compress_stage2(ab, cw["pe"], cw["w1"], cw["w2"])


def _flash_prob(s, mask, m_ref, l_ref, rows):
    m_prev = m_ref[rows, :]
    if mask is not None:
        s = jnp.where(mask, s, MASK_NEG)
    m_new = jnp.maximum(m_prev, jnp.max(s, axis=1, keepdims=True))
    alpha = jnp.exp(m_prev - m_new)
    m_wide = m_new if s.shape[1] == LANES else jnp.concatenate([m_new] * (s.shape[1] // LANES), axis=1)
    p = jnp.exp(s - m_wide)
    l_ref[rows, :] = alpha * l_ref[rows, :] + jnp.sum(p, axis=1, keepdims=True)
    m_ref[rows, :] = m_new
    return p.astype(BF16), alpha


def _flash_update(s, mask, v, m_ref, l_ref, acc_ref, rows):
    p, alpha = _flash_prob(s, mask, m_ref, l_ref, rows)
    acc_ref[rows, :] = alpha * acc_ref[rows, :] + _dot(p, v)


def _flash_init(m_ref, l_ref, acc_ref):
    m_ref[...] = jnp.full(m_ref.shape, NEG_INF, F32)
    l_ref[...] = jnp.zeros_like(l_ref)
    acc_ref[...] = jnp.zeros_like(acc_ref)


def _rank_select(score, eligible, blk, n_blocks, n_top, row_of):
    def body(j, cnt):
        rowj = row_of(j)
        beats = (rowj > score) | ((rowj == score) & (blk > j))
        return cnt + beats.astype(I32)
    cnt = lax.fori_loop(0, n_blocks, body, jnp.zeros(score.shape, I32))
    return jnp.where((cnt < n_top) & eligible, 1.0, 0.0)


NSA_HP = 2


def _nsa_prompt_body(q_ref, ks_ref, vs_ref, kw_ref, vw_ref, kc_ref, vc_ref, g_ref, bc_ref, bt_ref, wbt_ref,
                      ovl_ref, exp_ref,
                      o_ref, m_s, l_s, a_s, m_w, l_w, a_w, mk_ref, sc_ref, *, n_cmp, n_blk):
    qi = pl.program_id(2)
    rows = G * TQ
    gw = G * HEAD_DIM
    row5 = lax.broadcasted_iota(I32, (rows, LANES), 0)
    lane5 = lax.broadcasted_iota(I32, (rows, LANES), 1)
    tpos5 = qi * TQ + (row5 & (TQ - 1))
    mask_c = (tpos5 >= lane5 * CMP_STRIDE + (CMP_BLOCK - 1)) & (lane5 < n_cmp)
    nbr = ovl_ref.shape[0]
    blk = lax.broadcasted_iota(I32, (nbr, TQ), 0)
    tq_l = qi * TQ + lax.broadcasted_iota(I32, (nbr, TQ), 1)
    cur = lax.shift_right_arithmetic(tq_l, int(math.log2(SEL_BLOCK)))
    eligible = (blk * SEL_BLOCK <= tq_l) & (blk < n_blk)
    forced = (blk == 0) | (blk == cur) | (blk == cur - 1)
    row1 = lax.broadcasted_iota(I32, (TQ, TQ), 0)
    lane1 = lax.broadcasted_iota(I32, (TQ, TQ), 1)
    off_d = pl.multiple_of(qi * TQ, TQ)

    qs, o_cs = [], []
    for hp in range(NSA_HP):
        q = jnp.concatenate([q_ref[:, hp * gw + g * HEAD_DIM:hp * gw + (g + 1) * HEAD_DIM] for g in range(G)],
                            axis=0)
        qs.append(q)
        s = _dot_nt(q, kc_ref[hp].astype(BF16))
        s = s + jnp.concatenate([bc_ref[hp * G + g] for g in range(G)], axis=0)
        pb = _masked_softmax(s, mask_c).astype(BF16)
        o_cs.append(_dot(pb, vc_ref[hp].astype(BF16)))
        imp = _dot_nt(ovl_ref[...], pb[0:TQ])
        for g in range(1, G):
            imp = imp + _dot_nt(ovl_ref[...], pb[g * TQ:(g + 1) * TQ])
        score = jnp.where(eligible, imp + jnp.where(forced, FORCE_BONUS, 0.0), -jnp.inf)
        sc_ref[hp] = score
        sel_t = _rank_select(score, eligible, blk, n_blk, min(SEL_TOPK, n_blk),
                             lambda j, hp=hp: sc_ref[hp, pl.ds(j, 1), :])
        unsel = jnp.concatenate([1.0 - sel_t, jnp.zeros((LANES - nbr, TQ), F32)], axis=0)
        mk_ref[hp] = _dot(unsel.T.astype(BF16), exp_ref[...])
        mk_ref[hp, :, pl.ds(off_d, TQ)] = jnp.where(lane1 <= row1, mk_ref[hp, :, pl.ds(off_d, TQ)], MASK_NEG)

    @pl.when(qi + 1 < pl.num_programs(2))
    def _():
        for hp in range(NSA_HP):
            mk_ref[hp, :, pl.ds(pl.multiple_of((qi + 1) * TQ, TQ), TQ)] = jnp.full((TQ, TQ), MASK_NEG, F32)

    kt2 = 2 * TQ

    def tile_pair(ref, hp, j, index_of):
        tiles = []
        for i in range(2):
            off = pl.multiple_of(index_of(qi - (2 * j + i)) * TQ, TQ)
            tiles.append(jnp.concatenate([ref[hp * G + g, pl.ds(off, TQ), :] for g in range(G)], axis=0))
        return jnp.concatenate(tiles, axis=1)

    _flash_init(m_s, l_s, a_s)
    _flash_init(m_w, l_w, a_w)
    head_rows = [pl.ds(hp * rows, rows) for hp in range(NSA_HP)]
    head_cols = [slice(hp * HEAD_DIM, (hp + 1) * HEAD_DIM) for hp in range(NSA_HP)]

    def pair_step(j, with_win):
        off = pl.multiple_of(j * kt2, kt2)
        s_s = [_dot_nt(qs[hp], ks_ref[pl.ds(off, kt2), head_cols[hp]]) for hp in range(NSA_HP)]
        if with_win:
            s_w = [_dot_nt(qs[hp], kw_ref[pl.ds(off, kt2), head_cols[hp]]) for hp in range(NSA_HP)]
        sel, win = [], []
        for hp in range(NSA_HP):
            bias_s = tile_pair(bt_ref, hp, j, lambda d: jnp.clip(d, 0, 2))
            key_mask = jnp.concatenate([mk_ref[hp, :, pl.ds(off, kt2)]] * G, axis=0)
            sel.append(_flash_prob(s_s[hp] + bias_s + key_mask, None, m_s, l_s, head_rows[hp]))
            if with_win:
                bias_w = tile_pair(wbt_ref, hp, j,
                                   lambda d: jnp.where(d < 0, WIN_TILES - 1, jnp.minimum(d, WIN_TILES - 1)))
                win.append(_flash_prob(s_w[hp] + bias_w, None, m_w, l_w, head_rows[hp]))
        for hp in range(NSA_HP):
            p, alpha = sel[hp]
            a_s[head_rows[hp], :] = alpha * a_s[head_rows[hp], :] + _dot(p, vs_ref[pl.ds(off, kt2), head_cols[hp]])
            if with_win:
                p, alpha = win[hp]
                a_w[head_rows[hp], :] = (alpha * a_w[head_rows[hp], :]
                                         + _dot(p, vw_ref[pl.ds(off, kt2), head_cols[hp]]))

    def sel_only(j, c):
        pair_step(j, False)
        return c

    def sel_and_win(j, c):
        pair_step(j, True)
        return c

    lo = lax.shift_right_arithmetic(jnp.maximum(qi - WINDOW // TQ, 0), 1)
    hi = lax.shift_right_arithmetic(qi, 1) + 1
    lax.fori_loop(0, lo, sel_only, 0)
    lax.fori_loop(lo, hi, sel_and_win, 0)

    for hp in range(NSA_HP):
        o_s = a_s[head_rows[hp], :] * _safe_inv(l_s[head_rows[hp], :])
        o_w = a_w[head_rows[hp], :] * _safe_inv(l_w[head_rows[hp], :])
        gt = g_ref[:, hp * LANES:(hp + 1) * LANES]
        for g in range(G):
            r = slice(g * TQ, (g + 1) * TQ)
            out = (gt[:, 3 * g:3 * g + 1] * o_cs[hp][r] + gt[:, 3 * g + 1:3 * g + 2] * o_s[r]
                   + gt[:, 3 * g + 2:3 * g + 3] * o_w[r])
            o_ref[:, hp * gw + g * HEAD_DIM:hp * gw + (g + 1) * HEAD_DIM] = out.astype(o_ref.dtype)


def nsa_prompt_attention(q, kv4, kwin, gates, kcb, bias_all, b, t):
    nq = t // TQ
    n_chunks = t // CMP_STRIDE
    n_cmp = n_chunks - CMP_BLOCK // CMP_STRIDE + 1
    n_blk = -(-t // SEL_BLOCK)
    assert n_chunks == LANES and n_blk <= LANES and t % TQ == 0 and TQ >= FAR_DIST
    assert WIN_TILES % 3 == 0 and TQ == LANES and KV % NSA_HP == 0
    cmp_start = np.arange(LANES) * CMP_STRIDE
    jst = np.arange(LANES) * SEL_BLOCK
    overlap = ((cmp_start[:, None] < jst[None, :] + SEL_BLOCK) & (cmp_start[:, None] + CMP_BLOCK > jst[None, :])
               & (np.arange(LANES)[:, None] < n_cmp) & (np.arange(LANES)[None, :] < n_blk))
    nbr = -(-n_blk // (2 * SUBLANES)) * 2 * SUBLANES
    ovl_t = jnp.asarray(overlap.T[:nbr], BF16)
    expand = jnp.asarray(np.where(np.arange(LANES)[:, None] == (np.arange(t)[None, :] // SEL_BLOCK), MASK_NEG, 0.0),
                         BF16)
    hw = NSA_HP * HEAD_DIM
    hg = NSA_HP * G
    colblk = lambda base: (lambda bb, kp, qi: (bb, base // NSA_HP + kp))
    rowblk = lambda bb, kp, qi: (bb * nq + qi, kp)
    in_specs = [
        pl.BlockSpec((TQ, hg * HEAD_DIM), rowblk),
        pl.BlockSpec((t, hw), colblk(2 * KV)), pl.BlockSpec((t, hw), colblk(3 * KV)),
        pl.BlockSpec((t, hw), colblk(0)), pl.BlockSpec((t, hw), colblk(KV)),
        pl.BlockSpec((None, NSA_HP, n_chunks, HEAD_DIM), lambda bb, kp, qi: (bb, kp, 0, 0)),
        pl.BlockSpec((None, NSA_HP, n_chunks, HEAD_DIM), lambda bb, kp, qi: (bb, KV // NSA_HP + kp, 0, 0)),
        pl.BlockSpec((TQ, NSA_HP * LANES), rowblk),
        pl.BlockSpec((hg, TQ, LANES), lambda bb, kp, qi: (kp, WIN_TILES + 3 + qi, 0)),
        pl.BlockSpec((hg, 3 * TQ, LANES), lambda bb, kp, qi: (kp, WIN_TILES // 3, 0)),
        pl.BlockSpec((hg, WIN_TILES * TQ, LANES), lambda bb, kp, qi: (kp, 0, 0)),
        pl.BlockSpec((nbr, LANES), lambda bb, kp, qi: (0, 0)),
        pl.BlockSpec((LANES, t), lambda bb, kp, qi: (0, 0)),
    ]
    rows = NSA_HP * G * TQ
    scratch = [pltpu.VMEM((rows, LANES), F32) for _ in range(6)] + [pltpu.VMEM((NSA_HP, TQ, t), F32),
                                                                   pltpu.VMEM((NSA_HP, nbr, TQ), F32)]
    return pl.pallas_call(
        functools.partial(_nsa_prompt_body, n_cmp=n_cmp, n_blk=n_blk), grid=(b, KV // NSA_HP, nq),
        in_specs=in_specs, out_specs=pl.BlockSpec((TQ, hg * HEAD_DIM), rowblk),
        out_shape=jax.ShapeDtypeStruct((b * t, N_HEADS * HEAD_DIM), BF16),
        scratch_shapes=scratch,
        compiler_params=_params("parallel", "parallel", "arbitrary"), name="nsa_prompt_attn")(
            q, kv4, kv4, kwin, kwin, kcb, kcb, gates, bias_all, bias_all, bias_all, ovl_t, expand)


def _stick_pre(z, mask):
    sp = jnp.log(1.0 + jnp.exp(-jnp.abs(z)))
    log_beta = jnp.minimum(z, 0.0) - sp
    rest = jnp.maximum(z, 0.0) + sp
    if mask is not None:
        rest = jnp.where(mask, rest, 0.0)
    hi = rest.astype(BF16)
    lo = (rest - hi.astype(F32)).astype(BF16)
    return log_beta, jnp.sum(rest, axis=1, keepdims=True), jnp.concatenate([hi, lo], axis=0)


def _stick_post(log_beta, l2, carry, mask):
    n = log_beta.shape[0]
    a = jnp.exp(log_beta - (l2[0:n] + l2[n:2 * n] + carry))
    return a if mask is None else jnp.where(mask, a, 0.0)


SB_HEADS = 8
SB_DONE = 110.0


def _sb_prompt_body(q_ref, k_ref, v_ref, u_ref, o_ref, carry_ref, acc_ref):
    qi = pl.program_id(2)
    row = lax.broadcasted_iota(I32, (TQ, TQ), 0)
    lane = lax.broadcasted_iota(I32, (TQ, TQ), 1)

    def tile(kt, mask, first):
        off = pl.multiple_of(kt * TQ, TQ)
        cols = [slice(h * HEAD_DIM, (h + 1) * HEAD_DIM) for h in range(SB_HEADS)]
        zs = [_dot_nt(q_ref[:, c], k_ref[pl.ds(off, TQ), c]) for c in cols]
        pre = [_stick_pre(z, mask) for z in zs]
        l3s = [_dot(p[2], u_ref[...]) for p in pre]
        ws = [_stick_post(p[0], l3, 0.0 if first else carry_ref[h], mask).astype(BF16)
              for h, (p, l3) in enumerate(zip(pre, l3s))]
        pvs = [_dot(w, v_ref[pl.ds(off, TQ), c]) for w, c in zip(ws, cols)]
        for h in range(SB_HEADS):
            tot = pre[h][1]
            acc_ref[h] = pvs[h] if first else acc_ref[h] + pvs[h]
            carry_ref[h] = jnp.broadcast_to(tot, (TQ, TQ)) if first else carry_ref[h] + tot

    tile(qi, lane < row, True)

    def unfinished():
        c = carry_ref[0]
        for h in range(1, SB_HEADS):
            c = jnp.minimum(c, carry_ref[h])
        return (jnp.min(c) < SB_DONE).astype(I32)

    def step(state):
        i, _ = state
        tile(qi - 1 - i, None, False)
        return i + 1, unfinished()

    lax.while_loop(lambda state: (state[0] < qi) & (state[1] > 0), step, (jnp.int32(0), unfinished()))
    for h in range(SB_HEADS):
        o_ref[:, h * HEAD_DIM:(h + 1) * HEAD_DIM] = acc_ref[h].astype(o_ref.dtype)


def _suffix_matrix():
    j = np.arange(TQ)
    return jnp.asarray(j[:, None] > j[None, :], BF16)


def sb_prompt_attention(q, kvp, b, t):
    nq = t // TQ
    hg = N_HEADS // SB_HEADS
    w = SB_HEADS * HEAD_DIM
    rowblk = lambda bb, h, qi: (bb * nq + qi, h)
    return pl.pallas_call(
        _sb_prompt_body, grid=(b, hg, nq),
        in_specs=[pl.BlockSpec((TQ, w), rowblk),
                  pl.BlockSpec((t, w), lambda bb, h, qi: (bb, h)),
                  pl.BlockSpec((t, w), lambda bb, h, qi: (bb, hg + h)),
                  pl.BlockSpec((TQ, TQ), lambda bb, h, qi: (0, 0))],
        out_specs=pl.BlockSpec((TQ, w), rowblk),
        out_shape=jax.ShapeDtypeStruct((b * t, N_HEADS * HEAD_DIM), BF16),
        scratch_shapes=[pltpu.VMEM((SB_HEADS, TQ, TQ), F32), pltpu.VMEM((SB_HEADS, TQ, HEAD_DIM), F32)],
        compiler_params=_params("parallel", "parallel", "arbitrary"), name="sb_prompt_attn")(
            q, kvp, kvp, _suffix_matrix())


def _nsa_sample_select_body(q_ref, kcb_ref, bc_ref, mc_ref, ovl_ref, fadd_ref, elig_ref, oc_ref, sel_ref, sc_ref,
                            *, n_blk):
    nbp = kcb_ref.shape[1]
    grp = G * T8
    ps = []
    for kv in range(KV):
        qk = jnp.concatenate([q_ref[:, (kv * G + g) * HEAD_DIM:(kv * G + g + 1) * HEAD_DIM] for g in range(G)],
                             axis=0)
        s = _dot_nt(qk, kcb_ref[kv].astype(BF16)) + bc_ref[kv * grp:(kv + 1) * grp, :]
        p = _masked_softmax(s, mc_ref[...] > 0.5)
        oc_ref[kv * grp:(kv + 1) * grp, :] = _dot(p.astype(BF16), kcb_ref[KV + kv].astype(BF16))
        ps.append(p)
    pad = jnp.zeros((LANES - KV * T8, nbp), F32)
    imp = None
    for g in range(G):
        xg = jnp.concatenate([ps[kv][g * T8:(g + 1) * T8] for kv in range(KV)] + [pad], axis=0)
        part = _dot_nt(ovl_ref[...], xg.astype(BF16))
        imp = part if imp is None else imp + part
    eligible = elig_ref[...] > 0.5
    score = jnp.where(eligible, imp + fadd_ref[...], -jnp.inf)
    sc_ref[...] = score
    blk = lax.broadcasted_iota(I32, score.shape, 0)
    sel_ref[...] = _rank_select(score, eligible, blk, n_blk, min(SEL_TOPK, n_blk),
                                lambda j: sc_ref[pl.ds(j, 1), :])


SEL_KEYS = 2048
SEL_PAGES = SEL_KEYS // PAGE_SIZE


def _nsa_sample_attn_body(pt_ref, q_ref, *refs, n_steps, t_new, past):
    pages = [r.reshape(PAGE_SIZE * SUBLANES, HEAD_DIM) for r in refs[:SEL_PAGES]]
    (sel_ref, new_ref, win_ref, wnew_ref, g_ref, oc_ref, bl_ref, bf_ref, bn_ref, bw_ref, ex_ref,
     o_ref, m_s, l_s, a_s) = refs[SEL_PAGES:]

    def page_rows(c):
        return jnp.concatenate([pg[pl.ds(c, PAGE_SIZE, stride=SUBLANES), :] for pg in pages], axis=0).astype(BF16)

    step = pl.program_id(1)
    grp = G * T8
    kvw = KV * HEAD_DIM
    wb = win_ref.shape[0] // (2 * KV)
    is_last = step == n_steps - 1

    @pl.when(step == 0)
    def _():
        _flash_init(m_s, l_s, a_s)

    def q_of(kv):
        return jnp.concatenate([q_ref[:, (kv * G + g) * HEAD_DIM:(kv * G + g + 1) * HEAD_DIM] for g in range(G)],
                               axis=0)

    def pad_keys(x):
        return jnp.concatenate([x, jnp.zeros((TQ - x.shape[0], x.shape[1]), x.dtype)], axis=0).astype(BF16)

    def rows_of_blocks(piece):
        return jnp.concatenate([piece, jnp.zeros((LANES - piece.shape[0], LANES), F32)], axis=0).T

    bps = SEL_KEYS // SEL_BLOCK
    piece = sel_ref[pl.ds(pl.multiple_of(step * bps, bps), bps), :]
    mk = _dot(rows_of_blocks(piece).astype(BF16), ex_ref[...])

    kv_rows = [pl.ds(kv * grp, grp) for kv in range(KV)]
    scores = [_dot_nt(q_of(kv), page_rows(kv)) for kv in range(KV)]
    probs = []
    for kv in range(KV):
        bias = jnp.where(is_last, bl_ref[kv_rows[kv], :], bf_ref[kv_rows[kv], 0:1])
        mask = jnp.concatenate([mk[kv * T8:(kv + 1) * T8]] * G, axis=0) > 0.5
        probs.append(_flash_prob(scores[kv] + bias, mask, m_s, l_s, kv_rows[kv]))
    for kv in range(KV):
        p, alpha = probs[kv]
        a_s[kv_rows[kv], :] = alpha * a_s[kv_rows[kv], :] + _dot(p, page_rows(KV + kv))

    @pl.when(is_last)
    def _():
        lane = lax.broadcasted_iota(I32, (grp, LANES), 1)
        t_row = lax.broadcasted_iota(I32, (grp, LANES), 0) & (T8 - 1)
        lane_w = lax.broadcasted_iota(I32, (grp, wb), 1)
        t_row_w = lax.broadcasted_iota(I32, (grp, wb), 0) & (T8 - 1)
        new_blk = past // SEL_BLOCK
        picked_all = rows_of_blocks(sel_ref[new_blk:new_blk + SUBLANES, :])
        gt = g_ref[...]
        for kv in range(KV):
            rows = pl.ds(kv * grp, grp)
            qk = q_of(kv)
            kn = pad_keys(new_ref[:, 2 * kvw + kv * HEAD_DIM:2 * kvw + (kv + 1) * HEAD_DIM])
            vn = pad_keys(new_ref[:, 3 * kvw + kv * HEAD_DIM:3 * kvw + (kv + 1) * HEAD_DIM])
            s = _dot_nt(qk, kn) + bn_ref[rows, :]
            picked = jnp.concatenate([picked_all[kv * T8:(kv + 1) * T8, 0:1]] * G, axis=0) > 0.5
            mask = (lane <= t_row) & (lane < t_new) & picked
            _flash_update(s, mask, vn, m_s, l_s, a_s, rows)
            o_s = a_s[rows, :] * _safe_inv(l_s[rows, :])
            kw = win_ref[pl.ds(kv, wb, stride=2 * KV), :].astype(BF16)
            vw = win_ref[pl.ds(KV + kv, wb, stride=2 * KV), :].astype(BF16)
            kwn = pad_keys(wnew_ref[:, kv * HEAD_DIM:(kv + 1) * HEAD_DIM])
            vwn = pad_keys(wnew_ref[:, kvw + kv * HEAD_DIM:kvw + (kv + 1) * HEAD_DIM])
            s1 = _dot_nt(qk, kw) + bw_ref[rows, 0:wb]
            s2 = _dot_nt(qk, kwn) + bw_ref[rows, wb:wb + LANES]
            d1 = wb + t_row_w - lane_w
            mask1 = (d1 >= 0) & (d1 < WINDOW)
            d2 = t_row - lane
            mask2 = (d2 >= 0) & (d2 < WINDOW) & (lane < t_new)
            m = jnp.maximum(jnp.max(jnp.where(mask1, s1, NEG_INF), axis=1, keepdims=True),
                            jnp.max(jnp.where(mask2, s2, NEG_INF), axis=1, keepdims=True))
            e1 = jnp.where(mask1, jnp.exp(s1 - m), 0.0)
            e2 = jnp.where(mask2, jnp.exp(s2 - m), 0.0)
            l = jnp.sum(e1, axis=1, keepdims=True) + jnp.sum(e2, axis=1, keepdims=True)
            o_w = (_dot(e1.astype(BF16), vw) + _dot(e2.astype(BF16), vwn)) * _safe_inv(l)
            o_c = oc_ref[rows, :]
            for g in range(G):
                r = slice(g * T8, (g + 1) * T8)
                c = kv * LANES + 3 * g
                out = gt[:, c:c + 1] * o_c[r] + gt[:, c + 1:c + 2] * o_s[r] + gt[:, c + 2:c + 3] * o_w[r]
                o_ref[:, (kv * G + g) * HEAD_DIM:(kv * G + g + 1) * HEAD_DIM] = out.astype(o_ref.dtype)


def nsa_sample_attention(q8, cache5, page_table, kcb, new8, win3, wnew8, gates8, bias_s, past, t_new):
    bd = q8.shape[0]
    n_chunks = kcb.shape[2]
    n_cmp = n_chunks - CMP_BLOCK // CMP_STRIDE + 1
    total = past + t_new
    n_blk = -(-total // SEL_BLOCK)
    nblkp = -(-(n_blk + SUBLANES) // (2 * SUBLANES)) * 2 * SUBLANES
    wb = win3.shape[1] // (2 * KV)
    n_steps = past // SEL_KEYS
    assert past % SEL_KEYS == 0 and SEL_KEYS % SEL_BLOCK == 0 and t_new <= T8 and PAGE_SIZE > FAR_DIST
    assert (total // CMP_STRIDE) == n_chunks and wb % LANES == 0 and SEL_KEYS // SEL_BLOCK <= LANES

    pos_q = past + np.arange(T8)
    cmp_start = np.arange(n_chunks) * CMP_STRIDE
    cmp_end = cmp_start + CMP_BLOCK - 1
    mask_c = ((pos_q[:, None] >= cmp_end[None, :]) & (np.arange(n_chunks)[None, :] < n_cmp))
    mask_c = jnp.asarray(np.tile(mask_c, (G, 1)), F32)
    jst = np.arange(nblkp) * SEL_BLOCK
    overlap = ((cmp_start[:, None] < jst[None, :] + SEL_BLOCK) & (cmp_start[:, None] + CMP_BLOCK > jst[None, :])
               & (np.arange(n_chunks)[:, None] < n_cmp) & (np.arange(nblkp)[None, :] < n_blk))
    ovl_t = jnp.asarray(overlap.T, BF16)
    blk = np.arange(nblkp)[:, None]
    pq = np.tile(pos_q, KV)[None, :]
    tvalid = np.tile(np.arange(T8) < t_new, KV)[None, :]
    cur = pq // SEL_BLOCK
    elig = (blk * SEL_BLOCK <= pq) & (blk < n_blk) & tvalid
    forced = (blk == 0) | (blk == cur) | (blk == cur - 1)
    pad_l = LANES - KV * T8
    elig = jnp.asarray(np.pad(elig, ((0, 0), (0, pad_l))), F32)
    fadd = jnp.asarray(np.pad(np.where(forced, FORCE_BONUS, 0.0), ((0, 0), (0, pad_l))), F32)

    grp_all = KV * G * T8
    o_c, sel = pl.pallas_call(
        functools.partial(_nsa_sample_select_body, n_blk=n_blk), grid=(bd,),
        in_specs=[pl.BlockSpec((None, T8, N_HEADS * HEAD_DIM), lambda b: (b, 0, 0)),
                  pl.BlockSpec((None, 2 * KV, n_chunks, HEAD_DIM), lambda b: (b, 0, 0, 0)),
                  pl.BlockSpec((grp_all, n_chunks), lambda b: (0, 0)),
                  pl.BlockSpec((G * T8, n_chunks), lambda b: (0, 0)),
                  pl.BlockSpec((nblkp, n_chunks), lambda b: (0, 0)),
                  pl.BlockSpec((nblkp, LANES), lambda b: (0, 0)),
                  pl.BlockSpec((nblkp, LANES), lambda b: (0, 0))],
        out_specs=[pl.BlockSpec((None, grp_all, HEAD_DIM), lambda b: (b, 0, 0)),
                   pl.BlockSpec((None, nblkp, LANES), lambda b: (b, 0, 0))],
        out_shape=[jax.ShapeDtypeStruct((bd, grp_all, HEAD_DIM), F32),
                   jax.ShapeDtypeStruct((bd, nblkp, LANES), F32)],
        scratch_shapes=[pltpu.VMEM((nblkp, LANES), F32)],
        compiler_params=_params("parallel"), name="nsa_sample_select")(
            q8, kcb, bias_s["cmp"], mask_c, ovl_t, fadd, elig)

    kvw = KV * HEAD_DIM
    const2 = lambda b, s, pt: (0, 0)
    per_b = lambda b, s, pt: (b, 0, 0)
    expand = jnp.asarray(np.arange(LANES)[:, None] == (np.arange(SEL_KEYS)[None, :] // SEL_BLOCK), BF16)
    bias_last = jnp.concatenate([bias_s["far"]] * (SEL_KEYS // LANES - 1) + [bias_s["last"]], axis=1)

    def page_spec(i):
        return pl.BlockSpec((None, PAGE_SIZE, None, SUBLANES, HEAD_DIM),
                            lambda b, s, pt: (pt[b, s * SEL_PAGES + i], 0, 1, 0, 0))

    grid_spec = pltpu.PrefetchScalarGridSpec(
        num_scalar_prefetch=1, grid=(bd, n_steps),
        in_specs=[pl.BlockSpec((None, T8, N_HEADS * HEAD_DIM), per_b)] + [page_spec(i) for i in range(SEL_PAGES)]
        + [pl.BlockSpec((None, nblkp, LANES), per_b),
                  pl.BlockSpec((None, T8, 4 * kvw), per_b),
                  pl.BlockSpec((None, wb * 2 * KV, HEAD_DIM), per_b),
                  pl.BlockSpec((None, T8, 2 * kvw), per_b),
                  pl.BlockSpec((None, T8, KV * LANES), per_b),
                  pl.BlockSpec((None, grp_all, HEAD_DIM), per_b),
                  pl.BlockSpec((grp_all, SEL_KEYS), const2), pl.BlockSpec((grp_all, LANES), const2),
                  pl.BlockSpec((grp_all, LANES), const2), pl.BlockSpec((grp_all, wb + LANES), const2),
                  pl.BlockSpec((LANES, SEL_KEYS), const2)],
        out_specs=pl.BlockSpec((None, T8, N_HEADS * HEAD_DIM), per_b),
        scratch_shapes=[pltpu.VMEM((grp_all, LANES), F32) for _ in range(3)])
    return pl.pallas_call(
        functools.partial(_nsa_sample_attn_body, n_steps=n_steps, t_new=t_new, past=past), grid_spec=grid_spec,
        out_shape=jax.ShapeDtypeStruct((bd, T8, N_HEADS * HEAD_DIM), BF16),
        compiler_params=_params("parallel", "arbitrary"), name="nsa_sample_attn")(
            page_table, q8, *([cache5] * SEL_PAGES), sel, new8, win3, wnew8, gates8, o_c,
            bias_last, bias_s["far"], bias_s["new"], bias_s["win"], expand)


SB_PAGES = 8
SB_GROUPS = 2 * N_HEADS // SUBLANES


def _sb_sample_run(q_ref, u, carry_ref, acc_ref, n_sub, k_of, v_of, masks):
    def q_of(h):
        return q_ref[h * T8:(h + 1) * T8, :].astype(BF16)

    zs = [jnp.concatenate([_dot_nt(q_of(h), k_of(j, h)) for h in range(N_HEADS)], axis=0)
          for j in range(n_sub)]
    pre = [_stick_pre(zs[j], masks[j]) for j in range(n_sub)]
    l2s = [_dot(pre[j][2], u) for j in range(n_sub)]
    carry = carry_ref[...]
    ws = [None] * n_sub
    for j in reversed(range(n_sub)):
        ws[j] = _stick_post(pre[j][0], l2s[j], carry, masks[j])
        carry = carry + pre[j][1]
    carry_ref[...] = carry
    for h in range(N_HEADS):
        r = slice(h * T8, (h + 1) * T8)
        out = _dot(ws[0][r].astype(BF16), v_of(0, h))
        for j in range(1, n_sub):
            out = out + _dot(ws[j][r].astype(BF16), v_of(j, h))
        acc_ref[r, :] += out


def _sb_page_rows(refs, rows_per_block):
    blocks = [r.reshape(PAGE_SIZE * rows_per_block, HEAD_DIM) for r in refs]
    per_page = 2 * N_HEADS // rows_per_block

    def head_rows(j, row):
        blk = blocks[j * per_page + row // rows_per_block]
        return blk[pl.ds(row % rows_per_block, PAGE_SIZE, stride=rows_per_block), :].astype(BF16)
    return head_rows


def _sb_sample_head_body(pt_ref, q_ref, *refs, t_new):
    n_grp = SB_PAGES * SB_GROUPS
    head_rows = _sb_page_rows(refs[:n_grp], SUBLANES)
    new_ref, u_ref, acc_ref, carry_ref = refs[n_grp:]
    hw = N_HEADS * HEAD_DIM
    carry_ref[...] = jnp.zeros_like(carry_ref)
    acc_ref[...] = jnp.zeros_like(acc_ref)
    pad = jnp.zeros((TQ - T8, 2 * hw), F32)
    kvn = jnp.concatenate([new_ref[...], pad], axis=0)
    lane = lax.broadcasted_iota(I32, (TQ, TQ), 1)
    t_row = lax.broadcasted_iota(I32, (TQ, TQ), 0) & (T8 - 1)
    _sb_sample_run(q_ref, u_ref[...], carry_ref, acc_ref, 1,
                   lambda j, h: kvn[:, h * HEAD_DIM:(h + 1) * HEAD_DIM].astype(BF16),
                   lambda j, h: kvn[:, hw + h * HEAD_DIM:hw + (h + 1) * HEAD_DIM].astype(BF16),
                   [(lane < t_row) & (lane < t_new)])
    _sb_sample_run(q_ref, u_ref[...], carry_ref, acc_ref, SB_PAGES, lambda j, h: head_rows(j, h),
                   lambda j, h: head_rows(j, N_HEADS + h), [None] * SB_PAGES)


def _sb_sample_tail_body(pt_ref, need_ref, q_ref, *refs, n_steps):
    head_rows = _sb_page_rows(refs[:SB_PAGES], 2 * N_HEADS)
    u_ref, acc_in, carry_in, o_ref, carry_ref, acc_ref = refs[SB_PAGES:]
    step = pl.program_id(1)

    @pl.when(step == 0)
    def _():
        carry_ref[...] = carry_in[...]
        acc_ref[...] = acc_in[...]

    @pl.when((need_ref[pl.program_id(0)] > 0) & (jnp.min(carry_ref[...]) < SB_DONE))
    def _():
        _sb_sample_run(q_ref, u_ref[...], carry_ref, acc_ref, SB_PAGES, lambda j, h: head_rows(j, h),
                       lambda j, h: head_rows(j, N_HEADS + h), [None] * SB_PAGES)

    @pl.when(step == n_steps - 1)
    def _():
        o_ref[...] = acc_ref[...].astype(o_ref.dtype)


def sb_sample_attention(q_rows, cache5, page_table, new8, t_new):
    bd, n_pages = page_table.shape
    hw = N_HEADS * HEAD_DIM
    n_steps = n_pages // SB_PAGES - 1
    assert n_pages % SB_PAGES == 0 and n_steps >= 1 and cache5.shape[2] == SB_GROUPS
    rows = N_HEADS * T8
    grp_block = (None, PAGE_SIZE, None, SUBLANES, HEAD_DIM)
    n_grp = SB_PAGES * SB_GROUPS
    u = _suffix_matrix()

    def head_spec(i, g):
        return pl.BlockSpec(grp_block, lambda b, pt: (pt[b, n_pages - SB_PAGES + i], 0, g, 0, 0))

    acc, carry = pl.pallas_call(
        functools.partial(_sb_sample_head_body, t_new=t_new),
        grid_spec=pltpu.PrefetchScalarGridSpec(
            num_scalar_prefetch=1, grid=(bd,),
            in_specs=[pl.BlockSpec((None, rows, HEAD_DIM), lambda b, pt: (b, 0, 0))]
            + [head_spec(i, g) for i in range(SB_PAGES) for g in range(SB_GROUPS)]
            + [pl.BlockSpec((None, T8, 2 * hw), lambda b, pt: (b, 0, 0)),
               pl.BlockSpec((TQ, TQ), lambda b, pt: (0, 0))],
            out_specs=[pl.BlockSpec((None, rows, HEAD_DIM), lambda b, pt: (b, 0, 0)),
                       pl.BlockSpec((None, rows, LANES), lambda b, pt: (b, 0, 0))]),
        out_shape=[jax.ShapeDtypeStruct((bd, rows, HEAD_DIM), F32), jax.ShapeDtypeStruct((bd, rows, LANES), F32)],
        compiler_params=_params("parallel"), name="sb_sample_head")(
            page_table, q_rows, *([cache5] * n_grp), new8, u)

    need = (jnp.min(carry, axis=(1, 2)) < SB_DONE).astype(I32)

    def tail_spec(i):
        def index(b, s, pt, nd):
            page = jnp.where(nd[b] > 0, n_pages - (s + 2) * SB_PAGES + i, n_pages - SB_PAGES + i)
            return (pt[b, page], 0, 0, 0, 0)
        return pl.BlockSpec((None, PAGE_SIZE, SB_GROUPS, SUBLANES, HEAD_DIM), index)

    per_b = lambda b, s, pt, nd: (b, 0, 0)
    return pl.pallas_call(
        functools.partial(_sb_sample_tail_body, n_steps=n_steps),
        grid_spec=pltpu.PrefetchScalarGridSpec(
            num_scalar_prefetch=2, grid=(bd, n_steps),
            in_specs=[pl.BlockSpec((None, rows, HEAD_DIM), per_b)]
            + [tail_spec(i) for i in range(SB_PAGES)]
            + [pl.BlockSpec((TQ, TQ), lambda b, s, pt, nd: (0, 0)),
               pl.BlockSpec((None, rows, HEAD_DIM), per_b), pl.BlockSpec((None, rows, LANES), per_b)],
            out_specs=pl.BlockSpec((None, rows, HEAD_DIM), per_b),
            scratch_shapes=[pltpu.VMEM((rows, LANES), F32), pltpu.VMEM((rows, HEAD_DIM), F32)]),
        out_shape=jax.ShapeDtypeStruct((bd, rows, HEAD_DIM), BF16),
        compiler_params=_params("parallel", "arbitrary"), name="sb_sample_tail")(
            page_table, need, q_rows, *([cache5] * SB_PAGES), u, acc, carry)


def _pad_tokens(x, bd, t_new):
    x = x.reshape(bd, t_new, x.shape[-1])
    return jnp.pad(x, ((0, 0), (0, T8 - t_new), (0, 0)))


def _bias_distances(t, past, n_chunks_s, wb):
    r = np.arange(TQ)[:, None]
    c = np.arange(LANES)[None, :]
    win_tiles = np.concatenate([r - c + off * TQ for off in range(WIN_TILES)], axis=0)
    toeplitz = np.concatenate([r - c + off * TQ for off in range(3)], axis=0)
    cmp_p = np.arange(t)[:, None] - (c * CMP_STRIDE + CMP_BLOCK - 1)
    pos_q = past + np.arange(T8)[:, None]
    cmp_s = (pos_q - (np.arange(n_chunks_s)[None, :] * CMP_STRIDE + CMP_BLOCK - 1)).reshape(-1, LANES)
    last = pos_q - (past - PAGE_SIZE + c)
    far = np.full((T8, LANES), 4 * REL_MAX_DIST)
    new = np.arange(T8)[:, None] - c
    win = (wb + np.arange(T8)[:, None] - np.arange(wb + LANES)[None, :]).reshape(-1, LANES)
    parts = [win_tiles, toeplitz, cmp_p, cmp_s, last, far, new, win]
    rows = sum(p.shape[0] for p in parts)
    pad = -rows % LANES
    parts.append(np.zeros((pad, LANES), np.int64))
    offs = np.cumsum([0] + [p.shape[0] for p in parts])
    dist = np.concatenate(parts, axis=0)
    addmask = np.zeros(dist.shape, np.float32)
    addmask[:win_tiles.shape[0]] = np.where((win_tiles >= 0) & (win_tiles < WINDOW), 0.0, NEG_INF)
    return jnp.asarray(dist, I32), jnp.asarray(addmask), offs


def _sample_bias(bias_all, offs, n_chunks_s, wb):
    def rows_kgt(x, width):
        return x.reshape(N_HEADS * T8, width)
    seg = lambda i: bias_all[:, offs[i]:offs[i + 1]]
    return {"cmp": rows_kgt(seg(3), n_chunks_s), "last": rows_kgt(seg(4), LANES), "far": rows_kgt(seg(5), LANES),
            "new": rows_kgt(seg(6), LANES), "win": rows_kgt(seg(7), wb + LANES)}


def kernel(x_prompt, x_sample, cache_nsa, state_nsa_win, cache_sb, page_table, rel_bias, norm_w, final_norm_w,
           nsa_w_in, nsa_w_out, nsa_pe_k, nsa_pe_v, nsa_phi_k1, nsa_phi_k2, nsa_phi_v1, nsa_phi_v2,
           sb_w_in, sb_w_out, ffn_w1, ffn_w3, ffn_w2):
    b, t, d = x_prompt.shape
    bd, t_new, _ = x_sample.shape
    n_pages = page_table.shape[1]
    past = n_pages * PAGE_SIZE
    wb = state_nsa_win.shape[2]
    hd = N_HEADS * HEAD_DIM
    kvw = KV * HEAD_DIM
    n_chunks_s = (past + t_new) // CMP_STRIDE
    half = CMP_STRIDE * HEAD_DIM

    dist, addmask, offs = _bias_distances(t, past, n_chunks_s, wb)
    bias_all = bias_from_dist(rel_bias, dist, addmask)
    bias_s = _sample_bias(bias_all, offs, n_chunks_s, wb)

    w_in = nsa_w_in[0]
    wq = (w_in[:, :hd] * SCALE).astype(BF16)
    wkv4 = w_in[:, hd:hd + 4 * kvw].astype(BF16)
    wwin = w_in[:, hd + 4 * kvw:hd + 6 * kvw].astype(BF16)
    wg = w_in[:, hd + 6 * kvw:].reshape(d, KV, G * 3)
    wg = jnp.pad(wg, ((0, 0), (0, 0), (0, LANES - G * 3))).reshape(d, KV * LANES).astype(BF16)
    w_out_nsa = nsa_w_out[0].astype(BF16)
    cw = {
        "w1k": jnp.concatenate([nsa_phi_k1[0, :half], nsa_phi_k1[0, half:]], axis=1).astype(BF16),
        "w1v": jnp.concatenate([nsa_phi_v1[0, :half], nsa_phi_v1[0, half:]], axis=1).astype(BF16),
        "pe": jnp.broadcast_to(jnp.stack([nsa_pe_k[0].reshape(1, -1), nsa_pe_v[0].reshape(1, -1)]),
                               (2, SUBLANES, CMP_BLOCK * HEAD_DIM)),
        "w1": jnp.stack([nsa_phi_k1[0], nsa_phi_v1[0]]).astype(BF16),
        "w2": jnp.stack([nsa_phi_k2[0], nsa_phi_v2[0]]).astype(BF16),
    }
    wsb = sb_w_in[0]
    wsq = (wsb[:, :hd] * SCALE).astype(BF16)
    wskv = wsb[:, hd:].astype(BF16)
    w_out_sb = sb_w_out[0].astype(BF16)
    w1 = ffn_w1.astype(BF16)
    w3 = ffn_w3.astype(BF16)
    w2 = ffn_w2.astype(BF16)

    hp = x_prompt.reshape(b * t, d)
    hs = x_sample.reshape(bd * t_new, d)

    xp = rmsnorm_rows(hp, norm_w[0, 0], BF16)
    xs = rmsnorm_rows(hs, norm_w[0, 0], BF16)
    q_p, q_s = matmul(xp, wq, (BF16,)), matmul(xs, wq, (BF16,))
    (kv4_p, kv4b_p), kv4_s = matmul(xp, wkv4, (F32, BF16)), matmul(xs, wkv4, (F32,))
    (win_p, winb_p), win_s = matmul(xp, wwin, (F32, BF16)), matmul(xs, wwin, (F32,))
    g_p, g_s = matmul(xp, wg, (F32,), sigmoid=True), matmul(xs, wg, (F32,), sigmoid=True)

    pt_prompt = jnp.arange(b * (t // PAGE_SIZE), dtype=I32).reshape(b, t // PAGE_SIZE)
    kcb_p = compress(kv4_p.reshape(b * t // PAGE_SIZE, PAGE_SIZE, 4 * kvw), pt_prompt, cw)
    attn_p = nsa_prompt_attention(q_p, kv4b_p, winb_p, g_p, kcb_p, bias_all, b, t)

    cache5 = cache_nsa[0].reshape(cache_nsa.shape[1], PAGE_SIZE, 2, 2 * KV, HEAD_DIM)
    kcb_s = compress_paged(cache5, page_table, cw)
    attn_s = nsa_sample_attention(
        _pad_tokens(q_s, bd, t_new), cache5, page_table, kcb_s, _pad_tokens(kv4_s, bd, t_new),
        state_nsa_win[0].reshape(bd, wb * 2 * KV, HEAD_DIM), _pad_tokens(win_s, bd, t_new),
        _pad_tokens(g_s, bd, t_new), bias_s, past, t_new)
    attn_s = attn_s[:, :t_new].reshape(bd * t_new, hd)

    hp, xp = matmul_res_norm(attn_p, w_out_nsa, hp, norm_w[0, 1])
    hs, xs = matmul_res_norm(attn_s, w_out_nsa, hs, norm_w[0, 1])
    hp, xp = ffn_res_norm(xp, w1[0], w3[0], w2[0], hp, norm_w[1, 0], BF16)
    hs, xs = ffn_res_norm(xs, w1[0], w3[0], w2[0], hs, norm_w[1, 0], BF16)

    sq_p, sq_s = matmul(xp, wsq, (BF16,)), matmul(xs, wsq, (BF16,))
    (skv_p, skvb_p), skv_s = matmul(xp, wskv, (F32, BF16)), matmul(xs, wskv, (F32,))
    sattn_p = sb_prompt_attention(sq_p, skvb_p, b, t)

    q_rows = (_pad_tokens(sq_s, bd, t_new).reshape(bd, T8, N_HEADS, HEAD_DIM).transpose(0, 2, 1, 3)
              .reshape(bd, N_HEADS * T8, HEAD_DIM).astype(F32))
    csb5 = cache_sb[0].reshape(cache_sb.shape[1], PAGE_SIZE, SB_GROUPS, SUBLANES, HEAD_DIM)
    sattn_s = sb_sample_attention(q_rows, csb5, page_table, _pad_tokens(skv_s, bd, t_new), t_new)
    sattn_s = sattn_s.reshape(bd, N_HEADS, T8, HEAD_DIM)[:, :, :t_new].transpose(0, 2, 1, 3).reshape(bd * t_new, hd)

    hp, xp = matmul_res_norm(sattn_p, w_out_sb, hp, norm_w[1, 1])
    hs, xs = matmul_res_norm(sattn_s, w_out_sb, hs, norm_w[1, 1])
    y_p = ffn_res_norm(xp, w1[1], w3[1], w2[1], hp, final_norm_w, F32, emit_h=False)
    y_s = ffn_res_norm(xs, w1[1], w3[1], w2[1], hs, final_norm_w, F32, emit_h=False)

    win_keep = min(WINDOW, t)
    nsa_win_prompt = win_p.reshape(b, t, 2, KV, HEAD_DIM)[:, t - win_keep:]
    win_all = jnp.concatenate([state_nsa_win[0], win_s.reshape(bd, t_new, 2, KV, HEAD_DIM)], axis=1)
    return (y_p.reshape(b, t, d), y_s.reshape(bd, t_new, d),
            kv4_p.reshape(1, b, t, 4, KV, HEAD_DIM), kv4_s.reshape(1, bd, t_new, 4, KV, HEAD_DIM),
            nsa_win_prompt[None], win_all[:, t_new:][None],
            skv_p.reshape(1, b, t, 2, N_HEADS, HEAD_DIM), skv_s.reshape(1, bd, t_new, 2, N_HEADS, HEAD_DIM))
```

```python
import functools
import math

import numpy as np
import jax
import jax.numpy as jnp
from jax import lax
from jax.experimental import pallas as pl
from jax.experimental.pallas import tpu as pltpu

F32 = jnp.float32
BF16 = jnp.bfloat16
I32 = jnp.int32

HEAD_DIM = 128
N_HEADS = 16
NSA_KV_HEADS = 4
NSA_GROUP = N_HEADS // NSA_KV_HEADS
CMP_BLOCK = 32
CMP_STRIDE = 16
SEL_BLOCK = 64
SEL_TOPK = 16
WINDOW = 512
FORCE_BONUS = 1000.0
REL_BUCKETS = 32
REL_MAX_DIST = 128
RMS_EPS = 1e-6
NEG_INF = -1e30
MASK_NEG = 2 * NEG_INF
SCALE = HEAD_DIM ** -0.5
PAGE_SIZE = 128

LANES = 128
SUBLANES = 8
TQ = 128
WIN_TILES = WINDOW // TQ + 2
T8 = SUBLANES
VMEM_LIMIT = 56 * 1024 * 1024
KV = NSA_KV_HEADS
G = NSA_GROUP


def _t5_thresholds():
    d = np.arange(0, 4 * REL_MAX_DIST)
    max_exact = REL_BUCKETS // 2
    nf = np.maximum(d, 1).astype(np.float32)
    large = max_exact + (np.log(nf / np.float32(max_exact)) / np.float32(math.log(REL_MAX_DIST / max_exact))
                         * np.float32(REL_BUCKETS - max_exact)).astype(np.int32)
    b = np.where(d < max_exact, d, np.minimum(large, REL_BUCKETS - 1))
    return [int(np.argmax(b >= k)) for k in range(REL_BUCKETS)]


T5_THR = _t5_thresholds()
FAR_DIST = T5_THR[-1]


def _dot(a, b):
    return jnp.dot(a, b, preferred_element_type=F32)


def _dot_nt(a, b):
    return lax.dot_general(a, b, (((1,), (1,)), ((), ())), preferred_element_type=F32)


def _params(*sem):
    return pltpu.CompilerParams(dimension_semantics=sem, vmem_limit_bytes=VMEM_LIMIT)


def _rms(x, w):
    return x * lax.rsqrt(jnp.mean(x * x, axis=-1, keepdims=True) + RMS_EPS) * w


def _masked_softmax(s, mask):
    sm = jnp.where(mask, s, NEG_INF)
    m = jnp.max(sm, axis=1, keepdims=True)
    e = jnp.where(mask, jnp.exp(sm - m), 0.0)
    l = jnp.sum(e, axis=1, keepdims=True)
    return e * jnp.where(l > 0.0, 1.0 / l, 0.0)


def _safe_inv(l):
    return jnp.where(l > 0.0, 1.0 / l, 0.0)


def _rmsnorm_body(x_ref, w_ref, o_ref):
    o_ref[...] = _rms(x_ref[...], w_ref[...]).astype(o_ref.dtype)


def rmsnorm_rows(x, w, out_dtype):
    m, d = x.shape
    tm = min(m, 512)
    return pl.pallas_call(
        _rmsnorm_body, grid=(m // tm,),
        in_specs=[pl.BlockSpec((tm, d), lambda i: (i, 0)), pl.BlockSpec((1, d), lambda i: (0, 0))],
        out_specs=pl.BlockSpec((tm, d), lambda i: (i, 0)),
        out_shape=jax.ShapeDtypeStruct((m, d), out_dtype),
        compiler_params=_params("parallel"), name="rmsnorm")(x, w.reshape(1, d))


def _matmul_body(a_ref, w_ref, *o_refs, sigmoid):
    y = _dot(a_ref[...], w_ref[...])
    if sigmoid:
        y = jax.nn.sigmoid(y)
    for o_ref in o_refs:
        o_ref[...] = y.astype(o_ref.dtype)


def matmul(a, w, out_dtypes, sigmoid=False):
    m, k = a.shape
    n = w.shape[1]
    tm = min(m, 512)
    tn = min(n, 2048)
    outs = pl.pallas_call(
        functools.partial(_matmul_body, sigmoid=sigmoid), grid=(m // tm, n // tn),
        in_specs=[pl.BlockSpec((tm, k), lambda i, j: (i, 0)), pl.BlockSpec((k, tn), lambda i, j: (0, j))],
        out_specs=[pl.BlockSpec((tm, tn), lambda i, j: (i, j)) for _ in out_dtypes],
        out_shape=[jax.ShapeDtypeStruct((m, n), dt) for dt in out_dtypes],
        compiler_params=_params("parallel", "arbitrary"), name="matmul")(a, w)
    return outs if len(outs) > 1 else outs[0]


def _mm_res_norm_body(a_ref, w_ref, h_ref, nw_ref, hout_ref, xn_ref):
    h = h_ref[...] + _dot(a_ref[...], w_ref[...])
    hout_ref[...] = h
    xn_ref[...] = _rms(h, nw_ref[...]).astype(xn_ref.dtype)


def matmul_res_norm(a, w, h, nw):
    m, k = a.shape
    d = w.shape[1]
    tm = min(m, 512)
    return pl.pallas_call(
        _mm_res_norm_body, grid=(m // tm,),
        in_specs=[pl.BlockSpec((tm, k), lambda i: (i, 0)), pl.BlockSpec((k, d), lambda i: (0, 0)),
                  pl.BlockSpec((tm, d), lambda i: (i, 0)), pl.BlockSpec((1, d), lambda i: (0, 0))],
        out_specs=[pl.BlockSpec((tm, d), lambda i: (i, 0)), pl.BlockSpec((tm, d), lambda i: (i, 0))],
        out_shape=[jax.ShapeDtypeStruct((m, d), F32), jax.ShapeDtypeStruct((m, d), BF16)],
        compiler_params=_params("parallel"), name="matmul_res_norm")(a, w, h, nw.reshape(1, d))


def _ffn_body(xn_ref, w1_ref, w3_ref, w2_ref, h_ref, nw_ref, *out_refs):
    hout_ref = out_refs[0] if len(out_refs) == 3 else None
    yn_ref, acc_ref = out_refs[-2:]
    f = pl.program_id(1)

    @pl.when(f == 0)
    def _():
        acc_ref[...] = jnp.zeros_like(acc_ref)

    x = xn_ref[...]
    g = _dot(x, w1_ref[...])
    u = _dot(x, w3_ref[...])
    mid = (g * jax.nn.sigmoid(g) * u).astype(BF16)
    acc_ref[...] += _dot(mid, w2_ref[...])

    @pl.when(f == pl.num_programs(1) - 1)
    def _():
        h = h_ref[...] + acc_ref[...]
        if hout_ref is not None:
            hout_ref[...] = h
        yn_ref[...] = _rms(h, nw_ref[...]).astype(yn_ref.dtype)


def ffn_res_norm(xn, w1, w3, w2, h, nw, norm_dtype, emit_h=True):
    m, d = xn.shape
    dff = w1.shape[1]
    tm = min(m, 512)
    tf = 512
    n_out = 2 if emit_h else 1
    outs = pl.pallas_call(
        _ffn_body, grid=(m // tm, dff // tf),
        in_specs=[pl.BlockSpec((tm, d), lambda i, f: (i, 0)),
                  pl.BlockSpec((d, tf), lambda i, f: (0, f)), pl.BlockSpec((d, tf), lambda i, f: (0, f)),
                  pl.BlockSpec((tf, d), lambda i, f: (f, 0)),
                  pl.BlockSpec((tm, d), lambda i, f: (i, 0)), pl.BlockSpec((1, d), lambda i, f: (0, 0))],
        out_specs=[pl.BlockSpec((tm, d), lambda i, f: (i, 0)) for _ in range(n_out)],
        out_shape=[jax.ShapeDtypeStruct((m, d), F32), jax.ShapeDtypeStruct((m, d), norm_dtype)][-n_out:],
        scratch_shapes=[pltpu.VMEM((tm, d), F32)],
        compiler_params=_params("parallel", "arbitrary"), name="ffn")(xn, w1, w3, w2, h, nw.reshape(1, d))
    return outs if emit_h else outs[0]


def _bias_body(tab_ref, d_ref, m_ref, o_ref):
    h = pl.program_id(0)
    d = d_ref[...]
    acc = jnp.full(d.shape, tab_ref[0, h], F32)
    for b in range(1, REL_BUCKETS):
        acc = jnp.where(d >= T5_THR[b], tab_ref[b, h], acc)
    o_ref[...] = acc + m_ref[...]


def bias_from_dist(rel_bias, dist, addmask):
    r = dist.shape[0]
    return pl.pallas_call(
        _bias_body, grid=(N_HEADS,),
        in_specs=[pl.BlockSpec(memory_space=pltpu.SMEM), pl.BlockSpec((r, LANES), lambda h: (0, 0)),
                  pl.BlockSpec((r, LANES), lambda h: (0, 0))],
        out_specs=pl.BlockSpec((None, r, LANES), lambda h: (h, 0, 0)),
        out_shape=jax.ShapeDtypeStruct((N_HEADS, r, LANES), F32),
        compiler_params=_params("parallel"), name="rel_bias")(rel_bias, dist, addmask)


CMP_PAGES = 16


def _cmp1_body(pt_ref, *refs):
    pages = refs[:CMP_PAGES]
    perm = refs[CMP_PAGES][...]
    w_refs = refs[CMP_PAGES + 1:CMP_PAGES + 3]
    o_ref = refs[CMP_PAGES + 3]
    cpp = PAGE_SIZE // CMP_STRIDE
    for kind in range(2):
        w = w_refs[kind][...]
        for kv in range(KV):
            c0 = (kind * KV + kv) * HEAD_DIM
            xs = [_dot(perm, pg[:, c0:c0 + HEAD_DIM].astype(BF16)) for pg in pages]
            pieces = [jnp.concatenate([x[r * cpp:(r + 1) * cpp] for x in xs], axis=0).astype(BF16)
                      for r in range(CMP_STRIDE)]
            o_ref[kind * KV + kv] = _dot(jnp.concatenate(pieces, axis=1), w)


def compress_stage1(rows3, page_table, w1k, w1v):
    nb_seq, n_pages = page_table.shape
    steps = n_pages // CMP_PAGES
    cpp = PAGE_SIZE // CMP_STRIDE
    width = 2 * KV * HEAD_DIM

    def page_spec(i):
        return pl.BlockSpec((None, PAGE_SIZE, width), lambda b, s, pt: (pt[b, s * CMP_PAGES + i], 0, 0))

    wspec = pl.BlockSpec((CMP_STRIDE * HEAD_DIM, 2 * HEAD_DIM), lambda b, s, pt: (0, 0))
    src = np.arange(PAGE_SIZE)
    perm = np.zeros((PAGE_SIZE, PAGE_SIZE), np.float32)
    perm[(src % CMP_STRIDE) * cpp + src // CMP_STRIDE, src] = 1.0
    perm = jnp.asarray(perm, BF16)
    grid_spec = pltpu.PrefetchScalarGridSpec(
        num_scalar_prefetch=1, grid=(nb_seq, steps),
        in_specs=[page_spec(i) for i in range(CMP_PAGES)]
        + [pl.BlockSpec((PAGE_SIZE, PAGE_SIZE), lambda b, s, pt: (0, 0)), wspec, wspec],
        out_specs=pl.BlockSpec((None, 2 * KV, CMP_PAGES * cpp, 2 * HEAD_DIM), lambda b, s, pt: (b, 0, s, 0)))
    return pl.pallas_call(
        _cmp1_body, grid_spec=grid_spec,
        out_shape=jax.ShapeDtypeStruct((nb_seq, 2 * KV, n_pages * cpp, 2 * HEAD_DIM), F32),
        compiler_params=_params("parallel", "arbitrary"), name="compress1")(
            page_table, *([rows3] * CMP_PAGES), perm, w1k, w1v)


def _cmp1_paged_body(pt_ref, *refs):
    cmp_pages = [r.reshape(PAGE_SIZE * SUBLANES, HEAD_DIM) for r in refs[:CMP_PAGES]]
    w_refs = refs[CMP_PAGES:CMP_PAGES + 2]
    o_ref = refs[CMP_PAGES + 2]
    cpp = PAGE_SIZE // CMP_STRIDE
    for kind in range(2):
        w = w_refs[kind][...]
        for kv in range(KV):
            c = kind * KV + kv
            xs = [jnp.swapaxes(pg[pl.ds(c, PAGE_SIZE, stride=SUBLANES), :].reshape(cpp, CMP_STRIDE, HEAD_DIM), 0, 1)
                  for pg in cmp_pages]
            pieces = [jnp.concatenate([x[r] for x in xs], axis=0).astype(BF16) for r in range(CMP_STRIDE)]
            o_ref[c] = _dot(jnp.concatenate(pieces, axis=1), w)


def compress_stage1_paged(cache5, page_table, w1k, w1v):
    nb_seq, n_pages = page_table.shape
    steps = n_pages // CMP_PAGES
    cpp = PAGE_SIZE // CMP_STRIDE
    assert cache5.shape[2:] == (2, 2 * KV, HEAD_DIM) and 2 * KV == SUBLANES

    def group_spec(i):
        return pl.BlockSpec((None, PAGE_SIZE, None, SUBLANES, HEAD_DIM),
                            lambda b, s, pt: (pt[b, s * CMP_PAGES + i], 0, 0, 0, 0))

    wspec = pl.BlockSpec((CMP_STRIDE * HEAD_DIM, 2 * HEAD_DIM), lambda b, s, pt: (0, 0))
    grid_spec = pltpu.PrefetchScalarGridSpec(
        num_scalar_prefetch=1, grid=(nb_seq, steps),
        in_specs=[group_spec(i) for i in range(CMP_PAGES)] + [wspec, wspec],
        out_specs=pl.BlockSpec((None, 2 * KV, CMP_PAGES * cpp, 2 * HEAD_DIM), lambda b, s, pt: (b, 0, s, 0)))
    return pl.pallas_call(
        _cmp1_paged_body, grid_spec=grid_spec,
        out_shape=jax.ShapeDtypeStruct((nb_seq, 2 * KV, n_pages * cpp, 2 * HEAD_DIM), F32),
        compiler_params=_params("parallel", "arbitrary"), name="compress1_paged")(
            page_table, *([cache5] * CMP_PAGES), w1k, w1v)


def _cmp2_body(ab_ref, pe_ref, w1_ref, w2_ref, o_ref):
    nch = ab_ref.shape[0]
    ab = ab_ref[...]
    c = _dot(pe_ref[...].astype(BF16), w1_ref[...].astype(BF16))
    pre = ab[:, :HEAD_DIM] + pltpu.roll(ab[:, HEAD_DIM:], nch - 1, 0) + c[0:1]
    y = _dot((pre * jax.nn.sigmoid(pre)).astype(BF16), w2_ref[...].astype(BF16))
    rowi = lax.broadcasted_iota(I32, y.shape, 0)
    o_ref[...] = jnp.where(rowi < nch - 1, y, 0.0)


def compress_stage2(ab, pe, w1, w2):
    nb_seq, _, nch, _ = ab.shape
    return pl.pallas_call(
        _cmp2_body, grid=(nb_seq, 2 * KV),
        in_specs=[pl.BlockSpec((None, None, nch, 2 * HEAD_DIM), lambda b, j: (b, j, 0, 0)),
                  pl.BlockSpec((None, SUBLANES, CMP_BLOCK * HEAD_DIM), lambda b, j: (j // KV, 0, 0)),
                  pl.BlockSpec((None, CMP_BLOCK * HEAD_DIM, HEAD_DIM), lambda b, j: (j // KV, 0, 0)),
                  pl.BlockSpec((None, HEAD_DIM, HEAD_DIM), lambda b, j: (j // KV, 0, 0))],
        out_specs=pl.BlockSpec((None, None, nch, HEAD_DIM), lambda b, j: (b, j, 0, 0)),
        out_shape=jax.ShapeDtypeStruct((nb_seq, 2 * KV, nch, HEAD_DIM), F32),
        compiler_params=_params("parallel", "parallel"), name="compress2")(ab, pe, w1, w2)


def compress(rows3, page_table, cw):
    ab = compress_stage1(rows3, page_table, cw["w1k"], cw["w1v"])
    return compress_stage2(ab, cw["pe"], cw["w1"], cw["w2"])


def compress_paged(cache3, page_table, cw):
    ab = compress_stage1_paged(cache3, page_table, cw["w1k"], cw["w1v"])
    return compress_stage2(ab, cw["pe"], cw["w1"], cw["w2"])


def _flash_prob(s, mask, m_ref, l_ref, rows):
    m_prev = m_ref[rows, :]
    if mask is not None:
        s = jnp.where(mask, s, MASK_NEG)
    m_new = jnp.maximum(m_prev, jnp.max(s, axis=1, keepdims=True))
    alpha = jnp.exp(m_prev - m_new)
    m_wide = m_new if s.shape[1] == LANES else jnp.concatenate([m_new] * (s.shape[1] // LANES), axis=1)
    p = jnp.exp(s - m_wide)
    l_ref[rows, :] = alpha * l_ref[rows, :] + jnp.sum(p, axis=1, keepdims=True)
    m_ref[rows, :] = m_new
    return p.astype(BF16), alpha


def _flash_update(s, mask, v, m_ref, l_ref, acc_ref, rows):
    p, alpha = _flash_prob(s, mask, m_ref, l_ref, rows)
    acc_ref[rows, :] = alpha * acc_ref[rows, :] + _dot(p, v)


def _flash_init(m_ref, l_ref, acc_ref):
    m_ref[...] = jnp.full(m_ref.shape, NEG_INF, F32)
    l_ref[...] = jnp.zeros_like(l_ref)
    acc_ref[...] = jnp.zeros_like(acc_ref)


def _rank_select(score, eligible, blk, n_blocks, n_top, row_of):
    def body(j, cnt):
        rowj = row_of(j)
        beats = (rowj > score) | ((rowj == score) & (blk > j))
        return cnt + beats.astype(I32)
    cnt = lax.fori_loop(0, n_blocks, body, jnp.zeros(score.shape, I32))
    return jnp.where((cnt < n_top) & eligible, 1.0, 0.0)


def _extract_top(score, blk, n_top):
    def body(i, state):
        sc, sel = state
        m = jnp.max(sc, axis=0, keepdims=True)
        first = jnp.min(jnp.where(sc == m, blk, jnp.iinfo(jnp.int32).max), axis=0, keepdims=True)
        hit = (blk == first) & (m > -jnp.inf)
        return jnp.where(hit, -jnp.inf, sc), jnp.where(hit, 1.0, sel)
    return lax.fori_loop(0, n_top, body, (score, jnp.zeros(score.shape, F32)))[1]


NSA_HP = 2


def _nsa_prompt_body(q_ref, ks_ref, vs_ref, kw_ref, vw_ref, kc_ref, vc_ref, g_ref, bc_ref, bt_ref, wbt_ref,
                      ovl_ref, exp_ref,
                      o_ref, m_s, l_s, a_s, m_w, l_w, a_w, mk_ref, sc_ref, *, n_cmp, n_blk):
    qi = pl.program_id(2)
    rows = G * TQ
    gw = G * HEAD_DIM
    row5 = lax.broadcasted_iota(I32, (rows, LANES), 0)
    lane5 = lax.broadcasted_iota(I32, (rows, LANES), 1)
    tpos5 = qi * TQ + (row5 & (TQ - 1))
    mask_c = (tpos5 >= lane5 * CMP_STRIDE + (CMP_BLOCK - 1)) & (lane5 < n_cmp)
    nbr = ovl_ref.shape[0]
    blk = lax.broadcasted_iota(I32, (nbr, TQ), 0)
    tq_l = qi * TQ + lax.broadcasted_iota(I32, (nbr, TQ), 1)
    cur = lax.shift_right_arithmetic(tq_l, int(math.log2(SEL_BLOCK)))
    eligible = (blk * SEL_BLOCK <= tq_l) & (blk < n_blk)
    forced = (blk == 0) | (blk == cur) | (blk == cur - 1)
    row1 = lax.broadcasted_iota(I32, (TQ, TQ), 0)
    lane1 = lax.broadcasted_iota(I32, (TQ, TQ), 1)
    off_d = pl.multiple_of(qi * TQ, TQ)

    qs, o_cs = [], []
    for hp in range(NSA_HP):
        q = jnp.concatenate([q_ref[:, hp * gw + g * HEAD_DIM:hp * gw + (g + 1) * HEAD_DIM] for g in range(G)],
                            axis=0)
        qs.append(q)
        s = _dot_nt(q, kc_ref[hp].astype(BF16))
        s = s + jnp.concatenate([bc_ref[hp * G + g] for g in range(G)], axis=0)
        pb = _masked_softmax(s, mask_c).astype(BF16)
        o_cs.append(_dot(pb, vc_ref[hp].astype(BF16)))
        imp = _dot_nt(ovl_ref[...], pb[0:TQ])
        for g in range(1, G):
            imp = imp + _dot_nt(ovl_ref[...], pb[g * TQ:(g + 1) * TQ])
        score = jnp.where(eligible, imp + jnp.where(forced, FORCE_BONUS, 0.0), -jnp.inf)
        sc_ref[hp] = score
        sel_t = _rank_select(score, eligible, blk, n_blk, min(SEL_TOPK, n_blk),
                             lambda j, hp=hp: sc_ref[hp, pl.ds(j, 1), :])
        unsel = jnp.concatenate([1.0 - sel_t, jnp.zeros((LANES - nbr, TQ), F32)], axis=0)
        mk_ref[hp] = _dot(unsel.T.astype(BF16), exp_ref[...])
        mk_ref[hp, :, pl.ds(off_d, TQ)] = jnp.where(lane1 <= row1, mk_ref[hp, :, pl.ds(off_d, TQ)], MASK_NEG)

    @pl.when(qi + 1 < pl.num_programs(2))
    def _():
        for hp in range(NSA_HP):
            mk_ref[hp, :, pl.ds(pl.multiple_of((qi + 1) * TQ, TQ), TQ)] = jnp.full((TQ, TQ), MASK_NEG, F32)

    kt2 = 2 * TQ

    def tile_pair(ref, hp, j, index_of):
        tiles = []
        for i in range(2):
            off = pl.multiple_of(index_of(qi - (2 * j + i)) * TQ, TQ)
            tiles.append(jnp.concatenate([ref[hp * G + g, pl.ds(off, TQ), :] for g in range(G)], axis=0))
        return jnp.concatenate(tiles, axis=1)

    _flash_init(m_s, l_s, a_s)
    _flash_init(m_w, l_w, a_w)
    head_rows = [pl.ds(hp * rows, rows) for hp in range(NSA_HP)]
    head_cols = [slice(hp * HEAD_DIM, (hp + 1) * HEAD_DIM) for hp in range(NSA_HP)]

    def pair_step(j, with_win):
        off = pl.multiple_of(j * kt2, kt2)
        s_s = [_dot_nt(qs[hp], ks_ref[pl.ds(off, kt2), head_cols[hp]]) for hp in range(NSA_HP)]
        if with_win:
            s_w = [_dot_nt(qs[hp], kw_ref[pl.ds(off, kt2), head_cols[hp]]) for hp in range(NSA_HP)]
        sel, win = [], []
        for hp in range(NSA_HP):
            bias_s = tile_pair(bt_ref, hp, j, lambda d: jnp.clip(d, 0, 2))
            key_mask = jnp.concatenate([mk_ref[hp, :, pl.ds(off, kt2)]] * G, axis=0)
            sel.append(_flash_prob(s_s[hp] + bias_s + key_mask, None, m_s, l_s, head_rows[hp]))
            if with_win:
                bias_w = tile_pair(wbt_ref, hp, j,
                                   lambda d: jnp.where(d < 0, WIN_TILES - 1, jnp.minimum(d, WIN_TILES - 1)))
                win.append(_flash_prob(s_w[hp] + bias_w, None, m_w, l_w, head_rows[hp]))
        for hp in range(NSA_HP):
            p, alpha = sel[hp]
            a_s[head_rows[hp], :] = alpha * a_s[head_rows[hp], :] + _dot(p, vs_ref[pl.ds(off, kt2), head_cols[hp]])
            if with_win:
                p, alpha = win[hp]
                a_w[head_rows[hp], :] = (alpha * a_w[head_rows[hp], :]
                                         + _dot(p, vw_ref[pl.ds(off, kt2), head_cols[hp]]))

    def sel_only(j, c):
        pair_step(j, False)
        return c

    def sel_and_win(j, c):
        pair_step(j, True)
        return c

    lo = lax.shift_right_arithmetic(jnp.maximum(qi - WINDOW // TQ, 0), 1)
    hi = lax.shift_right_arithmetic(qi, 1) + 1
    lax.fori_loop(0, lo, sel_only, 0)
    lax.fori_loop(lo, hi, sel_and_win, 0)

    for hp in range(NSA_HP):
        o_s = a_s[head_rows[hp], :] * _safe_inv(l_s[head_rows[hp], :])
        o_w = a_w[head_rows[hp], :] * _safe_inv(l_w[head_rows[hp], :])
        gt = g_ref[:, hp * LANES:(hp + 1) * LANES]
        for g in range(G):
            r = slice(g * TQ, (g + 1) * TQ)
            out = (gt[:, 3 * g:3 * g + 1] * o_cs[hp][r] + gt[:, 3 * g + 1:3 * g + 2] * o_s[r]
                   + gt[:, 3 * g + 2:3 * g + 3] * o_w[r])
            o_ref[:, hp * gw + g * HEAD_DIM:hp * gw + (g + 1) * HEAD_DIM] = out.astype(o_ref.dtype)


def nsa_prompt_attention(q, kv4, kwin, gates, kcb, bias_all, b, t):
    nq = t // TQ
    n_chunks = t // CMP_STRIDE
    n_cmp = n_chunks - CMP_BLOCK // CMP_STRIDE + 1
    n_blk = -(-t // SEL_BLOCK)
    assert n_chunks == LANES and n_blk <= LANES and t % TQ == 0 and TQ >= FAR_DIST
    assert WIN_TILES % 3 == 0 and TQ == LANES and KV % NSA_HP == 0
    cmp_start = np.arange(LANES) * CMP_STRIDE
    jst = np.arange(LANES) * SEL_BLOCK
    overlap = ((cmp_start[:, None] < jst[None, :] + SEL_BLOCK) & (cmp_start[:, None] + CMP_BLOCK > jst[None, :])
               & (np.arange(LANES)[:, None] < n_cmp) & (np.arange(LANES)[None, :] < n_blk))
    nbr = -(-n_blk // (2 * SUBLANES)) * 2 * SUBLANES
    ovl_t = jnp.asarray(overlap.T[:nbr], BF16)
    expand = jnp.asarray(np.where(np.arange(LANES)[:, None] == (np.arange(t)[None, :] // SEL_BLOCK), MASK_NEG, 0.0),
                         BF16)
    hw = NSA_HP * HEAD_DIM
    hg = NSA_HP * G
    colblk = lambda base: (lambda bb, kp, qi: (bb, base // NSA_HP + kp))
    rowblk = lambda bb, kp, qi: (bb * nq + qi, kp)
    in_specs = [
        pl.BlockSpec((TQ, hg * HEAD_DIM), rowblk),
        pl.BlockSpec((t, hw), colblk(2 * KV)), pl.BlockSpec((t, hw), colblk(3 * KV)),
        pl.BlockSpec((t, hw), colblk(0)), pl.BlockSpec((t, hw), colblk(KV)),
        pl.BlockSpec((None, NSA_HP, n_chunks, HEAD_DIM), lambda bb, kp, qi: (bb, kp, 0, 0)),
        pl.BlockSpec((None, NSA_HP, n_chunks, HEAD_DIM), lambda bb, kp, qi: (bb, KV // NSA_HP + kp, 0, 0)),
        pl.BlockSpec((TQ, NSA_HP * LANES), rowblk),
        pl.BlockSpec((hg, TQ, LANES), lambda bb, kp, qi: (kp, WIN_TILES + 3 + qi, 0)),
        pl.BlockSpec((hg, 3 * TQ, LANES), lambda bb, kp, qi: (kp, WIN_TILES // 3, 0)),
        pl.BlockSpec((hg, WIN_TILES * TQ, LANES), lambda bb, kp, qi: (kp, 0, 0)),
        pl.BlockSpec((nbr, LANES), lambda bb, kp, qi: (0, 0)),
        pl.BlockSpec((LANES, t), lambda bb, kp, qi: (0, 0)),
    ]
    rows = NSA_HP * G * TQ
    scratch = [pltpu.VMEM((rows, LANES), F32) for _ in range(6)] + [pltpu.VMEM((NSA_HP, TQ, t), F32),
                                                                   pltpu.VMEM((NSA_HP, nbr, TQ), F32)]
    return pl.pallas_call(
        functools.partial(_nsa_prompt_body, n_cmp=n_cmp, n_blk=n_blk), grid=(b, KV // NSA_HP, nq),
        in_specs=in_specs, out_specs=pl.BlockSpec((TQ, hg * HEAD_DIM), rowblk),
        out_shape=jax.ShapeDtypeStruct((b * t, N_HEADS * HEAD_DIM), BF16),
        scratch_shapes=scratch,
        compiler_params=_params("parallel", "parallel", "arbitrary"), name="nsa_prompt_attn")(
            q, kv4, kv4, kwin, kwin, kcb, kcb, gates, bias_all, bias_all, bias_all, ovl_t, expand)


def _stick_pre(z, mask):
    sp = jnp.log(1.0 + jnp.exp(-jnp.abs(z)))
    log_beta = jnp.minimum(z, 0.0) - sp
    rest = jnp.maximum(z, 0.0) + sp
    if mask is not None:
        rest = jnp.where(mask, rest, 0.0)
    hi = rest.astype(BF16)
    lo = (rest - hi.astype(F32)).astype(BF16)
    return log_beta, jnp.sum(rest, axis=1, keepdims=True), jnp.concatenate([hi, lo], axis=0)


def _stick_post(log_beta, l2, carry, mask):
    n = log_beta.shape[0]
    a = jnp.exp(log_beta - (l2[0:n] + l2[n:2 * n] + carry))
    return a if mask is None else jnp.where(mask, a, 0.0)


SB_HEADS = 8
SB_DONE = 110.0


def _sb_prompt_body(q_ref, k_ref, v_ref, u_ref, o_ref, carry_ref, acc_ref):
    qi = pl.program_id(2)
    row = lax.broadcasted_iota(I32, (TQ, TQ), 0)
    lane = lax.broadcasted_iota(I32, (TQ, TQ), 1)

    def tile(kt, mask, first):
        off = pl.multiple_of(kt * TQ, TQ)
        cols = [slice(h * HEAD_DIM, (h + 1) * HEAD_DIM) for h in range(SB_HEADS)]
        zs = [_dot_nt(q_ref[:, c], k_ref[pl.ds(off, TQ), c]) for c in cols]
        pre = [_stick_pre(z, mask) for z in zs]
        l3s = [_dot(p[2], u_ref[...]) for p in pre]
        ws = [_stick_post(p[0], l3, 0.0 if first else carry_ref[h], mask).astype(BF16)
              for h, (p, l3) in enumerate(zip(pre, l3s))]
        pvs = [_dot(w, v_ref[pl.ds(off, TQ), c]) for w, c in zip(ws, cols)]
        for h in range(SB_HEADS):
            tot = pre[h][1]
            acc_ref[h] = pvs[h] if first else acc_ref[h] + pvs[h]
            carry_ref[h] = jnp.broadcast_to(tot, (TQ, TQ)) if first else carry_ref[h] + tot

    tile(qi, lane < row, True)

    def unfinished():
        c = carry_ref[0]
        for h in range(1, SB_HEADS):
            c = jnp.minimum(c, carry_ref[h])
        return (jnp.min(c) < SB_DONE).astype(I32)

    def step(state):
        i, _ = state
        tile(qi - 1 - i, None, False)
        return i + 1, unfinished()

    lax.while_loop(lambda state: (state[0] < qi) & (state[1] > 0), step, (jnp.int32(0), unfinished()))
    for h in range(SB_HEADS):
        o_ref[:, h * HEAD_DIM:(h + 1) * HEAD_DIM] = acc_ref[h].astype(o_ref.dtype)


def _suffix_matrix():
    j = np.arange(TQ)
    return jnp.asarray(j[:, None] > j[None, :], BF16)


def sb_prompt_attention(q, kvp, b, t):
    nq = t // TQ
    hg = N_HEADS // SB_HEADS
    w = SB_HEADS * HEAD_DIM
    rowblk = lambda bb, h, qi: (bb * nq + qi, h)
    return pl.pallas_call(
        _sb_prompt_body, grid=(b, hg, nq),
        in_specs=[pl.BlockSpec((TQ, w), rowblk),
                  pl.BlockSpec((t, w), lambda bb, h, qi: (bb, h)),
                  pl.BlockSpec((t, w), lambda bb, h, qi: (bb, hg + h)),
                  pl.BlockSpec((TQ, TQ), lambda bb, h, qi: (0, 0))],
        out_specs=pl.BlockSpec((TQ, w), rowblk),
        out_shape=jax.ShapeDtypeStruct((b * t, N_HEADS * HEAD_DIM), BF16),
        scratch_shapes=[pltpu.VMEM((SB_HEADS, TQ, TQ), F32), pltpu.VMEM((SB_HEADS, TQ, HEAD_DIM), F32)],
        compiler_params=_params("parallel", "parallel", "arbitrary"), name="sb_prompt_attn")(
            q, kvp, kvp, _suffix_matrix())


def _nsa_sample_select_body(q_ref, kcb_ref, bc_ref, mc_ref, ovl_ref, fadd_ref, elig_ref, oc_ref, sel_ref, *, n_blk):
    nbp = kcb_ref.shape[1]
    grp = G * T8
    ps = []
    for kv in range(KV):
        qk = jnp.concatenate([q_ref[:, (kv * G + g) * HEAD_DIM:(kv * G + g + 1) * HEAD_DIM] for g in range(G)],
                             axis=0)
        s = _dot_nt(qk, kcb_ref[kv].astype(BF16)) + bc_ref[kv * grp:(kv + 1) * grp, :]
        p = _masked_softmax(s, mc_ref[...] > 0.5)
        oc_ref[kv * grp:(kv + 1) * grp, :] = _dot(p.astype(BF16), kcb_ref[KV + kv].astype(BF16))
        ps.append(p)
    pad = jnp.zeros((LANES - KV * T8, nbp), F32)
    imp = None
    for g in range(G):
        xg = jnp.concatenate([ps[kv][g * T8:(g + 1) * T8] for kv in range(KV)] + [pad], axis=0)
        part = _dot_nt(ovl_ref[...], xg.astype(BF16))
        imp = part if imp is None else imp + part
    score = jnp.where(elig_ref[...] > 0.5, imp + fadd_ref[...], -jnp.inf)
    blk = lax.broadcasted_iota(I32, score.shape, 0)
    sel_ref[...] = _extract_top(score, blk, min(SEL_TOPK, n_blk))


SEL_KEYS = 2048
SEL_PAGES = SEL_KEYS // PAGE_SIZE


def _nsa_sample_attn_body(pt_ref, q_ref, *refs, n_steps, t_new, past):
    pages = [r.reshape(PAGE_SIZE * SUBLANES, HEAD_DIM) for r in refs[:SEL_PAGES]]
    (sel_ref, new_ref, win_ref, wnew_ref, g_ref, oc_ref, bl_ref, bf_ref, bn_ref, bw_ref, ex_ref,
     o_ref, m_s, l_s, a_s) = refs[SEL_PAGES:]

    def page_rows(c):
        return jnp.concatenate([pg[pl.ds(c, PAGE_SIZE, stride=SUBLANES), :] for pg in pages], axis=0).astype(BF16)

    step = pl.program_id(1)
    grp = G * T8
    kvw = KV * HEAD_DIM
    wb = win_ref.shape[0] // (2 * KV)
    is_last = step == n_steps - 1

    @pl.when(step == 0)
    def _():
        _flash_init(m_s, l_s, a_s)

    def q_of(kv):
        return jnp.concatenate([q_ref[:, (kv * G + g) * HEAD_DIM:(kv * G + g + 1) * HEAD_DIM] for g in range(G)],
                               axis=0)

    def pad_keys(x):
        return jnp.concatenate([x, jnp.zeros((TQ - x.shape[0], x.shape[1]), x.dtype)], axis=0).astype(BF16)

    def rows_of_blocks(piece):
        return jnp.concatenate([piece, jnp.zeros((LANES - piece.shape[0], LANES), F32)], axis=0).T

    bps = SEL_KEYS // SEL_BLOCK
    piece = sel_ref[pl.ds(pl.multiple_of(step * bps, bps), bps), :]
    mk = _dot(rows_of_blocks(piece).astype(BF16), ex_ref[...])

    kv_rows = [pl.ds(kv * grp, grp) for kv in range(KV)]
    scores = [_dot_nt(q_of(kv), page_rows(kv)) for kv in range(KV)]
    probs = []
    for kv in range(KV):
        bias = jnp.where(is_last, bl_ref[kv_rows[kv], :], bf_ref[kv_rows[kv], 0:1])
        mask = jnp.concatenate([mk[kv * T8:(kv + 1) * T8]] * G, axis=0) > 0.5
        probs.append(_flash_prob(scores[kv] + bias, mask, m_s, l_s, kv_rows[kv]))
    for kv in range(KV):
        p, alpha = probs[kv]
        a_s[kv_rows[kv], :] = alpha * a_s[kv_rows[kv], :] + _dot(p, page_rows(KV + kv))

    @pl.when(is_last)
    def _():
        lane = lax.broadcasted_iota(I32, (grp, LANES), 1)
        t_row = lax.broadcasted_iota(I32, (grp, LANES), 0) & (T8 - 1)
        lane_w = lax.broadcasted_iota(I32, (grp, wb), 1)
        t_row_w = lax.broadcasted_iota(I32, (grp, wb), 0) & (T8 - 1)
        new_blk = past // SEL_BLOCK
        picked_all = rows_of_blocks(sel_ref[new_blk:new_blk + SUBLANES, :])
        gt = g_ref[...]
        for kv in range(KV):
            rows = pl.ds(kv * grp, grp)
            qk = q_of(kv)
            kn = pad_keys(new_ref[:, 2 * kvw + kv * HEAD_DIM:2 * kvw + (kv + 1) * HEAD_DIM])
            vn = pad_keys(new_ref[:, 3 * kvw + kv * HEAD_DIM:3 * kvw + (kv + 1) * HEAD_DIM])
            s = _dot_nt(qk, kn) + bn_ref[rows, :]
            picked = jnp.concatenate([picked_all[kv * T8:(kv + 1) * T8, 0:1]] * G, axis=0) > 0.5
            mask = (lane <= t_row) & (lane < t_new) & picked
            _flash_update(s, mask, vn, m_s, l_s, a_s, rows)
            o_s = a_s[rows, :] * _safe_inv(l_s[rows, :])
            kw = win_ref[pl.ds(kv, wb, stride=2 * KV), :].astype(BF16)
            vw = win_ref[pl.ds(KV + kv, wb, stride=2 * KV), :].astype(BF16)
            kwn = pad_keys(wnew_ref[:, kv * HEAD_DIM:(kv + 1) * HEAD_DIM])
            vwn = pad_keys(wnew_ref[:, kvw + kv * HEAD_DIM:kvw + (kv + 1) * HEAD_DIM])
            s1 = _dot_nt(qk, kw) + bw_ref[rows, 0:wb]
            s2 = _dot_nt(qk, kwn) + bw_ref[rows, wb:wb + LANES]
            d1 = wb + t_row_w - lane_w
            mask1 = (d1 >= 0) & (d1 < WINDOW)
            d2 = t_row - lane
            mask2 = (d2 >= 0) & (d2 < WINDOW) & (lane < t_new)
            m = jnp.maximum(jnp.max(jnp.where(mask1, s1, NEG_INF), axis=1, keepdims=True),
                            jnp.max(jnp.where(mask2, s2, NEG_INF), axis=1, keepdims=True))
            e1 = jnp.where(mask1, jnp.exp(s1 - m), 0.0)
            e2 = jnp.where(mask2, jnp.exp(s2 - m), 0.0)
            l = jnp.sum(e1, axis=1, keepdims=True) + jnp.sum(e2, axis=1, keepdims=True)
            o_w = (_dot(e1.astype(BF16), vw) + _dot(e2.astype(BF16), vwn)) * _safe_inv(l)
            o_c = oc_ref[rows, :]
            for g in range(G):
                r = slice(g * T8, (g + 1) * T8)
                c = kv * LANES + 3 * g
                out = gt[:, c:c + 1] * o_c[r] + gt[:, c + 1:c + 2] * o_s[r] + gt[:, c + 2:c + 3] * o_w[r]
                o_ref[:, (kv * G + g) * HEAD_DIM:(kv * G + g + 1) * HEAD_DIM] = out.astype(o_ref.dtype)


def nsa_sample_attention(q8, cache5, page_table, kcb, new8, win3, wnew8, gates8, bias_s, past, t_new):
    bd = q8.shape[0]
    n_chunks = kcb.shape[2]
    n_cmp = n_chunks - CMP_BLOCK // CMP_STRIDE + 1
    total = past + t_new
    n_blk = -(-total // SEL_BLOCK)
    nblkp = -(-(n_blk + SUBLANES) // (2 * SUBLANES)) * 2 * SUBLANES
    wb = win3.shape[1] // (2 * KV)
    n_steps = past // SEL_KEYS
    assert past % SEL_KEYS == 0 and SEL_KEYS % SEL_BLOCK == 0 and t_new <= T8 and PAGE_SIZE > FAR_DIST
    assert (total // CMP_STRIDE) == n_chunks and wb % LANES == 0 and SEL_KEYS // SEL_BLOCK <= LANES

    pos_q = past + np.arange(T8)
    cmp_start = np.arange(n_chunks) * CMP_STRIDE
    cmp_end = cmp_start + CMP_BLOCK - 1
    mask_c = ((pos_q[:, None] >= cmp_end[None, :]) & (np.arange(n_chunks)[None, :] < n_cmp))
    mask_c = jnp.asarray(np.tile(mask_c, (G, 1)), F32)
    jst = np.arange(nblkp) * SEL_BLOCK
    overlap = ((cmp_start[:, None] < jst[None, :] + SEL_BLOCK) & (cmp_start[:, None] + CMP_BLOCK > jst[None, :])
               & (np.arange(n_chunks)[:, None] < n_cmp) & (np.arange(nblkp)[None, :] < n_blk))
    ovl_t = jnp.asarray(overlap.T, BF16)
    blk = np.arange(nblkp)[:, None]
    pq = np.tile(pos_q, KV)[None, :]
    tvalid = np.tile(np.arange(T8) < t_new, KV)[None, :]
    cur = pq // SEL_BLOCK
    elig = (blk * SEL_BLOCK <= pq) & (blk < n_blk) & tvalid
    forced = (blk == 0) | (blk == cur) | (blk == cur - 1)
    pad_l = LANES - KV * T8
    elig = jnp.asarray(np.pad(elig, ((0, 0), (0, pad_l))), F32)
    fadd = jnp.asarray(np.pad(np.where(forced, FORCE_BONUS, 0.0), ((0, 0), (0, pad_l))), F32)

    grp_all = KV * G * T8
    o_c, sel = pl.pallas_call(
        functools.partial(_nsa_sample_select_body, n_blk=n_blk), grid=(bd,),
        in_specs=[pl.BlockSpec((None, T8, N_HEADS * HEAD_DIM), lambda b: (b, 0, 0)),
                  pl.BlockSpec((None, 2 * KV, n_chunks, HEAD_DIM), lambda b: (b, 0, 0, 0)),
                  pl.BlockSpec((grp_all, n_chunks), lambda b: (0, 0)),
                  pl.BlockSpec((G * T8, n_chunks), lambda b: (0, 0)),
                  pl.BlockSpec((nblkp, n_chunks), lambda b: (0, 0)),
                  pl.BlockSpec((nblkp, LANES), lambda b: (0, 0)),
                  pl.BlockSpec((nblkp, LANES), lambda b: (0, 0))],
        out_specs=[pl.BlockSpec((None, grp_all, HEAD_DIM), lambda b: (b, 0, 0)),
                   pl.BlockSpec((None, nblkp, LANES), lambda b: (b, 0, 0))],
        out_shape=[jax.ShapeDtypeStruct((bd, grp_all, HEAD_DIM), F32),
                   jax.ShapeDtypeStruct((bd, nblkp, LANES), F32)],
        compiler_params=_params("parallel"), name="nsa_sample_select")(
            q8, kcb, bias_s["cmp"], mask_c, ovl_t, fadd, elig)

    kvw = KV * HEAD_DIM
    const2 = lambda b, s, pt: (0, 0)
    per_b = lambda b, s, pt: (b, 0, 0)
    expand = jnp.asarray(np.arange(LANES)[:, None] == (np.arange(SEL_KEYS)[None, :] // SEL_BLOCK), BF16)
    bias_last = jnp.concatenate([bias_s["far"]] * (SEL_KEYS // LANES - 1) + [bias_s["last"]], axis=1)

    def page_spec(i):
        return pl.BlockSpec((None, PAGE_SIZE, None, SUBLANES, HEAD_DIM),
                            lambda b, s, pt: (pt[b, s * SEL_PAGES + i], 0, 1, 0, 0))

    grid_spec = pltpu.PrefetchScalarGridSpec(
        num_scalar_prefetch=1, grid=(bd, n_steps),
        in_specs=[pl.BlockSpec((None, T8, N_HEADS * HEAD_DIM), per_b)] + [page_spec(i) for i in range(SEL_PAGES)]
        + [pl.BlockSpec((None, nblkp, LANES), per_b),
                  pl.BlockSpec((None, T8, 4 * kvw), per_b),
                  pl.BlockSpec((None, wb * 2 * KV, HEAD_DIM), per_b),
                  pl.BlockSpec((None, T8, 2 * kvw), per_b),
                  pl.BlockSpec((None, T8, KV * LANES), per_b),
                  pl.BlockSpec((None, grp_all, HEAD_DIM), per_b),
                  pl.BlockSpec((grp_all, SEL_KEYS), const2), pl.BlockSpec((grp_all, LANES), const2),
                  pl.BlockSpec((grp_all, LANES), const2), pl.BlockSpec((grp_all, wb + LANES), const2),
                  pl.BlockSpec((LANES, SEL_KEYS), const2)],
        out_specs=pl.BlockSpec((None, T8, N_HEADS * HEAD_DIM), per_b),
        scratch_shapes=[pltpu.VMEM((grp_all, LANES), F32) for _ in range(3)])
    return pl.pallas_call(
        functools.partial(_nsa_sample_attn_body, n_steps=n_steps, t_new=t_new, past=past), grid_spec=grid_spec,
        out_shape=jax.ShapeDtypeStruct((bd, T8, N_HEADS * HEAD_DIM), BF16),
        compiler_params=_params("parallel", "arbitrary"), name="nsa_sample_attn")(
            page_table, q8, *([cache5] * SEL_PAGES), sel, new8, win3, wnew8, gates8, o_c,
            bias_last, bias_s["far"], bias_s["new"], bias_s["win"], expand)


SB_PAGES = 8
SB_GROUPS = 2 * N_HEADS // SUBLANES


def _sb_sample_run(q_ref, u, carry_ref, acc_ref, n_sub, k_of, v_of, masks):
    def q_of(h):
        return q_ref[h * T8:(h + 1) * T8, :].astype(BF16)

    zs = [jnp.concatenate([_dot_nt(q_of(h), k_of(j, h)) for h in range(N_HEADS)], axis=0)
          for j in range(n_sub)]
    pre = [_stick_pre(zs[j], masks[j]) for j in range(n_sub)]
    l2s = [_dot(pre[j][2], u) for j in range(n_sub)]
    carry = carry_ref[...]
    ws = [None] * n_sub
    for j in reversed(range(n_sub)):
        ws[j] = _stick_post(pre[j][0], l2s[j], carry, masks[j])
        carry = carry + pre[j][1]
    carry_ref[...] = carry
    for h in range(N_HEADS):
        r = slice(h * T8, (h + 1) * T8)
        out = _dot(ws[0][r].astype(BF16), v_of(0, h))
        for j in range(1, n_sub):
            out = out + _dot(ws[j][r].astype(BF16), v_of(j, h))
        acc_ref[r, :] += out


def _sb_page_rows(refs, rows_per_block):
    blocks = [r.reshape(PAGE_SIZE * rows_per_block, HEAD_DIM) for r in refs]
    per_page = 2 * N_HEADS // rows_per_block

    def head_rows(j, row):
        blk = blocks[j * per_page + row // rows_per_block]
        return blk[pl.ds(row % rows_per_block, PAGE_SIZE, stride=rows_per_block), :].astype(BF16)
    return head_rows


def _sb_sample_head_body(pt_ref, q_ref, *refs, t_new):
    n_grp = SB_PAGES * SB_GROUPS
    head_rows = _sb_page_rows(refs[:n_grp], SUBLANES)
    new_ref, u_ref, acc_ref, carry_ref = refs[n_grp:]
    hw = N_HEADS * HEAD_DIM
    carry_ref[...] = jnp.zeros_like(carry_ref)
    acc_ref[...] = jnp.zeros_like(acc_ref)
    pad = jnp.zeros((TQ - T8, 2 * hw), F32)
    kvn = jnp.concatenate([new_ref[...], pad], axis=0)
    lane = lax.broadcasted_iota(I32, (TQ, TQ), 1)
    t_row = lax.broadcasted_iota(I32, (TQ, TQ), 0) & (T8 - 1)
    _sb_sample_run(q_ref, u_ref[...], carry_ref, acc_ref, 1,
                   lambda j, h: kvn[:, h * HEAD_DIM:(h + 1) * HEAD_DIM].astype(BF16),
                   lambda j, h: kvn[:, hw + h * HEAD_DIM:hw + (h + 1) * HEAD_DIM].astype(BF16),
                   [(lane < t_row) & (lane < t_new)])
    _sb_sample_run(q_ref, u_ref[...], carry_ref, acc_ref, SB_PAGES, lambda j, h: head_rows(j, h),
                   lambda j, h: head_rows(j, N_HEADS + h), [None] * SB_PAGES)


def _sb_sample_tail_body(pt_ref, need_ref, q_ref, *refs, n_steps):
    head_rows = _sb_page_rows(refs[:SB_PAGES], 2 * N_HEADS)
    u_ref, acc_in, carry_in, o_ref, carry_ref, acc_ref = refs[SB_PAGES:]
    step = pl.program_id(1)

    @pl.when(step == 0)
    def _():
        carry_ref[...] = carry_in[...]
        acc_ref[...] = acc_in[...]

    @pl.when((need_ref[pl.program_id(0)] > 0) & (jnp.min(carry_ref[...]) < SB_DONE))
    def _():
        _sb_sample_run(q_ref, u_ref[...], carry_ref, acc_ref, SB_PAGES, lambda j, h: head_rows(j, h),
                       lambda j, h: head_rows(j, N_HEADS + h), [None] * SB_PAGES)

    @pl.when(step == n_steps - 1)
    def _():
        o_ref[...] = acc_ref[...].astype(o_ref.dtype)


def sb_sample_attention(q_rows, cache5, page_table, new8, t_new):
    bd, n_pages = page_table.shape
    hw = N_HEADS * HEAD_DIM
    n_steps = n_pages // SB_PAGES - 1
    assert n_pages % SB_PAGES == 0 and n_steps >= 1 and cache5.shape[2] == SB_GROUPS
    rows = N_HEADS * T8
    grp_block = (None, PAGE_SIZE, None, SUBLANES, HEAD_DIM)
    n_grp = SB_PAGES * SB_GROUPS
    u = _suffix_matrix()

    def head_spec(i, g):
        return pl.BlockSpec(grp_block, lambda b, pt: (pt[b, n_pages - SB_PAGES + i], 0, g, 0, 0))

    acc, carry = pl.pallas_call(
        functools.partial(_sb_sample_head_body, t_new=t_new),
        grid_spec=pltpu.PrefetchScalarGridSpec(
            num_scalar_prefetch=1, grid=(bd,),
            in_specs=[pl.BlockSpec((None, rows, HEAD_DIM), lambda b, pt: (b, 0, 0))]
            + [head_spec(i, g) for i in range(SB_PAGES) for g in range(SB_GROUPS)]
            + [pl.BlockSpec((None, T8, 2 * hw), lambda b, pt: (b, 0, 0)),
               pl.BlockSpec((TQ, TQ), lambda b, pt: (0, 0))],
            out_specs=[pl.BlockSpec((None, rows, HEAD_DIM), lambda b, pt: (b, 0, 0)),
                       pl.BlockSpec((None, rows, LANES), lambda b, pt: (b, 0, 0))]),
        out_shape=[jax.ShapeDtypeStruct((bd, rows, HEAD_DIM), F32), jax.ShapeDtypeStruct((bd, rows, LANES), F32)],
        compiler_params=_params("parallel"), name="sb_sample_head")(
            page_table, q_rows, *([cache5] * n_grp), new8, u)

    need = (jnp.min(carry, axis=(1, 2)) < SB_DONE).astype(I32)

    def tail_spec(i):
        def index(b, s, pt, nd):
            page = jnp.where(nd[b] > 0, n_pages - (s + 2) * SB_PAGES + i, n_pages - SB_PAGES + i)
            return (pt[b, page], 0, 0, 0, 0)
        return pl.BlockSpec((None, PAGE_SIZE, SB_GROUPS, SUBLANES, HEAD_DIM), index)

    per_b = lambda b, s, pt, nd: (b, 0, 0)
    tail = pl.pallas_call(
        functools.partial(_sb_sample_tail_body, n_steps=n_steps),
        grid_spec=pltpu.PrefetchScalarGridSpec(
            num_scalar_prefetch=2, grid=(bd, n_steps),
            in_specs=[pl.BlockSpec((None, rows, HEAD_DIM), per_b)]
            + [tail_spec(i) for i in range(SB_PAGES)]
            + [pl.BlockSpec((TQ, TQ), lambda b, s, pt, nd: (0, 0)),
               pl.BlockSpec((None, rows, HEAD_DIM), per_b), pl.BlockSpec((None, rows, LANES), per_b)],
            out_specs=pl.BlockSpec((None, rows, HEAD_DIM), per_b),
            scratch_shapes=[pltpu.VMEM((rows, LANES), F32), pltpu.VMEM((rows, HEAD_DIM), F32)]),
        out_shape=jax.ShapeDtypeStruct((bd, rows, HEAD_DIM), BF16),
        compiler_params=_params("parallel", "arbitrary"), name="sb_sample_tail")
    return lax.cond(jnp.any(need > 0),
                    lambda: tail(page_table, need, q_rows, *([cache5] * SB_PAGES), u, acc, carry),
                    lambda: acc.astype(BF16))


def _pad_tokens(x, bd, t_new):
    x = x.reshape(bd, t_new, x.shape[-1])
    return jnp.pad(x, ((0, 0), (0, T8 - t_new), (0, 0)))


def _bias_distances(t, past, n_chunks_s, wb):
    r = np.arange(TQ)[:, None]
    c = np.arange(LANES)[None, :]
    win_tiles = np.concatenate([r - c + off * TQ for off in range(WIN_TILES)], axis=0)
    toeplitz = np.concatenate([r - c + off * TQ for off in range(3)], axis=0)
    cmp_p = np.arange(t)[:, None] - (c * CMP_STRIDE + CMP_BLOCK - 1)
    pos_q = past + np.arange(T8)[:, None]
    cmp_s = (pos_q - (np.arange(n_chunks_s)[None, :] * CMP_STRIDE + CMP_BLOCK - 1)).reshape(-1, LANES)
    last = pos_q - (past - PAGE_SIZE + c)
    far = np.full((T8, LANES), 4 * REL_MAX_DIST)
    new = np.arange(T8)[:, None] - c
    win = (wb + np.arange(T8)[:, None] - np.arange(wb + LANES)[None, :]).reshape(-1, LANES)
    parts = [win_tiles, toeplitz, cmp_p, cmp_s, last, far, new, win]
    rows = sum(p.shape[0] for p in parts)
    pad = -rows % LANES
    parts.append(np.zeros((pad, LANES), np.int64))
    offs = np.cumsum([0] + [p.shape[0] for p in parts])
    dist = np.concatenate(parts, axis=0)
    addmask = np.zeros(dist.shape, np.float32)
    addmask[:win_tiles.shape[0]] = np.where((win_tiles >= 0) & (win_tiles < WINDOW), 0.0, NEG_INF)
    return jnp.asarray(dist, I32), jnp.asarray(addmask), offs


def _sample_bias(bias_all, offs, n_chunks_s, wb):
    def rows_kgt(x, width):
        return x.reshape(N_HEADS * T8, width)
    seg = lambda i: bias_all[:, offs[i]:offs[i + 1]]
    return {"cmp": rows_kgt(seg(3), n_chunks_s), "last": rows_kgt(seg(4), LANES), "far": rows_kgt(seg(5), LANES),
            "new": rows_kgt(seg(6), LANES), "win": rows_kgt(seg(7), wb + LANES)}


def kernel(x_prompt, x_sample, cache_nsa, state_nsa_win, cache_sb, page_table, rel_bias, norm_w, final_norm_w,
           nsa_w_in, nsa_w_out, nsa_pe_k, nsa_pe_v, nsa_phi_k1, nsa_phi_k2, nsa_phi_v1, nsa_phi_v2,
           sb_w_in, sb_w_out, ffn_w1, ffn_w3, ffn_w2):
    b, t, d = x_prompt.shape
    bd, t_new, _ = x_sample.shape
    n_pages = page_table.shape[1]
    past = n_pages * PAGE_SIZE
    wb = state_nsa_win.shape[2]
    hd = N_HEADS * HEAD_DIM
    kvw = KV * HEAD_DIM
    n_chunks_s = (past + t_new) // CMP_STRIDE
    half = CMP_STRIDE * HEAD_DIM

    dist, addmask, offs = _bias_distances(t, past, n_chunks_s, wb)
    bias_all = bias_from_dist(rel_bias, dist, addmask)
    bias_s = _sample_bias(bias_all, offs, n_chunks_s, wb)

    w_in = nsa_w_in[0]
    wq = (w_in[:, :hd] * SCALE).astype(BF16)
    wkv4 = w_in[:, hd:hd + 4 * kvw].astype(BF16)
    wwin = w_in[:, hd + 4 * kvw:hd + 6 * kvw].astype(BF16)
    wg = w_in[:, hd + 6 * kvw:].reshape(d, KV, G * 3)
    wg = jnp.pad(wg, ((0, 0), (0, 0), (0, LANES - G * 3))).reshape(d, KV * LANES).astype(BF16)
    w_out_nsa = nsa_w_out[0].astype(BF16)
    cw = {
        "w1k": jnp.concatenate([nsa_phi_k1[0, :half], nsa_phi_k1[0, half:]], axis=1).astype(BF16),
        "w1v": jnp.concatenate([nsa_phi_v1[0, :half], nsa_phi_v1[0, half:]], axis=1).astype(BF16),
        "pe": jnp.broadcast_to(jnp.stack([nsa_pe_k[0].reshape(1, -1), nsa_pe_v[0].reshape(1, -1)]),
                               (2, SUBLANES, CMP_BLOCK * HEAD_DIM)),
        "w1": jnp.stack([nsa_phi_k1[0], nsa_phi_v1[0]]).astype(BF16),
        "w2": jnp.stack([nsa_phi_k2[0], nsa_phi_v2[0]]).astype(BF16),
    }
    wsb = sb_w_in[0]
    wsq = (wsb[:, :hd] * SCALE).astype(BF16)
    wskv = wsb[:, hd:].astype(BF16)
    w_out_sb = sb_w_out[0].astype(BF16)
    w1 = ffn_w1.astype(BF16)
    w3 = ffn_w3.astype(BF16)
    w2 = ffn_w2.astype(BF16)

    hp = x_prompt.reshape(b * t, d)
    hs = x_sample.reshape(bd * t_new, d)

    xp = rmsnorm_rows(hp, norm_w[0, 0], BF16)
    xs = rmsnorm_rows(hs, norm_w[0, 0], BF16)
    q_p, q_s = matmul(xp, wq, (BF16,)), matmul(xs, wq, (BF16,))
    (kv4_p, kv4b_p), kv4_s = matmul(xp, wkv4, (F32, BF16)), matmul(xs, wkv4, (F32,))
    (win_p, winb_p), win_s = matmul(xp, wwin, (F32, BF16)), matmul(xs, wwin, (F32,))
    g_p, g_s = matmul(xp, wg, (F32,), sigmoid=True), matmul(xs, wg, (F32,), sigmoid=True)

    pt_prompt = jnp.arange(b * (t // PAGE_SIZE), dtype=I32).reshape(b, t // PAGE_SIZE)
    kcb_p = compress(kv4_p.reshape(b * t // PAGE_SIZE, PAGE_SIZE, 4 * kvw), pt_prompt, cw)
    attn_p = nsa_prompt_attention(q_p, kv4b_p, winb_p, g_p, kcb_p, bias_all, b, t)

    cache5 = cache_nsa[0].reshape(cache_nsa.shape[1], PAGE_SIZE, 2, 2 * KV, HEAD_DIM)
    kcb_s = compress_paged(cache5, page_table, cw)
    attn_s = nsa_sample_attention(
        _pad_tokens(q_s, bd, t_new), cache5, page_table, kcb_s, _pad_tokens(kv4_s, bd, t_new),
        state_nsa_win[0].reshape(bd, wb * 2 * KV, HEAD_DIM), _pad_tokens(win_s, bd, t_new),
        _pad_tokens(g_s, bd, t_new), bias_s, past, t_new)
    attn_s = attn_s[:, :t_new].reshape(bd * t_new, hd)

    hp, xp = matmul_res_norm(attn_p, w_out_nsa, hp, norm_w[0, 1])
    hs, xs = matmul_res_norm(attn_s, w_out_nsa, hs, norm_w[0, 1])
    hp, xp = ffn_res_norm(xp, w1[0], w3[0], w2[0], hp, norm_w[1, 0], BF16)
    hs, xs = ffn_res_norm(xs, w1[0], w3[0], w2[0], hs, norm_w[1, 0], BF16)

    sq_p, sq_s = matmul(xp, wsq, (BF16,)), matmul(xs, wsq, (BF16,))
    (skv_p, skvb_p), skv_s = matmul(xp, wskv, (F32, BF16)), matmul(xs, wskv, (F32,))
    sattn_p = sb_prompt_attention(sq_p, skvb_p, b, t)

    q_rows = (_pad_tokens(sq_s, bd, t_new).reshape(bd, T8, N_HEADS, HEAD_DIM).transpose(0, 2, 1, 3)
              .reshape(bd, N_HEADS * T8, HEAD_DIM).astype(F32))
    csb5 = cache_sb[0].reshape(cache_sb.shape[1], PAGE_SIZE, SB_GROUPS, SUBLANES, HEAD_DIM)
    sattn_s = sb_sample_attention(q_rows, csb5, page_table, _pad_tokens(skv_s, bd, t_new), t_new)
    sattn_s = sattn_s.reshape(bd, N_HEADS, T8, HEAD_DIM)[:, :, :t_new].transpose(0, 2, 1, 3).reshape(bd * t_new, hd)

    hp, xp = matmul_res_norm(sattn_p, w_out_sb, hp, norm_w[1, 1])
    hs, xs = matmul_res_norm(sattn_s, w_out_sb, hs, norm_w[1, 1])
    y_p = ffn_res_norm(xp, w1[1], w3[1], w2[1], hp, final_norm_w, F32, emit_h=False)
    y_s = ffn_res_norm(xs, w1[1], w3[1], w2[1], hs, final_norm_w, F32, emit_h=False)

    win_keep = min(WINDOW, t)
    nsa_win_prompt = win_p.reshape(b, t, 2, KV, HEAD_DIM)[:, t - win_keep:]
    win_all = jnp.concatenate([state_nsa_win[0], win_s.reshape(bd, t_new, 2, KV, HEAD_DIM)], axis=1)
    return (y_p.reshape(b, t, d), y_s.reshape(bd, t_new, d),
            kv4_p.reshape(1, b, t, 4, KV, HEAD_DIM), kv4_s.reshape(1, bd, t_new, 4, KV, HEAD_DIM),
            nsa_win_prompt[None], win_all[:, t_new:][None],
            skv_p.reshape(1, b, t, 2, N_HEADS, HEAD_DIM), skv_s.reshape(1, bd, t_new, 2, N_HEADS, HEAD_DIM))
```

```python
import functools
import math

import numpy as np
import jax
import jax.numpy as jnp
from jax import lax
from jax.experimental import pallas as pl
from jax.experimental.pallas import tpu as pltpu

F32 = jnp.float32
BF16 = jnp.bfloat16
I32 = jnp.int32

HEAD_DIM = 128
N_HEADS = 16
NSA_KV_HEADS = 4
NSA_GROUP = N_HEADS // NSA_KV_HEADS
CMP_BLOCK = 32
CMP_STRIDE = 16
SEL_BLOCK = 64
SEL_TOPK = 16
WINDOW = 512
FORCE_BONUS = 1000.0
REL_BUCKETS = 32
REL_MAX_DIST = 128
RMS_EPS = 1e-6
NEG_INF = -1e30
MASK_NEG = 2 * NEG_INF
SCALE = HEAD_DIM ** -0.5
PAGE_SIZE = 128

LANES = 128
SUBLANES = 8
TQ = 128
WIN_TILES = WINDOW // TQ + 2
T8 = SUBLANES
VMEM_LIMIT = 56 * 1024 * 1024
KV = NSA_KV_HEADS
G = NSA_GROUP


def _t5_thresholds():
    d = np.arange(0, 4 * REL_MAX_DIST)
    max_exact = REL_BUCKETS // 2
    nf = np.maximum(d, 1).astype(np.float32)
    large = max_exact + (np.log(nf / np.float32(max_exact)) / np.float32(math.log(REL_MAX_DIST / max_exact))
                         * np.float32(REL_BUCKETS - max_exact)).astype(np.int32)
    b = np.where(d < max_exact, d, np.minimum(large, REL_BUCKETS - 1))
    return [int(np.argmax(b >= k)) for k in range(REL_BUCKETS)]


T5_THR = _t5_thresholds()
FAR_DIST = T5_THR[-1]


def _dot(a, b):
    return jnp.dot(a, b, preferred_element_type=F32)


def _dot_nt(a, b):
    return lax.dot_general(a, b, (((1,), (1,)), ((), ())), preferred_element_type=F32)


def _params(*sem):
    return pltpu.CompilerParams(dimension_semantics=sem, vmem_limit_bytes=VMEM_LIMIT)


def _rms(x, w):
    return x * lax.rsqrt(jnp.mean(x * x, axis=-1, keepdims=True) + RMS_EPS) * w


def _masked_softmax(s, mask):
    sm = jnp.where(mask, s, NEG_INF)
    m = jnp.max(sm, axis=1, keepdims=True)
    e = jnp.where(mask, jnp.exp(sm - m), 0.0)
    l = jnp.sum(e, axis=1, keepdims=True)
    return e * jnp.where(l > 0.0, 1.0 / l, 0.0)


def _safe_inv(l):
    return jnp.where(l > 0.0, 1.0 / l, 0.0)


def _rmsnorm_body(x_ref, w_ref, o_ref):
    o_ref[...] = _rms(x_ref[...], w_ref[...]).astype(o_ref.dtype)


def rmsnorm_rows(x, w, out_dtype):
    m, d = x.shape
    tm = min(m, 512)
    return pl.pallas_call(
        _rmsnorm_body, grid=(m // tm,),
        in_specs=[pl.BlockSpec((tm, d), lambda i: (i, 0)), pl.BlockSpec((1, d), lambda i: (0, 0))],
        out_specs=pl.BlockSpec((tm, d), lambda i: (i, 0)),
        out_shape=jax.ShapeDtypeStruct((m, d), out_dtype),
        compiler_params=_params("parallel"), name="rmsnorm")(x, w.reshape(1, d))


def _matmul_body(a_ref, w_ref, *o_refs, sigmoid):
    y = _dot(a_ref[...], w_ref[...])
    if sigmoid:
        y = jax.nn.sigmoid(y)
    for o_ref in o_refs:
        o_ref[...] = y.astype(o_ref.dtype)


def matmul(a, w, out_dtypes, sigmoid=False):
    m, k = a.shape
    n = w.shape[1]
    tm = min(m, 512)
    tn = min(n, 2048)
    outs = pl.pallas_call(
        functools.partial(_matmul_body, sigmoid=sigmoid), grid=(m // tm, n // tn),
        in_specs=[pl.BlockSpec((tm, k), lambda i, j: (i, 0)), pl.BlockSpec((k, tn), lambda i, j: (0, j))],
        out_specs=[pl.BlockSpec((tm, tn), lambda i, j: (i, j)) for _ in out_dtypes],
        out_shape=[jax.ShapeDtypeStruct((m, n), dt) for dt in out_dtypes],
        compiler_params=_params("parallel", "arbitrary"), name="matmul")(a, w)
    return outs if len(outs) > 1 else outs[0]


def _mm_res_norm_body(a_ref, w_ref, h_ref, nw_ref, hout_ref, xn_ref):
    h = h_ref[...] + _dot(a_ref[...], w_ref[...])
    hout_ref[...] = h
    xn_ref[...] = _rms(h, nw_ref[...]).astype(xn_ref.dtype)


def matmul_res_norm(a, w, h, nw):
    m, k = a.shape
    d = w.shape[1]
    tm = min(m, 512)
    return pl.pallas_call(
        _mm_res_norm_body, grid=(m // tm,),
        in_specs=[pl.BlockSpec((tm, k), lambda i: (i, 0)), pl.BlockSpec((k, d), lambda i: (0, 0)),
                  pl.BlockSpec((tm, d), lambda i: (i, 0)), pl.BlockSpec((1, d), lambda i: (0, 0))],
        out_specs=[pl.BlockSpec((tm, d), lambda i: (i, 0)), pl.BlockSpec((tm, d), lambda i: (i, 0))],
        out_shape=[jax.ShapeDtypeStruct((m, d), F32), jax.ShapeDtypeStruct((m, d), BF16)],
        compiler_params=_params("parallel"), name="matmul_res_norm")(a, w, h, nw.reshape(1, d))


def _ffn_body(xn_ref, w1_ref, w3_ref, w2_ref, h_ref, nw_ref, *out_refs):
    hout_ref = out_refs[0] if len(out_refs) == 3 else None
    yn_ref, acc_ref = out_refs[-2:]
    f = pl.program_id(1)

    @pl.when(f == 0)
    def _():
        acc_ref[...] = jnp.zeros_like(acc_ref)

    x = xn_ref[...]
    g = _dot(x, w1_ref[...])
    u = _dot(x, w3_ref[...])
    mid = (g * jax.nn.sigmoid(g) * u).astype(BF16)
    acc_ref[...] += _dot(mid, w2_ref[...])

    @pl.when(f == pl.num_programs(1) - 1)
    def _():
        h = h_ref[...] + acc_ref[...]
        if hout_ref is not None:
            hout_ref[...] = h
        yn_ref[...] = _rms(h, nw_ref[...]).astype(yn_ref.dtype)


def ffn_res_norm(xn, w1, w3, w2, h, nw, norm_dtype, emit_h=True):
    m, d = xn.shape
    dff = w1.shape[1]
    tm = min(m, 512)
    tf = 512
    n_out = 2 if emit_h else 1
    outs = pl.pallas_call(
        _ffn_body, grid=(m // tm, dff // tf),
        in_specs=[pl.BlockSpec((tm, d), lambda i, f: (i, 0)),
                  pl.BlockSpec((d, tf), lambda i, f: (0, f)), pl.BlockSpec((d, tf), lambda i, f: (0, f)),
                  pl.BlockSpec((tf, d), lambda i, f: (f, 0)),
                  pl.BlockSpec((tm, d), lambda i, f: (i, 0)), pl.BlockSpec((1, d), lambda i, f: (0, 0))],
        out_specs=[pl.BlockSpec((tm, d), lambda i, f: (i, 0)) for _ in range(n_out)],
        out_shape=[jax.ShapeDtypeStruct((m, d), F32), jax.ShapeDtypeStruct((m, d), norm_dtype)][-n_out:],
        scratch_shapes=[pltpu.VMEM((tm, d), F32)],
        compiler_params=_params("parallel", "arbitrary"), name="ffn")(xn, w1, w3, w2, h, nw.reshape(1, d))
    return outs if emit_h else outs[0]


def _bias_body(tab_ref, d_ref, m_ref, o_ref):
    h = pl.program_id(0)
    dvec = lax.broadcasted_iota(I32, (SUBLANES, LANES), 1)
    bvec = jnp.full(dvec.shape, tab_ref[0, h], F32)
    for b in range(1, REL_BUCKETS):
        bvec = jnp.where(dvec >= T5_THR[b], tab_ref[b, h], bvec)
    d = jnp.clip(d_ref[...], 0, LANES - 1)
    table = jnp.broadcast_to(bvec[0:1], d.shape)
    o_ref[...] = jnp.take_along_axis(table, d, axis=1) + m_ref[...]


def bias_from_dist(rel_bias, dist, addmask):
    r = dist.shape[0]
    assert FAR_DIST < LANES
    return pl.pallas_call(
        _bias_body, grid=(N_HEADS,),
        in_specs=[pl.BlockSpec(memory_space=pltpu.SMEM), pl.BlockSpec((r, LANES), lambda h: (0, 0)),
                  pl.BlockSpec((r, LANES), lambda h: (0, 0))],
        out_specs=pl.BlockSpec((None, r, LANES), lambda h: (h, 0, 0)),
        out_shape=jax.ShapeDtypeStruct((N_HEADS, r, LANES), F32),
        compiler_params=_params("parallel"), name="rel_bias")(rel_bias, dist, addmask)


CMP_PAGES = 16


def _cmp1_body(pt_ref, *refs):
    pages = refs[:CMP_PAGES]
    perm = refs[CMP_PAGES][...]
    w_refs = refs[CMP_PAGES + 1:CMP_PAGES + 3]
    o_ref = refs[CMP_PAGES + 3]
    cpp = PAGE_SIZE // CMP_STRIDE
    for kind in range(2):
        w = w_refs[kind][...]
        for kv in range(KV):
            c0 = (kind * KV + kv) * HEAD_DIM
            xs = [_dot(perm, pg[:, c0:c0 + HEAD_DIM].astype(BF16)) for pg in pages]
            pieces = [jnp.concatenate([x[r * cpp:(r + 1) * cpp] for x in xs], axis=0).astype(BF16)
                      for r in range(CMP_STRIDE)]
            o_ref[kind * KV + kv] = _dot(jnp.concatenate(pieces, axis=1), w)


def compress_stage1(rows3, page_table, w1k, w1v):
    nb_seq, n_pages = page_table.shape
    steps = n_pages // CMP_PAGES
    cpp = PAGE_SIZE // CMP_STRIDE
    width = 2 * KV * HEAD_DIM

    def page_spec(i):
        return pl.BlockSpec((None, PAGE_SIZE, width), lambda b, s, pt: (pt[b, s * CMP_PAGES + i], 0, 0))

    wspec = pl.BlockSpec((CMP_STRIDE * HEAD_DIM, 2 * HEAD_DIM), lambda b, s, pt: (0, 0))
    src = np.arange(PAGE_SIZE)
    perm = np.zeros((PAGE_SIZE, PAGE_SIZE), np.float32)
    perm[(src % CMP_STRIDE) * cpp + src // CMP_STRIDE, src] = 1.0
    perm = jnp.asarray(perm, BF16)
    grid_spec = pltpu.PrefetchScalarGridSpec(
        num_scalar_prefetch=1, grid=(nb_seq, steps),
        in_specs=[page_spec(i) for i in range(CMP_PAGES)]
        + [pl.BlockSpec((PAGE_SIZE, PAGE_SIZE), lambda b, s, pt: (0, 0)), wspec, wspec],
        out_specs=pl.BlockSpec((None, 2 * KV, CMP_PAGES * cpp, 2 * HEAD_DIM), lambda b, s, pt: (b, 0, s, 0)))
    return pl.pallas_call(
        _cmp1_body, grid_spec=grid_spec,
        out_shape=jax.ShapeDtypeStruct((nb_seq, 2 * KV, n_pages * cpp, 2 * HEAD_DIM), F32),
        compiler_params=_params("parallel", "arbitrary"), name="compress1")(
            page_table, *([rows3] * CMP_PAGES), perm, w1k, w1v)


def _cmp1_paged_body(pt_ref, *refs):
    cmp_pages = [r.reshape(PAGE_SIZE * SUBLANES, HEAD_DIM) for r in refs[:CMP_PAGES]]
    w_refs = refs[CMP_PAGES:CMP_PAGES + 2]
    o_ref = refs[CMP_PAGES + 2]
    cpp = PAGE_SIZE // CMP_STRIDE
    for kind in range(2):
        w = w_refs[kind][...]
        for kv in range(KV):
            c = kind * KV + kv
            xs = [jnp.swapaxes(pg[pl.ds(c, PAGE_SIZE, stride=SUBLANES), :].reshape(cpp, CMP_STRIDE, HEAD_DIM), 0, 1)
                  for pg in cmp_pages]
            pieces = [jnp.concatenate([x[r] for x in xs], axis=0).astype(BF16) for r in range(CMP_STRIDE)]
            o_ref[c] = _dot(jnp.concatenate(pieces, axis=1), w)


def compress_stage1_paged(cache5, page_table, w1k, w1v):
    nb_seq, n_pages = page_table.shape
    steps = n_pages // CMP_PAGES
    cpp = PAGE_SIZE // CMP_STRIDE
    assert cache5.shape[2:] == (2, 2 * KV, HEAD_DIM) and 2 * KV == SUBLANES

    def group_spec(i):
        return pl.BlockSpec((None, PAGE_SIZE, None, SUBLANES, HEAD_DIM),
                            lambda b, s, pt: (pt[b, s * CMP_PAGES + i], 0, 0, 0, 0))

    wspec = pl.BlockSpec((CMP_STRIDE * HEAD_DIM, 2 * HEAD_DIM), lambda b, s, pt: (0, 0))
    grid_spec = pltpu.PrefetchScalarGridSpec(
        num_scalar_prefetch=1, grid=(nb_seq, steps),
        in_specs=[group_spec(i) for i in range(CMP_PAGES)] + [wspec, wspec],
        out_specs=pl.BlockSpec((None, 2 * KV, CMP_PAGES * cpp, 2 * HEAD_DIM), lambda b, s, pt: (b, 0, s, 0)))
    return pl.pallas_call(
        _cmp1_paged_body, grid_spec=grid_spec,
        out_shape=jax.ShapeDtypeStruct((nb_seq, 2 * KV, n_pages * cpp, 2 * HEAD_DIM), F32),
        compiler_params=_params("parallel", "arbitrary"), name="compress1_paged")(
            page_table, *([cache5] * CMP_PAGES), w1k, w1v)


def _cmp2_body(ab_ref, pe_ref, w1_ref, w2_ref, o_ref):
    nch = ab_ref.shape[0]
    ab = ab_ref[...]
    c = _dot(pe_ref[...].astype(BF16), w1_ref[...].astype(BF16))
    pre = ab[:, :HEAD_DIM] + pltpu.roll(ab[:, HEAD_DIM:], nch - 1, 0) + c[0:1]
    y = _dot((pre * jax.nn.sigmoid(pre)).astype(BF16), w2_ref[...].astype(BF16))
    rowi = lax.broadcasted_iota(I32, y.shape, 0)
    o_ref[...] = jnp.where(rowi < nch - 1, y, 0.0)


def compress_stage2(ab, pe, w1, w2):
    nb_seq, _, nch, _ = ab.shape
    return pl.pallas_call(
        _cmp2_body, grid=(nb_seq, 2 * KV),
        in_specs=[pl.BlockSpec((None, None, nch, 2 * HEAD_DIM), lambda b, j: (b, j, 0, 0)),
                  pl.BlockSpec((None, SUBLANES, CMP_BLOCK * HEAD_DIM), lambda b, j: (j // KV, 0, 0)),
                  pl.BlockSpec((None, CMP_BLOCK * HEAD_DIM, HEAD_DIM), lambda b, j: (j // KV, 0, 0)),
                  pl.BlockSpec((None, HEAD_DIM, HEAD_DIM), lambda b, j: (j // KV, 0, 0))],
        out_specs=pl.BlockSpec((None, None, nch, HEAD_DIM), lambda b, j: (b, j, 0, 0)),
        out_shape=jax.ShapeDtypeStruct((nb_seq, 2 * KV, nch, HEAD_DIM), F32),
        compiler_params=_params("parallel", "parallel"), name="compress2")(ab, pe, w1, w2)


def compress(rows3, page_table, cw):
    ab = compress_stage1(rows3, page_table, cw["w1k"], cw["w1v"])
    return compress_stage2(ab, cw["pe"], cw["w1"], cw["w2"])


def compress_paged(cache3, page_table, cw):
    ab = compress_stage1_paged(cache3, page_table, cw["w1k"], cw["w1v"])
    return compress_stage2(ab, cw["pe"], cw["w1"], cw["w2"])


def _flash_prob(s, mask, m_ref, l_ref, rows):
    m_prev = m_ref[rows, :]
    if mask is not None:
        s = jnp.where(mask, s, MASK_NEG)
    m_new = jnp.maximum(m_prev, jnp.max(s, axis=1, keepdims=True))
    alpha = jnp.exp(m_prev - m_new)
    m_wide = m_new if s.shape[1] == LANES else jnp.concatenate([m_new] * (s.shape[1] // LANES), axis=1)
    p = jnp.exp(s - m_wide)
    l_ref[rows, :] = alpha * l_ref[rows, :] + jnp.sum(p, axis=1, keepdims=True)
    m_ref[rows, :] = m_new
    return p.astype(BF16), alpha


def _flash_update(s, mask, v, m_ref, l_ref, acc_ref, rows):
    p, alpha = _flash_prob(s, mask, m_ref, l_ref, rows)
    acc_ref[rows, :] = alpha * acc_ref[rows, :] + _dot(p, v)


def _flash_init(m_ref, l_ref, acc_ref):
    m_ref[...] = jnp.full(m_ref.shape, NEG_INF, F32)
    l_ref[...] = jnp.zeros_like(l_ref)
    acc_ref[...] = jnp.zeros_like(acc_ref)


def _rank_select(score, eligible, blk, n_blocks, n_top, row_of):
    def body(j, cnt):
        rowj = row_of(j)
        beats = (rowj > score) | ((rowj == score) & (blk > j))
        return cnt + beats.astype(I32)
    cnt = lax.fori_loop(0, n_blocks, body, jnp.zeros(score.shape, I32))
    return jnp.where((cnt < n_top) & eligible, 1.0, 0.0)


def _extract_top(score, blk, n_top):
    def body(i, state):
        sc, sel = state
        m = jnp.max(sc, axis=0, keepdims=True)
        first = jnp.min(jnp.where(sc == m, blk, jnp.iinfo(jnp.int32).max), axis=0, keepdims=True)
        hit = (blk == first) & (m > -jnp.inf)
        return jnp.where(hit, -jnp.inf, sc), jnp.where(hit, 1.0, sel)
    return lax.fori_loop(0, n_top, body, (score, jnp.zeros(score.shape, F32)))[1]


NSA_HP = 2


def _nsa_prompt_body(q_ref, ks_ref, vs_ref, kw_ref, vw_ref, kc_ref, vc_ref, g_ref, bc_ref, bt_ref, wbt_ref,
                      ovl_ref, exp_ref,
                      o_ref, m_s, l_s, a_s, m_w, l_w, a_w, mk_ref, sc_ref, *, n_cmp, n_blk):
    qi = pl.program_id(2)
    rows = G * TQ
    gw = G * HEAD_DIM
    row5 = lax.broadcasted_iota(I32, (rows, LANES), 0)
    lane5 = lax.broadcasted_iota(I32, (rows, LANES), 1)
    tpos5 = qi * TQ + (row5 & (TQ - 1))
    mask_c = (tpos5 >= lane5 * CMP_STRIDE + (CMP_BLOCK - 1)) & (lane5 < n_cmp)
    nbr = ovl_ref.shape[0]
    blk = lax.broadcasted_iota(I32, (nbr, TQ), 0)
    tq_l = qi * TQ + lax.broadcasted_iota(I32, (nbr, TQ), 1)
    cur = lax.shift_right_arithmetic(tq_l, int(math.log2(SEL_BLOCK)))
    eligible = (blk * SEL_BLOCK <= tq_l) & (blk < n_blk)
    forced = (blk == 0) | (blk == cur) | (blk == cur - 1)
    row1 = lax.broadcasted_iota(I32, (TQ, TQ), 0)
    lane1 = lax.broadcasted_iota(I32, (TQ, TQ), 1)
    off_d = pl.multiple_of(qi * TQ, TQ)

    qs, o_cs = [], []
    for hp in range(NSA_HP):
        q = jnp.concatenate([q_ref[:, hp * gw + g * HEAD_DIM:hp * gw + (g + 1) * HEAD_DIM] for g in range(G)],
                            axis=0)
        qs.append(q)
        s = _dot_nt(q, kc_ref[hp].astype(BF16))
        s = s + jnp.concatenate([bc_ref[hp * G + g] for g in range(G)], axis=0)
        pb = _masked_softmax(s, mask_c).astype(BF16)
        o_cs.append(_dot(pb, vc_ref[hp].astype(BF16)))
        imp = _dot_nt(ovl_ref[...], pb[0:TQ])
        for g in range(1, G):
            imp = imp + _dot_nt(ovl_ref[...], pb[g * TQ:(g + 1) * TQ])
        score = jnp.where(eligible, imp + jnp.where(forced, FORCE_BONUS, 0.0), -jnp.inf)
        sc_ref[hp] = score
        sel_t = _rank_select(score, eligible, blk, n_blk, min(SEL_TOPK, n_blk),
                             lambda j, hp=hp: sc_ref[hp, pl.ds(j, 1), :])
        unsel = jnp.concatenate([1.0 - sel_t, jnp.zeros((LANES - nbr, TQ), F32)], axis=0)
        mk_ref[hp] = _dot(unsel.T.astype(BF16), exp_ref[...])
        mk_ref[hp, :, pl.ds(off_d, TQ)] = jnp.where(lane1 <= row1, mk_ref[hp, :, pl.ds(off_d, TQ)], MASK_NEG)

    @pl.when(qi + 1 < pl.num_programs(2))
    def _():
        for hp in range(NSA_HP):
            mk_ref[hp, :, pl.ds(pl.multiple_of((qi + 1) * TQ, TQ), TQ)] = jnp.full((TQ, TQ), MASK_NEG, F32)

    kt2 = 2 * TQ

    def tile_pair(ref, hp, j, index_of):
        tiles = []
        for i in range(2):
            off = pl.multiple_of(index_of(qi - (2 * j + i)) * TQ, TQ)
            tiles.append(jnp.concatenate([ref[hp * G + g, pl.ds(off, TQ), :] for g in range(G)], axis=0))
        return jnp.concatenate(tiles, axis=1)

    _flash_init(m_s, l_s, a_s)
    _flash_init(m_w, l_w, a_w)
    head_rows = [pl.ds(hp * rows, rows) for hp in range(NSA_HP)]
    head_cols = [slice(hp * HEAD_DIM, (hp + 1) * HEAD_DIM) for hp in range(NSA_HP)]

    def pair_step(j, with_win):
        off = pl.multiple_of(j * kt2, kt2)
        s_s = [_dot_nt(qs[hp], ks_ref[pl.ds(off, kt2), head_cols[hp]]) for hp in range(NSA_HP)]
        if with_win:
            s_w = [_dot_nt(qs[hp], kw_ref[pl.ds(off, kt2), head_cols[hp]]) for hp in range(NSA_HP)]
        sel, win = [], []
        for hp in range(NSA_HP):
            bias_s = tile_pair(bt_ref, hp, j, lambda d: jnp.clip(d, 0, 2))
            key_mask = jnp.concatenate([mk_ref[hp, :, pl.ds(off, kt2)]] * G, axis=0)
            sel.append(_flash_prob(s_s[hp] + bias_s + key_mask, None, m_s, l_s, head_rows[hp]))
            if with_win:
                bias_w = tile_pair(wbt_ref, hp, j,
                                   lambda d: jnp.where(d < 0, WIN_TILES - 1, jnp.minimum(d, WIN_TILES - 1)))
                win.append(_flash_prob(s_w[hp] + bias_w, None, m_w, l_w, head_rows[hp]))
        for hp in range(NSA_HP):
            p, alpha = sel[hp]
            a_s[head_rows[hp], :] = alpha * a_s[head_rows[hp], :] + _dot(p, vs_ref[pl.ds(off, kt2), head_cols[hp]])
            if with_win:
                p, alpha = win[hp]
                a_w[head_rows[hp], :] = (alpha * a_w[head_rows[hp], :]
                                         + _dot(p, vw_ref[pl.ds(off, kt2), head_cols[hp]]))

    def sel_only(j, c):
        pair_step(j, False)
        return c

    def sel_and_win(j, c):
        pair_step(j, True)
        return c

    lo = lax.shift_right_arithmetic(jnp.maximum(qi - WINDOW // TQ, 0), 1)
    hi = lax.shift_right_arithmetic(qi, 1) + 1
    lax.fori_loop(0, lo, sel_only, 0)
    lax.fori_loop(lo, hi, sel_and_win, 0)

    for hp in range(NSA_HP):
        o_s = a_s[head_rows[hp], :] * _safe_inv(l_s[head_rows[hp], :])
        o_w = a_w[head_rows[hp], :] * _safe_inv(l_w[head_rows[hp], :])
        gt = g_ref[:, hp * LANES:(hp + 1) * LANES]
        for g in range(G):
            r = slice(g * TQ, (g + 1) * TQ)
            out = (gt[:, 3 * g:3 * g + 1] * o_cs[hp][r] + gt[:, 3 * g + 1:3 * g + 2] * o_s[r]
                   + gt[:, 3 * g + 2:3 * g + 3] * o_w[r])
            o_ref[:, hp * gw + g * HEAD_DIM:hp * gw + (g + 1) * HEAD_DIM] = out.astype(o_ref.dtype)


def nsa_prompt_attention(q, kv4, kwin, gates, kcb, bias_all, b, t):
    nq = t // TQ
    n_chunks = t // CMP_STRIDE
    n_cmp = n_chunks - CMP_BLOCK // CMP_STRIDE + 1
    n_blk = -(-t // SEL_BLOCK)
    assert n_chunks == LANES and n_blk <= LANES and t % TQ == 0 and TQ >= FAR_DIST
    assert WIN_TILES % 3 == 0 and TQ == LANES and KV % NSA_HP == 0
    cmp_start = np.arange(LANES) * CMP_STRIDE
    jst = np.arange(LANES) * SEL_BLOCK
    overlap = ((cmp_start[:, None] < jst[None, :] + SEL_BLOCK) & (cmp_start[:, None] + CMP_BLOCK > jst[None, :])
               & (np.arange(LANES)[:, None] < n_cmp) & (np.arange(LANES)[None, :] < n_blk))
    nbr = -(-n_blk // (2 * SUBLANES)) * 2 * SUBLANES
    ovl_t = jnp.asarray(overlap.T[:nbr], BF16)
    expand = jnp.asarray(np.where(np.arange(LANES)[:, None] == (np.arange(t)[None, :] // SEL_BLOCK), MASK_NEG, 0.0),
                         BF16)
    hw = NSA_HP * HEAD_DIM
    hg = NSA_HP * G
    colblk = lambda base: (lambda bb, kp, qi: (bb, base // NSA_HP + kp))
    rowblk = lambda bb, kp, qi: (bb * nq + qi, kp)
    in_specs = [
        pl.BlockSpec((TQ, hg * HEAD_DIM), rowblk),
        pl.BlockSpec((t, hw), colblk(2 * KV)), pl.BlockSpec((t, hw), colblk(3 * KV)),
        pl.BlockSpec((t, hw), colblk(0)), pl.BlockSpec((t, hw), colblk(KV)),
        pl.BlockSpec((None, NSA_HP, n_chunks, HEAD_DIM), lambda bb, kp, qi: (bb, kp, 0, 0)),
        pl.BlockSpec((None, NSA_HP, n_chunks, HEAD_DIM), lambda bb, kp, qi: (bb, KV // NSA_HP + kp, 0, 0)),
        pl.BlockSpec((TQ, NSA_HP * LANES), rowblk),
        pl.BlockSpec((hg, TQ, LANES), lambda bb, kp, qi: (kp, WIN_TILES + 3 + qi, 0)),
        pl.BlockSpec((hg, 3 * TQ, LANES), lambda bb, kp, qi: (kp, WIN_TILES // 3, 0)),
        pl.BlockSpec((hg, WIN_TILES * TQ, LANES), lambda bb, kp, qi: (kp, 0, 0)),
        pl.BlockSpec((nbr, LANES), lambda bb, kp, qi: (0, 0)),
        pl.BlockSpec((LANES, t), lambda bb, kp, qi: (0, 0)),
    ]
    rows = NSA_HP * G * TQ
    scratch = [pltpu.VMEM((rows, LANES), F32) for _ in range(6)] + [pltpu.VMEM((NSA_HP, TQ, t), F32),
                                                                   pltpu.VMEM((NSA_HP, nbr, TQ), F32)]
    return pl.pallas_call(
        functools.partial(_nsa_prompt_body, n_cmp=n_cmp, n_blk=n_blk), grid=(b, KV // NSA_HP, nq),
        in_specs=in_specs, out_specs=pl.BlockSpec((TQ, hg * HEAD_DIM), rowblk),
        out_shape=jax.ShapeDtypeStruct((b * t, N_HEADS * HEAD_DIM), BF16),
        scratch_shapes=scratch,
        compiler_params=_params("parallel", "parallel", "arbitrary"), name="nsa_prompt_attn")(
            q, kv4, kv4, kwin, kwin, kcb, kcb, gates, bias_all, bias_all, bias_all, ovl_t, expand)


def _stick_pre(z, mask):
    sp = jnp.log(1.0 + jnp.exp(-jnp.abs(z)))
    log_beta = jnp.minimum(z, 0.0) - sp
    rest = jnp.maximum(z, 0.0) + sp
    if mask is not None:
        rest = jnp.where(mask, rest, 0.0)
    hi = rest.astype(BF16)
    lo = (rest - hi.astype(F32)).astype(BF16)
    return log_beta, jnp.sum(rest, axis=1, keepdims=True), jnp.concatenate([hi, lo], axis=0)


def _stick_post(log_beta, l2, carry, mask):
    n = log_beta.shape[0]
    a = jnp.exp(log_beta - (l2[0:n] + l2[n:2 * n] + carry))
    return a if mask is None else jnp.where(mask, a, 0.0)


SB_HEADS = 8
SB_DONE = 110.0


def _sb_prompt_body(q_ref, k_ref, v_ref, u_ref, o_ref, carry_ref, acc_ref):
    qi = pl.program_id(2)
    row = lax.broadcasted_iota(I32, (TQ, TQ), 0)
    lane = lax.broadcasted_iota(I32, (TQ, TQ), 1)

    def tile(kt, mask, first):
        off = pl.multiple_of(kt * TQ, TQ)
        cols = [slice(h * HEAD_DIM, (h + 1) * HEAD_DIM) for h in range(SB_HEADS)]
        zs = [_dot_nt(q_ref[:, c], k_ref[pl.ds(off, TQ), c]) for c in cols]
        pre = [_stick_pre(z, mask) for z in zs]
        l3s = [_dot(p[2], u_ref[...]) for p in pre]
        ws = [_stick_post(p[0], l3, 0.0 if first else carry_ref[h], mask).astype(BF16)
              for h, (p, l3) in enumerate(zip(pre, l3s))]
        pvs = [_dot(w, v_ref[pl.ds(off, TQ), c]) for w, c in zip(ws, cols)]
        for h in range(SB_HEADS):
            tot = pre[h][1]
            acc_ref[h] = pvs[h] if first else acc_ref[h] + pvs[h]
            carry_ref[h] = jnp.broadcast_to(tot, (TQ, TQ)) if first else carry_ref[h] + tot

    tile(qi, lane < row, True)

    def unfinished():
        c = carry_ref[0]
        for h in range(1, SB_HEADS):
            c = jnp.minimum(c, carry_ref[h])
        return (jnp.min(c) < SB_DONE).astype(I32)

    def step(state):
        i, _ = state
        tile(qi - 1 - i, None, False)
        return i + 1, unfinished()

    lax.while_loop(lambda state: (state[0] < qi) & (state[1] > 0), step, (jnp.int32(0), unfinished()))
    for h in range(SB_HEADS):
        o_ref[:, h * HEAD_DIM:(h + 1) * HEAD_DIM] = acc_ref[h].astype(o_ref.dtype)


def _suffix_matrix():
    j = np.arange(TQ)
    return jnp.asarray(j[:, None] > j[None, :], BF16)


def sb_prompt_attention(q, kvp, b, t):
    nq = t // TQ
    hg = N_HEADS // SB_HEADS
    w = SB_HEADS * HEAD_DIM
    rowblk = lambda bb, h, qi: (bb * nq + qi, h)
    return pl.pallas_call(
        _sb_prompt_body, grid=(b, hg, nq),
        in_specs=[pl.BlockSpec((TQ, w), rowblk),
                  pl.BlockSpec((t, w), lambda bb, h, qi: (bb, h)),
                  pl.BlockSpec((t, w), lambda bb, h, qi: (bb, hg + h)),
                  pl.BlockSpec((TQ, TQ), lambda bb, h, qi: (0, 0))],
        out_specs=pl.BlockSpec((TQ, w), rowblk),
        out_shape=jax.ShapeDtypeStruct((b * t, N_HEADS * HEAD_DIM), BF16),
        scratch_shapes=[pltpu.VMEM((SB_HEADS, TQ, TQ), F32), pltpu.VMEM((SB_HEADS, TQ, HEAD_DIM), F32)],
        compiler_params=_params("parallel", "parallel", "arbitrary"), name="sb_prompt_attn")(
            q, kvp, kvp, _suffix_matrix())


def _nsa_sample_select_body(q_ref, kcb_ref, bc_ref, mc_ref, ovl_ref, fadd_ref, elig_ref, oc_ref, sel_ref, *, n_blk):
    nbp = kcb_ref.shape[1]
    grp = G * T8
    ps = []
    for kv in range(KV):
        qk = jnp.concatenate([q_ref[:, (kv * G + g) * HEAD_DIM:(kv * G + g + 1) * HEAD_DIM] for g in range(G)],
                             axis=0)
        s = _dot_nt(qk, kcb_ref[kv].astype(BF16)) + bc_ref[kv * grp:(kv + 1) * grp, :]
        p = _masked_softmax(s, mc_ref[...] > 0.5)
        oc_ref[kv * grp:(kv + 1) * grp, :] = _dot(p.astype(BF16), kcb_ref[KV + kv].astype(BF16))
        ps.append(p)
    pad = jnp.zeros((LANES - KV * T8, nbp), F32)
    imp = None
    for g in range(G):
        xg = jnp.concatenate([ps[kv][g * T8:(g + 1) * T8] for kv in range(KV)] + [pad], axis=0)
        part = _dot_nt(ovl_ref[...], xg.astype(BF16))
        imp = part if imp is None else imp + part
    score = jnp.where(elig_ref[...] > 0.5, imp + fadd_ref[...], -jnp.inf)
    blk = lax.broadcasted_iota(I32, score.shape, 0)
    sel_ref[...] = _extract_top(score, blk, min(SEL_TOPK, n_blk))


SEL_KEYS = 2048
SEL_PAGES = SEL_KEYS // PAGE_SIZE


def _nsa_sample_attn_body(pt_ref, q_ref, *refs, n_steps, t_new, past):
    pages = [r.reshape(PAGE_SIZE * SUBLANES, HEAD_DIM) for r in refs[:SEL_PAGES]]
    (sel_ref, new_ref, win_ref, wnew_ref, g_ref, oc_ref, bl_ref, bf_ref, bn_ref, bw_ref, ex_ref,
     o_ref, m_s, l_s, a_s) = refs[SEL_PAGES:]

    def page_rows(c):
        return jnp.concatenate([pg[pl.ds(c, PAGE_SIZE, stride=SUBLANES), :] for pg in pages], axis=0).astype(BF16)

    step = pl.program_id(1)
    grp = G * T8
    kvw = KV * HEAD_DIM
    wb = win_ref.shape[0] // (2 * KV)
    is_last = step == n_steps - 1

    @pl.when(step == 0)
    def _():
        _flash_init(m_s, l_s, a_s)

    def q_of(kv):
        return jnp.concatenate([q_ref[:, (kv * G + g) * HEAD_DIM:(kv * G + g + 1) * HEAD_DIM] for g in range(G)],
                               axis=0)

    def pad_keys(x):
        return jnp.concatenate([x, jnp.zeros((TQ - x.shape[0], x.shape[1]), x.dtype)], axis=0).astype(BF16)

    def rows_of_blocks(piece):
        return jnp.concatenate([piece, jnp.zeros((LANES - piece.shape[0], LANES), F32)], axis=0).T

    bps = SEL_KEYS // SEL_BLOCK
    piece = sel_ref[pl.ds(pl.multiple_of(step * bps, bps), bps), :]
    mk = _dot(rows_of_blocks(piece).astype(BF16), ex_ref[...])

    kv_rows = [pl.ds(kv * grp, grp) for kv in range(KV)]
    scores = [_dot_nt(q_of(kv), page_rows(kv)) for kv in range(KV)]
    probs = []
    for kv in range(KV):
        bias = jnp.where(is_last, bl_ref[kv_rows[kv], :], bf_ref[kv_rows[kv], 0:1])
        mask = jnp.concatenate([mk[kv * T8:(kv + 1) * T8]] * G, axis=0) > 0.5
        probs.append(_flash_prob(scores[kv] + bias, mask, m_s, l_s, kv_rows[kv]))
    for kv in range(KV):
        p, alpha = probs[kv]
        a_s[kv_rows[kv], :] = alpha * a_s[kv_rows[kv], :] + _dot(p, page_rows(KV + kv))

    @pl.when(is_last)
    def _():
        lane = lax.broadcasted_iota(I32, (grp, LANES), 1)
        t_row = lax.broadcasted_iota(I32, (grp, LANES), 0) & (T8 - 1)
        lane_w = lax.broadcasted_iota(I32, (grp, wb), 1)
        t_row_w = lax.broadcasted_iota(I32, (grp, wb), 0) & (T8 - 1)
        new_blk = past // SEL_BLOCK
        picked_all = rows_of_blocks(sel_ref[new_blk:new_blk + SUBLANES, :])
        gt = g_ref[...]
        for kv in range(KV):
            rows = pl.ds(kv * grp, grp)
            qk = q_of(kv)
            kn = pad_keys(new_ref[:, 2 * kvw + kv * HEAD_DIM:2 * kvw + (kv + 1) * HEAD_DIM])
            vn = pad_keys(new_ref[:, 3 * kvw + kv * HEAD_DIM:3 * kvw + (kv + 1) * HEAD_DIM])
            s = _dot_nt(qk, kn) + bn_ref[rows, :]
            picked = jnp.concatenate([picked_all[kv * T8:(kv + 1) * T8, 0:1]] * G, axis=0) > 0.5
            mask = (lane <= t_row) & (lane < t_new) & picked
            _flash_update(s, mask, vn, m_s, l_s, a_s, rows)
            o_s = a_s[rows, :] * _safe_inv(l_s[rows, :])
            kw = win_ref[pl.ds(kv, wb, stride=2 * KV), :].astype(BF16)
            vw = win_ref[pl.ds(KV + kv, wb, stride=2 * KV), :].astype(BF16)
            kwn = pad_keys(wnew_ref[:, kv * HEAD_DIM:(kv + 1) * HEAD_DIM])
            vwn = pad_keys(wnew_ref[:, kvw + kv * HEAD_DIM:kvw + (kv + 1) * HEAD_DIM])
            s1 = _dot_nt(qk, kw) + bw_ref[rows, 0:wb]
            s2 = _dot_nt(qk, kwn) + bw_ref[rows, wb:wb + LANES]
            d1 = wb + t_row_w - lane_w
            mask1 = (d1 >= 0) & (d1 < WINDOW)
            d2 = t_row - lane
            mask2 = (d2 >= 0) & (d2 < WINDOW) & (lane < t_new)
            m = jnp.maximum(jnp.max(jnp.where(mask1, s1, NEG_INF), axis=1, keepdims=True),
                            jnp.max(jnp.where(mask2, s2, NEG_INF), axis=1, keepdims=True))
            e1 = jnp.where(mask1, jnp.exp(s1 - m), 0.0)
            e2 = jnp.where(mask2, jnp.exp(s2 - m), 0.0)
            l = jnp.sum(e1, axis=1, keepdims=True) + jnp.sum(e2, axis=1, keepdims=True)
            o_w = (_dot(e1.astype(BF16), vw) + _dot(e2.astype(BF16), vwn)) * _safe_inv(l)
            o_c = oc_ref[rows, :]
            for g in range(G):
                r = slice(g * T8, (g + 1) * T8)
                c = kv * LANES + 3 * g
                out = gt[:, c:c + 1] * o_c[r] + gt[:, c + 1:c + 2] * o_s[r] + gt[:, c + 2:c + 3] * o_w[r]
                o_ref[:, (kv * G + g) * HEAD_DIM:(kv * G + g + 1) * HEAD_DIM] = out.astype(o_ref.dtype)


def nsa_sample_attention(q8, cache5, page_table, kcb, new8, win3, wnew8, gates8, bias_s, past, t_new):
    bd = q8.shape[0]
    n_chunks = kcb.shape[2]
    n_cmp = n_chunks - CMP_BLOCK // CMP_STRIDE + 1
    total = past + t_new
    n_blk = -(-total // SEL_BLOCK)
    nblkp = -(-(n_blk + SUBLANES) // (2 * SUBLANES)) * 2 * SUBLANES
    wb = win3.shape[1] // (2 * KV)
    n_steps = past // SEL_KEYS
    assert past % SEL_KEYS == 0 and SEL_KEYS % SEL_BLOCK == 0 and t_new <= T8 and PAGE_SIZE > FAR_DIST
    assert (total // CMP_STRIDE) == n_chunks and wb % LANES == 0 and SEL_KEYS // SEL_BLOCK <= LANES

    pos_q = past + np.arange(T8)
    cmp_start = np.arange(n_chunks) * CMP_STRIDE
    cmp_end = cmp_start + CMP_BLOCK - 1
    mask_c = ((pos_q[:, None] >= cmp_end[None, :]) & (np.arange(n_chunks)[None, :] < n_cmp))
    mask_c = jnp.asarray(np.tile(mask_c, (G, 1)), F32)
    jst = np.arange(nblkp) * SEL_BLOCK
    overlap = ((cmp_start[:, None] < jst[None, :] + SEL_BLOCK) & (cmp_start[:, None] + CMP_BLOCK > jst[None, :])
               & (np.arange(n_chunks)[:, None] < n_cmp) & (np.arange(nblkp)[None, :] < n_blk))
    ovl_t = jnp.asarray(overlap.T, BF16)
    blk = np.arange(nblkp)[:, None]
    pq = np.tile(pos_q, KV)[None, :]
    tvalid = np.tile(np.arange(T8) < t_new, KV)[None, :]
    cur = pq // SEL_BLOCK
    elig = (blk * SEL_BLOCK <= pq) & (blk < n_blk) & tvalid
    forced = (blk == 0) | (blk == cur) | (blk == cur - 1)
    pad_l = LANES - KV * T8
    elig = jnp.asarray(np.pad(elig, ((0, 0), (0, pad_l))), F32)
    fadd = jnp.asarray(np.pad(np.where(forced, FORCE_BONUS, 0.0), ((0, 0), (0, pad_l))), F32)

    grp_all = KV * G * T8
    o_c, sel = pl.pallas_call(
        functools.partial(_nsa_sample_select_body, n_blk=n_blk), grid=(bd,),
        in_specs=[pl.BlockSpec((None, T8, N_HEADS * HEAD_DIM), lambda b: (b, 0, 0)),
                  pl.BlockSpec((None, 2 * KV, n_chunks, HEAD_DIM), lambda b: (b, 0, 0, 0)),
                  pl.BlockSpec((grp_all, n_chunks), lambda b: (0, 0)),
                  pl.BlockSpec((G * T8, n_chunks), lambda b: (0, 0)),
                  pl.BlockSpec((nblkp, n_chunks), lambda b: (0, 0)),
                  pl.BlockSpec((nblkp, LANES), lambda b: (0, 0)),
                  pl.BlockSpec((nblkp, LANES), lambda b: (0, 0))],
        out_specs=[pl.BlockSpec((None, grp_all, HEAD_DIM), lambda b: (b, 0, 0)),
                   pl.BlockSpec((None, nblkp, LANES), lambda b: (b, 0, 0))],
        out_shape=[jax.ShapeDtypeStruct((bd, grp_all, HEAD_DIM), F32),
                   jax.ShapeDtypeStruct((bd, nblkp, LANES), F32)],
        compiler_params=_params("parallel"), name="nsa_sample_select")(
            q8, kcb, bias_s["cmp"], mask_c, ovl_t, fadd, elig)

    kvw = KV * HEAD_DIM
    const2 = lambda b, s, pt: (0, 0)
    per_b = lambda b, s, pt: (b, 0, 0)
    expand = jnp.asarray(np.arange(LANES)[:, None] == (np.arange(SEL_KEYS)[None, :] // SEL_BLOCK), BF16)
    bias_last = jnp.concatenate([bias_s["far"]] * (SEL_KEYS // LANES - 1) + [bias_s["last"]], axis=1)

    def page_spec(i):
        return pl.BlockSpec((None, PAGE_SIZE, None, SUBLANES, HEAD_DIM),
                            lambda b, s, pt: (pt[b, s * SEL_PAGES + i], 0, 1, 0, 0))

    grid_spec = pltpu.PrefetchScalarGridSpec(
        num_scalar_prefetch=1, grid=(bd, n_steps),
        in_specs=[pl.BlockSpec((None, T8, N_HEADS * HEAD_DIM), per_b)] + [page_spec(i) for i in range(SEL_PAGES)]
        + [pl.BlockSpec((None, nblkp, LANES), per_b),
                  pl.BlockSpec((None, T8, 4 * kvw), per_b),
                  pl.BlockSpec((None, wb * 2 * KV, HEAD_DIM), per_b),
                  pl.BlockSpec((None, T8, 2 * kvw), per_b),
                  pl.BlockSpec((None, T8, KV * LANES), per_b),
                  pl.BlockSpec((None, grp_all, HEAD_DIM), per_b),
                  pl.BlockSpec((grp_all, SEL_KEYS), const2), pl.BlockSpec((grp_all, LANES), const2),
                  pl.BlockSpec((grp_all, LANES), const2), pl.BlockSpec((grp_all, wb + LANES), const2),
                  pl.BlockSpec((LANES, SEL_KEYS), const2)],
        out_specs=pl.BlockSpec((None, T8, N_HEADS * HEAD_DIM), per_b),
        scratch_shapes=[pltpu.VMEM((grp_all, LANES), F32) for _ in range(3)])
    return pl.pallas_call(
        functools.partial(_nsa_sample_attn_body, n_steps=n_steps, t_new=t_new, past=past), grid_spec=grid_spec,
        out_shape=jax.ShapeDtypeStruct((bd, T8, N_HEADS * HEAD_DIM), BF16),
        compiler_params=_params("parallel", "arbitrary"), name="nsa_sample_attn")(
            page_table, q8, *([cache5] * SEL_PAGES), sel, new8, win3, wnew8, gates8, o_c,
            bias_last, bias_s["far"], bias_s["new"], bias_s["win"], expand)


SB_PAGES = 4
SB_GROUPS = 2 * N_HEADS // SUBLANES


def _sb_sample_run(q_ref, u, carry_ref, acc_ref, n_sub, k_of, v_of, masks):
    def q_of(h):
        return q_ref[h * T8:(h + 1) * T8, :].astype(BF16)

    zs = [jnp.concatenate([_dot_nt(q_of(h), k_of(j, h)) for h in range(N_HEADS)], axis=0)
          for j in range(n_sub)]
    pre = [_stick_pre(zs[j], masks[j]) for j in range(n_sub)]
    l2s = [_dot(pre[j][2], u) for j in range(n_sub)]
    carry = carry_ref[...]
    ws = [None] * n_sub
    for j in reversed(range(n_sub)):
        ws[j] = _stick_post(pre[j][0], l2s[j], carry, masks[j])
        carry = carry + pre[j][1]
    carry_ref[...] = carry
    for h in range(N_HEADS):
        r = slice(h * T8, (h + 1) * T8)
        out = _dot(ws[0][r].astype(BF16), v_of(0, h))
        for j in range(1, n_sub):
            out = out + _dot(ws[j][r].astype(BF16), v_of(j, h))
        acc_ref[r, :] += out


def _sb_page_rows(refs, rows_per_block):
    blocks = [r.reshape(PAGE_SIZE * rows_per_block, HEAD_DIM) for r in refs]
    per_page = 2 * N_HEADS // rows_per_block

    def head_rows(j, row):
        blk = blocks[j * per_page + row // rows_per_block]
        return blk[pl.ds(row % rows_per_block, PAGE_SIZE, stride=rows_per_block), :].astype(BF16)
    return head_rows


def _sb_sample_head_body(pt_ref, q_ref, *refs, t_new):
    n_grp = SB_PAGES * SB_GROUPS
    head_rows = _sb_page_rows(refs[:n_grp], SUBLANES)
    new_ref, u_ref, acc_ref, carry_ref = refs[n_grp:]
    hw = N_HEADS * HEAD_DIM
    carry_ref[...] = jnp.zeros_like(carry_ref)
    acc_ref[...] = jnp.zeros_like(acc_ref)
    pad = jnp.zeros((TQ - T8, 2 * hw), F32)
    kvn = jnp.concatenate([new_ref[...], pad], axis=0)
    lane = lax.broadcasted_iota(I32, (TQ, TQ), 1)
    t_row = lax.broadcasted_iota(I32, (TQ, TQ), 0) & (T8 - 1)
    _sb_sample_run(q_ref, u_ref[...], carry_ref, acc_ref, 1,
                   lambda j, h: kvn[:, h * HEAD_DIM:(h + 1) * HEAD_DIM].astype(BF16),
                   lambda j, h: kvn[:, hw + h * HEAD_DIM:hw + (h + 1) * HEAD_DIM].astype(BF16),
                   [(lane < t_row) & (lane < t_new)])
    _sb_sample_run(q_ref, u_ref[...], carry_ref, acc_ref, SB_PAGES, lambda j, h: head_rows(j, h),
                   lambda j, h: head_rows(j, N_HEADS + h), [None] * SB_PAGES)


def _sb_sample_tail_body(pt_ref, need_ref, q_ref, *refs, n_steps):
    head_rows = _sb_page_rows(refs[:SB_PAGES], 2 * N_HEADS)
    u_ref, acc_in, carry_in, o_ref, carry_ref, acc_ref = refs[SB_PAGES:]
    step = pl.program_id(1)

    @pl.when(step == 0)
    def _():
        carry_ref[...] = carry_in[...]
        acc_ref[...] = acc_in[...]

    @pl.when((need_ref[pl.program_id(0)] > 0) & (jnp.min(carry_ref[...]) < SB_DONE))
    def _():
        _sb_sample_run(q_ref, u_ref[...], carry_ref, acc_ref, SB_PAGES, lambda j, h: head_rows(j, h),
                       lambda j, h: head_rows(j, N_HEADS + h), [None] * SB_PAGES)

    @pl.when(step == n_steps - 1)
    def _():
        o_ref[...] = acc_ref[...].astype(o_ref.dtype)


def sb_sample_attention(q_rows, cache5, page_table, new8, t_new):
    bd, n_pages = page_table.shape
    hw = N_HEADS * HEAD_DIM
    n_steps = n_pages // SB_PAGES - 1
    assert n_pages % SB_PAGES == 0 and n_steps >= 1 and cache5.shape[2] == SB_GROUPS
    rows = N_HEADS * T8
    grp_block = (None, PAGE_SIZE, None, SUBLANES, HEAD_DIM)
    n_grp = SB_PAGES * SB_GROUPS
    u = _suffix_matrix()

    def head_spec(i, g):
        return pl.BlockSpec(grp_block, lambda b, pt: (pt[b, n_pages - SB_PAGES + i], 0, g, 0, 0))

    acc, carry = pl.pallas_call(
        functools.partial(_sb_sample_head_body, t_new=t_new),
        grid_spec=pltpu.PrefetchScalarGridSpec(
            num_scalar_prefetch=1, grid=(bd,),
            in_specs=[pl.BlockSpec((None, rows, HEAD_DIM), lambda b, pt: (b, 0, 0))]
            + [head_spec(i, g) for i in range(SB_PAGES) for g in range(SB_GROUPS)]
            + [pl.BlockSpec((None, T8, 2 * hw), lambda b, pt: (b, 0, 0)),
               pl.BlockSpec((TQ, TQ), lambda b, pt: (0, 0))],
            out_specs=[pl.BlockSpec((None, rows, HEAD_DIM), lambda b, pt: (b, 0, 0)),
                       pl.BlockSpec((None, rows, LANES), lambda b, pt: (b, 0, 0))]),
        out_shape=[jax.ShapeDtypeStruct((bd, rows, HEAD_DIM), F32), jax.ShapeDtypeStruct((bd, rows, LANES), F32)],
        compiler_params=_params("parallel"), name="sb_sample_head")(
            page_table, q_rows, *([cache5] * n_grp), new8, u)

    need = (jnp.min(carry, axis=(1, 2)) < SB_DONE).astype(I32)

    def tail_spec(i):
        def index(b, s, pt, nd):
            page = jnp.where(nd[b] > 0, n_pages - (s + 2) * SB_PAGES + i, n_pages - SB_PAGES + i)
            return (pt[b, page], 0, 0, 0, 0)
        return pl.BlockSpec((None, PAGE_SIZE, SB_GROUPS, SUBLANES, HEAD_DIM), index)

    per_b = lambda b, s, pt, nd: (b, 0, 0)
    tail = pl.pallas_call(
        functools.partial(_sb_sample_tail_body, n_steps=n_steps),
        grid_spec=pltpu.PrefetchScalarGridSpec(
            num_scalar_prefetch=2, grid=(bd, n_steps),
            in_specs=[pl.BlockSpec((None, rows, HEAD_DIM), per_b)]
            + [tail_spec(i) for i in range(SB_PAGES)]
            + [pl.BlockSpec((TQ, TQ), lambda b, s, pt, nd: (0, 0)),
               pl.BlockSpec((None, rows, HEAD_DIM), per_b), pl.BlockSpec((None, rows, LANES), per_b)],
            out_specs=pl.BlockSpec((None, rows, HEAD_DIM), per_b),
            scratch_shapes=[pltpu.VMEM((rows, LANES), F32), pltpu.VMEM((rows, HEAD_DIM), F32)]),
        out_shape=jax.ShapeDtypeStruct((bd, rows, HEAD_DIM), BF16),
        compiler_params=_params("parallel", "arbitrary"), name="sb_sample_tail")
    return lax.cond(jnp.any(need > 0),
                    lambda: tail(page_table, need, q_rows, *([cache5] * SB_PAGES), u, acc, carry),
                    lambda: acc.astype(BF16))


def _pad_tokens(x, bd, t_new):
    x = x.reshape(bd, t_new, x.shape[-1])
    return jnp.pad(x, ((0, 0), (0, T8 - t_new), (0, 0)))


def _bias_distances(t, past, n_chunks_s, wb):
    r = np.arange(TQ)[:, None]
    c = np.arange(LANES)[None, :]
    win_tiles = np.concatenate([r - c + off * TQ for off in range(WIN_TILES)], axis=0)
    toeplitz = np.concatenate([r - c + off * TQ for off in range(3)], axis=0)
    cmp_p = np.arange(t)[:, None] - (c * CMP_STRIDE + CMP_BLOCK - 1)
    pos_q = past + np.arange(T8)[:, None]
    cmp_s = (pos_q - (np.arange(n_chunks_s)[None, :] * CMP_STRIDE + CMP_BLOCK - 1)).reshape(-1, LANES)
    last = pos_q - (past - PAGE_SIZE + c)
    far = np.full((T8, LANES), 4 * REL_MAX_DIST)
    new = np.arange(T8)[:, None] - c
    win = (wb + np.arange(T8)[:, None] - np.arange(wb + LANES)[None, :]).reshape(-1, LANES)
    parts = [win_tiles, toeplitz, cmp_p, cmp_s, last, far, new, win]
    rows = sum(p.shape[0] for p in parts)
    pad = -rows % LANES
    parts.append(np.zeros((pad, LANES), np.int64))
    offs = np.cumsum([0] + [p.shape[0] for p in parts])
    dist = np.concatenate(parts, axis=0)
    addmask = np.zeros(dist.shape, np.float32)
    addmask[:win_tiles.shape[0]] = np.where((win_tiles >= 0) & (win_tiles < WINDOW), 0.0, NEG_INF)
    return jnp.asarray(dist, I32), jnp.asarray(addmask), offs


def _sample_bias(bias_all, offs, n_chunks_s, wb):
    def rows_kgt(x, width):
        return x.reshape(N_HEADS * T8, width)
    seg = lambda i: bias_all[:, offs[i]:offs[i + 1]]
    return {"cmp": rows_kgt(seg(3), n_chunks_s), "last": rows_kgt(seg(4), LANES), "far": rows_kgt(seg(5), LANES),
            "new": rows_kgt(seg(6), LANES), "win": rows_kgt(seg(7), wb + LANES)}


def kernel(x_prompt, x_sample, cache_nsa, state_nsa_win, cache_sb, page_table, rel_bias, norm_w, final_norm_w,
           nsa_w_in, nsa_w_out, nsa_pe_k, nsa_pe_v, nsa_phi_k1, nsa_phi_k2, nsa_phi_v1, nsa_phi_v2,
           sb_w_in, sb_w_out, ffn_w1, ffn_w3, ffn_w2):
    b, t, d = x_prompt.shape
    bd, t_new, _ = x_sample.shape
    n_pages = page_table.shape[1]
    past = n_pages * PAGE_SIZE
    wb = state_nsa_win.shape[2]
    hd = N_HEADS * HEAD_DIM
    kvw = KV * HEAD_DIM
    n_chunks_s = (past + t_new) // CMP_STRIDE
    half = CMP_STRIDE * HEAD_DIM

    dist, addmask, offs = _bias_distances(t, past, n_chunks_s, wb)
    bias_all = bias_from_dist(rel_bias, dist, addmask)
    bias_s = _sample_bias(bias_all, offs, n_chunks_s, wb)

    w_in = nsa_w_in[0]
    wq = (w_in[:, :hd] * SCALE).astype(BF16)
    wkv4 = w_in[:, hd:hd + 4 * kvw].astype(BF16)
    wwin = w_in[:, hd + 4 * kvw:hd + 6 * kvw].astype(BF16)
    wg = w_in[:, hd + 6 * kvw:].reshape(d, KV, G * 3)
    wg = jnp.pad(wg, ((0, 0), (0, 0), (0, LANES - G * 3))).reshape(d, KV * LANES).astype(BF16)
    w_out_nsa = nsa_w_out[0].astype(BF16)
    cw = {
        "w1k": jnp.concatenate([nsa_phi_k1[0, :half], nsa_phi_k1[0, half:]], axis=1).astype(BF16),
        "w1v": jnp.concatenate([nsa_phi_v1[0, :half], nsa_phi_v1[0, half:]], axis=1).astype(BF16),
        "pe": jnp.broadcast_to(jnp.stack([nsa_pe_k[0].reshape(1, -1), nsa_pe_v[0].reshape(1, -1)]),
                               (2, SUBLANES, CMP_BLOCK * HEAD_DIM)),
        "w1": jnp.stack([nsa_phi_k1[0], nsa_phi_v1[0]]).astype(BF16),
        "w2": jnp.stack([nsa_phi_k2[0], nsa_phi_v2[0]]).astype(BF16),
    }
    wsb = sb_w_in[0]
    wsq = (wsb[:, :hd] * SCALE).astype(BF16)
    wskv = wsb[:, hd:].astype(BF16)
    w_out_sb = sb_w_out[0].astype(BF16)
    w1 = ffn_w1.astype(BF16)
    w3 = ffn_w3.astype(BF16)
    w2 = ffn_w2.astype(BF16)

    hp = x_prompt.reshape(b * t, d)
    hs = x_sample.reshape(bd * t_new, d)

    xp = rmsnorm_rows(hp, norm_w[0, 0], BF16)
    xs = rmsnorm_rows(hs, norm_w[0, 0], BF16)
    q_p, q_s = matmul(xp, wq, (BF16,)), matmul(xs, wq, (BF16,))
    (kv4_p, kv4b_p), kv4_s = matmul(xp, wkv4, (F32, BF16)), matmul(xs, wkv4, (F32,))
    (win_p, winb_p), win_s = matmul(xp, wwin, (F32, BF16)), matmul(xs, wwin, (F32,))
    g_p, g_s = matmul(xp, wg, (F32,), sigmoid=True), matmul(xs, wg, (F32,), sigmoid=True)

    pt_prompt = jnp.arange(b * (t // PAGE_SIZE), dtype=I32).reshape(b, t // PAGE_SIZE)
    kcb_p = compress(kv4_p.reshape(b * t // PAGE_SIZE, PAGE_SIZE, 4 * kvw), pt_prompt, cw)
    attn_p = nsa_prompt_attention(q_p, kv4b_p, winb_p, g_p, kcb_p, bias_all, b, t)

    cache5 = cache_nsa[0].reshape(cache_nsa.shape[1], PAGE_SIZE, 2, 2 * KV, HEAD_DIM)
    kcb_s = compress_paged(cache5, page_table, cw)
    attn_s = nsa_sample_attention(
        _pad_tokens(q_s, bd, t_new), cache5, page_table, kcb_s, _pad_tokens(kv4_s, bd, t_new),
        state_nsa_win[0].reshape(bd, wb * 2 * KV, HEAD_DIM), _pad_tokens(win_s, bd, t_new),
        _pad_tokens(g_s, bd, t_new), bias_s, past, t_new)
    attn_s = attn_s[:, :t_new].reshape(bd * t_new, hd)

    hp, xp = matmul_res_norm(attn_p, w_out_nsa, hp, norm_w[0, 1])
    hs, xs = matmul_res_norm(attn_s, w_out_nsa, hs, norm_w[0, 1])
    hp, xp = ffn_res_norm(xp, w1[0], w3[0], w2[0], hp, norm_w[1, 0], BF16)
    hs, xs = ffn_res_norm(xs, w1[0], w3[0], w2[0], hs, norm_w[1, 0], BF16)

    sq_p, sq_s = matmul(xp, wsq, (BF16,)), matmul(xs, wsq, (BF16,))
    (skv_p, skvb_p), skv_s = matmul(xp, wskv, (F32, BF16)), matmul(xs, wskv, (F32,))
    sattn_p = sb_prompt_attention(sq_p, skvb_p, b, t)

    q_rows = (_pad_tokens(sq_s, bd, t_new).reshape(bd, T8, N_HEADS, HEAD_DIM).transpose(0, 2, 1, 3)
              .reshape(bd, N_HEADS * T8, HEAD_DIM).astype(F32))
    csb5 = cache_sb[0].reshape(cache_sb.shape[1], PAGE_SIZE, SB_GROUPS, SUBLANES, HEAD_DIM)
    sattn_s = sb_sample_attention(q_rows, csb5, page_table, _pad_tokens(skv_s, bd, t_new), t_new)
    sattn_s = sattn_s.reshape(bd, N_HEADS, T8, HEAD_DIM)[:, :, :t_new].transpose(0, 2, 1, 3).reshape(bd * t_new, hd)

    hp, xp = matmul_res_norm(sattn_p, w_out_sb, hp, norm_w[1, 1])
    hs, xs = matmul_res_norm(sattn_s, w_out_sb, hs, norm_w[1, 1])
    y_p = ffn_res_norm(xp, w1[1], w3[1], w2[1], hp, final_norm_w, F32, emit_h=False)
    y_s = ffn_res_norm(xs, w1[1], w3[1], w2[1], hs, final_norm_w, F32, emit_h=False)

    win_keep = min(WINDOW, t)
    nsa_win_prompt = win_p.reshape(b, t, 2, KV, HEAD_DIM)[:, t - win_keep:]
    win_all = jnp.concatenate([state_nsa_win[0], win_s.reshape(bd, t_new, 2, KV, HEAD_DIM)], axis=1)
    return (y_p.reshape(b, t, d), y_s.reshape(bd, t_new, d),
            kv4_p.reshape(1, b, t, 4, KV, HEAD_DIM), kv4_s.reshape(1, bd, t_new, 4, KV, HEAD_DIM),
            nsa_win_prompt[None], win_all[:, t_new:][None],
            skv_p.reshape(1, b, t, 2, N_HEADS, HEAD_DIM), skv_s.reshape(1, bd, t_new, 2, N_HEADS, HEAD_DIM))
```
